```python
import jax, jax.numpy as jnp
from jax import lax
import numpy as np

D_MODEL = 2048
BATCH = 4
SEQ = 4096
DEPTH = 1

HEAD_DIM = 128
DIL_PATTERNS = ((128, 1), (512, 4), (2048, 16))
HEADS_PER_DIL_GROUP = 4
N_HEADS_DIL = HEADS_PER_DIL_GROUP * len(DIL_PATTERNS)
N_HEADS_SB = 8
WIDTH_DIL = N_HEADS_DIL * HEAD_DIM
WIDTH_SB = N_HEADS_SB * HEAD_DIM
D_DIL_OUT = HEADS_PER_DIL_GROUP * HEAD_DIM
Q_BLOCK = 128
ROPE_THETA = 500000.0
ROPE_DIM = HEAD_DIM // 4
N_BRANCHES = 2
D_IN = 3 * (WIDTH_DIL + WIDTH_SB) + N_BRANCHES * D_MODEL
N_EXPERTS = 64
TOP_K = 8
N_EXPERT_GROUPS = 8
TOPK_GROUPS = 4
D_EXPERT = 512
D_SHARED = 512
ROUTED_SCALE = 2.5
MOE_BLOCK = 256
RMS_EPS = 1e-6
N_MOD = 6

kernel_name = "hybrid_dilated_stickbreak_moe_layer"


def rms_norm(x, g):
    xf = x.astype(jnp.float32)
    y = xf * lax.rsqrt(jnp.mean(xf * xf, axis=-1, keepdims=True) + RMS_EPS)
    return (y * g.astype(jnp.float32)).astype(x.dtype)


def partial_rope(x, positions):
    inv_freq = ROPE_THETA ** (-jnp.arange(0, ROPE_DIM, 2, dtype=jnp.float32) / ROPE_DIM)
    ang = positions.astype(jnp.float32)[..., None] * inv_freq
    cos = jnp.cos(ang)[:, :, None, :]
    sin = jnp.sin(ang)[:, :, None, :]
    xr = x[..., :ROPE_DIM].astype(jnp.float32)
    x1, x2 = jnp.split(xr, 2, axis=-1)
    rot = jnp.concatenate([x1 * cos - x2 * sin, x2 * cos + x1 * sin], axis=-1)
    return jnp.concatenate([rot.astype(x.dtype), x[..., ROPE_DIM:]], axis=-1)


def dilated_window_attention(q, k, v, window, dilation):
    b, s, h, dh = q.shape
    w_sub = window // dilation
    L = s // dilation
    nb = -(-L // Q_BLOCK)
    Lp = nb * Q_BLOCK

    def split(t):
        t = t.reshape(b, L, dilation, h, dh).transpose(0, 2, 3, 1, 4)
        return jnp.pad(t, ((0, 0), (0, 0), (0, 0), (0, Lp - L), (0, 0)))

    def band(t):
        tp = jnp.pad(t, ((0, 0), (0, 0), (0, 0), (Q_BLOCK, 0), (0, 0)))
        tp = tp.reshape(b, dilation, h, nb + 1, Q_BLOCK, dh)
        return jnp.concatenate([tp[:, :, :, :-1], tp[:, :, :, 1:]], axis=4)

    qb = split(q).reshape(b, dilation, h, nb, Q_BLOCK, dh)
    kb = band(split(k))
    vb = band(split(v))
    scores = jnp.einsum('bdhnqe,bdhnke->bdhnqk', qb, kb).astype(jnp.float32) * (dh ** -0.5)
    qi = jnp.arange(nb)[:, None] * Q_BLOCK + jnp.arange(Q_BLOCK)[None, :]
    kj = (jnp.arange(nb)[:, None] - 1) * Q_BLOCK + jnp.arange(2 * Q_BLOCK)[None, :]
    rel = qi[:, :, None] - kj[:, None, :]
    mask = (rel >= 0) & (rel <= w_sub) & (kj[:, None, :] >= 0)
    scores = jnp.where(mask, scores, -jnp.inf)
    m = jnp.max(scores, axis=-1, keepdims=True)
    p = jnp.exp(scores - m)
    l = jnp.sum(p, axis=-1, keepdims=True)
    o = jnp.einsum('bdhnqk,bdhnke->bdhnqe', (p / l).astype(v.dtype), vb)
    lse = (m + jnp.log(l))[..., 0]
    o = o.reshape(b, dilation, h, Lp, dh)[:, :, :, :L].transpose(0, 3, 1, 2, 4).reshape(b, s, h, dh)
    lse = lse.reshape(b, dilation, h, Lp)[..., :L].transpose(0, 3, 1, 2).reshape(b, s, h)
    return o, lse


def stick_breaking_attention(q, k, v):
    b, s, h, dh = q.shape
    nb = s // Q_BLOCK
    qb = q.reshape(b, nb, Q_BLOCK, h, dh).transpose(1, 0, 3, 2, 4)
    kt = k.transpose(0, 2, 1, 3)
    vt = v.transpose(0, 2, 1, 3)
    key_pos = jnp.arange(s)

    def block(args):
        qblk, n = args
        z = jnp.einsum('bhqe,bhke->bhqk', qblk, kt).astype(jnp.float32) * (dh ** -0.5)
        qpos = n * Q_BLOCK + jnp.arange(Q_BLOCK)
        mask = key_pos[None, :] < qpos[:, None]
        log_1m = jnp.where(mask, jax.nn.log_sigmoid(-z), 0.0)
        tail = lax.cumsum(log_1m, axis=3, reverse=True) - log_1m
        a = jnp.where(mask, jnp.exp(jax.nn.log_sigmoid(z) + tail), 0.0)
        return jnp.einsum('bhqk,bhke->bhqe', a.astype(v.dtype), vt)

    o = lax.map(block, (qb, jnp.arange(nb)))
    return o.transpose(1, 0, 3, 2, 4).reshape(b, s, h * dh)


def hybrid_mixer(h, positions, w_in, w_dil_out, w_sb_out, w_mix_out):
    b, s, _ = h.shape
    proj = h @ w_in
    sizes = [WIDTH_DIL] * 3 + [WIDTH_SB] * 3 + [N_BRANCHES * D_MODEL]
    idx = np.cumsum(sizes)[:-1].tolist()
    qa, ka, va, qs, ks, vs, gate_logits = jnp.split(proj, idx, axis=-1)

    qa = partial_rope(qa.reshape(b, s, N_HEADS_DIL, HEAD_DIM), positions)
    ka = partial_rope(ka.reshape(b, s, N_HEADS_DIL, HEAD_DIM), positions)
    va = va.reshape(b, s, N_HEADS_DIL, HEAD_DIM)
    outs, lses = [], []
    for g, (window, dilation) in enumerate(DIL_PATTERNS):
        sl = slice(g * HEADS_PER_DIL_GROUP, (g + 1) * HEADS_PER_DIL_GROUP)
        o_g, lse_g = dilated_window_attention(qa[:, :, sl], ka[:, :, sl], va[:, :, sl], window, dilation)
        outs.append(o_g)
        lses.append(lse_g)
    o_stack = jnp.stack(outs, axis=2).astype(jnp.float32)
    wts = jax.nn.softmax(jnp.stack(lses, axis=2), axis=2)
    o_dil = jnp.sum(wts[..., None] * o_stack, axis=2).astype(h.dtype).reshape(b, s, D_DIL_OUT)

    o_sb = stick_breaking_attention(qs.reshape(b, s, N_HEADS_SB, HEAD_DIM),
                                    ks.reshape(b, s, N_HEADS_SB, HEAD_DIM),
                                    vs.reshape(b, s, N_HEADS_SB, HEAD_DIM))

    y_dil = o_dil @ w_dil_out
    y_sb = o_sb @ w_sb_out
    g_dil, g_sb = jnp.split(jax.nn.sigmoid(gate_logits), N_BRANCHES, axis=-1)
    return (g_dil * y_dil + g_sb * y_sb) @ w_mix_out


def swiglu(x, wg, wu, wd):
    return (jax.nn.silu(x @ wg) * (x @ wu)) @ wd


def routed_experts(uf, top_idx, top_w, w_gate_e, w_up_e, w_down_e):
    n, d = uf.shape
    nk = n * TOP_K
    flat_e = top_idx.reshape(-1)
    flat_tok = jnp.arange(nk, dtype=jnp.int32) // TOP_K
    flat_w = top_w.reshape(-1)
    order = jnp.argsort(flat_e)
    se = flat_e[order]
    counts = jnp.bincount(flat_e, length=N_EXPERTS)
    padded = (counts + MOE_BLOCK - 1) // MOE_BLOCK * MOE_BLOCK
    starts = jnp.cumsum(counts) - counts
    pends = jnp.cumsum(padded)
    pstarts = pends - padded
    dest = pstarts[se] + jnp.arange(nk, dtype=jnp.int32) - starts[se]
    n_blocks = -(-nk // MOE_BLOCK) + N_EXPERTS
    cap = n_blocks * MOE_BLOCK
    slot_tok = jnp.full((cap,), n, dtype=jnp.int32).at[dest].set(flat_tok[order])
    slot_w = jnp.zeros((cap,), jnp.float32).at[dest].set(flat_w[order])
    block_e = jnp.minimum(jnp.searchsorted(pends, jnp.arange(n_blocks) * MOE_BLOCK, side='right'),
                          N_EXPERTS - 1)
    u_pad = jnp.concatenate([uf, jnp.zeros((1, d), uf.dtype)], axis=0)

    def block(args):
        tok, wb, e = args
        yb = swiglu(u_pad[tok], w_gate_e[e], w_up_e[e], w_down_e[e])
        return yb * wb[:, None].astype(yb.dtype)

    yb = lax.map(block, (slot_tok.reshape(n_blocks, MOE_BLOCK), slot_w.reshape(n_blocks, MOE_BLOCK), block_e))
    out = jnp.zeros((n + 1, d), uf.dtype).at[slot_tok].add(yb.reshape(cap, d))
    return out[:n]


def moe_ffn(u, w_router, router_bias, w_gate_e, w_up_e, w_down_e, w_gate_s, w_up_s, w_down_s):
    b, s, d = u.shape
    n = b * s
    uf = u.reshape(n, d)
    scores = jax.nn.sigmoid((uf @ w_router).astype(jnp.float32))
    sel = scores + router_bias.astype(jnp.float32)
    grouped = sel.reshape(n, N_EXPERT_GROUPS, N_EXPERTS // N_EXPERT_GROUPS)
    group_score = lax.top_k(grouped, 2)[0].sum(-1)
    _, top_groups = lax.top_k(group_score, TOPK_GROUPS)
    group_mask = jax.nn.one_hot(top_groups, N_EXPERT_GROUPS, dtype=jnp.float32).sum(1) > 0
    expert_mask = jnp.repeat(group_mask, N_EXPERTS // N_EXPERT_GROUPS, axis=-1)
    _, top_idx = lax.top_k(jnp.where(expert_mask, sel, -jnp.inf), TOP_K)
    top_s = jnp.take_along_axis(scores, top_idx, axis=-1)
    top_w = top_s / jnp.sum(top_s, axis=-1, keepdims=True) * ROUTED_SCALE
    routed = routed_experts(uf, top_idx.astype(jnp.int32), top_w, w_gate_e, w_up_e, w_down_e)
    shared = swiglu(uf, w_gate_s, w_up_s, w_down_s)
    return (routed + shared).reshape(b, s, d)


def setup_inputs(seed: int = 0) -> dict:
    key = jax.random.key(seed)
    ks = jax.random.split(key, 24)
    f32 = jnp.float32
    D, L = D_MODEL, DEPTH

    def nrm(k, shape, fan_in, scale=1.0):
        return jax.random.normal(k, shape, f32) * (scale * fan_in ** -0.5)

    offsets = jax.random.randint(ks[2], (BATCH, 1), 0, SEQ, dtype=jnp.int32)
    positions = offsets + jnp.arange(SEQ, dtype=jnp.int32)[None, :]
    return {
        "x": jax.random.normal(ks[0], (BATCH, SEQ, D), f32),
        "c": jax.random.normal(ks[1], (BATCH, D), f32),
        "positions": positions,
        "w_ada": nrm(ks[3], (L, D, N_MOD * D), D, 0.5),
        "b_ada": 0.02 * jax.random.normal(ks[4], (L, N_MOD * D), f32),
        "g_pre_mix": 1.0 + 0.05 * jax.random.normal(ks[5], (L, D), f32),
        "g_post_mix": 1.0 + 0.05 * jax.random.normal(ks[6], (L, D), f32),
        "g_pre_ffn": 1.0 + 0.05 * jax.random.normal(ks[7], (L, D), f32),
        "g_post_ffn": 1.0 + 0.05 * jax.random.normal(ks[8], (L, D), f32),
        "w_in": nrm(ks[9], (L, D, D_IN), D),
        "w_dil_out": nrm(ks[10], (L, D_DIL_OUT, D), D_DIL_OUT),
        "w_sb_out": nrm(ks[11], (L, WIDTH_SB, D), WIDTH_SB),
        "w_mix_out": nrm(ks[12], (L, D, D), D),
        "w_router": nrm(ks[13], (L, D, N_EXPERTS), D),
        "router_bias": 0.01 * jax.random.normal(ks[14], (L, N_EXPERTS), f32),
        "w_gate_e": nrm(ks[15], (L, N_EXPERTS, D, D_EXPERT), D),
        "w_up_e": nrm(ks[16], (L, N_EXPERTS, D, D_EXPERT), D),
        "w_down_e": nrm(ks[17], (L, N_EXPERTS, D_EXPERT, D), D_EXPERT),
        "w_gate_s": nrm(ks[18], (L, D, D_SHARED), D),
        "w_up_s": nrm(ks[19], (L, D, D_SHARED), D),
        "w_down_s": nrm(ks[20], (L, D_SHARED, D), D_SHARED),
    }


def reference(x, c, positions, w_ada, b_ada, g_pre_mix, g_post_mix, g_pre_ffn, g_post_ffn,
              w_in, w_dil_out, w_sb_out, w_mix_out, w_router, router_bias,
              w_gate_e, w_up_e, w_down_e, w_gate_s, w_up_s, w_down_s):
    for l in range(DEPTH):
        mod = jax.nn.silu(c) @ w_ada[l] + b_ada[l]
        shift1, scale1, gate1, shift2, scale2, gate2 = [m[:, None, :] for m in jnp.split(mod, N_MOD, axis=-1)]
        h = rms_norm(x, g_pre_mix[l]) * (1.0 + scale1) + shift1
        y = hybrid_mixer(h, positions, w_in[l], w_dil_out[l], w_sb_out[l], w_mix_out[l])
        x = x + gate1 * rms_norm(y, g_post_mix[l])
        h = rms_norm(x, g_pre_ffn[l]) * (1.0 + scale2) + shift2
        y = moe_ffn(h, w_router[l], router_bias[l], w_gate_e[l], w_up_e[l], w_down_e[l],
                    w_gate_s[l], w_up_s[l], w_down_s[l])
        x = x + gate2 * rms_norm(y, g_post_ffn[l])
    return x
```

```python
import functools

import jax
import jax.numpy as jnp
from jax import lax
from jax.experimental import pallas as pl
from jax.experimental.pallas import tpu as pltpu

F32 = jnp.float32
BF16 = jnp.bfloat16
I32 = jnp.int32
U32 = jnp.uint32

D_MODEL = 2048
HEAD_DIM = 128
DIL_PATTERNS = ((128, 1), (512, 4), (2048, 16))
HEADS_PER_GROUP = 4
N_HEADS_DIL = 12
N_HEADS_SB = 8
WIDTH_DIL = N_HEADS_DIL * HEAD_DIM
WIDTH_SB = N_HEADS_SB * HEAD_DIM
D_DIL_OUT = HEADS_PER_GROUP * HEAD_DIM
Q_BLOCK = 128
ROPE_THETA = 500000.0
ROPE_DIM = HEAD_DIM // 4
N_GATE = 2 * D_MODEL
D_IN = N_GATE + 3 * WIDTH_DIL + 3 * WIDTH_SB
N_EXPERTS = 64
TOP_K = 8
N_GROUPS = 8
GROUP_SIZE = N_EXPERTS // N_GROUPS
TOPK_GROUPS = 4
D_EXPERT = 512
D_SHARED = 512
ROUTED_SCALE = 2.5
RMS_EPS = 1e-6
N_MOD = 6
ATTN_SCALE = HEAD_DIM ** -0.5

OFF_DIL = N_GATE
OFF_SB = N_GATE + 3 * WIDTH_DIL

LANES = 128
SUBLANES = 8
VMEM_LIMIT = 56 * 1024 * 1024

ROW_BLOCK = 256
SB_DEAD = -110.0


def _cparams(sem, **kw):
    return pltpu.CompilerParams(dimension_semantics=sem, vmem_limit_bytes=VMEM_LIMIT, **kw)


def _adaln_body(ct_ref, w_ref, b_ref, o_ref, *, kc):
    nb = ct_ref.shape[1]
    nk = w_ref.shape[0] // kc

    def step(i, acc):
        k0 = pl.multiple_of(i * kc, kc)
        w = w_ref[pl.ds(k0, kc), :]
        c = ct_ref[pl.ds(k0, kc), :]
        s = c * jax.nn.sigmoid(c)
        parts = [jnp.sum(w * s[:, b:b + 1], axis=0, keepdims=True) for b in range(nb)]
        return acc + jnp.concatenate(parts, axis=0)

    acc = lax.fori_loop(0, nk, step, jnp.zeros(o_ref.shape, F32))
    o_ref[...] = acc + b_ref[...]


def _adaln(c, w_ada, b_ada):
    nb, d = c.shape
    n_out = w_ada.shape[1]
    tn = 1024
    return pl.pallas_call(
        functools.partial(_adaln_body, kc=256),
        grid=(n_out // tn,),
        in_specs=[
            pl.BlockSpec((d, nb), lambda j: (0, 0)),
            pl.BlockSpec((d, tn), lambda j: (0, j)),
            pl.BlockSpec((1, tn), lambda j: (0, j)),
        ],
        out_specs=pl.BlockSpec((nb, tn), lambda j: (0, j)),
        out_shape=jax.ShapeDtypeStruct((nb, n_out), F32),
        compiler_params=_cparams(("arbitrary",)),
        name="adaln",
    )(c.T, w_ada, b_ada.reshape(1, n_out))


def _rms(x):
    return x * lax.rsqrt(jnp.mean(x * x, axis=-1, keepdims=True) + RMS_EPS)


def _inproj_body(x_ref, g_ref, mod_ref, w_ref, o_ref, h_ref):
    @pl.when(pl.program_id(1) == 0)
    def _():
        y = _rms(x_ref[...]) * g_ref[...]
        shift = mod_ref[0, 0:1, :]
        scale = mod_ref[0, 1:2, :]
        h_ref[...] = (y * (1.0 + scale) + shift).astype(BF16)

    o_ref[...] = jnp.dot(h_ref[...], w_ref[...], preferred_element_type=F32).astype(o_ref.dtype)


def _inproj(xf, g_pre, mod3, w_in_b, seq):
    n, d = xf.shape
    d_in = w_in_b.shape[1]
    tm, tn = 512, d_in // 4
    per_b = seq // tm
    return pl.pallas_call(
        _inproj_body,
        grid=(n // tm, d_in // tn),
        in_specs=[
            pl.BlockSpec((tm, d), lambda i, j: (i, 0)),
            pl.BlockSpec((1, d), lambda i, j: (0, 0)),
            pl.BlockSpec((1, N_MOD, d), lambda i, j: (i // per_b, 0, 0)),
            pl.BlockSpec((d, tn), lambda i, j: (0, j)),
        ],
        out_specs=pl.BlockSpec((tm, tn), lambda i, j: (i, j)),
        out_shape=jax.ShapeDtypeStruct((n, d_in), BF16),
        scratch_shapes=[pltpu.VMEM((tm, d), BF16)],
        compiler_params=_cparams(("arbitrary", "arbitrary")),
        name="inproj",
    )(xf, g_pre.reshape(1, d), mod3, w_in_b)


def _rope_body(pos_ref, f_ref, o_ref):
    ang = pos_ref[...].astype(F32) * f_ref[...]
    c = jnp.cos(ang)
    s = jnp.sin(ang)
    lane = lax.broadcasted_iota(I32, ang.shape, 1)
    half = ROPE_DIM // 2
    o_ref[:, 0:LANES] = c
    o_ref[:, LANES:2 * LANES] = jnp.where(lane >= half, s, 0.0)
    o_ref[:, 2 * LANES:3 * LANES] = jnp.where(lane < half, -s, 0.0)


def _rope_tables(positions):
    n = positions.size
    half = ROPE_DIM // 2
    inv_freq = ROPE_THETA ** (-jnp.arange(0, ROPE_DIM, 2, dtype=F32) / ROPE_DIM)
    f = jnp.concatenate([inv_freq, inv_freq, jnp.zeros((LANES - 2 * half,), F32)]).reshape(1, LANES)
    tm = 2048
    return pl.pallas_call(
        _rope_body,
        grid=(n // tm,),
        in_specs=[pl.BlockSpec((tm, 1), lambda i: (i, 0)), pl.BlockSpec((1, LANES), lambda i: (0, 0))],
        out_specs=pl.BlockSpec((tm, 3 * LANES), lambda i: (i, 0)),
        out_shape=jax.ShapeDtypeStruct((n, 3 * LANES), F32),
        compiler_params=_cparams(("arbitrary",)),
        name="rope_tables",
    )(positions.reshape(n, 1), f)


def _apply_rope(x, t):
    half = ROPE_DIM // 2
    return (x * t[:, 0:LANES]
            + pltpu.roll(x, half, 1) * t[:, LANES:2 * LANES]
            + pltpu.roll(x, LANES - half, 1) * t[:, 2 * LANES:3 * LANES])


def _dil_body(q_ref, kc_ref, vc_ref, kp_ref, vp_ref, tc_ref, tp_ref, o_ref, lse_ref, *, nsub):
    n = pl.program_id(2)
    tq = tc_ref[0]
    tp = tp_ref[0]
    row = lax.broadcasted_iota(I32, (Q_BLOCK, 2 * Q_BLOCK), 0)
    col = lax.broadcasted_iota(I32, (Q_BLOCK, 2 * Q_BLOCK), 1)
    rel = row + Q_BLOCK - col
    band = jnp.where(rel >= 0, jnp.where(rel <= Q_BLOCK, 1.0, 0.0), 0.0)
    first = jnp.where(col >= Q_BLOCK, band, jnp.where(n > 0, band, 0.0))
    for h in range(HEADS_PER_GROUP):
        sl = slice(h * HEAD_DIM, (h + 1) * HEAD_DIM)
        q = _apply_rope(q_ref[0, :, sl].astype(F32), tq)
        kc = _apply_rope(kc_ref[0, :, sl].astype(F32), tq).astype(BF16)
        kp = _apply_rope(kp_ref[0, :, sl].astype(F32), tp).astype(BF16)
        vc = vc_ref[0, :, sl]
        vp = vp_ref[0, :, sl]
        for j in range(nsub):
            rs = slice(j * Q_BLOCK, (j + 1) * Q_BLOCK)
            ps = slice((j - 1) * Q_BLOCK, j * Q_BLOCK)
            kcat = jnp.concatenate([kp if j == 0 else kc[ps], kc[rs]], axis=0)
            vcat = jnp.concatenate([vp if j == 0 else vc[ps], vc[rs]], axis=0)
            s = lax.dot_general(q[rs].astype(BF16), kcat, (((1,), (1,)), ((), ())),
                                preferred_element_type=F32) * ATTN_SCALE
            s = jnp.where((first if j == 0 else band) > 0.0, s, -jnp.inf)
            m = jnp.max(s, axis=-1, keepdims=True)
            p = jnp.exp(s - m)
            l = jnp.sum(p, axis=-1, keepdims=True)
            o = jnp.dot((p / l).astype(BF16), vcat, preferred_element_type=F32)
            o_ref[0, rs, sl] = o.astype(o_ref.dtype)
            lse_ref[0, rs, sl] = jnp.broadcast_to(m + jnp.log(l), (Q_BLOCK, HEAD_DIM))


def _dilated_attention(proj, tables, g, dilation, bsz, seq):
    d = dilation
    length = seq // d
    tq = min(2 * Q_BLOCK, length)
    nsub = tq // Q_BLOCK
    d_in = proj.shape[1]
    wblk = d_in // D_DIL_OUT
    base = OFF_DIL // D_DIL_OUT
    pv = proj.reshape(bsz, length, d * d_in)
    tv = tables.reshape(bsz, length, d * 3 * LANES)

    def cur(off):
        return pl.BlockSpec((1, tq, D_DIL_OUT), lambda b, r, n: (b, n, r * wblk + base + off + g))

    def prev(off):
        return pl.BlockSpec((1, Q_BLOCK, D_DIL_OUT),
                            lambda b, r, n: (b, jnp.maximum(n * nsub - 1, 0), r * wblk + base + off + g))

    nq, nk = 0, WIDTH_DIL // D_DIL_OUT
    nv = 2 * nk
    o, lse = pl.pallas_call(
        functools.partial(_dil_body, nsub=nsub),
        grid=(bsz, d, length // tq),
        in_specs=[
            cur(nq), cur(nk), cur(nv), prev(nk), prev(nv),
            pl.BlockSpec((1, tq, 3 * LANES), lambda b, r, n: (b, n, r)),
            pl.BlockSpec((1, Q_BLOCK, 3 * LANES), lambda b, r, n: (b, jnp.maximum(n * nsub - 1, 0), r)),
        ],
        out_specs=[
            pl.BlockSpec((1, tq, D_DIL_OUT), lambda b, r, n: (b, n, r)),
            pl.BlockSpec((1, tq, D_DIL_OUT), lambda b, r, n: (b, n, r)),
        ],
        out_shape=[
            jax.ShapeDtypeStruct((bsz, length, d * D_DIL_OUT), BF16),
            jax.ShapeDtypeStruct((bsz, length, d * D_DIL_OUT), F32),
        ],
        compiler_params=_cparams(("arbitrary", "arbitrary", "arbitrary")),
        name=f"dilated_d{d}",
    )(pv, pv, pv, pv, pv, tv, tv)
    n = bsz * seq
    return o.reshape(n, D_DIL_OUT), lse.reshape(n, D_DIL_OUT)


def _sb_body(q_ref, k_ref, v_ref, o_ref, acc_ref, car_ref):
    nblk = q_ref.shape[1] // Q_BLOCK
    r = lax.broadcasted_iota(I32, (Q_BLOCK, Q_BLOCK), 0)
    c = lax.broadcasted_iota(I32, (Q_BLOCK, Q_BLOCK), 1)
    causal = c < r
    rr = lax.broadcasted_iota(I32, (Q_BLOCK, 2 * Q_BLOCK), 0)
    cc = lax.broadcasted_iota(I32, (Q_BLOCK, 2 * Q_BLOCK), 1)
    uo = jnp.where(cc >= Q_BLOCK, 1.0, jnp.where(rr > cc, 1.0, 0.0)).astype(BF16)

    def tile(qj, kb, carry, diag):
        k0 = pl.multiple_of(kb * Q_BLOCK, Q_BLOCK)
        kt = k_ref[0, pl.ds(k0, Q_BLOCK), :]
        vt = v_ref[0, pl.ds(k0, Q_BLOCK), :]
        z = lax.dot_general(qj, kt, (((1,), (1,)), ((), ())), preferred_element_type=F32) * ATTN_SCALE
        sp = jnp.log(1.0 + jnp.exp(-jnp.abs(z)))
        mx = jnp.maximum(z, 0.0)
        log_1m = -(mx + sp)
        log_s = (z - mx) - sp
        if diag:
            log_1m = jnp.where(causal, log_1m, 0.0)
        hi = log_1m.astype(BF16)
        lo = (log_1m - hi.astype(F32)).astype(BF16)
        r2 = jnp.dot(jnp.concatenate([hi, lo], axis=0), uo, preferred_element_type=F32)
        sums = r2[:Q_BLOCK] + r2[Q_BLOCK:]
        a = jnp.exp(log_s + carry + sums[:, :Q_BLOCK])
        if diag:
            a = jnp.where(causal, a, 0.0)
        pv = jnp.dot(a.astype(BF16), vt, preferred_element_type=F32)
        return pv, carry + sums[:, Q_BLOCK:]

    def qblock(qi, _):
        q0 = pl.multiple_of(qi * Q_BLOCK, Q_BLOCK)
        qj = q_ref[0, pl.ds(q0, Q_BLOCK), :]
        pv, car = tile(qj, qi, jnp.zeros((Q_BLOCK, Q_BLOCK), F32), True)
        acc_ref[...] = pv
        car_ref[...] = car

        def cond(st):
            return jnp.logical_and(st[0] >= 0, st[1] > SB_DEAD)

        def body(st):
            pv, car = tile(qj, st[0], car_ref[...], False)
            acc_ref[...] += pv
            car_ref[...] = car
            return st[0] - 1, jnp.max(car)

        lax.while_loop(cond, body, (qi - 1, jnp.max(car)))
        o_ref[0, pl.ds(q0, Q_BLOCK), :] = acc_ref[...].astype(o_ref.dtype)
        return 0

    lax.fori_loop(0, nblk, qblock, 0)


def _stick_breaking(proj, bsz, seq):
    d_in = proj.shape[1]
    pv = proj.reshape(bsz, seq, d_in)
    base = OFF_SB // HEAD_DIM

    def spec(off):
        return pl.BlockSpec((1, seq, HEAD_DIM), lambda b, h: (b, 0, base + off + h))

    o = pl.pallas_call(
        _sb_body,
        grid=(bsz, N_HEADS_SB),
        in_specs=[spec(0), spec(N_HEADS_SB), spec(2 * N_HEADS_SB)],
        out_specs=pl.BlockSpec((1, seq, HEAD_DIM), lambda b, h: (b, 0, h)),
        out_shape=jax.ShapeDtypeStruct((bsz, seq, WIDTH_SB), BF16),
        scratch_shapes=[pltpu.VMEM((Q_BLOCK, Q_BLOCK), F32), pltpu.VMEM((Q_BLOCK, Q_BLOCK), F32)],
        compiler_params=_cparams(("arbitrary", "arbitrary")),
        name="stick_breaking",
    )(pv, pv, pv)
    return o.reshape(bsz * seq, WIDTH_SB)


def _mixout_body(o1, o2, o3, l1, l2, l3, osb, gd_ref, gs_ref, x_ref, mod_ref, gpost, gpre,
                 wd, ws, wm, x1_ref, h2_ref):
    la, lb, lc = l1[...], l2[...], l3[...]
    m = jnp.maximum(la, jnp.maximum(lb, lc))
    ea, eb, ec = jnp.exp(la - m), jnp.exp(lb - m), jnp.exp(lc - m)
    od = (ea * o1[...].astype(F32) + eb * o2[...].astype(F32) + ec * o3[...].astype(F32)) / (ea + eb + ec)
    yd = jnp.dot(od.astype(BF16), wd[...], preferred_element_type=F32)
    ys = jnp.dot(osb[...], ws[...], preferred_element_type=F32)
    mix = jax.nn.sigmoid(gd_ref[...].astype(F32)) * yd + jax.nn.sigmoid(gs_ref[...].astype(F32)) * ys
    y = jnp.dot(mix.astype(BF16), wm[...], preferred_element_type=F32)
    gate1 = mod_ref[0, 2:3, :]
    shift2 = mod_ref[0, 3:4, :]
    scale2 = mod_ref[0, 4:5, :]
    x1 = x_ref[...] + gate1 * (_rms(y) * gpost[...])
    x1_ref[...] = x1
    h2_ref[...] = ((_rms(x1) * gpre[...]) * (1.0 + scale2) + shift2).astype(h2_ref.dtype)


def _const_spec(shape):
    return pl.BlockSpec(shape, lambda i: (0,) * len(shape), pipeline_mode=pl.Buffered(1))


def _mixout(o_dil, lse_dil, o_sb, proj, xf, mod3, g_post, g_pre, wd_b, ws_b, wm_b, seq):
    n, d = xf.shape
    tm = 256
    per_b = seq // tm
    row = lambda w: pl.BlockSpec((tm, w), lambda i: (i, 0))
    in_specs = (
        [row(D_DIL_OUT)] * 6 + [row(WIDTH_SB)]
        + [pl.BlockSpec((tm, d), lambda i: (i, 0)), pl.BlockSpec((tm, d), lambda i: (i, 1))]
        + [row(d), pl.BlockSpec((1, N_MOD, d), lambda i: (i // per_b, 0, 0))]
        + [_const_spec((1, d)), _const_spec((1, d))]
        + [_const_spec(wd_b.shape), _const_spec(ws_b.shape), _const_spec(wm_b.shape)]
    )
    return pl.pallas_call(
        _mixout_body,
        grid=(n // tm,),
        in_specs=in_specs,
        out_specs=[row(d), row(d)],
        out_shape=[jax.ShapeDtypeStruct((n, d), F32), jax.ShapeDtypeStruct((n, d), BF16)],
        compiler_params=_cparams(("arbitrary",)),
        name="mixout",
    )(*o_dil, *lse_dil, o_sb, proj, proj, xf, mod3, g_post.reshape(1, d), g_pre.reshape(1, d), wd_b, ws_b, wm_b)


def _topk_rows(x, k, iota0):
    big = x.shape[0]
    out = []
    for _ in range(k):
        m = jnp.max(x, axis=0, keepdims=True)
        i = jnp.min(jnp.where(x == m, iota0, big), axis=0, keepdims=True)
        out.append((m, i))
        x = jnp.where(iota0 == i, -jnp.inf, x)
    return out


def _router_body(h_ref, wr_ref, bias_ref, idx_ref, wt_ref):
    tm = h_ref.shape[0]
    logits = lax.dot_general(wr_ref[...], h_ref[...], (((1,), (1,)), ((), ())), preferred_element_type=F32)
    scores = jax.nn.sigmoid(logits)
    sel = scores + bias_ref[...]
    sub = lax.broadcasted_iota(I32, (GROUP_SIZE, tm), 0)
    grp = []
    for g in range(N_GROUPS):
        (m1, _), (m2, _) = _topk_rows(sel[g * GROUP_SIZE:(g + 1) * GROUP_SIZE], 2, sub)
        grp.append(m1 + m2)
    gscore = jnp.concatenate(grp, axis=0)
    giota = lax.broadcasted_iota(I32, (N_GROUPS, tm), 0)
    gmask = jnp.zeros((N_GROUPS, tm), F32)
    for _, gi in _topk_rows(gscore, TOPK_GROUPS, giota):
        gmask = jnp.where(giota == gi, 1.0, gmask)
    masked = jnp.concatenate(
        [jnp.where(gmask[g:g + 1] > 0.0, sel[g * GROUP_SIZE:(g + 1) * GROUP_SIZE], -jnp.inf)
         for g in range(N_GROUPS)], axis=0)
    eiota = lax.broadcasted_iota(I32, (N_EXPERTS, tm), 0)
    picks = _topk_rows(masked, TOP_K, eiota)
    idx = jnp.concatenate([i for _, i in picks], axis=0)
    top_s = jnp.concatenate(
        [jnp.sum(jnp.where(eiota == i, scores, 0.0), axis=0, keepdims=True) for _, i in picks], axis=0)
    top_w = top_s / jnp.sum(top_s, axis=0, keepdims=True) * ROUTED_SCALE
    idx_ref[...] = idx
    wpad = jnp.concatenate([top_w, jnp.zeros((LANES - TOP_K, tm), F32)], axis=0)
    wt_ref[...] = wpad.T


def _router(h2, w_router, router_bias):
    n, d = h2.shape
    tm = 512
    return pl.pallas_call(
        _router_body,
        grid=(n // tm,),
        in_specs=[
            pl.BlockSpec((tm, d), lambda i: (i, 0)),
            pl.BlockSpec((N_EXPERTS, d), lambda i: (0, 0)),
            pl.BlockSpec((N_EXPERTS, 1), lambda i: (0, 0)),
        ],
        out_specs=[pl.BlockSpec((TOP_K, tm), lambda i: (0, i)), pl.BlockSpec((tm, LANES), lambda i: (i, 0))],
        out_shape=[jax.ShapeDtypeStruct((TOP_K, n), I32), jax.ShapeDtypeStruct((n, LANES), F32)],
        compiler_params=_cparams(("arbitrary",)),
        name="router",
    )(h2, w_router.T.astype(BF16), router_bias.reshape(N_EXPERTS, 1))


def _plan_tile(idx):
    tm = idx.shape[1]
    eiota = lax.broadcasted_iota(I32, (N_EXPERTS, tm), 0)
    hit = jnp.zeros((N_EXPERTS, tm), F32)
    for k in range(TOP_K):
        hit = jnp.where(eiota == idx[k:k + 1], 1.0, hit)
    r = lax.broadcasted_iota(I32, (tm, tm), 0)
    c = lax.broadcasted_iota(I32, (tm, tm), 1)
    before = jnp.where(r < c, 1.0, 0.0).astype(BF16)
    excl = jnp.dot(hit.astype(BF16), before, preferred_element_type=F32)
    tot = excl[:, tm - 1:tm] + hit[:, tm - 1:tm]
    return eiota, hit, excl, tot


def _plan_counts_body(idx_ref, cnt_ref):
    @pl.when(pl.program_id(0) == 0)
    def _():
        cnt_ref[...] = jnp.zeros(cnt_ref.shape, F32)

    _, _, _, tot = _plan_tile(idx_ref[...])
    cnt_ref[...] += tot


def _plan_counts(top_idx):
    n = top_idx.shape[1]
    tm = 512
    return pl.pallas_call(
        _plan_counts_body,
        grid=(n // tm,),
        in_specs=[pl.BlockSpec((TOP_K, tm), lambda i: (0, i))],
        out_specs=pl.BlockSpec((N_EXPERTS, LANES), lambda i: (0, 0)),
        out_shape=jax.ShapeDtypeStruct((N_EXPERTS, LANES), F32),
        compiler_params=_cparams(("arbitrary",)),
        name="plan_counts",
    )(top_idx)


def _plan_dest_body(idx_ref, base_ref, dest_ref, run_ref):
    @pl.when(pl.program_id(0) == 0)
    def _():
        run_ref[...] = base_ref[...]

    idx = idx_ref[...]
    eiota, _, excl, tot = _plan_tile(idx)
    pos = run_ref[:, 0:1] + excl
    rows = [jnp.sum(jnp.where(eiota == idx[k:k + 1], pos, 0.0), axis=0, keepdims=True) for k in range(TOP_K)]
    dest_ref[...] = jnp.concatenate(rows, axis=0).astype(I32)
    run_ref[...] += tot


def _plan_dest(top_idx, pstart):
    n = top_idx.shape[1]
    tm = 512
    base = jnp.broadcast_to(pstart.astype(F32).reshape(N_EXPERTS, 1), (N_EXPERTS, LANES))
    return pl.pallas_call(
        _plan_dest_body,
        grid=(n // tm,),
        in_specs=[pl.BlockSpec((TOP_K, tm), lambda i: (0, i)), pl.BlockSpec((N_EXPERTS, LANES), lambda i: (0, 0))],
        out_specs=pl.BlockSpec((TOP_K, tm), lambda i: (0, i)),
        out_shape=jax.ShapeDtypeStruct((TOP_K, n), I32),
        scratch_shapes=[pltpu.VMEM((N_EXPERTS, LANES), F32)],
        compiler_params=_cparams(("arbitrary",)),
        name="plan_dest",
    )(top_idx, base)


def _capacity(n):
    return n * TOP_K + N_EXPERTS * ROW_BLOCK


def _pack_halves(y):
    w = y.shape[1] // 2
    hi = pltpu.bitcast(y[:, :w].astype(BF16).astype(F32), U32)
    lo = pltpu.bitcast(y[:, w:].astype(BF16).astype(F32), U32)
    return hi | (lo >> 16)


def _unpack_halves(p):
    a = pltpu.bitcast(p & jnp.uint32(0xFFFF0000), F32)
    b = pltpu.bitcast(p << 16, F32)
    return a, b


def _pad_chunks():
    sizes, s = [], ROW_BLOCK // 2
    while s >= SUBLANES:
        sizes.append(s)
        s //= 2
    return sizes


def _dispatch_body(pstart_ref, cnt_ref, dest_ref, h_ref, xs_ref, xbuf, zbuf, sem, zsem):
    tm = h_ref.shape[0]
    step = pl.program_id(0)

    def pad_dmas(wait):
        def per_expert(e, _):
            cnt = cnt_ref[e]
            pad = (ROW_BLOCK - cnt % ROW_BLOCK) % ROW_BLOCK
            start = pstart_ref[e] + cnt
            head = (SUBLANES - start % SUBLANES) % SUBLANES
            for r in range(SUBLANES - 1):
                @pl.when(r < head)
                def _():
                    cp = pltpu.make_async_copy(zbuf.at[pl.ds(0, 1)], xs_ref.at[pl.ds(start + r, 1)], zsem)
                    cp.wait() if wait else cp.start()
            body = start + head
            rem = pad - head
            for size in _pad_chunks():
                @pl.when((rem & size) != 0)
                def _():
                    off = pl.multiple_of(body + (rem & ~(2 * size - 1)), SUBLANES)
                    cp = pltpu.make_async_copy(zbuf.at[pl.ds(0, size)], xs_ref.at[pl.ds(off, size)], zsem)
                    cp.wait() if wait else cp.start()
            return 0
        lax.fori_loop(0, N_EXPERTS, per_expert, 0)

    @pl.when(step == 0)
    def _():
        zbuf[...] = jnp.zeros(zbuf.shape, zbuf.dtype)
        pad_dmas(False)
        pad_dmas(True)

    xbuf[...] = _pack_halves(h_ref[...].astype(F32))

    def row_copy(t, k):
        return pltpu.make_async_copy(xbuf.at[pl.ds(t, 1)], xs_ref.at[pl.ds(dest_ref[k, t], 1)], sem)

    def issue(t, _):
        for k in range(TOP_K):
            row_copy(t, k).start()
        return 0

    def drain(t, _):
        for k in range(TOP_K):
            row_copy(t, k).wait()
        return 0

    lax.fori_loop(0, tm, issue, 0)
    lax.fori_loop(0, tm, drain, 0)


def _dispatch(h2, dest, pstart, counts):
    n, d = h2.shape
    tm = 256
    w = d // 2
    grid_spec = pltpu.PrefetchScalarGridSpec(
        num_scalar_prefetch=2,
        grid=(n // tm,),
        in_specs=[
            pl.BlockSpec((TOP_K, tm), lambda i, *_: (0, i), memory_space=pltpu.SMEM),
            pl.BlockSpec((tm, d), lambda i, *_: (i, 0)),
        ],
        out_specs=pl.BlockSpec(memory_space=pl.ANY),
        scratch_shapes=[
            pltpu.VMEM((tm, w), U32),
            pltpu.VMEM((ROW_BLOCK // 2, w), U32),
            pltpu.SemaphoreType.DMA(()),
            pltpu.SemaphoreType.DMA(()),
        ],
    )
    return pl.pallas_call(
        _dispatch_body,
        grid_spec=grid_spec,
        out_shape=jax.ShapeDtypeStruct((_capacity(n), w), U32),
        compiler_params=_cparams(("arbitrary",), has_side_effects=True, disable_bounds_checks=True),
        name="dispatch",
    )(pstart, counts, dest, h2)


def _experts_body(be_ref, nu_ref, x_ref, wg_ref, wu_ref, wd_ref, y_ref, wg_b, wu_b, wd_b):
    i = pl.program_id(0)
    used = i < nu_ref[0]
    prev = be_ref[jnp.maximum(i - 1, 0)]
    fresh = jnp.logical_or(i == 0, be_ref[i] != prev)

    @pl.when(jnp.logical_and(used, fresh))
    def _():
        wg_b[...] = wg_ref[0].astype(BF16)
        wu_b[...] = wu_ref[0].astype(BF16)
        wd_b[...] = wd_ref[0].astype(BF16)

    @pl.when(used)
    def _():
        half = x_ref.shape[1]
        xa, xb = _unpack_halves(x_ref[...])
        xa = xa.astype(BF16)
        xb = xb.astype(BF16)
        g = (jnp.dot(xa, wg_b[:half], preferred_element_type=F32)
             + jnp.dot(xb, wg_b[half:], preferred_element_type=F32))
        u = (jnp.dot(xa, wu_b[:half], preferred_element_type=F32)
             + jnp.dot(xb, wu_b[half:], preferred_element_type=F32))
        hmid = (g * jax.nn.sigmoid(g) * u).astype(BF16)
        y_ref[...] = _pack_halves(jnp.dot(hmid, wd_b[...], preferred_element_type=F32))


def _experts(xs, block_e, n_used, w_gate_e, w_up_e, w_down_e):
    cap, w = xs.shape
    d = 2 * w
    nblk = cap // ROW_BLOCK

    def row_map(i, be, nu):
        return (jnp.minimum(i, nu[0] - 1), 0)

    def w_map(i, be, nu):
        return (be[i], 0, 0)

    grid_spec = pltpu.PrefetchScalarGridSpec(
        num_scalar_prefetch=2,
        grid=(nblk,),
        in_specs=[
            pl.BlockSpec((ROW_BLOCK, w), row_map),
            pl.BlockSpec((1, d, D_EXPERT), w_map),
            pl.BlockSpec((1, d, D_EXPERT), w_map),
            pl.BlockSpec((1, D_EXPERT, d), w_map),
        ],
        out_specs=pl.BlockSpec((ROW_BLOCK, w), row_map),
        scratch_shapes=[
            pltpu.VMEM((d, D_EXPERT), BF16),
            pltpu.VMEM((d, D_EXPERT), BF16),
            pltpu.VMEM((D_EXPERT, d), BF16),
        ],
    )
    return pl.pallas_call(
        _experts_body,
        grid_spec=grid_spec,
        out_shape=jax.ShapeDtypeStruct((cap, w), U32),
        compiler_params=_cparams(("arbitrary",)),
        name="experts",
    )(block_e, n_used, xs, w_gate_e, w_up_e, w_down_e)


def _combine_body(dest_ref, ys_ref, wt_ref, h_ref, x1_ref, mod_ref, gpost, wg, wu, wd, o_ref, buf, sem):
    tm = h_ref.shape[0]

    def row_copy(t, k):
        return pltpu.make_async_copy(ys_ref.at[pl.ds(dest_ref[k, t], 1)], buf.at[k, pl.ds(t, 1)], sem)

    def issue(t, _):
        for k in range(TOP_K):
            row_copy(t, k).start()
        return 0

    def drain(t, _):
        for k in range(TOP_K):
            row_copy(t, k).wait()
        return 0

    lax.fori_loop(0, tm, issue, 0)

    h = h_ref[...]
    g = jnp.dot(h, wg[...], preferred_element_type=F32)
    u = jnp.dot(h, wu[...], preferred_element_type=F32)
    shared = jnp.dot((g * jax.nn.sigmoid(g) * u).astype(BF16), wd[...], preferred_element_type=F32)

    lax.fori_loop(0, tm, drain, 0)

    half = buf.shape[2]
    wt = wt_ref[...]
    ya = shared[:, :half]
    yb = shared[:, half:]
    for k in range(TOP_K):
        a, b = _unpack_halves(buf[k])
        ya = ya + wt[:, k:k + 1] * a
        yb = yb + wt[:, k:k + 1] * b
    y = jnp.concatenate([ya, yb], axis=1)
    gate2 = mod_ref[0, 5:6, :]
    o_ref[...] = x1_ref[...] + gate2 * (_rms(y) * gpost[...])


def _combine(ys, dest, wt, h2, x1, mod3, g_post, wg_b, wu_b, wd_b, seq):
    n, d = h2.shape
    tm = 256
    w = d // 2
    per_b = seq // tm
    const = lambda shape: pl.BlockSpec(shape, lambda i, *_: (0,) * len(shape), pipeline_mode=pl.Buffered(1))
    grid_spec = pltpu.PrefetchScalarGridSpec(
        num_scalar_prefetch=0,
        grid=(n // tm,),
        in_specs=[
            pl.BlockSpec((TOP_K, tm), lambda i: (0, i), memory_space=pltpu.SMEM),
            pl.BlockSpec(memory_space=pl.ANY),
            pl.BlockSpec((tm, LANES), lambda i: (i, 0)),
            pl.BlockSpec((tm, d), lambda i: (i, 0)),
            pl.BlockSpec((tm, d), lambda i: (i, 0)),
            pl.BlockSpec((1, N_MOD, d), lambda i: (i // per_b, 0, 0)),
            const((1, d)), const(wg_b.shape), const(wu_b.shape), const(wd_b.shape),
        ],
        out_specs=pl.BlockSpec((tm, d), lambda i: (i, 0)),
        scratch_shapes=[pltpu.VMEM((TOP_K, tm, w), U32), pltpu.SemaphoreType.DMA(())],
    )
    return pl.pallas_call(
        _combine_body,
        grid_spec=grid_spec,
        out_shape=jax.ShapeDtypeStruct((n, d), F32),
        compiler_params=_cparams(("arbitrary",), disable_bounds_checks=True),
        name="combine",
    )(dest, ys, wt, h2, x1, mod3, g_post.reshape(1, d), wg_b, wu_b, wd_b)


def _layer(x, mod, positions, g_pre_mix, g_post_mix, g_pre_ffn, g_post_ffn, w_in, w_dil_out, w_sb_out, w_mix_out,
           w_router, router_bias, w_gate_e, w_up_e, w_down_e, w_gate_s, w_up_s, w_down_s):
    bsz, seq, d = x.shape
    n = bsz * seq
    xf = x.reshape(n, d)
    mod3 = mod.reshape(bsz, N_MOD, d)

    nq = 3 * (WIDTH_DIL + WIDTH_SB)
    w_in_b = jnp.concatenate([w_in[:, nq:], w_in[:, :nq]], axis=1).astype(BF16)
    proj = _inproj(xf, g_pre_mix, mod3, w_in_b, seq)

    tables = _rope_tables(positions)
    o_dil, lse_dil = [], []
    for g, (window, dilation) in enumerate(DIL_PATTERNS):
        assert window // dilation == Q_BLOCK
        o, lse = _dilated_attention(proj, tables, g, dilation, bsz, seq)
        o_dil.append(o)
        lse_dil.append(lse)
    o_sb = _stick_breaking(proj, bsz, seq)

    x1, h2 = _mixout(o_dil, lse_dil, o_sb, proj, xf, mod3, g_post_mix, g_pre_ffn,
                     w_dil_out.astype(BF16), w_sb_out.astype(BF16), w_mix_out.astype(BF16), seq)

    top_idx, top_wt = _router(h2, w_router, router_bias)
    counts = _plan_counts(top_idx)[:, 0].astype(I32)
    padded = (counts + ROW_BLOCK - 1) // ROW_BLOCK * ROW_BLOCK
    pends = jnp.cumsum(padded)
    pstart = pends - padded
    nblk = _capacity(n) // ROW_BLOCK
    n_used = (pends[-1] // ROW_BLOCK).astype(I32)
    blk = jnp.minimum(jnp.arange(nblk, dtype=I32), n_used - 1)
    block_e = jnp.minimum(jnp.searchsorted(pends, blk * ROW_BLOCK, side="right"), N_EXPERTS - 1).astype(I32)
    dest = _plan_dest(top_idx, pstart)

    xs = _dispatch(h2, dest, pstart.astype(I32), counts)
    ys = _experts(xs, block_e, n_used.reshape(1), w_gate_e, w_up_e, w_down_e)
    out = _combine(ys, dest, top_wt, h2, x1, mod3, g_post_ffn,
                   w_gate_s.astype(BF16), w_up_s.astype(BF16), w_down_s.astype(BF16), seq)
    return out.reshape(bsz, seq, d)


def kernel(x, c, positions, w_ada, b_ada, g_pre_mix, g_post_mix, g_pre_ffn, g_post_ffn, w_in, w_dil_out,
           w_sb_out, w_mix_out, w_router, router_bias, w_gate_e, w_up_e, w_down_e, w_gate_s, w_up_s, w_down_s):
    for l in range(w_ada.shape[0]):
        mod = _adaln(c, w_ada[l], b_ada[l])
        x = _layer(x, mod, positions, g_pre_mix[l], g_post_mix[l], g_pre_ffn[l], g_post_ffn[l], w_in[l],
                   w_dil_out[l], w_sb_out[l], w_mix_out[l], w_router[l], router_bias[l],
                   w_gate_e[l], w_up_e[l], w_down_e[l], w_gate_s[l], w_up_s[l], w_down_s[l])
    return x
```

```python
import functools

import jax
import jax.numpy as jnp
from jax import lax
from jax.experimental import pallas as pl
from jax.experimental.pallas import tpu as pltpu

F32 = jnp.float32
BF16 = jnp.bfloat16
I32 = jnp.int32
U32 = jnp.uint32

D_MODEL = 2048
HEAD_DIM = 128
DIL_PATTERNS = ((128, 1), (512, 4), (2048, 16))
HEADS_PER_GROUP = 4
N_HEADS_DIL = 12
N_HEADS_SB = 8
WIDTH_DIL = N_HEADS_DIL * HEAD_DIM
WIDTH_SB = N_HEADS_SB * HEAD_DIM
D_DIL_OUT = HEADS_PER_GROUP * HEAD_DIM
Q_BLOCK = 128
ROPE_THETA = 500000.0
ROPE_DIM = HEAD_DIM // 4
N_GATE = 2 * D_MODEL
N_EXPERTS = 64
TOP_K = 8
N_GROUPS = 8
GROUP_SIZE = N_EXPERTS // N_GROUPS
TOPK_GROUPS = 4
D_EXPERT = 512
D_SHARED = 512
ROUTED_SCALE = 2.5
RMS_EPS = 1e-6
N_MOD = 6
ATTN_SCALE = HEAD_DIM ** -0.5

LANES = 128
SUBLANES = 8
VMEM_LIMIT = 56 * 1024 * 1024

ROW_BLOCK = 256
SB_DEAD = -110.0


def _cparams(sem, **kw):
    return pltpu.CompilerParams(dimension_semantics=sem, vmem_limit_bytes=VMEM_LIMIT, **kw)


def _adaln_body(ct_ref, w_ref, b_ref, o_ref, *, kc):
    nb = ct_ref.shape[1]
    nk = w_ref.shape[0] // kc

    def step(i, acc):
        k0 = pl.multiple_of(i * kc, kc)
        w = w_ref[pl.ds(k0, kc), :]
        c = ct_ref[pl.ds(k0, kc), :]
        s = c * jax.nn.sigmoid(c)
        parts = [jnp.sum(w * s[:, b:b + 1], axis=0, keepdims=True) for b in range(nb)]
        return acc + jnp.concatenate(parts, axis=0)

    acc = lax.fori_loop(0, nk, step, jnp.zeros(o_ref.shape, F32))
    o_ref[...] = acc + b_ref[...]


def _adaln(c, w_ada, b_ada):
    nb, d = c.shape
    n_out = w_ada.shape[1]
    tn = 1024
    return pl.pallas_call(
        functools.partial(_adaln_body, kc=256),
        grid=(n_out // tn,),
        in_specs=[
            pl.BlockSpec((d, nb), lambda j: (0, 0)),
            pl.BlockSpec((d, tn), lambda j: (0, j)),
            pl.BlockSpec((1, tn), lambda j: (0, j)),
        ],
        out_specs=pl.BlockSpec((nb, tn), lambda j: (0, j)),
        out_shape=jax.ShapeDtypeStruct((nb, n_out), F32),
        compiler_params=_cparams(("arbitrary",)),
        name="adaln",
    )(c.T, w_ada, b_ada.reshape(1, n_out))


def _rms(x):
    return x * lax.rsqrt(jnp.mean(x * x, axis=-1, keepdims=True) + RMS_EPS)


def _prenorm(x_ref, g_ref, mod_ref):
    y = _rms(x_ref[...]) * g_ref[...]
    shift = mod_ref[0, 0:1, :]
    scale = mod_ref[0, 1:2, :]
    return (y * (1.0 + scale) + shift).astype(BF16)


def _inproj_body(x_ref, g_ref, mod_ref, w_ref, o_ref, h_ref):
    @pl.when(pl.program_id(1) == 0)
    def _():
        h_ref[...] = _prenorm(x_ref, g_ref, mod_ref)

    o_ref[...] = jnp.dot(h_ref[...], w_ref[...], preferred_element_type=F32).astype(o_ref.dtype)


def _inproj(xf, g_pre, mod3, w_b, seq):
    n, d = xf.shape
    width = w_b.shape[1]
    tm, tn = 512, width // 4
    per_b = seq // tm
    return pl.pallas_call(
        _inproj_body,
        grid=(n // tm, width // tn),
        in_specs=[
            pl.BlockSpec((tm, d), lambda i, j: (i, 0)),
            pl.BlockSpec((1, d), lambda i, j: (0, 0)),
            pl.BlockSpec((1, N_MOD, d), lambda i, j: (i // per_b, 0, 0)),
            pl.BlockSpec((d, tn), lambda i, j: (0, j)),
        ],
        out_specs=pl.BlockSpec((tm, tn), lambda i, j: (i, j)),
        out_shape=jax.ShapeDtypeStruct((n, width), BF16),
        scratch_shapes=[pltpu.VMEM((tm, d), BF16)],
        compiler_params=_cparams(("arbitrary", "arbitrary")),
        name="inproj",
    )(xf, g_pre.reshape(1, d), mod3, w_b)


def _inproj_dil_body(x_ref, g_ref, mod_ref, w_ref, t_ref, o0, o1, o2, res_ref):
    tm = x_ref.shape[0]
    h = _prenorm(x_ref, g_ref, mod_ref)
    t = t_ref[...]
    gw = 3 * D_DIL_OUT
    for gi, o_ref in enumerate((o0, o1, o2)):
        dil = DIL_PATTERNS[gi][1]
        res = jnp.dot(h, w_ref[:, gi * gw:(gi + 1) * gw], preferred_element_type=F32)
        for hs in range(3 * HEADS_PER_GROUP):
            sl = slice(hs * HEAD_DIM, (hs + 1) * HEAD_DIM)
            res_ref[hs] = _apply_rope(res[:, sl], t) if hs < 2 * HEADS_PER_GROUP else res[:, sl]
        for r in range(dil):
            for hs in range(3 * HEADS_PER_GROUP):
                rows = res_ref[hs] if dil == 1 else res_ref[hs, pl.ds(r, tm // dil, stride=dil), :]
                o_ref[0, r, :, hs * HEAD_DIM:(hs + 1) * HEAD_DIM] = rows.astype(o_ref.dtype)


def _inproj_dil(xf, g_pre, mod3, w_b, tables, bsz, seq):
    n, d = xf.shape
    gw = 3 * D_DIL_OUT
    tm = 512
    per_b = seq // tm
    dils = [p[1] for p in DIL_PATTERNS]
    return pl.pallas_call(
        _inproj_dil_body,
        grid=(n // tm,),
        in_specs=[
            pl.BlockSpec((tm, d), lambda i: (i, 0)),
            pl.BlockSpec((1, d), lambda i: (0, 0)),
            pl.BlockSpec((1, N_MOD, d), lambda i: (i // per_b, 0, 0)),
            pl.BlockSpec(w_b.shape, lambda i: (0, 0), pipeline_mode=pl.Buffered(1)),
            pl.BlockSpec((tm, 3 * LANES), lambda i: (i, 0)),
        ],
        out_specs=[pl.BlockSpec((1, dl, tm // dl, gw), lambda i: (i // per_b, 0, i % per_b, 0)) for dl in dils],
        out_shape=[jax.ShapeDtypeStruct((bsz, dl, seq // dl, gw), BF16) for dl in dils],
        scratch_shapes=[pltpu.VMEM((gw // HEAD_DIM, tm, HEAD_DIM), F32)],
        compiler_params=_cparams(("arbitrary",)),
        name="inproj_dil",
    )(xf, g_pre.reshape(1, d), mod3, w_b, tables)


def _rope_body(pos_ref, f_ref, o_ref):
    ang = pos_ref[...].astype(F32) * f_ref[...]
    c = jnp.cos(ang)
    s = jnp.sin(ang)
    lane = lax.broadcasted_iota(I32, ang.shape, 1)
    half = ROPE_DIM // 2
    o_ref[:, 0:LANES] = c
    o_ref[:, LANES:2 * LANES] = jnp.where(lane >= half, s, 0.0)
    o_ref[:, 2 * LANES:3 * LANES] = jnp.where(lane < half, -s, 0.0)


def _rope_tables(positions):
    n = positions.size
    half = ROPE_DIM // 2
    inv_freq = ROPE_THETA ** (-jnp.arange(0, ROPE_DIM, 2, dtype=F32) / ROPE_DIM)
    f = jnp.concatenate([inv_freq, inv_freq, jnp.zeros((LANES - 2 * half,), F32)]).reshape(1, LANES)
    tm = 2048
    return pl.pallas_call(
        _rope_body,
        grid=(n // tm,),
        in_specs=[pl.BlockSpec((tm, 1), lambda i: (i, 0)), pl.BlockSpec((1, LANES), lambda i: (0, 0))],
        out_specs=pl.BlockSpec((tm, 3 * LANES), lambda i: (i, 0)),
        out_shape=jax.ShapeDtypeStruct((n, 3 * LANES), F32),
        compiler_params=_cparams(("arbitrary",)),
        name="rope_tables",
    )(positions.reshape(n, 1), f)


def _apply_rope(x, t):
    half = ROPE_DIM // 2
    return (x * t[:, 0:LANES]
            + pltpu.roll(x, half, 1) * t[:, LANES:2 * LANES]
            + pltpu.roll(x, LANES - half, 1) * t[:, 2 * LANES:3 * LANES])


def _dil_body(cur_ref, kp_ref, vp_ref, o_ref, lse_ref, obuf, lbuf, *, dil, nsub):
    n = pl.program_id(1)
    tq = nsub * Q_BLOCK
    row = lax.broadcasted_iota(I32, (Q_BLOCK, 2 * Q_BLOCK), 0)
    col = lax.broadcasted_iota(I32, (Q_BLOCK, 2 * Q_BLOCK), 1)
    rel = row + Q_BLOCK - col
    band = jnp.where(rel >= 0, jnp.where(rel <= Q_BLOCK, 1.0, 0.0), 0.0)
    first = jnp.where(col >= Q_BLOCK, band, jnp.where(n > 0, band, 0.0))
    for r in range(dil):
        for h in range(HEADS_PER_GROUP):
            sl = slice(h * HEAD_DIM, (h + 1) * HEAD_DIM)
            ksl = slice(D_DIL_OUT + h * HEAD_DIM, D_DIL_OUT + (h + 1) * HEAD_DIM)
            vsl = slice(2 * D_DIL_OUT + h * HEAD_DIM, 2 * D_DIL_OUT + (h + 1) * HEAD_DIM)
            for j in range(nsub):
                rs = slice(j * Q_BLOCK, (j + 1) * Q_BLOCK)
                ps = slice((j - 1) * Q_BLOCK, j * Q_BLOCK)
                kprev = kp_ref[0, r, :, sl] if j == 0 else cur_ref[0, r, ps, ksl]
                vprev = vp_ref[0, r, :, sl] if j == 0 else cur_ref[0, r, ps, vsl]
                kcat = jnp.concatenate([kprev, cur_ref[0, r, rs, ksl]], axis=0)
                vcat = jnp.concatenate([vprev, cur_ref[0, r, rs, vsl]], axis=0)
                s = lax.dot_general(cur_ref[0, r, rs, sl], kcat, (((1,), (1,)), ((), ())),
                                    preferred_element_type=F32) * ATTN_SCALE
                s = jnp.where((first if j == 0 else band) > 0.0, s, -jnp.inf)
                m = jnp.max(s, axis=-1, keepdims=True)
                p = jnp.exp(s - m)
                l = jnp.sum(p, axis=-1, keepdims=True)
                o = jnp.dot((p / l).astype(BF16), vcat, preferred_element_type=F32)
                lse = jnp.broadcast_to(m + jnp.log(l), (Q_BLOCK, HEAD_DIM))
                if dil == 1:
                    o_ref[h, rs, :] = o
                    lse_ref[h, rs, :] = lse
                else:
                    obuf[h, rs, :] = o
                    lbuf[h, rs, :] = lse
            if dil > 1:
                o_ref[h, pl.ds(r, tq, stride=dil), :] = obuf[h]
                lse_ref[h, pl.ds(r, tq, stride=dil), :] = lbuf[h]


def _dilated_attention(qkv, g, bsz, seq):
    dil = DIL_PATTERNS[g][1]
    length = seq // dil
    tq = min(4 * Q_BLOCK, (16 * Q_BLOCK) // dil, length)
    nsub = tq // Q_BLOCK
    nq = length // tq
    gw = 3 * D_DIL_OUT
    n = bsz * seq

    def prev(colblk):
        return pl.BlockSpec((1, dil, Q_BLOCK, D_DIL_OUT),
                            lambda b, i: (b, 0, jnp.maximum(i * nsub - 1, 0), colblk))

    nh = HEADS_PER_GROUP
    out_spec = pl.BlockSpec((nh, tq * dil, HEAD_DIM), lambda b, i: (0, b * nq + i, 0))
    out_shape = jax.ShapeDtypeStruct((nh, n, HEAD_DIM), F32)
    return pl.pallas_call(
        functools.partial(_dil_body, dil=dil, nsub=nsub),
        grid=(bsz, nq),
        in_specs=[pl.BlockSpec((1, dil, tq, gw), lambda b, i: (b, 0, i, 0)), prev(1), prev(2)],
        out_specs=[out_spec, out_spec],
        out_shape=[out_shape, out_shape],
        scratch_shapes=[pltpu.VMEM((nh, tq, HEAD_DIM), F32), pltpu.VMEM((nh, tq, HEAD_DIM), F32)],
        compiler_params=_cparams(("arbitrary", "arbitrary")),
        name=f"dilated_d{dil}",
    )(qkv, qkv, qkv)


def _sb_body(q_ref, k_ref, v_ref, o_ref, acc_ref, car_ref):
    nblk = q_ref.shape[1] // Q_BLOCK
    r = lax.broadcasted_iota(I32, (Q_BLOCK, Q_BLOCK), 0)
    c = lax.broadcasted_iota(I32, (Q_BLOCK, Q_BLOCK), 1)
    causal = c < r
    rr = lax.broadcasted_iota(I32, (Q_BLOCK, 2 * Q_BLOCK), 0)
    cc = lax.broadcasted_iota(I32, (Q_BLOCK, 2 * Q_BLOCK), 1)
    uo = jnp.where(cc >= Q_BLOCK, 1.0, jnp.where(rr > cc, 1.0, 0.0)).astype(BF16)

    def tile(qj, kb, carry, diag, h):
        k0 = pl.multiple_of(kb * Q_BLOCK, Q_BLOCK)
        kt = k_ref[0, pl.ds(k0, Q_BLOCK), h * HEAD_DIM:(h + 1) * HEAD_DIM]
        vt = v_ref[0, pl.ds(k0, Q_BLOCK), h * HEAD_DIM:(h + 1) * HEAD_DIM]
        z = lax.dot_general(qj, kt, (((1,), (1,)), ((), ())), preferred_element_type=F32) * ATTN_SCALE
        sp = jnp.log(1.0 + jnp.exp(-jnp.abs(z)))
        mx = jnp.maximum(z, 0.0)
        log_1m = -(mx + sp)
        log_s = (z - mx) - sp
        if diag:
            log_1m = jnp.where(causal, log_1m, 0.0)
        hi = log_1m.astype(BF16)
        lo = (log_1m - hi.astype(F32)).astype(BF16)
        r2 = jnp.dot(jnp.concatenate([hi, lo], axis=0), uo, preferred_element_type=F32)
        sums = r2[:Q_BLOCK] + r2[Q_BLOCK:]
        a = jnp.exp(log_s + carry + sums[:, :Q_BLOCK])
        if diag:
            a = jnp.where(causal, a, 0.0)
        pv = jnp.dot(a.astype(BF16), vt, preferred_element_type=F32)
        return pv, carry + sums[:, Q_BLOCK:]

    nh = acc_ref.shape[0]

    def qblock(qi, _):
        q0 = pl.multiple_of(qi * Q_BLOCK, Q_BLOCK)
        qs = [q_ref[0, pl.ds(q0, Q_BLOCK), h * HEAD_DIM:(h + 1) * HEAD_DIM] for h in range(nh)]
        live = None
        for h in range(nh):
            pv, car = tile(qs[h], qi, jnp.zeros((Q_BLOCK, Q_BLOCK), F32), True, h)
            acc_ref[h] = pv
            car_ref[h] = car
            live = car if live is None else jnp.maximum(live, car)

        def cond(st):
            return jnp.logical_and(st[0] >= 0, st[1] > SB_DEAD)

        def body(st):
            live = None
            for h in range(nh):
                pv, car = tile(qs[h], st[0], car_ref[h], False, h)
                acc_ref[h] += pv
                car_ref[h] = car
                live = car if live is None else jnp.maximum(live, car)
            return st[0] - 1, jnp.max(live)

        lax.while_loop(cond, body, (qi - 1, jnp.max(live)))
        for h in range(nh):
            o_ref[0, pl.ds(q0, Q_BLOCK), h * HEAD_DIM:(h + 1) * HEAD_DIM] = acc_ref[h].astype(o_ref.dtype)
        return 0

    lax.fori_loop(0, nblk, qblock, 0)


SB_HEADS_PER_STEP = 4


def _stick_breaking(proj, bsz, seq):
    width = proj.shape[1]
    pv = proj.reshape(bsz, seq, width)
    nh = SB_HEADS_PER_STEP
    bw = nh * HEAD_DIM
    base = N_GATE // bw

    def spec(off):
        return pl.BlockSpec((1, seq, bw), lambda b, h: (b, 0, base + off + h))

    nstep = N_HEADS_SB // nh
    o = pl.pallas_call(
        _sb_body,
        grid=(bsz, nstep),
        in_specs=[spec(0), spec(nstep), spec(2 * nstep)],
        out_specs=pl.BlockSpec((1, seq, bw), lambda b, h: (b, 0, h)),
        out_shape=jax.ShapeDtypeStruct((bsz, seq, WIDTH_SB), BF16),
        scratch_shapes=[pltpu.VMEM((nh, Q_BLOCK, Q_BLOCK), F32), pltpu.VMEM((nh, Q_BLOCK, Q_BLOCK), F32)],
        compiler_params=_cparams(("arbitrary", "arbitrary")),
        name="stick_breaking",
    )(pv, pv, pv)
    return o.reshape(bsz * seq, WIDTH_SB)


def _mixout_body(o1, o2, o3, l1, l2, l3, osb, gd_ref, gs_ref, x_ref, mod_ref, gpost, gpre,
                 wd, ws, wm, x1_ref, h2_ref):
    heads = []
    for h in range(HEADS_PER_GROUP):
        la, lb, lc = l1[h], l2[h], l3[h]
        m = jnp.maximum(la, jnp.maximum(lb, lc))
        ea, eb, ec = jnp.exp(la - m), jnp.exp(lb - m), jnp.exp(lc - m)
        heads.append(((ea * o1[h] + eb * o2[h] + ec * o3[h]) / (ea + eb + ec)).astype(BF16))
    yd = jnp.dot(jnp.concatenate(heads, axis=1), wd[...], preferred_element_type=F32)
    ys = jnp.dot(osb[...], ws[...], preferred_element_type=F32)
    mix = jax.nn.sigmoid(gd_ref[...].astype(F32)) * yd + jax.nn.sigmoid(gs_ref[...].astype(F32)) * ys
    y = jnp.dot(mix.astype(BF16), wm[...], preferred_element_type=F32)
    gate1 = mod_ref[0, 2:3, :]
    shift2 = mod_ref[0, 3:4, :]
    scale2 = mod_ref[0, 4:5, :]
    x1 = x_ref[...] + gate1 * (_rms(y) * gpost[...])
    x1_ref[...] = x1
    h2_ref[...] = ((_rms(x1) * gpre[...]) * (1.0 + scale2) + shift2).astype(h2_ref.dtype)


def _const_spec(shape):
    return pl.BlockSpec(shape, lambda i: (0,) * len(shape), pipeline_mode=pl.Buffered(1))


def _mixout(o_dil, lse_dil, o_sb, proj, xf, mod3, g_post, g_pre, wd_b, ws_b, wm_b, seq):
    n, d = xf.shape
    tm = 256
    per_b = seq // tm
    row = lambda w: pl.BlockSpec((tm, w), lambda i: (i, 0))
    head_major = pl.BlockSpec((HEADS_PER_GROUP, tm, HEAD_DIM), lambda i: (0, i, 0))
    in_specs = (
        [head_major] * 6 + [row(WIDTH_SB)]
        + [pl.BlockSpec((tm, d), lambda i: (i, 0)), pl.BlockSpec((tm, d), lambda i: (i, 1))]
        + [row(d), pl.BlockSpec((1, N_MOD, d), lambda i: (i // per_b, 0, 0))]
        + [_const_spec((1, d)), _const_spec((1, d))]
        + [_const_spec(wd_b.shape), _const_spec(ws_b.shape), _const_spec(wm_b.shape)]
    )
    return pl.pallas_call(
        _mixout_body,
        grid=(n // tm,),
        in_specs=in_specs,
        out_specs=[row(d), row(d)],
        out_shape=[jax.ShapeDtypeStruct((n, d), F32), jax.ShapeDtypeStruct((n, d), BF16)],
        compiler_params=_cparams(("arbitrary",)),
        name="mixout",
    )(*o_dil, *lse_dil, o_sb, proj, proj, xf, mod3, g_post.reshape(1, d), g_pre.reshape(1, d), wd_b, ws_b, wm_b)


def _topk_rows(x, k, iota0):
    big = x.shape[0]
    out = []
    for _ in range(k):
        m = jnp.max(x, axis=0, keepdims=True)
        i = jnp.min(jnp.where(x == m, iota0, big), axis=0, keepdims=True)
        out.append((m, i))
        x = jnp.where(iota0 == i, -jnp.inf, x)
    return out


def _router_body(h_ref, wr_ref, bias_ref, idx_ref, wt_ref):
    tm = h_ref.shape[0]
    logits = lax.dot_general(wr_ref[...], h_ref[...], (((1,), (1,)), ((), ())), preferred_element_type=F32)
    scores = jax.nn.sigmoid(logits)
    sel = scores + bias_ref[...]
    sub = lax.broadcasted_iota(I32, (GROUP_SIZE, tm), 0)
    grp = []
    for g in range(N_GROUPS):
        (m1, _), (m2, _) = _topk_rows(sel[g * GROUP_SIZE:(g + 1) * GROUP_SIZE], 2, sub)
        grp.append(m1 + m2)
    gscore = jnp.concatenate(grp, axis=0)
    giota = lax.broadcasted_iota(I32, (N_GROUPS, tm), 0)
    gmask = jnp.zeros((N_GROUPS, tm), F32)
    for _, gi in _topk_rows(gscore, TOPK_GROUPS, giota):
        gmask = jnp.where(giota == gi, 1.0, gmask)
    masked = jnp.concatenate(
        [jnp.where(gmask[g:g + 1] > 0.0, sel[g * GROUP_SIZE:(g + 1) * GROUP_SIZE], -jnp.inf)
         for g in range(N_GROUPS)], axis=0)
    eiota = lax.broadcasted_iota(I32, (N_EXPERTS, tm), 0)
    picks = _topk_rows(masked, TOP_K, eiota)
    idx = jnp.concatenate([i for _, i in picks], axis=0)
    top_s = jnp.concatenate(
        [jnp.sum(jnp.where(eiota == i, scores, 0.0), axis=0, keepdims=True) for _, i in picks], axis=0)
    top_w = top_s / jnp.sum(top_s, axis=0, keepdims=True) * ROUTED_SCALE
    idx_ref[...] = idx
    wpad = jnp.concatenate([top_w, jnp.zeros((LANES - TOP_K, tm), F32)], axis=0)
    wt_ref[...] = wpad.T


def _router(h2, w_router, router_bias):
    n, d = h2.shape
    tm = 512
    return pl.pallas_call(
        _router_body,
        grid=(n // tm,),
        in_specs=[
            pl.BlockSpec((tm, d), lambda i: (i, 0)),
            pl.BlockSpec((N_EXPERTS, d), lambda i: (0, 0)),
            pl.BlockSpec((N_EXPERTS, 1), lambda i: (0, 0)),
        ],
        out_specs=[pl.BlockSpec((TOP_K, tm), lambda i: (0, i)), pl.BlockSpec((tm, LANES), lambda i: (i, 0))],
        out_shape=[jax.ShapeDtypeStruct((TOP_K, n), I32), jax.ShapeDtypeStruct((n, LANES), F32)],
        compiler_params=_cparams(("arbitrary",)),
        name="router",
    )(h2, w_router.T.astype(BF16), router_bias.reshape(N_EXPERTS, 1))


def _plan_tile(idx):
    tm = idx.shape[1]
    eiota = lax.broadcasted_iota(I32, (N_EXPERTS, tm), 0)
    hit = jnp.zeros((N_EXPERTS, tm), F32)
    for k in range(TOP_K):
        hit = jnp.where(eiota == idx[k:k + 1], 1.0, hit)
    r = lax.broadcasted_iota(I32, (tm, tm), 0)
    c = lax.broadcasted_iota(I32, (tm, tm), 1)
    before = jnp.where(r < c, 1.0, 0.0).astype(BF16)
    excl = jnp.dot(hit.astype(BF16), before, preferred_element_type=F32)
    tot = excl[:, tm - 1:tm] + hit[:, tm - 1:tm]
    return eiota, hit, excl, tot


def _plan_counts_body(idx_ref, cnt_ref):
    @pl.when(pl.program_id(0) == 0)
    def _():
        cnt_ref[...] = jnp.zeros(cnt_ref.shape, F32)

    _, _, _, tot = _plan_tile(idx_ref[...])
    cnt_ref[...] += tot


def _plan_counts(top_idx):
    n = top_idx.shape[1]
    tm = 512
    return pl.pallas_call(
        _plan_counts_body,
        grid=(n // tm,),
        in_specs=[pl.BlockSpec((TOP_K, tm), lambda i: (0, i))],
        out_specs=pl.BlockSpec((N_EXPERTS, LANES), lambda i: (0, 0)),
        out_shape=jax.ShapeDtypeStruct((N_EXPERTS, LANES), F32),
        compiler_params=_cparams(("arbitrary",)),
        name="plan_counts",
    )(top_idx)


def _plan_dest_body(idx_ref, base_ref, dest_ref, run_ref):
    @pl.when(pl.program_id(0) == 0)
    def _():
        run_ref[...] = base_ref[...]

    idx = idx_ref[...]
    eiota, _, excl, tot = _plan_tile(idx)
    pos = run_ref[:, 0:1] + excl
    rows = [jnp.sum(jnp.where(eiota == idx[k:k + 1], pos, 0.0), axis=0, keepdims=True) for k in range(TOP_K)]
    dest_ref[...] = jnp.concatenate(rows, axis=0).astype(I32)
    run_ref[...] += tot


def _plan_dest(top_idx, pstart):
    n = top_idx.shape[1]
    tm = 512
    base = jnp.broadcast_to(pstart.astype(F32).reshape(N_EXPERTS, 1), (N_EXPERTS, LANES))
    return pl.pallas_call(
        _plan_dest_body,
        grid=(n // tm,),
        in_specs=[pl.BlockSpec((TOP_K, tm), lambda i: (0, i)), pl.BlockSpec((N_EXPERTS, LANES), lambda i: (0, 0))],
        out_specs=pl.BlockSpec((TOP_K, tm), lambda i: (0, i)),
        out_shape=jax.ShapeDtypeStruct((TOP_K, n), I32),
        scratch_shapes=[pltpu.VMEM((N_EXPERTS, LANES), F32)],
        compiler_params=_cparams(("arbitrary",)),
        name="plan_dest",
    )(top_idx, base)


def _capacity(n):
    return n * TOP_K + N_EXPERTS * ROW_BLOCK


def _pack_halves(y):
    w = y.shape[1] // 2
    hi = pltpu.bitcast(y[:, :w].astype(BF16).astype(F32), U32)
    lo = pltpu.bitcast(y[:, w:].astype(BF16).astype(F32), U32)
    return hi | (lo >> 16)


def _unpack_halves(p):
    a = pltpu.bitcast(p & jnp.uint32(0xFFFF0000), F32)
    b = pltpu.bitcast(p << 16, F32)
    return a, b


def _pad_chunks():
    sizes, s = [], ROW_BLOCK // 2
    while s >= SUBLANES:
        sizes.append(s)
        s //= 2
    return sizes


def _dispatch_body(pstart_ref, cnt_ref, dest_ref, h_ref, xs_ref, xbuf, zbuf, sem, zsem):
    tm = h_ref.shape[0]
    step = pl.program_id(0)

    def pad_dmas(wait):
        def per_expert(e, _):
            cnt = cnt_ref[e]
            pad = (ROW_BLOCK - cnt % ROW_BLOCK) % ROW_BLOCK
            start = pstart_ref[e] + cnt
            head = (SUBLANES - start % SUBLANES) % SUBLANES
            for r in range(SUBLANES - 1):
                @pl.when(r < head)
                def _():
                    cp = pltpu.make_async_copy(zbuf.at[pl.ds(0, 1)], xs_ref.at[pl.ds(start + r, 1)], zsem)
                    cp.wait() if wait else cp.start()
            body = start + head
            rem = pad - head
            for size in _pad_chunks():
                @pl.when((rem & size) != 0)
                def _():
                    off = pl.multiple_of(body + (rem & ~(2 * size - 1)), SUBLANES)
                    cp = pltpu.make_async_copy(zbuf.at[pl.ds(0, size)], xs_ref.at[pl.ds(off, size)], zsem)
                    cp.wait() if wait else cp.start()
            return 0
        lax.fori_loop(0, N_EXPERTS, per_expert, 0)

    @pl.when(step == 0)
    def _():
        zbuf[...] = jnp.zeros(zbuf.shape, zbuf.dtype)
        pad_dmas(False)
        pad_dmas(True)

    xbuf[...] = _pack_halves(h_ref[...].astype(F32))

    def row_copy(t, k):
        return pltpu.make_async_copy(xbuf.at[pl.ds(t, 1)], xs_ref.at[pl.ds(dest_ref[k, t], 1)], sem)

    def issue(t, _):
        for k in range(TOP_K):
            row_copy(t, k).start()
        return 0

    def drain(t, _):
        for k in range(TOP_K):
            row_copy(t, k).wait()
        return 0

    lax.fori_loop(0, tm, issue, 0)
    lax.fori_loop(0, tm, drain, 0)


def _dispatch(h2, dest, pstart, counts):
    n, d = h2.shape
    tm = 256
    w = d // 2
    grid_spec = pltpu.PrefetchScalarGridSpec(
        num_scalar_prefetch=2,
        grid=(n // tm,),
        in_specs=[
            pl.BlockSpec((TOP_K, tm), lambda i, *_: (0, i), memory_space=pltpu.SMEM),
            pl.BlockSpec((tm, d), lambda i, *_: (i, 0)),
        ],
        out_specs=pl.BlockSpec(memory_space=pl.ANY),
        scratch_shapes=[
            pltpu.VMEM((tm, w), U32),
            pltpu.VMEM((ROW_BLOCK // 2, w), U32),
            pltpu.SemaphoreType.DMA(()),
            pltpu.SemaphoreType.DMA(()),
        ],
    )
    return pl.pallas_call(
        _dispatch_body,
        grid_spec=grid_spec,
        out_shape=jax.ShapeDtypeStruct((_capacity(n), w), U32),
        compiler_params=_cparams(("arbitrary",), has_side_effects=True, disable_bounds_checks=True),
        name="dispatch",
    )(pstart, counts, dest, h2)


def _experts_body(be_ref, nu_ref, x_ref, wg_ref, wu_ref, wd_ref, y_ref, wg_b, wu_b, wd_b):
    i = pl.program_id(0)
    used = i < nu_ref[0]
    prev = be_ref[jnp.maximum(i - 1, 0)]
    fresh = jnp.logical_or(i == 0, be_ref[i] != prev)

    @pl.when(jnp.logical_and(used, fresh))
    def _():
        wg_b[...] = wg_ref[0].astype(BF16)
        wu_b[...] = wu_ref[0].astype(BF16)
        wd_b[...] = wd_ref[0].astype(BF16)

    @pl.when(used)
    def _():
        half = x_ref.shape[1]
        xa, xb = _unpack_halves(x_ref[...])
        xa = xa.astype(BF16)
        xb = xb.astype(BF16)
        g = (jnp.dot(xa, wg_b[:half], preferred_element_type=F32)
             + jnp.dot(xb, wg_b[half:], preferred_element_type=F32))
        u = (jnp.dot(xa, wu_b[:half], preferred_element_type=F32)
             + jnp.dot(xb, wu_b[half:], preferred_element_type=F32))
        hmid = (g * jax.nn.sigmoid(g) * u).astype(BF16)
        y_ref[...] = _pack_halves(jnp.dot(hmid, wd_b[...], preferred_element_type=F32))


def _experts(xs, block_e, n_used, w_gate_e, w_up_e, w_down_e):
    cap, w = xs.shape
    d = 2 * w
    nblk = cap // ROW_BLOCK

    def row_map(i, be, nu):
        return (jnp.minimum(i, nu[0] - 1), 0)

    def w_map(i, be, nu):
        return (be[i], 0, 0)

    grid_spec = pltpu.PrefetchScalarGridSpec(
        num_scalar_prefetch=2,
        grid=(nblk,),
        in_specs=[
            pl.BlockSpec((ROW_BLOCK, w), row_map),
            pl.BlockSpec((1, d, D_EXPERT), w_map),
            pl.BlockSpec((1, d, D_EXPERT), w_map),
            pl.BlockSpec((1, D_EXPERT, d), w_map),
        ],
        out_specs=pl.BlockSpec((ROW_BLOCK, w), row_map),
        scratch_shapes=[
            pltpu.VMEM((d, D_EXPERT), BF16),
            pltpu.VMEM((d, D_EXPERT), BF16),
            pltpu.VMEM((D_EXPERT, d), BF16),
        ],
    )
    return pl.pallas_call(
        _experts_body,
        grid_spec=grid_spec,
        out_shape=jax.ShapeDtypeStruct((cap, w), U32),
        compiler_params=_cparams(("arbitrary",)),
        name="experts",
    )(block_e, n_used, xs, w_gate_e, w_up_e, w_down_e)


def _combine_body(dest_ref, ys_ref, wt_ref, h_ref, x1_ref, mod_ref, gpost, wg, wu, wd, o_ref, buf, sem):
    tm = h_ref.shape[0]

    def row_copy(t, k):
        return pltpu.make_async_copy(ys_ref.at[pl.ds(dest_ref[k, t], 1)], buf.at[k, pl.ds(t, 1)], sem)

    def issue(t, _):
        for k in range(TOP_K):
            row_copy(t, k).start()
        return 0

    def drain(t, _):
        for k in range(TOP_K):
            row_copy(t, k).wait()
        return 0

    lax.fori_loop(0, tm, issue, 0)

    h = h_ref[...]
    g = jnp.dot(h, wg[...], preferred_element_type=F32)
    u = jnp.dot(h, wu[...], preferred_element_type=F32)
    shared = jnp.dot((g * jax.nn.sigmoid(g) * u).astype(BF16), wd[...], preferred_element_type=F32)

    lax.fori_loop(0, tm, drain, 0)

    half = buf.shape[2]
    wt = wt_ref[...]
    ya = shared[:, :half]
    yb = shared[:, half:]
    for k in range(TOP_K):
        a, b = _unpack_halves(buf[k])
        ya = ya + wt[:, k:k + 1] * a
        yb = yb + wt[:, k:k + 1] * b
    y = jnp.concatenate([ya, yb], axis=1)
    gate2 = mod_ref[0, 5:6, :]
    o_ref[...] = x1_ref[...] + gate2 * (_rms(y) * gpost[...])


def _combine(ys, dest, wt, h2, x1, mod3, g_post, wg_b, wu_b, wd_b, seq):
    n, d = h2.shape
    tm = 256
    w = d // 2
    per_b = seq // tm
    const = lambda shape: pl.BlockSpec(shape, lambda i, *_: (0,) * len(shape), pipeline_mode=pl.Buffered(1))
    grid_spec = pltpu.PrefetchScalarGridSpec(
        num_scalar_prefetch=0,
        grid=(n // tm,),
        in_specs=[
            pl.BlockSpec((TOP_K, tm), lambda i: (0, i), memory_space=pltpu.SMEM),
            pl.BlockSpec(memory_space=pl.ANY),
            pl.BlockSpec((tm, LANES), lambda i: (i, 0)),
            pl.BlockSpec((tm, d), lambda i: (i, 0)),
            pl.BlockSpec((tm, d), lambda i: (i, 0)),
            pl.BlockSpec((1, N_MOD, d), lambda i: (i // per_b, 0, 0)),
            const((1, d)), const(wg_b.shape), const(wu_b.shape), const(wd_b.shape),
        ],
        out_specs=pl.BlockSpec((tm, d), lambda i: (i, 0)),
        scratch_shapes=[pltpu.VMEM((TOP_K, tm, w), U32), pltpu.SemaphoreType.DMA(())],
    )
    return pl.pallas_call(
        _combine_body,
        grid_spec=grid_spec,
        out_shape=jax.ShapeDtypeStruct((n, d), F32),
        compiler_params=_cparams(("arbitrary",), disable_bounds_checks=True),
        name="combine",
    )(dest, ys, wt, h2, x1, mod3, g_post.reshape(1, d), wg_b, wu_b, wd_b)


def _layer(x, mod, positions, g_pre_mix, g_post_mix, g_pre_ffn, g_post_ffn, w_in, w_dil_out, w_sb_out, w_mix_out,
           w_router, router_bias, w_gate_e, w_up_e, w_down_e, w_gate_s, w_up_s, w_down_s):
    bsz, seq, d = x.shape
    n = bsz * seq
    xf = x.reshape(n, d)
    mod3 = mod.reshape(bsz, N_MOD, d)

    nd, nq = 3 * WIDTH_DIL, 3 * (WIDTH_DIL + WIDTH_SB)
    w_plain = jnp.concatenate([w_in[:, nq:], w_in[:, nd:nq]], axis=1).astype(BF16)
    cols = []
    for g in range(len(DIL_PATTERNS)):
        for part in range(3):
            lo = part * WIDTH_DIL + g * D_DIL_OUT
            cols.append(w_in[:, lo:lo + D_DIL_OUT])
    w_dil = jnp.concatenate(cols, axis=1).astype(BF16)

    proj = _inproj(xf, g_pre_mix, mod3, w_plain, seq)
    tables = _rope_tables(positions)
    qkv_dil = _inproj_dil(xf, g_pre_mix, mod3, w_dil, tables, bsz, seq)
    o_dil, lse_dil = [], []
    for g, (window, dilation) in enumerate(DIL_PATTERNS):
        assert window // dilation == Q_BLOCK
        o, lse = _dilated_attention(qkv_dil[g], g, bsz, seq)
        o_dil.append(o)
        lse_dil.append(lse)
    o_sb = _stick_breaking(proj, bsz, seq)

    x1, h2 = _mixout(o_dil, lse_dil, o_sb, proj, xf, mod3, g_post_mix, g_pre_ffn,
                     w_dil_out.astype(BF16), w_sb_out.astype(BF16), w_mix_out.astype(BF16), seq)

    top_idx, top_wt = _router(h2, w_router, router_bias)
    counts = _plan_counts(top_idx)[:, 0].astype(I32)
    padded = (counts + ROW_BLOCK - 1) // ROW_BLOCK * ROW_BLOCK
    pends = jnp.cumsum(padded)
    pstart = pends - padded
    nblk = _capacity(n) // ROW_BLOCK
    n_used = (pends[-1] // ROW_BLOCK).astype(I32)
    blk = jnp.minimum(jnp.arange(nblk, dtype=I32), n_used - 1)
    block_e = jnp.minimum(jnp.sum(pends[None, :] <= (blk * ROW_BLOCK)[:, None], axis=1), N_EXPERTS - 1).astype(I32)
    dest = _plan_dest(top_idx, pstart)

    xs = _dispatch(h2, dest, pstart.astype(I32), counts)
    ys = _experts(xs, block_e, n_used.reshape(1), w_gate_e, w_up_e, w_down_e)
    out = _combine(ys, dest, top_wt, h2, x1, mod3, g_post_ffn,
                   w_gate_s.astype(BF16), w_up_s.astype(BF16), w_down_s.astype(BF16), seq)
    return out.reshape(bsz, seq, d)


def kernel(x, c, positions, w_ada, b_ada, g_pre_mix, g_post_mix, g_pre_ffn, g_post_ffn, w_in, w_dil_out,
           w_sb_out, w_mix_out, w_router, router_bias, w_gate_e, w_up_e, w_down_e, w_gate_s, w_up_s, w_down_s):
    for l in range(w_ada.shape[0]):
        mod = _adaln(c, w_ada[l], b_ada[l])
        x = _layer(x, mod, positions, g_pre_mix[l], g_post_mix[l], g_pre_ffn[l], g_post_ffn[l], w_in[l],
                   w_dil_out[l], w_sb_out[l], w_mix_out[l], w_router[l], router_bias[l],
                   w_gate_e[l], w_up_e[l], w_down_e[l], w_gate_s[l], w_up_s[l], w_down_s[l])
    return x
```

```python
import functools

import jax
import jax.numpy as jnp
from jax import lax
from jax.experimental import pallas as pl
from jax.experimental.pallas import tpu as pltpu

F32 = jnp.float32
BF16 = jnp.bfloat16
I32 = jnp.int32
U32 = jnp.uint32

D_MODEL = 2048
HEAD_DIM = 128
DIL_PATTERNS = ((128, 1), (512, 4), (2048, 16))
HEADS_PER_GROUP = 4
N_HEADS_DIL = 12
N_HEADS_SB = 8
WIDTH_DIL = N_HEADS_DIL * HEAD_DIM
WIDTH_SB = N_HEADS_SB * HEAD_DIM
D_DIL_OUT = HEADS_PER_GROUP * HEAD_DIM
Q_BLOCK = 128
ROPE_THETA = 500000.0
ROPE_DIM = HEAD_DIM // 4
N_GATE = 2 * D_MODEL
N_EXPERTS = 64
TOP_K = 8
N_GROUPS = 8
GROUP_SIZE = N_EXPERTS // N_GROUPS
TOPK_GROUPS = 4
D_EXPERT = 512
D_SHARED = 512
ROUTED_SCALE = 2.5
RMS_EPS = 1e-6
N_MOD = 6
ATTN_SCALE = HEAD_DIM ** -0.5

LANES = 128
SUBLANES = 8
VMEM_LIMIT = 56 * 1024 * 1024

ROW_BLOCK = 256
SB_DEAD = -110.0


def _cparams(sem, **kw):
    return pltpu.CompilerParams(dimension_semantics=sem, vmem_limit_bytes=VMEM_LIMIT, **kw)


def _adaln_body(ct_ref, w_ref, b_ref, o_ref, *, kc):
    nb = ct_ref.shape[1]
    nk = w_ref.shape[0] // kc

    def step(i, acc):
        k0 = pl.multiple_of(i * kc, kc)
        w = w_ref[pl.ds(k0, kc), :]
        c = ct_ref[pl.ds(k0, kc), :]
        s = c * jax.nn.sigmoid(c)
        parts = [jnp.sum(w * s[:, b:b + 1], axis=0, keepdims=True) for b in range(nb)]
        return acc + jnp.concatenate(parts, axis=0)

    acc = lax.fori_loop(0, nk, step, jnp.zeros(o_ref.shape, F32))
    o_ref[...] = acc + b_ref[...]


def _adaln(c, w_ada, b_ada):
    nb, d = c.shape
    n_out = w_ada.shape[1]
    tn = 1024
    return pl.pallas_call(
        functools.partial(_adaln_body, kc=256),
        grid=(n_out // tn,),
        in_specs=[
            pl.BlockSpec((d, nb), lambda j: (0, 0)),
            pl.BlockSpec((d, tn), lambda j: (0, j)),
            pl.BlockSpec((1, tn), lambda j: (0, j)),
        ],
        out_specs=pl.BlockSpec((nb, tn), lambda j: (0, j)),
        out_shape=jax.ShapeDtypeStruct((nb, n_out), F32),
        compiler_params=_cparams(("arbitrary",)),
        name="adaln",
    )(c.T, w_ada, b_ada.reshape(1, n_out))


def _rms(x):
    return x * lax.rsqrt(jnp.mean(x * x, axis=-1, keepdims=True) + RMS_EPS)


def _prenorm(x_ref, g_ref, mod_ref):
    y = _rms(x_ref[...]) * g_ref[...]
    shift = mod_ref[0, 0:1, :]
    scale = mod_ref[0, 1:2, :]
    return (y * (1.0 + scale) + shift).astype(BF16)


def _inproj_body(x_ref, g_ref, mod_ref, w_ref, o_ref, h_ref):
    @pl.when(pl.program_id(1) == 0)
    def _():
        h_ref[...] = _prenorm(x_ref, g_ref, mod_ref)

    o_ref[...] = jnp.dot(h_ref[...], w_ref[...], preferred_element_type=F32).astype(o_ref.dtype)


def _inproj(xf, g_pre, mod3, w_b, seq):
    n, d = xf.shape
    width = w_b.shape[1]
    tm, tn = 512, width // 4
    per_b = seq // tm
    return pl.pallas_call(
        _inproj_body,
        grid=(n // tm, width // tn),
        in_specs=[
            pl.BlockSpec((tm, d), lambda i, j: (i, 0)),
            pl.BlockSpec((1, d), lambda i, j: (0, 0)),
            pl.BlockSpec((1, N_MOD, d), lambda i, j: (i // per_b, 0, 0)),
            pl.BlockSpec((d, tn), lambda i, j: (0, j)),
        ],
        out_specs=pl.BlockSpec((tm, tn), lambda i, j: (i, j)),
        out_shape=jax.ShapeDtypeStruct((n, width), BF16),
        scratch_shapes=[pltpu.VMEM((tm, d), BF16)],
        compiler_params=_cparams(("arbitrary", "arbitrary")),
        name="inproj",
    )(xf, g_pre.reshape(1, d), mod3, w_b)


def _inproj_dil_body(x_ref, g_ref, mod_ref, w_ref, t_ref, o0, o1, o2, res_ref):
    tm = x_ref.shape[0]
    h = _prenorm(x_ref, g_ref, mod_ref)
    t = t_ref[...]
    gw = 3 * D_DIL_OUT
    for gi, o_ref in enumerate((o0, o1, o2)):
        dil = DIL_PATTERNS[gi][1]
        res = jnp.dot(h, w_ref[:, gi * gw:(gi + 1) * gw], preferred_element_type=F32)
        for hs in range(3 * HEADS_PER_GROUP):
            sl = slice(hs * HEAD_DIM, (hs + 1) * HEAD_DIM)
            res_ref[hs] = _apply_rope(res[:, sl], t) if hs < 2 * HEADS_PER_GROUP else res[:, sl]
        for r in range(dil):
            for hs in range(3 * HEADS_PER_GROUP):
                rows = res_ref[hs] if dil == 1 else res_ref[hs, pl.ds(r, tm // dil, stride=dil), :]
                o_ref[0, r, :, hs * HEAD_DIM:(hs + 1) * HEAD_DIM] = rows.astype(o_ref.dtype)


def _inproj_dil(xf, g_pre, mod3, w_b, tables, bsz, seq):
    n, d = xf.shape
    gw = 3 * D_DIL_OUT
    tm = 512
    per_b = seq // tm
    dils = [p[1] for p in DIL_PATTERNS]
    return pl.pallas_call(
        _inproj_dil_body,
        grid=(n // tm,),
        in_specs=[
            pl.BlockSpec((tm, d), lambda i: (i, 0)),
            pl.BlockSpec((1, d), lambda i: (0, 0)),
            pl.BlockSpec((1, N_MOD, d), lambda i: (i // per_b, 0, 0)),
            pl.BlockSpec(w_b.shape, lambda i: (0, 0), pipeline_mode=pl.Buffered(1)),
            pl.BlockSpec((tm, 3 * LANES), lambda i: (i, 0)),
        ],
        out_specs=[pl.BlockSpec((1, dl, tm // dl, gw), lambda i: (i // per_b, 0, i % per_b, 0)) for dl in dils],
        out_shape=[jax.ShapeDtypeStruct((bsz, dl, seq // dl, gw), BF16) for dl in dils],
        scratch_shapes=[pltpu.VMEM((gw // HEAD_DIM, tm, HEAD_DIM), F32)],
        compiler_params=_cparams(("arbitrary",)),
        name="inproj_dil",
    )(xf, g_pre.reshape(1, d), mod3, w_b, tables)


def _rope_body(pos_ref, f_ref, o_ref):
    ang = pos_ref[...].astype(F32) * f_ref[...]
    c = jnp.cos(ang)
    s = jnp.sin(ang)
    lane = lax.broadcasted_iota(I32, ang.shape, 1)
    half = ROPE_DIM // 2
    o_ref[:, 0:LANES] = c
    o_ref[:, LANES:2 * LANES] = jnp.where(lane >= half, s, 0.0)
    o_ref[:, 2 * LANES:3 * LANES] = jnp.where(lane < half, -s, 0.0)


def _rope_tables(positions):
    n = positions.size
    half = ROPE_DIM // 2
    inv_freq = ROPE_THETA ** (-jnp.arange(0, ROPE_DIM, 2, dtype=F32) / ROPE_DIM)
    f = jnp.concatenate([inv_freq, inv_freq, jnp.zeros((LANES - 2 * half,), F32)]).reshape(1, LANES)
    tm = 2048
    return pl.pallas_call(
        _rope_body,
        grid=(n // tm,),
        in_specs=[pl.BlockSpec((tm, 1), lambda i: (i, 0)), pl.BlockSpec((1, LANES), lambda i: (0, 0))],
        out_specs=pl.BlockSpec((tm, 3 * LANES), lambda i: (i, 0)),
        out_shape=jax.ShapeDtypeStruct((n, 3 * LANES), F32),
        compiler_params=_cparams(("arbitrary",)),
        name="rope_tables",
    )(positions.reshape(n, 1), f)


def _apply_rope(x, t):
    half = ROPE_DIM // 2
    return (x * t[:, 0:LANES]
            + pltpu.roll(x, half, 1) * t[:, LANES:2 * LANES]
            + pltpu.roll(x, LANES - half, 1) * t[:, 2 * LANES:3 * LANES])


def _dil_body(cur_ref, kp_ref, vp_ref, o_ref, lse_ref, obuf, lbuf, *, dil, nsub):
    n = pl.program_id(1)
    tq = nsub * Q_BLOCK
    row = lax.broadcasted_iota(I32, (Q_BLOCK, 2 * Q_BLOCK), 0)
    col = lax.broadcasted_iota(I32, (Q_BLOCK, 2 * Q_BLOCK), 1)
    rel = row + Q_BLOCK - col
    band = jnp.where(rel >= 0, jnp.where(rel <= Q_BLOCK, 1.0, 0.0), 0.0)
    first = jnp.where(col >= Q_BLOCK, band, jnp.where(n > 0, band, 0.0))
    for r in range(dil):
        for h in range(HEADS_PER_GROUP):
            sl = slice(h * HEAD_DIM, (h + 1) * HEAD_DIM)
            ksl = slice(D_DIL_OUT + h * HEAD_DIM, D_DIL_OUT + (h + 1) * HEAD_DIM)
            vsl = slice(2 * D_DIL_OUT + h * HEAD_DIM, 2 * D_DIL_OUT + (h + 1) * HEAD_DIM)
            for j in range(nsub):
                rs = slice(j * Q_BLOCK, (j + 1) * Q_BLOCK)
                ps = slice((j - 1) * Q_BLOCK, j * Q_BLOCK)
                kprev = kp_ref[0, r, :, sl] if j == 0 else cur_ref[0, r, ps, ksl]
                vprev = vp_ref[0, r, :, sl] if j == 0 else cur_ref[0, r, ps, vsl]
                kcat = jnp.concatenate([kprev, cur_ref[0, r, rs, ksl]], axis=0)
                vcat = jnp.concatenate([vprev, cur_ref[0, r, rs, vsl]], axis=0)
                s = lax.dot_general(cur_ref[0, r, rs, sl], kcat, (((1,), (1,)), ((), ())),
                                    preferred_element_type=F32) * ATTN_SCALE
                s = jnp.where((first if j == 0 else band) > 0.0, s, -jnp.inf)
                m = jnp.max(s, axis=-1, keepdims=True)
                p = jnp.exp(s - m)
                l = jnp.sum(p, axis=-1, keepdims=True)
                o = jnp.dot((p / l).astype(BF16), vcat, preferred_element_type=F32)
                lse = jnp.broadcast_to(m + jnp.log(l), (Q_BLOCK, HEAD_DIM))
                if dil == 1:
                    o_ref[h, rs, :] = o
                    lse_ref[h, rs, :] = lse
                else:
                    obuf[h, rs, :] = o
                    lbuf[h, rs, :] = lse
            if dil > 1:
                o_ref[h, pl.ds(r, tq, stride=dil), :] = obuf[h]
                lse_ref[h, pl.ds(r, tq, stride=dil), :] = lbuf[h]


def _dilated_attention(qkv, g, bsz, seq):
    dil = DIL_PATTERNS[g][1]
    length = seq // dil
    tq = min(4 * Q_BLOCK, (16 * Q_BLOCK) // dil, length)
    nsub = tq // Q_BLOCK
    nq = length // tq
    gw = 3 * D_DIL_OUT
    n = bsz * seq

    def prev(colblk):
        return pl.BlockSpec((1, dil, Q_BLOCK, D_DIL_OUT),
                            lambda b, i: (b, 0, jnp.maximum(i * nsub - 1, 0), colblk))

    nh = HEADS_PER_GROUP
    out_spec = pl.BlockSpec((nh, tq * dil, HEAD_DIM), lambda b, i: (0, b * nq + i, 0))
    out_shape = jax.ShapeDtypeStruct((nh, n, HEAD_DIM), F32)
    return pl.pallas_call(
        functools.partial(_dil_body, dil=dil, nsub=nsub),
        grid=(bsz, nq),
        in_specs=[pl.BlockSpec((1, dil, tq, gw), lambda b, i: (b, 0, i, 0)), prev(1), prev(2)],
        out_specs=[out_spec, out_spec],
        out_shape=[out_shape, out_shape],
        scratch_shapes=[pltpu.VMEM((nh, tq, HEAD_DIM), F32), pltpu.VMEM((nh, tq, HEAD_DIM), F32)],
        compiler_params=_cparams(("arbitrary", "arbitrary")),
        name=f"dilated_d{dil}",
    )(qkv, qkv, qkv)


def _sb_body(q_ref, k_ref, v_ref, o_ref, acc_ref, car_ref):
    nblk = q_ref.shape[1] // Q_BLOCK
    r = lax.broadcasted_iota(I32, (Q_BLOCK, Q_BLOCK), 0)
    c = lax.broadcasted_iota(I32, (Q_BLOCK, Q_BLOCK), 1)
    causal = c < r
    rr = lax.broadcasted_iota(I32, (Q_BLOCK, 2 * Q_BLOCK), 0)
    cc = lax.broadcasted_iota(I32, (Q_BLOCK, 2 * Q_BLOCK), 1)
    uo = jnp.where(cc >= Q_BLOCK, 1.0, jnp.where(rr > cc, 1.0, 0.0)).astype(BF16)

    nh = acc_ref.shape[0]
    heads = range(nh)

    def tiles(qs, kb, carries, diag):
        k0 = pl.multiple_of(kb * Q_BLOCK, Q_BLOCK)
        hs = [slice(h * HEAD_DIM, (h + 1) * HEAD_DIM) for h in heads]
        zs = [lax.dot_general(qs[h], k_ref[0, pl.ds(k0, Q_BLOCK), hs[h]], (((1,), (1,)), ((), ())),
                              preferred_element_type=F32) * ATTN_SCALE for h in heads]
        stacked, log_s = [], []
        for z in zs:
            sp = jnp.log(1.0 + jnp.exp(-jnp.abs(z)))
            mx = jnp.maximum(z, 0.0)
            log_1m = -(mx + sp)
            if diag:
                log_1m = jnp.where(causal, log_1m, 0.0)
            hi = log_1m.astype(BF16)
            lo = (log_1m - hi.astype(F32)).astype(BF16)
            stacked.append(jnp.concatenate([hi, lo], axis=0))
            log_s.append((z - mx) - sp)
        r2s = [jnp.dot(s, uo, preferred_element_type=F32) for s in stacked]
        probs, new_carries = [], []
        for h in heads:
            sums = r2s[h][:Q_BLOCK] + r2s[h][Q_BLOCK:]
            a = jnp.exp(log_s[h] + carries[h] + sums[:, :Q_BLOCK])
            if diag:
                a = jnp.where(causal, a, 0.0)
            probs.append(a.astype(BF16))
            new_carries.append(carries[h] + sums[:, Q_BLOCK:])
        pvs = [jnp.dot(probs[h], v_ref[0, pl.ds(k0, Q_BLOCK), hs[h]], preferred_element_type=F32) for h in heads]
        return pvs, new_carries

    def all_max(xs):
        m = xs[0]
        for x in xs[1:]:
            m = jnp.maximum(m, x)
        return jnp.max(m)

    def qblock(qi, _):
        q0 = pl.multiple_of(qi * Q_BLOCK, Q_BLOCK)
        qs = [q_ref[0, pl.ds(q0, Q_BLOCK), h * HEAD_DIM:(h + 1) * HEAD_DIM] for h in heads]
        zero = jnp.zeros((Q_BLOCK, Q_BLOCK), F32)
        pvs, cars = tiles(qs, qi, [zero] * nh, True)
        for h in heads:
            acc_ref[h] = pvs[h]
            car_ref[h] = cars[h]

        def cond(st):
            return jnp.logical_and(st[0] >= 0, st[1] > SB_DEAD)

        def body(st):
            pvs, cars = tiles(qs, st[0], [car_ref[h] for h in heads], False)
            for h in heads:
                acc_ref[h] += pvs[h]
                car_ref[h] = cars[h]
            return st[0] - 1, all_max(cars)

        lax.while_loop(cond, body, (qi - 1, all_max(cars)))
        for h in range(nh):
            o_ref[0, pl.ds(q0, Q_BLOCK), h * HEAD_DIM:(h + 1) * HEAD_DIM] = acc_ref[h].astype(o_ref.dtype)
        return 0

    lax.fori_loop(0, nblk, qblock, 0)


SB_HEADS_PER_STEP = 4


def _stick_breaking(proj, bsz, seq):
    width = proj.shape[1]
    pv = proj.reshape(bsz, seq, width)
    nh = SB_HEADS_PER_STEP
    bw = nh * HEAD_DIM
    base = N_GATE // bw

    def spec(off):
        return pl.BlockSpec((1, seq, bw), lambda b, h: (b, 0, base + off + h))

    nstep = N_HEADS_SB // nh
    o = pl.pallas_call(
        _sb_body,
        grid=(bsz, nstep),
        in_specs=[spec(0), spec(nstep), spec(2 * nstep)],
        out_specs=pl.BlockSpec((1, seq, bw), lambda b, h: (b, 0, h)),
        out_shape=jax.ShapeDtypeStruct((bsz, seq, WIDTH_SB), BF16),
        scratch_shapes=[pltpu.VMEM((nh, Q_BLOCK, Q_BLOCK), F32), pltpu.VMEM((nh, Q_BLOCK, Q_BLOCK), F32)],
        compiler_params=_cparams(("arbitrary", "arbitrary")),
        name="stick_breaking",
    )(pv, pv, pv)
    return o.reshape(bsz * seq, WIDTH_SB)


def _mixout_body(o1, o2, o3, l1, l2, l3, osb, gd_ref, gs_ref, x_ref, mod_ref, gpost, gpre,
                 wd, ws, wm, x1_ref, h2_ref):
    heads = []
    for h in range(HEADS_PER_GROUP):
        la, lb, lc = l1[h], l2[h], l3[h]
        m = jnp.maximum(la, jnp.maximum(lb, lc))
        ea, eb, ec = jnp.exp(la - m), jnp.exp(lb - m), jnp.exp(lc - m)
        heads.append(((ea * o1[h] + eb * o2[h] + ec * o3[h]) / (ea + eb + ec)).astype(BF16))
    yd = jnp.dot(jnp.concatenate(heads, axis=1), wd[...], preferred_element_type=F32)
    ys = jnp.dot(osb[...], ws[...], preferred_element_type=F32)
    mix = jax.nn.sigmoid(gd_ref[...].astype(F32)) * yd + jax.nn.sigmoid(gs_ref[...].astype(F32)) * ys
    y = jnp.dot(mix.astype(BF16), wm[...], preferred_element_type=F32)
    gate1 = mod_ref[0, 2:3, :]
    shift2 = mod_ref[0, 3:4, :]
    scale2 = mod_ref[0, 4:5, :]
    x1 = x_ref[...] + gate1 * (_rms(y) * gpost[...])
    x1_ref[...] = x1
    h2_ref[...] = ((_rms(x1) * gpre[...]) * (1.0 + scale2) + shift2).astype(h2_ref.dtype)


def _const_spec(shape):
    return pl.BlockSpec(shape, lambda i: (0,) * len(shape), pipeline_mode=pl.Buffered(1))


def _mixout(o_dil, lse_dil, o_sb, proj, xf, mod3, g_post, g_pre, wd_b, ws_b, wm_b, seq):
    n, d = xf.shape
    tm = 256
    per_b = seq // tm
    row = lambda w: pl.BlockSpec((tm, w), lambda i: (i, 0))
    head_major = pl.BlockSpec((HEADS_PER_GROUP, tm, HEAD_DIM), lambda i: (0, i, 0))
    in_specs = (
        [head_major] * 6 + [row(WIDTH_SB)]
        + [pl.BlockSpec((tm, d), lambda i: (i, 0)), pl.BlockSpec((tm, d), lambda i: (i, 1))]
        + [row(d), pl.BlockSpec((1, N_MOD, d), lambda i: (i // per_b, 0, 0))]
        + [_const_spec((1, d)), _const_spec((1, d))]
        + [_const_spec(wd_b.shape), _const_spec(ws_b.shape), _const_spec(wm_b.shape)]
    )
    return pl.pallas_call(
        _mixout_body,
        grid=(n // tm,),
        in_specs=in_specs,
        out_specs=[row(d), row(d)],
        out_shape=[jax.ShapeDtypeStruct((n, d), F32), jax.ShapeDtypeStruct((n, d), BF16)],
        compiler_params=_cparams(("arbitrary",)),
        name="mixout",
    )(*o_dil, *lse_dil, o_sb, proj, proj, xf, mod3, g_post.reshape(1, d), g_pre.reshape(1, d), wd_b, ws_b, wm_b)


def _topk_rows(x, k, iota0):
    big = x.shape[0]
    out = []
    for _ in range(k):
        m = jnp.max(x, axis=0, keepdims=True)
        i = jnp.min(jnp.where(x == m, iota0, big), axis=0, keepdims=True)
        out.append((m, i))
        x = jnp.where(iota0 == i, -jnp.inf, x)
    return out


def _router_body(h_ref, wr_ref, bias_ref, idx_ref, wt_ref):
    tm = h_ref.shape[0]
    logits = lax.dot_general(wr_ref[...], h_ref[...], (((1,), (1,)), ((), ())), preferred_element_type=F32)
    scores = jax.nn.sigmoid(logits)
    sel = scores + bias_ref[...]
    sub = lax.broadcasted_iota(I32, (GROUP_SIZE, tm), 0)
    grp = []
    for g in range(N_GROUPS):
        (m1, _), (m2, _) = _topk_rows(sel[g * GROUP_SIZE:(g + 1) * GROUP_SIZE], 2, sub)
        grp.append(m1 + m2)
    gscore = jnp.concatenate(grp, axis=0)
    giota = lax.broadcasted_iota(I32, (N_GROUPS, tm), 0)
    gmask = jnp.zeros((N_GROUPS, tm), F32)
    for _, gi in _topk_rows(gscore, TOPK_GROUPS, giota):
        gmask = jnp.where(giota == gi, 1.0, gmask)
    masked = jnp.concatenate(
        [jnp.where(gmask[g:g + 1] > 0.0, sel[g * GROUP_SIZE:(g + 1) * GROUP_SIZE], -jnp.inf)
         for g in range(N_GROUPS)], axis=0)
    eiota = lax.broadcasted_iota(I32, (N_EXPERTS, tm), 0)
    picks = _topk_rows(masked, TOP_K, eiota)
    idx = jnp.concatenate([i for _, i in picks], axis=0)
    top_s = jnp.concatenate(
        [jnp.sum(jnp.where(eiota == i, scores, 0.0), axis=0, keepdims=True) for _, i in picks], axis=0)
    top_w = top_s / jnp.sum(top_s, axis=0, keepdims=True) * ROUTED_SCALE
    idx_ref[...] = idx
    wpad = jnp.concatenate([top_w, jnp.zeros((LANES - TOP_K, tm), F32)], axis=0)
    wt_ref[...] = wpad.T


def _router(h2, w_router, router_bias):
    n, d = h2.shape
    tm = 512
    return pl.pallas_call(
        _router_body,
        grid=(n // tm,),
        in_specs=[
            pl.BlockSpec((tm, d), lambda i: (i, 0)),
            pl.BlockSpec((N_EXPERTS, d), lambda i: (0, 0)),
            pl.BlockSpec((N_EXPERTS, 1), lambda i: (0, 0)),
        ],
        out_specs=[pl.BlockSpec((TOP_K, tm), lambda i: (0, i)), pl.BlockSpec((tm, LANES), lambda i: (i, 0))],
        out_shape=[jax.ShapeDtypeStruct((TOP_K, n), I32), jax.ShapeDtypeStruct((n, LANES), F32)],
        compiler_params=_cparams(("arbitrary",)),
        name="router",
    )(h2, w_router.T.astype(BF16), router_bias.reshape(N_EXPERTS, 1))


def _plan_tile(idx):
    tm = idx.shape[1]
    eiota = lax.broadcasted_iota(I32, (N_EXPERTS, tm), 0)
    hit = jnp.zeros((N_EXPERTS, tm), F32)
    for k in range(TOP_K):
        hit = jnp.where(eiota == idx[k:k + 1], 1.0, hit)
    r = lax.broadcasted_iota(I32, (tm, tm), 0)
    c = lax.broadcasted_iota(I32, (tm, tm), 1)
    before = jnp.where(r < c, 1.0, 0.0).astype(BF16)
    excl = jnp.dot(hit.astype(BF16), before, preferred_element_type=F32)
    tot = excl[:, tm - 1:tm] + hit[:, tm - 1:tm]
    return eiota, hit, excl, tot


def _plan_counts_body(idx_ref, cnt_ref):
    @pl.when(pl.program_id(0) == 0)
    def _():
        cnt_ref[...] = jnp.zeros(cnt_ref.shape, F32)

    _, _, _, tot = _plan_tile(idx_ref[...])
    cnt_ref[...] += tot


def _plan_counts(top_idx):
    n = top_idx.shape[1]
    tm = 512
    return pl.pallas_call(
        _plan_counts_body,
        grid=(n // tm,),
        in_specs=[pl.BlockSpec((TOP_K, tm), lambda i: (0, i))],
        out_specs=pl.BlockSpec((N_EXPERTS, LANES), lambda i: (0, 0)),
        out_shape=jax.ShapeDtypeStruct((N_EXPERTS, LANES), F32),
        compiler_params=_cparams(("arbitrary",)),
        name="plan_counts",
    )(top_idx)


def _plan_dest_body(idx_ref, base_ref, dest_ref, run_ref):
    @pl.when(pl.program_id(0) == 0)
    def _():
        run_ref[...] = base_ref[...]

    idx = idx_ref[...]
    eiota, _, excl, tot = _plan_tile(idx)
    pos = run_ref[:, 0:1] + excl
    rows = [jnp.sum(jnp.where(eiota == idx[k:k + 1], pos, 0.0), axis=0, keepdims=True) for k in range(TOP_K)]
    dest_ref[...] = jnp.concatenate(rows, axis=0).astype(I32)
    run_ref[...] += tot


def _plan_dest(top_idx, pstart):
    n = top_idx.shape[1]
    tm = 512
    base = jnp.broadcast_to(pstart.astype(F32).reshape(N_EXPERTS, 1), (N_EXPERTS, LANES))
    return pl.pallas_call(
        _plan_dest_body,
        grid=(n // tm,),
        in_specs=[pl.BlockSpec((TOP_K, tm), lambda i: (0, i)), pl.BlockSpec((N_EXPERTS, LANES), lambda i: (0, 0))],
        out_specs=pl.BlockSpec((TOP_K, tm), lambda i: (0, i)),
        out_shape=jax.ShapeDtypeStruct((TOP_K, n), I32),
        scratch_shapes=[pltpu.VMEM((N_EXPERTS, LANES), F32)],
        compiler_params=_cparams(("arbitrary",)),
        name="plan_dest",
    )(top_idx, base)


def _capacity(n):
    return n * TOP_K + N_EXPERTS * ROW_BLOCK


def _pack_halves(y):
    w = y.shape[1] // 2
    hi = pltpu.bitcast(y[:, :w].astype(BF16).astype(F32), U32)
    lo = pltpu.bitcast(y[:, w:].astype(BF16).astype(F32), U32)
    return hi | (lo >> 16)


def _unpack_halves(p):
    a = pltpu.bitcast(p & jnp.uint32(0xFFFF0000), F32)
    b = pltpu.bitcast(p << 16, F32)
    return a, b


def _pad_chunks():
    sizes, s = [], ROW_BLOCK // 2
    while s >= SUBLANES:
        sizes.append(s)
        s //= 2
    return sizes


def _dispatch_body(pstart_ref, cnt_ref, dest_ref, h_ref, xs_ref, xbuf, zbuf, sem, zsem):
    tm = h_ref.shape[0]
    step = pl.program_id(0)

    def pad_dmas(wait):
        def per_expert(e, _):
            cnt = cnt_ref[e]
            pad = (ROW_BLOCK - cnt % ROW_BLOCK) % ROW_BLOCK
            start = pstart_ref[e] + cnt
            head = (SUBLANES - start % SUBLANES) % SUBLANES
            for r in range(SUBLANES - 1):
                @pl.when(r < head)
                def _():
                    cp = pltpu.make_async_copy(zbuf.at[pl.ds(0, 1)], xs_ref.at[pl.ds(start + r, 1)], zsem)
                    cp.wait() if wait else cp.start()
            body = start + head
            rem = pad - head
            for size in _pad_chunks():
                @pl.when((rem & size) != 0)
                def _():
                    off = pl.multiple_of(body + (rem & ~(2 * size - 1)), SUBLANES)
                    cp = pltpu.make_async_copy(zbuf.at[pl.ds(0, size)], xs_ref.at[pl.ds(off, size)], zsem)
                    cp.wait() if wait else cp.start()
            return 0
        lax.fori_loop(0, N_EXPERTS, per_expert, 0)

    @pl.when(step == 0)
    def _():
        zbuf[...] = jnp.zeros(zbuf.shape, zbuf.dtype)
        pad_dmas(False)
        pad_dmas(True)

    xbuf[...] = _pack_halves(h_ref[...].astype(F32))

    def row_copy(t, k):
        return pltpu.make_async_copy(xbuf.at[pl.ds(t, 1)], xs_ref.at[pl.ds(dest_ref[k, t], 1)], sem)

    def issue(t, _):
        for k in range(TOP_K):
            row_copy(t, k).start()
        return 0

    def drain(t, _):
        for k in range(TOP_K):
            row_copy(t, k).wait()
        return 0

    lax.fori_loop(0, tm, issue, 0)
    lax.fori_loop(0, tm, drain, 0)


def _dispatch(h2, dest, pstart, counts):
    n, d = h2.shape
    tm = 256
    w = d // 2
    grid_spec = pltpu.PrefetchScalarGridSpec(
        num_scalar_prefetch=2,
        grid=(n // tm,),
        in_specs=[
            pl.BlockSpec((TOP_K, tm), lambda i, *_: (0, i), memory_space=pltpu.SMEM),
            pl.BlockSpec((tm, d), lambda i, *_: (i, 0)),
        ],
        out_specs=pl.BlockSpec(memory_space=pl.ANY),
        scratch_shapes=[
            pltpu.VMEM((tm, w), U32),
            pltpu.VMEM((ROW_BLOCK // 2, w), U32),
            pltpu.SemaphoreType.DMA(()),
            pltpu.SemaphoreType.DMA(()),
        ],
    )
    return pl.pallas_call(
        _dispatch_body,
        grid_spec=grid_spec,
        out_shape=jax.ShapeDtypeStruct((_capacity(n), w), U32),
        compiler_params=_cparams(("arbitrary",), has_side_effects=True, disable_bounds_checks=True),
        name="dispatch",
    )(pstart, counts, dest, h2)


def _experts_body(be_ref, nu_ref, x_ref, wg_ref, wu_ref, wd_ref, y_ref, wg_b, wu_b, wd_b):
    i = pl.program_id(0)
    used = i < nu_ref[0]
    prev = be_ref[jnp.maximum(i - 1, 0)]
    fresh = jnp.logical_or(i == 0, be_ref[i] != prev)

    @pl.when(jnp.logical_and(used, fresh))
    def _():
        wg_b[...] = wg_ref[0].astype(BF16)
        wu_b[...] = wu_ref[0].astype(BF16)
        wd_b[...] = wd_ref[0].astype(BF16)

    @pl.when(used)
    def _():
        half = x_ref.shape[1]
        xa, xb = _unpack_halves(x_ref[...])
        xa = xa.astype(BF16)
        xb = xb.astype(BF16)
        g = (jnp.dot(xa, wg_b[:half], preferred_element_type=F32)
             + jnp.dot(xb, wg_b[half:], preferred_element_type=F32))
        u = (jnp.dot(xa, wu_b[:half], preferred_element_type=F32)
             + jnp.dot(xb, wu_b[half:], preferred_element_type=F32))
        hmid = (g * jax.nn.sigmoid(g) * u).astype(BF16)
        y_ref[...] = _pack_halves(jnp.dot(hmid, wd_b[...], preferred_element_type=F32))


def _experts(xs, block_e, n_used, w_gate_e, w_up_e, w_down_e):
    cap, w = xs.shape
    d = 2 * w
    nblk = cap // ROW_BLOCK

    def row_map(i, be, nu):
        return (jnp.minimum(i, nu[0] - 1), 0)

    def w_map(i, be, nu):
        return (be[i], 0, 0)

    grid_spec = pltpu.PrefetchScalarGridSpec(
        num_scalar_prefetch=2,
        grid=(nblk,),
        in_specs=[
            pl.BlockSpec((ROW_BLOCK, w), row_map),
            pl.BlockSpec((1, d, D_EXPERT), w_map),
            pl.BlockSpec((1, d, D_EXPERT), w_map),
            pl.BlockSpec((1, D_EXPERT, d), w_map),
        ],
        out_specs=pl.BlockSpec((ROW_BLOCK, w), row_map),
        scratch_shapes=[
            pltpu.VMEM((d, D_EXPERT), BF16),
            pltpu.VMEM((d, D_EXPERT), BF16),
            pltpu.VMEM((D_EXPERT, d), BF16),
        ],
    )
    return pl.pallas_call(
        _experts_body,
        grid_spec=grid_spec,
        out_shape=jax.ShapeDtypeStruct((cap, w), U32),
        compiler_params=_cparams(("arbitrary",)),
        name="experts",
    )(block_e, n_used, xs, w_gate_e, w_up_e, w_down_e)


def _combine_body(dest_ref, ys_ref, wt_ref, h_ref, x1_ref, mod_ref, gpost, wg, wu, wd, o_ref, buf, sem):
    tm = h_ref.shape[0]

    def row_copy(t, k):
        return pltpu.make_async_copy(ys_ref.at[pl.ds(dest_ref[k, t], 1)], buf.at[k, pl.ds(t, 1)], sem)

    def issue(t, _):
        for k in range(TOP_K):
            row_copy(t, k).start()
        return 0

    def drain(t, _):
        for k in range(TOP_K):
            row_copy(t, k).wait()
        return 0

    lax.fori_loop(0, tm, issue, 0)

    h = h_ref[...]
    g = jnp.dot(h, wg[...], preferred_element_type=F32)
    u = jnp.dot(h, wu[...], preferred_element_type=F32)
    shared = jnp.dot((g * jax.nn.sigmoid(g) * u).astype(BF16), wd[...], preferred_element_type=F32)

    lax.fori_loop(0, tm, drain, 0)

    half = buf.shape[2]
    wt = wt_ref[...]
    ya = shared[:, :half]
    yb = shared[:, half:]
    for k in range(TOP_K):
        a, b = _unpack_halves(buf[k])
        ya = ya + wt[:, k:k + 1] * a
        yb = yb + wt[:, k:k + 1] * b
    y = jnp.concatenate([ya, yb], axis=1)
    gate2 = mod_ref[0, 5:6, :]
    o_ref[...] = x1_ref[...] + gate2 * (_rms(y) * gpost[...])


def _combine(ys, dest, wt, h2, x1, mod3, g_post, wg_b, wu_b, wd_b, seq):
    n, d = h2.shape
    tm = 256
    w = d // 2
    per_b = seq // tm
    const = lambda shape: pl.BlockSpec(shape, lambda i, *_: (0,) * len(shape), pipeline_mode=pl.Buffered(1))
    grid_spec = pltpu.PrefetchScalarGridSpec(
        num_scalar_prefetch=0,
        grid=(n // tm,),
        in_specs=[
            pl.BlockSpec((TOP_K, tm), lambda i: (0, i), memory_space=pltpu.SMEM),
            pl.BlockSpec(memory_space=pl.ANY),
            pl.BlockSpec((tm, LANES), lambda i: (i, 0)),
            pl.BlockSpec((tm, d), lambda i: (i, 0)),
            pl.BlockSpec((tm, d), lambda i: (i, 0)),
            pl.BlockSpec((1, N_MOD, d), lambda i: (i // per_b, 0, 0)),
            const((1, d)), const(wg_b.shape), const(wu_b.shape), const(wd_b.shape),
        ],
        out_specs=pl.BlockSpec((tm, d), lambda i: (i, 0)),
        scratch_shapes=[pltpu.VMEM((TOP_K, tm, w), U32), pltpu.SemaphoreType.DMA(())],
    )
    return pl.pallas_call(
        _combine_body,
        grid_spec=grid_spec,
        out_shape=jax.ShapeDtypeStruct((n, d), F32),
        compiler_params=_cparams(("arbitrary",), disable_bounds_checks=True),
        name="combine",
    )(dest, ys, wt, h2, x1, mod3, g_post.reshape(1, d), wg_b, wu_b, wd_b)


def _layer(x, mod, positions, g_pre_mix, g_post_mix, g_pre_ffn, g_post_ffn, w_in, w_dil_out, w_sb_out, w_mix_out,
           w_router, router_bias, w_gate_e, w_up_e, w_down_e, w_gate_s, w_up_s, w_down_s):
    bsz, seq, d = x.shape
    n = bsz * seq
    xf = x.reshape(n, d)
    mod3 = mod.reshape(bsz, N_MOD, d)

    nd, nq = 3 * WIDTH_DIL, 3 * (WIDTH_DIL + WIDTH_SB)
    w_plain = jnp.concatenate([w_in[:, nq:], w_in[:, nd:nq]], axis=1).astype(BF16)
    cols = []
    for g in range(len(DIL_PATTERNS)):
        for part in range(3):
            lo = part * WIDTH_DIL + g * D_DIL_OUT
            cols.append(w_in[:, lo:lo + D_DIL_OUT])
    w_dil = jnp.concatenate(cols, axis=1).astype(BF16)

    proj = _inproj(xf, g_pre_mix, mod3, w_plain, seq)
    tables = _rope_tables(positions)
    qkv_dil = _inproj_dil(xf, g_pre_mix, mod3, w_dil, tables, bsz, seq)
    o_dil, lse_dil = [], []
    for g, (window, dilation) in enumerate(DIL_PATTERNS):
        assert window // dilation == Q_BLOCK
        o, lse = _dilated_attention(qkv_dil[g], g, bsz, seq)
        o_dil.append(o)
        lse_dil.append(lse)
    o_sb = _stick_breaking(proj, bsz, seq)

    x1, h2 = _mixout(o_dil, lse_dil, o_sb, proj, xf, mod3, g_post_mix, g_pre_ffn,
                     w_dil_out.astype(BF16), w_sb_out.astype(BF16), w_mix_out.astype(BF16), seq)

    top_idx, top_wt = _router(h2, w_router, router_bias)
    counts = _plan_counts(top_idx)[:, 0].astype(I32)
    padded = (counts + ROW_BLOCK - 1) // ROW_BLOCK * ROW_BLOCK
    pends = jnp.cumsum(padded)
    pstart = pends - padded
    nblk = _capacity(n) // ROW_BLOCK
    n_used = (pends[-1] // ROW_BLOCK).astype(I32)
    blk = jnp.minimum(jnp.arange(nblk, dtype=I32), n_used - 1)
    block_e = jnp.minimum(jnp.sum(pends[None, :] <= (blk * ROW_BLOCK)[:, None], axis=1), N_EXPERTS - 1).astype(I32)
    dest = _plan_dest(top_idx, pstart)

    xs = _dispatch(h2, dest, pstart.astype(I32), counts)
    ys = _experts(xs, block_e, n_used.reshape(1), w_gate_e, w_up_e, w_down_e)
    out = _combine(ys, dest, top_wt, h2, x1, mod3, g_post_ffn,
                   w_gate_s.astype(BF16), w_up_s.astype(BF16), w_down_s.astype(BF16), seq)
    return out.reshape(bsz, seq, d)


def kernel(x, c, positions, w_ada, b_ada, g_pre_mix, g_post_mix, g_pre_ffn, g_post_ffn, w_in, w_dil_out,
           w_sb_out, w_mix_out, w_router, router_bias, w_gate_e, w_up_e, w_down_e, w_gate_s, w_up_s, w_down_s):
    for l in range(w_ada.shape[0]):
        mod = _adaln(c, w_ada[l], b_ada[l])
        x = _layer(x, mod, positions, g_pre_mix[l], g_post_mix[l], g_pre_ffn[l], g_post_ffn[l], w_in[l],
                   w_dil_out[l], w_sb_out[l], w_mix_out[l], w_router[l], router_bias[l],
                   w_gate_e[l], w_up_e[l], w_down_e[l], w_gate_s[l], w_up_s[l], w_down_s[l])
    return x
```

```python
import functools

import jax
import jax.numpy as jnp
from jax import lax
from jax.experimental import pallas as pl
from jax.experimental.pallas import tpu as pltpu

F32 = jnp.float32
BF16 = jnp.bfloat16
I32 = jnp.int32
U32 = jnp.uint32

D_MODEL = 2048
HEAD_DIM = 128
DIL_PATTERNS = ((128, 1), (512, 4), (2048, 16))
HEADS_PER_GROUP = 4
N_HEADS_DIL = 12
N_HEADS_SB = 8
WIDTH_DIL = N_HEADS_DIL * HEAD_DIM
WIDTH_SB = N_HEADS_SB * HEAD_DIM
D_DIL_OUT = HEADS_PER_GROUP * HEAD_DIM
Q_BLOCK = 128
ROPE_THETA = 500000.0
ROPE_DIM = HEAD_DIM // 4
N_GATE = 2 * D_MODEL
N_EXPERTS = 64
TOP_K = 8
N_GROUPS = 8
GROUP_SIZE = N_EXPERTS // N_GROUPS
TOPK_GROUPS = 4
D_EXPERT = 512
D_SHARED = 512
ROUTED_SCALE = 2.5
RMS_EPS = 1e-6
N_MOD = 6
ATTN_SCALE = HEAD_DIM ** -0.5

LANES = 128
SUBLANES = 8
VMEM_LIMIT = 56 * 1024 * 1024

ROW_BLOCK = 256
SB_DEAD = -110.0


def _cparams(sem, **kw):
    return pltpu.CompilerParams(dimension_semantics=sem, vmem_limit_bytes=VMEM_LIMIT, **kw)


def _adaln_body(ct_ref, w_ref, b_ref, o_ref, *, kc):
    nb = ct_ref.shape[1]
    nk = w_ref.shape[0] // kc

    def step(i, acc):
        k0 = pl.multiple_of(i * kc, kc)
        w = w_ref[pl.ds(k0, kc), :]
        c = ct_ref[pl.ds(k0, kc), :]
        s = c * jax.nn.sigmoid(c)
        parts = [jnp.sum(w * s[:, b:b + 1], axis=0, keepdims=True) for b in range(nb)]
        return acc + jnp.concatenate(parts, axis=0)

    acc = lax.fori_loop(0, nk, step, jnp.zeros(o_ref.shape, F32))
    o_ref[...] = acc + b_ref[...]


def _adaln(c, w_ada, b_ada):
    nb, d = c.shape
    n_out = w_ada.shape[1]
    tn = 1024
    return pl.pallas_call(
        functools.partial(_adaln_body, kc=256),
        grid=(n_out // tn,),
        in_specs=[
            pl.BlockSpec((d, nb), lambda j: (0, 0)),
            pl.BlockSpec((d, tn), lambda j: (0, j)),
            pl.BlockSpec((1, tn), lambda j: (0, j)),
        ],
        out_specs=pl.BlockSpec((nb, tn), lambda j: (0, j)),
        out_shape=jax.ShapeDtypeStruct((nb, n_out), F32),
        compiler_params=_cparams(("arbitrary",)),
        name="adaln",
    )(c.T, w_ada, b_ada.reshape(1, n_out))


def _rms(x):
    return x * lax.rsqrt(jnp.mean(x * x, axis=-1, keepdims=True) + RMS_EPS)


def _prenorm(x_ref, g_ref, mod_ref):
    y = _rms(x_ref[...]) * g_ref[...]
    shift = mod_ref[0, 0:1, :]
    scale = mod_ref[0, 1:2, :]
    return (y * (1.0 + scale) + shift).astype(BF16)


def _inproj_body(x_ref, g_ref, mod_ref, w_ref, o_ref, h_ref):
    @pl.when(pl.program_id(1) == 0)
    def _():
        h_ref[...] = _prenorm(x_ref, g_ref, mod_ref)

    o_ref[...] = jnp.dot(h_ref[...], w_ref[...], preferred_element_type=F32).astype(o_ref.dtype)


def _inproj(xf, g_pre, mod3, w_b, seq):
    n, d = xf.shape
    width = w_b.shape[1]
    tm, tn = 512, width // 4
    per_b = seq // tm
    return pl.pallas_call(
        _inproj_body,
        grid=(n // tm, width // tn),
        in_specs=[
            pl.BlockSpec((tm, d), lambda i, j: (i, 0)),
            pl.BlockSpec((1, d), lambda i, j: (0, 0)),
            pl.BlockSpec((1, N_MOD, d), lambda i, j: (i // per_b, 0, 0)),
            pl.BlockSpec((d, tn), lambda i, j: (0, j)),
        ],
        out_specs=pl.BlockSpec((tm, tn), lambda i, j: (i, j)),
        out_shape=jax.ShapeDtypeStruct((n, width), BF16),
        scratch_shapes=[pltpu.VMEM((tm, d), BF16)],
        compiler_params=_cparams(("arbitrary", "arbitrary")),
        name="inproj",
    )(xf, g_pre.reshape(1, d), mod3, w_b)


def _inproj_dil_body(x_ref, g_ref, mod_ref, w_ref, t_ref, o0, o1, o2, res_ref):
    tm = x_ref.shape[0]
    h = _prenorm(x_ref, g_ref, mod_ref)
    t = t_ref[...]
    gw = 3 * D_DIL_OUT
    for gi, o_ref in enumerate((o0, o1, o2)):
        dil = DIL_PATTERNS[gi][1]
        res = jnp.dot(h, w_ref[:, gi * gw:(gi + 1) * gw], preferred_element_type=F32)
        for hs in range(3 * HEADS_PER_GROUP):
            sl = slice(hs * HEAD_DIM, (hs + 1) * HEAD_DIM)
            res_ref[hs] = _apply_rope(res[:, sl], t) if hs < 2 * HEADS_PER_GROUP else res[:, sl]
        for r in range(dil):
            for hs in range(3 * HEADS_PER_GROUP):
                rows = res_ref[hs] if dil == 1 else res_ref[hs, pl.ds(r, tm // dil, stride=dil), :]
                o_ref[0, r, :, hs * HEAD_DIM:(hs + 1) * HEAD_DIM] = rows.astype(o_ref.dtype)


def _inproj_dil(xf, g_pre, mod3, w_b, tables, bsz, seq):
    n, d = xf.shape
    gw = 3 * D_DIL_OUT
    tm = 512
    per_b = seq // tm
    dils = [p[1] for p in DIL_PATTERNS]
    return pl.pallas_call(
        _inproj_dil_body,
        grid=(n // tm,),
        in_specs=[
            pl.BlockSpec((tm, d), lambda i: (i, 0)),
            pl.BlockSpec((1, d), lambda i: (0, 0)),
            pl.BlockSpec((1, N_MOD, d), lambda i: (i // per_b, 0, 0)),
            pl.BlockSpec(w_b.shape, lambda i: (0, 0), pipeline_mode=pl.Buffered(1)),
            pl.BlockSpec((tm, 3 * LANES), lambda i: (i, 0)),
        ],
        out_specs=[pl.BlockSpec((1, dl, tm // dl, gw), lambda i: (i // per_b, 0, i % per_b, 0)) for dl in dils],
        out_shape=[jax.ShapeDtypeStruct((bsz, dl, seq // dl, gw), BF16) for dl in dils],
        scratch_shapes=[pltpu.VMEM((gw // HEAD_DIM, tm, HEAD_DIM), F32)],
        compiler_params=_cparams(("arbitrary",)),
        name="inproj_dil",
    )(xf, g_pre.reshape(1, d), mod3, w_b, tables)


def _rope_body(pos_ref, f_ref, o_ref):
    ang = pos_ref[...].astype(F32) * f_ref[...]
    c = jnp.cos(ang)
    s = jnp.sin(ang)
    lane = lax.broadcasted_iota(I32, ang.shape, 1)
    half = ROPE_DIM // 2
    o_ref[:, 0:LANES] = c
    o_ref[:, LANES:2 * LANES] = jnp.where(lane >= half, s, 0.0)
    o_ref[:, 2 * LANES:3 * LANES] = jnp.where(lane < half, -s, 0.0)


def _rope_tables(positions):
    n = positions.size
    half = ROPE_DIM // 2
    inv_freq = ROPE_THETA ** (-jnp.arange(0, ROPE_DIM, 2, dtype=F32) / ROPE_DIM)
    f = jnp.concatenate([inv_freq, inv_freq, jnp.zeros((LANES - 2 * half,), F32)]).reshape(1, LANES)
    tm = 2048
    return pl.pallas_call(
        _rope_body,
        grid=(n // tm,),
        in_specs=[pl.BlockSpec((tm, 1), lambda i: (i, 0)), pl.BlockSpec((1, LANES), lambda i: (0, 0))],
        out_specs=pl.BlockSpec((tm, 3 * LANES), lambda i: (i, 0)),
        out_shape=jax.ShapeDtypeStruct((n, 3 * LANES), F32),
        compiler_params=_cparams(("arbitrary",)),
        name="rope_tables",
    )(positions.reshape(n, 1), f)


def _apply_rope(x, t):
    half = ROPE_DIM // 2
    return (x * t[:, 0:LANES]
            + pltpu.roll(x, half, 1) * t[:, LANES:2 * LANES]
            + pltpu.roll(x, LANES - half, 1) * t[:, 2 * LANES:3 * LANES])


def _dil_body(cur_ref, kp_ref, vp_ref, o_ref, lse_ref, obuf, lbuf, *, dil, nsub):
    n = pl.program_id(1)
    tq = nsub * Q_BLOCK
    row = lax.broadcasted_iota(I32, (Q_BLOCK, 2 * Q_BLOCK), 0)
    col = lax.broadcasted_iota(I32, (Q_BLOCK, 2 * Q_BLOCK), 1)
    rel = row + Q_BLOCK - col
    band = jnp.where(rel >= 0, jnp.where(rel <= Q_BLOCK, 1.0, 0.0), 0.0)
    first = jnp.where(col >= Q_BLOCK, band, jnp.where(n > 0, band, 0.0))
    for r in range(dil):
        for h in range(HEADS_PER_GROUP):
            sl = slice(h * HEAD_DIM, (h + 1) * HEAD_DIM)
            ksl = slice(D_DIL_OUT + h * HEAD_DIM, D_DIL_OUT + (h + 1) * HEAD_DIM)
            vsl = slice(2 * D_DIL_OUT + h * HEAD_DIM, 2 * D_DIL_OUT + (h + 1) * HEAD_DIM)
            for j in range(nsub):
                rs = slice(j * Q_BLOCK, (j + 1) * Q_BLOCK)
                ps = slice((j - 1) * Q_BLOCK, j * Q_BLOCK)
                kprev = kp_ref[0, r, :, sl] if j == 0 else cur_ref[0, r, ps, ksl]
                vprev = vp_ref[0, r, :, sl] if j == 0 else cur_ref[0, r, ps, vsl]
                kcat = jnp.concatenate([kprev, cur_ref[0, r, rs, ksl]], axis=0)
                vcat = jnp.concatenate([vprev, cur_ref[0, r, rs, vsl]], axis=0)
                s = lax.dot_general(cur_ref[0, r, rs, sl], kcat, (((1,), (1,)), ((), ())),
                                    preferred_element_type=F32) * ATTN_SCALE
                s = jnp.where((first if j == 0 else band) > 0.0, s, -jnp.inf)
                m = jnp.max(s, axis=-1, keepdims=True)
                p = jnp.exp(s - m)
                l = jnp.sum(p, axis=-1, keepdims=True)
                o = jnp.dot((p / l).astype(BF16), vcat, preferred_element_type=F32)
                lse = jnp.broadcast_to(m + jnp.log(l), (Q_BLOCK, HEAD_DIM))
                if dil == 1:
                    o_ref[h, rs, :] = o
                    lse_ref[h, rs, :] = lse
                else:
                    obuf[h, rs, :] = o
                    lbuf[h, rs, :] = lse
            if dil > 1:
                o_ref[h, pl.ds(r, tq, stride=dil), :] = obuf[h]
                lse_ref[h, pl.ds(r, tq, stride=dil), :] = lbuf[h]


def _dilated_attention(qkv, g, bsz, seq):
    dil = DIL_PATTERNS[g][1]
    length = seq // dil
    tq = min(4 * Q_BLOCK, (16 * Q_BLOCK) // dil, length)
    nsub = tq // Q_BLOCK
    nq = length // tq
    gw = 3 * D_DIL_OUT
    n = bsz * seq

    def prev(colblk):
        return pl.BlockSpec((1, dil, Q_BLOCK, D_DIL_OUT),
                            lambda b, i: (b, 0, jnp.maximum(i * nsub - 1, 0), colblk))

    nh = HEADS_PER_GROUP
    out_spec = pl.BlockSpec((nh, tq * dil, HEAD_DIM), lambda b, i: (0, b * nq + i, 0))
    out_shape = jax.ShapeDtypeStruct((nh, n, HEAD_DIM), F32)
    return pl.pallas_call(
        functools.partial(_dil_body, dil=dil, nsub=nsub),
        grid=(bsz, nq),
        in_specs=[pl.BlockSpec((1, dil, tq, gw), lambda b, i: (b, 0, i, 0)), prev(1), prev(2)],
        out_specs=[out_spec, out_spec],
        out_shape=[out_shape, out_shape],
        scratch_shapes=[pltpu.VMEM((nh, tq, HEAD_DIM), F32), pltpu.VMEM((nh, tq, HEAD_DIM), F32)],
        compiler_params=_cparams(("arbitrary", "arbitrary")),
        name=f"dilated_d{dil}",
    )(qkv, qkv, qkv)


def _sb_body(q_ref, k_ref, v_ref, o_ref, acc_ref, car_ref):
    nblk = q_ref.shape[1] // Q_BLOCK
    r = lax.broadcasted_iota(I32, (Q_BLOCK, Q_BLOCK), 0)
    c = lax.broadcasted_iota(I32, (Q_BLOCK, Q_BLOCK), 1)
    causal = c < r
    rr = lax.broadcasted_iota(I32, (Q_BLOCK, 2 * Q_BLOCK), 0)
    cc = lax.broadcasted_iota(I32, (Q_BLOCK, 2 * Q_BLOCK), 1)
    uo = jnp.where(cc >= Q_BLOCK, 1.0, jnp.where(rr > cc, 1.0, 0.0)).astype(BF16)

    nh = acc_ref.shape[0]
    heads = range(nh)

    def tiles(qs, kb, carries, diag):
        k0 = pl.multiple_of(kb * Q_BLOCK, Q_BLOCK)
        hs = [slice(h * HEAD_DIM, (h + 1) * HEAD_DIM) for h in heads]
        zs = [lax.dot_general(qs[h], k_ref[0, pl.ds(k0, Q_BLOCK), hs[h]], (((1,), (1,)), ((), ())),
                              preferred_element_type=F32) * ATTN_SCALE for h in heads]
        stacked, log_s = [], []
        for z in zs:
            sp = jnp.log(1.0 + jnp.exp(-jnp.abs(z)))
            mx = jnp.maximum(z, 0.0)
            log_1m = -(mx + sp)
            if diag:
                log_1m = jnp.where(causal, log_1m, 0.0)
            hi = log_1m.astype(BF16)
            lo = (log_1m - hi.astype(F32)).astype(BF16)
            stacked.append(jnp.concatenate([hi, lo], axis=0))
            log_s.append((z - mx) - sp)
        r2s = [jnp.dot(s, uo, preferred_element_type=F32) for s in stacked]
        probs, new_carries = [], []
        for h in heads:
            sums = r2s[h][:Q_BLOCK] + r2s[h][Q_BLOCK:]
            a = jnp.exp(log_s[h] + carries[h] + sums[:, :Q_BLOCK])
            if diag:
                a = jnp.where(causal, a, 0.0)
            probs.append(a.astype(BF16))
            new_carries.append(carries[h] + sums[:, Q_BLOCK:])
        pvs = [jnp.dot(probs[h], v_ref[0, pl.ds(k0, Q_BLOCK), hs[h]], preferred_element_type=F32) for h in heads]
        return pvs, new_carries

    def all_max(xs):
        m = xs[0]
        for x in xs[1:]:
            m = jnp.maximum(m, x)
        return jnp.max(m)

    def qblock(qi, _):
        q0 = pl.multiple_of(qi * Q_BLOCK, Q_BLOCK)
        qs = [q_ref[0, pl.ds(q0, Q_BLOCK), h * HEAD_DIM:(h + 1) * HEAD_DIM] for h in heads]
        zero = jnp.zeros((Q_BLOCK, Q_BLOCK), F32)
        pvs, cars = tiles(qs, qi, [zero] * nh, True)
        for h in heads:
            acc_ref[h] = pvs[h]
            car_ref[h] = cars[h]

        def cond(st):
            return jnp.logical_and(st[0] >= 0, st[1] > SB_DEAD)

        def body(st):
            pvs, cars = tiles(qs, st[0], [car_ref[h] for h in heads], False)
            for h in heads:
                acc_ref[h] += pvs[h]
                car_ref[h] = cars[h]
            return st[0] - 1, all_max(cars)

        lax.while_loop(cond, body, (qi - 1, all_max(cars)))
        for h in range(nh):
            o_ref[0, pl.ds(q0, Q_BLOCK), h * HEAD_DIM:(h + 1) * HEAD_DIM] = acc_ref[h].astype(o_ref.dtype)
        return 0

    lax.fori_loop(0, nblk, qblock, 0)


SB_HEADS_PER_STEP = 4


def _stick_breaking(proj, bsz, seq):
    width = proj.shape[1]
    pv = proj.reshape(bsz, seq, width)
    nh = SB_HEADS_PER_STEP
    bw = nh * HEAD_DIM
    base = N_GATE // bw

    def spec(off):
        return pl.BlockSpec((1, seq, bw), lambda b, h: (b, 0, base + off + h))

    nstep = N_HEADS_SB // nh
    o = pl.pallas_call(
        _sb_body,
        grid=(bsz, nstep),
        in_specs=[spec(0), spec(nstep), spec(2 * nstep)],
        out_specs=pl.BlockSpec((1, seq, bw), lambda b, h: (b, 0, h)),
        out_shape=jax.ShapeDtypeStruct((bsz, seq, WIDTH_SB), BF16),
        scratch_shapes=[pltpu.VMEM((nh, Q_BLOCK, Q_BLOCK), F32), pltpu.VMEM((nh, Q_BLOCK, Q_BLOCK), F32)],
        compiler_params=_cparams(("arbitrary", "arbitrary")),
        name="stick_breaking",
    )(pv, pv, pv)
    return o.reshape(bsz * seq, WIDTH_SB)


def _mixout_body(o1, o2, o3, l1, l2, l3, osb, gd_ref, gs_ref, x_ref, mod_ref, gpost, gpre,
                 wd, ws, wm, x1_ref, h2_ref):
    heads = []
    for h in range(HEADS_PER_GROUP):
        la, lb, lc = l1[h], l2[h], l3[h]
        m = jnp.maximum(la, jnp.maximum(lb, lc))
        ea, eb, ec = jnp.exp(la - m), jnp.exp(lb - m), jnp.exp(lc - m)
        heads.append(((ea * o1[h] + eb * o2[h] + ec * o3[h]) / (ea + eb + ec)).astype(BF16))
    yd = jnp.dot(jnp.concatenate(heads, axis=1), wd[...], preferred_element_type=F32)
    ys = jnp.dot(osb[...], ws[...], preferred_element_type=F32)
    mix = jax.nn.sigmoid(gd_ref[...].astype(F32)) * yd + jax.nn.sigmoid(gs_ref[...].astype(F32)) * ys
    y = jnp.dot(mix.astype(BF16), wm[...], preferred_element_type=F32)
    gate1 = mod_ref[0, 2:3, :]
    shift2 = mod_ref[0, 3:4, :]
    scale2 = mod_ref[0, 4:5, :]
    x1 = x_ref[...] + gate1 * (_rms(y) * gpost[...])
    x1_ref[...] = x1
    h2_ref[...] = ((_rms(x1) * gpre[...]) * (1.0 + scale2) + shift2).astype(h2_ref.dtype)


def _const_spec(shape):
    return pl.BlockSpec(shape, lambda i: (0,) * len(shape), pipeline_mode=pl.Buffered(1))


def _mixout(o_dil, lse_dil, o_sb, proj, xf, mod3, g_post, g_pre, wd_b, ws_b, wm_b, seq):
    n, d = xf.shape
    tm = 256
    per_b = seq // tm
    row = lambda w: pl.BlockSpec((tm, w), lambda i: (i, 0))
    head_major = pl.BlockSpec((HEADS_PER_GROUP, tm, HEAD_DIM), lambda i: (0, i, 0))
    in_specs = (
        [head_major] * 6 + [row(WIDTH_SB)]
        + [pl.BlockSpec((tm, d), lambda i: (i, 0)), pl.BlockSpec((tm, d), lambda i: (i, 1))]
        + [row(d), pl.BlockSpec((1, N_MOD, d), lambda i: (i // per_b, 0, 0))]
        + [_const_spec((1, d)), _const_spec((1, d))]
        + [_const_spec(wd_b.shape), _const_spec(ws_b.shape), _const_spec(wm_b.shape)]
    )
    return pl.pallas_call(
        _mixout_body,
        grid=(n // tm,),
        in_specs=in_specs,
        out_specs=[row(d), row(d)],
        out_shape=[jax.ShapeDtypeStruct((n, d), F32), jax.ShapeDtypeStruct((n, d), BF16)],
        compiler_params=_cparams(("arbitrary",)),
        name="mixout",
    )(*o_dil, *lse_dil, o_sb, proj, proj, xf, mod3, g_post.reshape(1, d), g_pre.reshape(1, d), wd_b, ws_b, wm_b)


def _topk_rows(x, k, iota0):
    big = x.shape[0]
    out = []
    for _ in range(k):
        m = jnp.max(x, axis=0, keepdims=True)
        i = jnp.min(jnp.where(x == m, iota0, big), axis=0, keepdims=True)
        out.append((m, i))
        x = jnp.where(iota0 == i, -jnp.inf, x)
    return out


def _router_body(h_ref, wr_ref, bias_ref, idx_ref, wt_ref):
    tm = h_ref.shape[0]
    logits = lax.dot_general(wr_ref[...], h_ref[...], (((1,), (1,)), ((), ())), preferred_element_type=F32)
    scores = jax.nn.sigmoid(logits)
    sel = scores + bias_ref[...]
    sub = lax.broadcasted_iota(I32, (GROUP_SIZE, tm), 0)
    grp = []
    for g in range(N_GROUPS):
        (m1, _), (m2, _) = _topk_rows(sel[g * GROUP_SIZE:(g + 1) * GROUP_SIZE], 2, sub)
        grp.append(m1 + m2)
    gscore = jnp.concatenate(grp, axis=0)
    giota = lax.broadcasted_iota(I32, (N_GROUPS, tm), 0)
    gmask = jnp.zeros((N_GROUPS, tm), F32)
    for _, gi in _topk_rows(gscore, TOPK_GROUPS, giota):
        gmask = jnp.where(giota == gi, 1.0, gmask)
    masked = jnp.concatenate(
        [jnp.where(gmask[g:g + 1] > 0.0, sel[g * GROUP_SIZE:(g + 1) * GROUP_SIZE], -jnp.inf)
         for g in range(N_GROUPS)], axis=0)
    eiota = lax.broadcasted_iota(I32, (N_EXPERTS, tm), 0)
    picks = _topk_rows(masked, TOP_K, eiota)
    idx = jnp.concatenate([i for _, i in picks], axis=0)
    top_s = jnp.concatenate(
        [jnp.sum(jnp.where(eiota == i, scores, 0.0), axis=0, keepdims=True) for _, i in picks], axis=0)
    top_w = top_s / jnp.sum(top_s, axis=0, keepdims=True) * ROUTED_SCALE
    idx_ref[...] = idx
    wpad = jnp.concatenate([top_w, jnp.zeros((LANES - TOP_K, tm), F32)], axis=0)
    wt_ref[...] = wpad.T


def _router(h2, w_router, router_bias):
    n, d = h2.shape
    tm = 512
    return pl.pallas_call(
        _router_body,
        grid=(n // tm,),
        in_specs=[
            pl.BlockSpec((tm, d), lambda i: (i, 0)),
            pl.BlockSpec((N_EXPERTS, d), lambda i: (0, 0)),
            pl.BlockSpec((N_EXPERTS, 1), lambda i: (0, 0)),
        ],
        out_specs=[pl.BlockSpec((TOP_K, tm), lambda i: (0, i)), pl.BlockSpec((tm, LANES), lambda i: (i, 0))],
        out_shape=[jax.ShapeDtypeStruct((TOP_K, n), I32), jax.ShapeDtypeStruct((n, LANES), F32)],
        compiler_params=_cparams(("arbitrary",)),
        name="router",
    )(h2, w_router.T.astype(BF16), router_bias.reshape(N_EXPERTS, 1))


def _plan_tile(idx):
    tm = idx.shape[1]
    eiota = lax.broadcasted_iota(I32, (N_EXPERTS, tm), 0)
    hit = jnp.zeros((N_EXPERTS, tm), F32)
    for k in range(TOP_K):
        hit = jnp.where(eiota == idx[k:k + 1], 1.0, hit)
    r = lax.broadcasted_iota(I32, (tm, tm), 0)
    c = lax.broadcasted_iota(I32, (tm, tm), 1)
    before = jnp.where(r < c, 1.0, 0.0).astype(BF16)
    excl = jnp.dot(hit.astype(BF16), before, preferred_element_type=F32)
    tot = excl[:, tm - 1:tm] + hit[:, tm - 1:tm]
    return eiota, hit, excl, tot


def _plan_counts_body(idx_ref, cnt_ref):
    @pl.when(pl.program_id(0) == 0)
    def _():
        cnt_ref[...] = jnp.zeros(cnt_ref.shape, F32)

    _, _, _, tot = _plan_tile(idx_ref[...])
    cnt_ref[...] += tot


def _plan_counts(top_idx):
    n = top_idx.shape[1]
    tm = 512
    return pl.pallas_call(
        _plan_counts_body,
        grid=(n // tm,),
        in_specs=[pl.BlockSpec((TOP_K, tm), lambda i: (0, i))],
        out_specs=pl.BlockSpec((N_EXPERTS, LANES), lambda i: (0, 0)),
        out_shape=jax.ShapeDtypeStruct((N_EXPERTS, LANES), F32),
        compiler_params=_cparams(("arbitrary",)),
        name="plan_counts",
    )(top_idx)


def _plan_dest_body(idx_ref, base_ref, dest_ref, run_ref):
    @pl.when(pl.program_id(0) == 0)
    def _():
        run_ref[...] = base_ref[...]

    idx = idx_ref[...]
    eiota, _, excl, tot = _plan_tile(idx)
    pos = run_ref[:, 0:1] + excl
    rows = [jnp.sum(jnp.where(eiota == idx[k:k + 1], pos, 0.0), axis=0, keepdims=True) for k in range(TOP_K)]
    dest_ref[...] = jnp.concatenate(rows, axis=0).astype(I32)
    run_ref[...] += tot


def _plan_dest(top_idx, pstart):
    n = top_idx.shape[1]
    tm = 512
    base = jnp.broadcast_to(pstart.astype(F32).reshape(N_EXPERTS, 1), (N_EXPERTS, LANES))
    return pl.pallas_call(
        _plan_dest_body,
        grid=(n // tm,),
        in_specs=[pl.BlockSpec((TOP_K, tm), lambda i: (0, i)), pl.BlockSpec((N_EXPERTS, LANES), lambda i: (0, 0))],
        out_specs=pl.BlockSpec((TOP_K, tm), lambda i: (0, i)),
        out_shape=jax.ShapeDtypeStruct((TOP_K, n), I32),
        scratch_shapes=[pltpu.VMEM((N_EXPERTS, LANES), F32)],
        compiler_params=_cparams(("arbitrary",)),
        name="plan_dest",
    )(top_idx, base)


def _capacity(n):
    return n * TOP_K + N_EXPERTS * ROW_BLOCK


def _pack_halves(y):
    w = y.shape[1] // 2
    hi = pltpu.bitcast(y[:, :w].astype(BF16).astype(F32), U32)
    lo = pltpu.bitcast(y[:, w:].astype(BF16).astype(F32), U32)
    return hi | (lo >> 16)


def _unpack_halves(p):
    a = pltpu.bitcast(p & jnp.uint32(0xFFFF0000), F32)
    b = pltpu.bitcast(p << 16, F32)
    return a, b


TOKENS_PER_ISSUE = 2


def _row_dma_loops(row_copy):
    def issue(i, _):
        for u in range(TOKENS_PER_ISSUE):
            for k in range(TOP_K):
                row_copy(i * TOKENS_PER_ISSUE + u, k).start(priority=k % 2)
        return 0

    def drain(i, _):
        for u in range(TOKENS_PER_ISSUE):
            for k in range(TOP_K):
                row_copy(i * TOKENS_PER_ISSUE + u, k).wait()
        return 0

    return issue, drain


def _pad_chunks():
    sizes, s = [], ROW_BLOCK // 2
    while s >= SUBLANES:
        sizes.append(s)
        s //= 2
    return sizes


def _dispatch_body(pstart_ref, cnt_ref, dest_ref, h_ref, xs_ref, xbuf, zbuf, sem, zsem):
    tm = h_ref.shape[0]
    step = pl.program_id(0)

    def pad_dmas(wait):
        def per_expert(e, _):
            cnt = cnt_ref[e]
            pad = (ROW_BLOCK - cnt % ROW_BLOCK) % ROW_BLOCK
            start = pstart_ref[e] + cnt
            head = (SUBLANES - start % SUBLANES) % SUBLANES
            for r in range(SUBLANES - 1):
                @pl.when(r < head)
                def _():
                    cp = pltpu.make_async_copy(zbuf.at[pl.ds(0, 1)], xs_ref.at[pl.ds(start + r, 1)], zsem)
                    cp.wait() if wait else cp.start()
            body = start + head
            rem = pad - head
            for size in _pad_chunks():
                @pl.when((rem & size) != 0)
                def _():
                    off = pl.multiple_of(body + (rem & ~(2 * size - 1)), SUBLANES)
                    cp = pltpu.make_async_copy(zbuf.at[pl.ds(0, size)], xs_ref.at[pl.ds(off, size)], zsem)
                    cp.wait() if wait else cp.start()
            return 0
        lax.fori_loop(0, N_EXPERTS, per_expert, 0)

    @pl.when(step == 0)
    def _():
        zbuf[...] = jnp.zeros(zbuf.shape, zbuf.dtype)
        pad_dmas(False)
        pad_dmas(True)

    xbuf[...] = _pack_halves(h_ref[...].astype(F32))

    def row_copy(t, k):
        return pltpu.make_async_copy(xbuf.at[pl.ds(t, 1)], xs_ref.at[pl.ds(dest_ref[k, t], 1)], sem)

    issue, drain = _row_dma_loops(row_copy)
    lax.fori_loop(0, tm // TOKENS_PER_ISSUE, issue, 0)
    lax.fori_loop(0, tm // TOKENS_PER_ISSUE, drain, 0)


def _dispatch(h2, dest, pstart, counts):
    n, d = h2.shape
    tm = 256
    w = d // 2
    grid_spec = pltpu.PrefetchScalarGridSpec(
        num_scalar_prefetch=2,
        grid=(n // tm,),
        in_specs=[
            pl.BlockSpec((TOP_K, tm), lambda i, *_: (0, i), memory_space=pltpu.SMEM),
            pl.BlockSpec((tm, d), lambda i, *_: (i, 0)),
        ],
        out_specs=pl.BlockSpec(memory_space=pl.ANY),
        scratch_shapes=[
            pltpu.VMEM((tm, w), U32),
            pltpu.VMEM((ROW_BLOCK // 2, w), U32),
            pltpu.SemaphoreType.DMA(()),
            pltpu.SemaphoreType.DMA(()),
        ],
    )
    return pl.pallas_call(
        _dispatch_body,
        grid_spec=grid_spec,
        out_shape=jax.ShapeDtypeStruct((_capacity(n), w), U32),
        compiler_params=_cparams(("arbitrary",), has_side_effects=True, disable_bounds_checks=True),
        name="dispatch",
    )(pstart, counts, dest, h2)


def _experts_body(be_ref, nu_ref, x_ref, wg_ref, wu_ref, wd_ref, y_ref, wg_b, wu_b, wd_b):
    i = pl.program_id(0)
    used = i < nu_ref[0]
    prev = be_ref[jnp.maximum(i - 1, 0)]
    fresh = jnp.logical_or(i == 0, be_ref[i] != prev)

    @pl.when(jnp.logical_and(used, fresh))
    def _():
        wg_b[...] = wg_ref[0].astype(BF16)
        wu_b[...] = wu_ref[0].astype(BF16)
        wd_b[...] = wd_ref[0].astype(BF16)

    @pl.when(used)
    def _():
        half = x_ref.shape[1]
        xa, xb = _unpack_halves(x_ref[...])
        xa = xa.astype(BF16)
        xb = xb.astype(BF16)
        g = (jnp.dot(xa, wg_b[:half], preferred_element_type=F32)
             + jnp.dot(xb, wg_b[half:], preferred_element_type=F32))
        u = (jnp.dot(xa, wu_b[:half], preferred_element_type=F32)
             + jnp.dot(xb, wu_b[half:], preferred_element_type=F32))
        hmid = (g * jax.nn.sigmoid(g) * u).astype(BF16)
        y_ref[...] = _pack_halves(jnp.dot(hmid, wd_b[...], preferred_element_type=F32))


def _experts(xs, block_e, n_used, w_gate_e, w_up_e, w_down_e):
    cap, w = xs.shape
    d = 2 * w
    nblk = cap // ROW_BLOCK

    def row_map(i, be, nu):
        return (jnp.minimum(i, nu[0] - 1), 0)

    def w_map(i, be, nu):
        return (be[i], 0, 0)

    grid_spec = pltpu.PrefetchScalarGridSpec(
        num_scalar_prefetch=2,
        grid=(nblk,),
        in_specs=[
            pl.BlockSpec((ROW_BLOCK, w), row_map),
            pl.BlockSpec((1, d, D_EXPERT), w_map),
            pl.BlockSpec((1, d, D_EXPERT), w_map),
            pl.BlockSpec((1, D_EXPERT, d), w_map),
        ],
        out_specs=pl.BlockSpec((ROW_BLOCK, w), row_map),
        scratch_shapes=[
            pltpu.VMEM((d, D_EXPERT), BF16),
            pltpu.VMEM((d, D_EXPERT), BF16),
            pltpu.VMEM((D_EXPERT, d), BF16),
        ],
    )
    return pl.pallas_call(
        _experts_body,
        grid_spec=grid_spec,
        out_shape=jax.ShapeDtypeStruct((cap, w), U32),
        compiler_params=_cparams(("arbitrary",)),
        name="experts",
    )(block_e, n_used, xs, w_gate_e, w_up_e, w_down_e)


def _combine_body(dest_ref, ys_ref, wt_ref, h_ref, x1_ref, mod_ref, gpost, wg, wu, wd, o_ref, buf, sem):
    tm = h_ref.shape[0]

    def row_copy(t, k):
        return pltpu.make_async_copy(ys_ref.at[pl.ds(dest_ref[k, t], 1)], buf.at[k, pl.ds(t, 1)], sem)

    issue, drain = _row_dma_loops(row_copy)
    lax.fori_loop(0, tm // TOKENS_PER_ISSUE, issue, 0)

    h = h_ref[...]
    g = jnp.dot(h, wg[...], preferred_element_type=F32)
    u = jnp.dot(h, wu[...], preferred_element_type=F32)
    shared = jnp.dot((g * jax.nn.sigmoid(g) * u).astype(BF16), wd[...], preferred_element_type=F32)

    lax.fori_loop(0, tm // TOKENS_PER_ISSUE, drain, 0)

    half = buf.shape[2]
    wt = wt_ref[...]
    ya = shared[:, :half]
    yb = shared[:, half:]
    for k in range(TOP_K):
        a, b = _unpack_halves(buf[k])
        ya = ya + wt[:, k:k + 1] * a
        yb = yb + wt[:, k:k + 1] * b
    y = jnp.concatenate([ya, yb], axis=1)
    gate2 = mod_ref[0, 5:6, :]
    o_ref[...] = x1_ref[...] + gate2 * (_rms(y) * gpost[...])


def _combine(ys, dest, wt, h2, x1, mod3, g_post, wg_b, wu_b, wd_b, seq):
    n, d = h2.shape
    tm = 256
    w = d // 2
    per_b = seq // tm
    const = lambda shape: pl.BlockSpec(shape, lambda i, *_: (0,) * len(shape), pipeline_mode=pl.Buffered(1))
    grid_spec = pltpu.PrefetchScalarGridSpec(
        num_scalar_prefetch=0,
        grid=(n // tm,),
        in_specs=[
            pl.BlockSpec((TOP_K, tm), lambda i: (0, i), memory_space=pltpu.SMEM),
            pl.BlockSpec(memory_space=pl.ANY),
            pl.BlockSpec((tm, LANES), lambda i: (i, 0)),
            pl.BlockSpec((tm, d), lambda i: (i, 0)),
            pl.BlockSpec((tm, d), lambda i: (i, 0)),
            pl.BlockSpec((1, N_MOD, d), lambda i: (i // per_b, 0, 0)),
            const((1, d)), const(wg_b.shape), const(wu_b.shape), const(wd_b.shape),
        ],
        out_specs=pl.BlockSpec((tm, d), lambda i: (i, 0)),
        scratch_shapes=[pltpu.VMEM((TOP_K, tm, w), U32), pltpu.SemaphoreType.DMA(())],
    )
    return pl.pallas_call(
        _combine_body,
        grid_spec=grid_spec,
        out_shape=jax.ShapeDtypeStruct((n, d), F32),
        compiler_params=_cparams(("arbitrary",), disable_bounds_checks=True),
        name="combine",
    )(dest, ys, wt, h2, x1, mod3, g_post.reshape(1, d), wg_b, wu_b, wd_b)


def _layer(x, mod, positions, g_pre_mix, g_post_mix, g_pre_ffn, g_post_ffn, w_in, w_dil_out, w_sb_out, w_mix_out,
           w_router, router_bias, w_gate_e, w_up_e, w_down_e, w_gate_s, w_up_s, w_down_s):
    bsz, seq, d = x.shape
    n = bsz * seq
    xf = x.reshape(n, d)
    mod3 = mod.reshape(bsz, N_MOD, d)

    nd, nq = 3 * WIDTH_DIL, 3 * (WIDTH_DIL + WIDTH_SB)
    w_plain = jnp.concatenate([w_in[:, nq:], w_in[:, nd:nq]], axis=1).astype(BF16)
    cols = []
    for g in range(len(DIL_PATTERNS)):
        for part in range(3):
            lo = part * WIDTH_DIL + g * D_DIL_OUT
            cols.append(w_in[:, lo:lo + D_DIL_OUT])
    w_dil = jnp.concatenate(cols, axis=1).astype(BF16)

    proj = _inproj(xf, g_pre_mix, mod3, w_plain, seq)
    tables = _rope_tables(positions)
    qkv_dil = _inproj_dil(xf, g_pre_mix, mod3, w_dil, tables, bsz, seq)
    o_dil, lse_dil = [], []
    for g, (window, dilation) in enumerate(DIL_PATTERNS):
        assert window // dilation == Q_BLOCK
        o, lse = _dilated_attention(qkv_dil[g], g, bsz, seq)
        o_dil.append(o)
        lse_dil.append(lse)
    o_sb = _stick_breaking(proj, bsz, seq)

    x1, h2 = _mixout(o_dil, lse_dil, o_sb, proj, xf, mod3, g_post_mix, g_pre_ffn,
                     w_dil_out.astype(BF16), w_sb_out.astype(BF16), w_mix_out.astype(BF16), seq)

    top_idx, top_wt = _router(h2, w_router, router_bias)
    counts = _plan_counts(top_idx)[:, 0].astype(I32)
    padded = (counts + ROW_BLOCK - 1) // ROW_BLOCK * ROW_BLOCK
    pends = jnp.cumsum(padded)
    pstart = pends - padded
    nblk = _capacity(n) // ROW_BLOCK
    n_used = (pends[-1] // ROW_BLOCK).astype(I32)
    blk = jnp.minimum(jnp.arange(nblk, dtype=I32), n_used - 1)
    block_e = jnp.minimum(jnp.sum(pends[None, :] <= (blk * ROW_BLOCK)[:, None], axis=1), N_EXPERTS - 1).astype(I32)
    dest = _plan_dest(top_idx, pstart)

    xs = _dispatch(h2, dest, pstart.astype(I32), counts)
    ys = _experts(xs, block_e, n_used.reshape(1), w_gate_e, w_up_e, w_down_e)
    out = _combine(ys, dest, top_wt, h2, x1, mod3, g_post_ffn,
                   w_gate_s.astype(BF16), w_up_s.astype(BF16), w_down_s.astype(BF16), seq)
    return out.reshape(bsz, seq, d)


def kernel(x, c, positions, w_ada, b_ada, g_pre_mix, g_post_mix, g_pre_ffn, g_post_ffn, w_in, w_dil_out,
           w_sb_out, w_mix_out, w_router, router_bias, w_gate_e, w_up_e, w_down_e, w_gate_s, w_up_s, w_down_s):
    for l in range(w_ada.shape[0]):
        mod = _adaln(c, w_ada[l], b_ada[l])
        x = _layer(x, mod, positions, g_pre_mix[l], g_post_mix[l], g_pre_ffn[l], g_post_ffn[l], w_in[l],
                   w_dil_out[l], w_sb_out[l], w_mix_out[l], w_router[l], router_bias[l],
                   w_gate_e[l], w_up_e[l], w_down_e[l], w_gate_s[l], w_up_s[l], w_down_s[l])
    return x
```

```python
import functools

import jax
import jax.numpy as jnp
from jax import lax
from jax.experimental import pallas as pl
from jax.experimental.pallas import tpu as pltpu

F32 = jnp.float32
BF16 = jnp.bfloat16
I32 = jnp.int32
U32 = jnp.uint32

D_MODEL = 2048
HEAD_DIM = 128
DIL_PATTERNS = ((128, 1), (512, 4), (2048, 16))
HEADS_PER_GROUP = 4
N_HEADS_DIL = 12
N_HEADS_SB = 8
WIDTH_DIL = N_HEADS_DIL * HEAD_DIM
WIDTH_SB = N_HEADS_SB * HEAD_DIM
D_DIL_OUT = HEADS_PER_GROUP * HEAD_DIM
Q_BLOCK = 128
ROPE_THETA = 500000.0
ROPE_DIM = HEAD_DIM // 4
N_GATE = 2 * D_MODEL
N_EXPERTS = 64
TOP_K = 8
N_GROUPS = 8
GROUP_SIZE = N_EXPERTS // N_GROUPS
TOPK_GROUPS = 4
D_EXPERT = 512
D_SHARED = 512
ROUTED_SCALE = 2.5
RMS_EPS = 1e-6
N_MOD = 6
ATTN_SCALE = HEAD_DIM ** -0.5

LANES = 128
SUBLANES = 8
VMEM_LIMIT = 56 * 1024 * 1024

ROW_BLOCK = 256
SB_DEAD = -110.0


def _cparams(sem, **kw):
    return pltpu.CompilerParams(dimension_semantics=sem, vmem_limit_bytes=VMEM_LIMIT, **kw)


def _adaln_body(ct_ref, w_ref, b_ref, o_ref, *, kc):
    nb = ct_ref.shape[1]
    nk = w_ref.shape[0] // kc

    def step(i, acc):
        k0 = pl.multiple_of(i * kc, kc)
        w = w_ref[pl.ds(k0, kc), :]
        c = ct_ref[pl.ds(k0, kc), :]
        s = c * jax.nn.sigmoid(c)
        parts = [jnp.sum(w * s[:, b:b + 1], axis=0, keepdims=True) for b in range(nb)]
        return acc + jnp.concatenate(parts, axis=0)

    acc = lax.fori_loop(0, nk, step, jnp.zeros(o_ref.shape, F32))
    o_ref[...] = acc + b_ref[...]


def _adaln(c, w_ada, b_ada):
    nb, d = c.shape
    n_out = w_ada.shape[1]
    tn = 1024
    return pl.pallas_call(
        functools.partial(_adaln_body, kc=256),
        grid=(n_out // tn,),
        in_specs=[
            pl.BlockSpec((d, nb), lambda j: (0, 0)),
            pl.BlockSpec((d, tn), lambda j: (0, j)),
            pl.BlockSpec((1, tn), lambda j: (0, j)),
        ],
        out_specs=pl.BlockSpec((nb, tn), lambda j: (0, j)),
        out_shape=jax.ShapeDtypeStruct((nb, n_out), F32),
        compiler_params=_cparams(("arbitrary",)),
        name="adaln",
    )(c.T, w_ada, b_ada.reshape(1, n_out))


def _rms(x):
    return x * lax.rsqrt(jnp.mean(x * x, axis=-1, keepdims=True) + RMS_EPS)


def _prenorm(x_ref, g_ref, mod_ref):
    y = _rms(x_ref[...]) * g_ref[...]
    shift = mod_ref[0, 0:1, :]
    scale = mod_ref[0, 1:2, :]
    return (y * (1.0 + scale) + shift).astype(BF16)


def _inproj_body(x_ref, g_ref, mod_ref, w_ref, o_ref, h_ref):
    @pl.when(pl.program_id(1) == 0)
    def _():
        h_ref[...] = _prenorm(x_ref, g_ref, mod_ref)

    o_ref[...] = jnp.dot(h_ref[...], w_ref[...], preferred_element_type=F32).astype(o_ref.dtype)


def _inproj(xf, g_pre, mod3, w_b, seq):
    n, d = xf.shape
    width = w_b.shape[1]
    tm, tn = 512, width // 4
    per_b = seq // tm
    return pl.pallas_call(
        _inproj_body,
        grid=(n // tm, width // tn),
        in_specs=[
            pl.BlockSpec((tm, d), lambda i, j: (i, 0)),
            pl.BlockSpec((1, d), lambda i, j: (0, 0)),
            pl.BlockSpec((1, N_MOD, d), lambda i, j: (i // per_b, 0, 0)),
            pl.BlockSpec((d, tn), lambda i, j: (0, j)),
        ],
        out_specs=pl.BlockSpec((tm, tn), lambda i, j: (i, j)),
        out_shape=jax.ShapeDtypeStruct((n, width), BF16),
        scratch_shapes=[pltpu.VMEM((tm, d), BF16)],
        compiler_params=_cparams(("arbitrary", "arbitrary")),
        name="inproj",
    )(xf, g_pre.reshape(1, d), mod3, w_b)


def _inproj_dil_body(x_ref, g_ref, mod_ref, w_ref, t_ref, o0, o1, o2, res_ref):
    tm = x_ref.shape[0]
    h = _prenorm(x_ref, g_ref, mod_ref)
    t = t_ref[...]
    gw = 3 * D_DIL_OUT
    for gi, o_ref in enumerate((o0, o1, o2)):
        dil = DIL_PATTERNS[gi][1]
        res = jnp.dot(h, w_ref[:, gi * gw:(gi + 1) * gw], preferred_element_type=F32)
        for hs in range(3 * HEADS_PER_GROUP):
            sl = slice(hs * HEAD_DIM, (hs + 1) * HEAD_DIM)
            res_ref[hs] = _apply_rope(res[:, sl], t) if hs < 2 * HEADS_PER_GROUP else res[:, sl]
        for r in range(dil):
            for hs in range(3 * HEADS_PER_GROUP):
                rows = res_ref[hs] if dil == 1 else res_ref[hs, pl.ds(r, tm // dil, stride=dil), :]
                o_ref[0, r, :, hs * HEAD_DIM:(hs + 1) * HEAD_DIM] = rows.astype(o_ref.dtype)


def _inproj_dil(xf, g_pre, mod3, w_b, tables, bsz, seq):
    n, d = xf.shape
    gw = 3 * D_DIL_OUT
    tm = 512
    per_b = seq // tm
    dils = [p[1] for p in DIL_PATTERNS]
    return pl.pallas_call(
        _inproj_dil_body,
        grid=(n // tm,),
        in_specs=[
            pl.BlockSpec((tm, d), lambda i: (i, 0)),
            pl.BlockSpec((1, d), lambda i: (0, 0)),
            pl.BlockSpec((1, N_MOD, d), lambda i: (i // per_b, 0, 0)),
            pl.BlockSpec(w_b.shape, lambda i: (0, 0), pipeline_mode=pl.Buffered(1)),
            pl.BlockSpec((tm, 3 * LANES), lambda i: (i, 0)),
        ],
        out_specs=[pl.BlockSpec((1, dl, tm // dl, gw), lambda i: (i // per_b, 0, i % per_b, 0)) for dl in dils],
        out_shape=[jax.ShapeDtypeStruct((bsz, dl, seq // dl, gw), BF16) for dl in dils],
        scratch_shapes=[pltpu.VMEM((gw // HEAD_DIM, tm, HEAD_DIM), F32)],
        compiler_params=_cparams(("arbitrary",)),
        name="inproj_dil",
    )(xf, g_pre.reshape(1, d), mod3, w_b, tables)


def _rope_body(pos_ref, f_ref, o_ref):
    ang = pos_ref[...].astype(F32) * f_ref[...]
    c = jnp.cos(ang)
    s = jnp.sin(ang)
    lane = lax.broadcasted_iota(I32, ang.shape, 1)
    half = ROPE_DIM // 2
    o_ref[:, 0:LANES] = c
    o_ref[:, LANES:2 * LANES] = jnp.where(lane >= half, s, 0.0)
    o_ref[:, 2 * LANES:3 * LANES] = jnp.where(lane < half, -s, 0.0)


def _rope_tables(positions):
    n = positions.size
    half = ROPE_DIM // 2
    inv_freq = ROPE_THETA ** (-jnp.arange(0, ROPE_DIM, 2, dtype=F32) / ROPE_DIM)
    f = jnp.concatenate([inv_freq, inv_freq, jnp.zeros((LANES - 2 * half,), F32)]).reshape(1, LANES)
    tm = 2048
    return pl.pallas_call(
        _rope_body,
        grid=(n // tm,),
        in_specs=[pl.BlockSpec((tm, 1), lambda i: (i, 0)), pl.BlockSpec((1, LANES), lambda i: (0, 0))],
        out_specs=pl.BlockSpec((tm, 3 * LANES), lambda i: (i, 0)),
        out_shape=jax.ShapeDtypeStruct((n, 3 * LANES), F32),
        compiler_params=_cparams(("arbitrary",)),
        name="rope_tables",
    )(positions.reshape(n, 1), f)


def _apply_rope(x, t):
    half = ROPE_DIM // 2
    return (x * t[:, 0:LANES]
            + pltpu.roll(x, half, 1) * t[:, LANES:2 * LANES]
            + pltpu.roll(x, LANES - half, 1) * t[:, 2 * LANES:3 * LANES])


def _dil_body(cur_ref, kp_ref, vp_ref, o_ref, lse_ref, obuf, lbuf, *, dil, nsub):
    n = pl.program_id(1)
    tq = nsub * Q_BLOCK
    row = lax.broadcasted_iota(I32, (Q_BLOCK, 2 * Q_BLOCK), 0)
    col = lax.broadcasted_iota(I32, (Q_BLOCK, 2 * Q_BLOCK), 1)
    rel = row + Q_BLOCK - col
    band = jnp.where(rel >= 0, jnp.where(rel <= Q_BLOCK, 1.0, 0.0), 0.0)
    first = jnp.where(col >= Q_BLOCK, band, jnp.where(n > 0, band, 0.0))
    for r in range(dil):
        for h in range(HEADS_PER_GROUP):
            sl = slice(h * HEAD_DIM, (h + 1) * HEAD_DIM)
            ksl = slice(D_DIL_OUT + h * HEAD_DIM, D_DIL_OUT + (h + 1) * HEAD_DIM)
            vsl = slice(2 * D_DIL_OUT + h * HEAD_DIM, 2 * D_DIL_OUT + (h + 1) * HEAD_DIM)
            for j in range(nsub):
                rs = slice(j * Q_BLOCK, (j + 1) * Q_BLOCK)
                ps = slice((j - 1) * Q_BLOCK, j * Q_BLOCK)
                kprev = kp_ref[0, r, :, sl] if j == 0 else cur_ref[0, r, ps, ksl]
                vprev = vp_ref[0, r, :, sl] if j == 0 else cur_ref[0, r, ps, vsl]
                kcat = jnp.concatenate([kprev, cur_ref[0, r, rs, ksl]], axis=0)
                vcat = jnp.concatenate([vprev, cur_ref[0, r, rs, vsl]], axis=0)
                s = lax.dot_general(cur_ref[0, r, rs, sl], kcat, (((1,), (1,)), ((), ())),
                                    preferred_element_type=F32) * ATTN_SCALE
                s = jnp.where((first if j == 0 else band) > 0.0, s, -jnp.inf)
                m = jnp.max(s, axis=-1, keepdims=True)
                p = jnp.exp(s - m)
                l = jnp.sum(p, axis=-1, keepdims=True)
                o = jnp.dot((p / l).astype(BF16), vcat, preferred_element_type=F32)
                lse = jnp.broadcast_to(m + jnp.log(l), (Q_BLOCK, HEAD_DIM))
                if dil == 1:
                    o_ref[h, rs, :] = o
                    lse_ref[h, rs, :] = lse
                else:
                    obuf[h, rs, :] = o
                    lbuf[h, rs, :] = lse
            if dil > 1:
                o_ref[h, pl.ds(r, tq, stride=dil), :] = obuf[h]
                lse_ref[h, pl.ds(r, tq, stride=dil), :] = lbuf[h]


def _dilated_attention(qkv, g, bsz, seq):
    dil = DIL_PATTERNS[g][1]
    length = seq // dil
    tq = min(4 * Q_BLOCK, (16 * Q_BLOCK) // dil, length)
    nsub = tq // Q_BLOCK
    nq = length // tq
    gw = 3 * D_DIL_OUT
    n = bsz * seq

    def prev(colblk):
        return pl.BlockSpec((1, dil, Q_BLOCK, D_DIL_OUT),
                            lambda b, i: (b, 0, jnp.maximum(i * nsub - 1, 0), colblk))

    nh = HEADS_PER_GROUP
    out_spec = pl.BlockSpec((nh, tq * dil, HEAD_DIM), lambda b, i: (0, b * nq + i, 0))
    out_shape = jax.ShapeDtypeStruct((nh, n, HEAD_DIM), F32)
    return pl.pallas_call(
        functools.partial(_dil_body, dil=dil, nsub=nsub),
        grid=(bsz, nq),
        in_specs=[pl.BlockSpec((1, dil, tq, gw), lambda b, i: (b, 0, i, 0)), prev(1), prev(2)],
        out_specs=[out_spec, out_spec],
        out_shape=[out_shape, out_shape],
        scratch_shapes=[pltpu.VMEM((nh, tq, HEAD_DIM), F32), pltpu.VMEM((nh, tq, HEAD_DIM), F32)],
        compiler_params=_cparams(("arbitrary", "arbitrary")),
        name=f"dilated_d{dil}",
    )(qkv, qkv, qkv)


def _sb_body(q_ref, k_ref, v_ref, o_ref, acc_ref, car_ref):
    nblk = q_ref.shape[1] // Q_BLOCK
    r = lax.broadcasted_iota(I32, (Q_BLOCK, Q_BLOCK), 0)
    c = lax.broadcasted_iota(I32, (Q_BLOCK, Q_BLOCK), 1)
    causal = c < r
    rr = lax.broadcasted_iota(I32, (Q_BLOCK, 2 * Q_BLOCK), 0)
    cc = lax.broadcasted_iota(I32, (Q_BLOCK, 2 * Q_BLOCK), 1)
    uo = jnp.where(cc >= Q_BLOCK, 1.0, jnp.where(rr > cc, 1.0, 0.0)).astype(BF16)

    nh = acc_ref.shape[0]
    heads = range(nh)

    def tiles(qs, kb, carries, diag):
        k0 = pl.multiple_of(kb * Q_BLOCK, Q_BLOCK)
        hs = [slice(h * HEAD_DIM, (h + 1) * HEAD_DIM) for h in heads]
        zs = [lax.dot_general(qs[h], k_ref[0, pl.ds(k0, Q_BLOCK), hs[h]], (((1,), (1,)), ((), ())),
                              preferred_element_type=F32) * ATTN_SCALE for h in heads]
        stacked, log_s = [], []
        for z in zs:
            sp = jnp.log(1.0 + jnp.exp(-jnp.abs(z)))
            mx = jnp.maximum(z, 0.0)
            log_1m = -(mx + sp)
            if diag:
                log_1m = jnp.where(causal, log_1m, 0.0)
            hi = log_1m.astype(BF16)
            lo = (log_1m - hi.astype(F32)).astype(BF16)
            stacked.append(jnp.concatenate([hi, lo], axis=0))
            log_s.append((z - mx) - sp)
        r2s = [jnp.dot(s, uo, preferred_element_type=F32) for s in stacked]
        probs, new_carries = [], []
        for h in heads:
            sums = r2s[h][:Q_BLOCK] + r2s[h][Q_BLOCK:]
            a = jnp.exp(log_s[h] + carries[h] + sums[:, :Q_BLOCK])
            if diag:
                a = jnp.where(causal, a, 0.0)
            probs.append(a.astype(BF16))
            new_carries.append(carries[h] + sums[:, Q_BLOCK:])
        pvs = [jnp.dot(probs[h], v_ref[0, pl.ds(k0, Q_BLOCK), hs[h]], preferred_element_type=F32) for h in heads]
        return pvs, new_carries

    def all_max(xs):
        m = xs[0]
        for x in xs[1:]:
            m = jnp.maximum(m, x)
        return jnp.max(m)

    def qblock(qi, _):
        q0 = pl.multiple_of(qi * Q_BLOCK, Q_BLOCK)
        qs = [q_ref[0, pl.ds(q0, Q_BLOCK), h * HEAD_DIM:(h + 1) * HEAD_DIM] for h in heads]
        zero = jnp.zeros((Q_BLOCK, Q_BLOCK), F32)
        pvs, cars = tiles(qs, qi, [zero] * nh, True)
        for h in heads:
            acc_ref[h] = pvs[h]
            car_ref[h] = cars[h]

        def cond(st):
            return jnp.logical_and(st[0] >= 0, st[1] > SB_DEAD)

        def body(st):
            pvs, cars = tiles(qs, st[0], [car_ref[h] for h in heads], False)
            for h in heads:
                acc_ref[h] += pvs[h]
                car_ref[h] = cars[h]
            return st[0] - 1, all_max(cars)

        lax.while_loop(cond, body, (qi - 1, all_max(cars)))
        for h in range(nh):
            o_ref[0, pl.ds(q0, Q_BLOCK), h * HEAD_DIM:(h + 1) * HEAD_DIM] = acc_ref[h].astype(o_ref.dtype)
        return 0

    lax.fori_loop(0, nblk, qblock, 0)


SB_HEADS_PER_STEP = 4


def _stick_breaking(proj, bsz, seq):
    width = proj.shape[1]
    pv = proj.reshape(bsz, seq, width)
    nh = SB_HEADS_PER_STEP
    bw = nh * HEAD_DIM
    base = N_GATE // bw

    def spec(off):
        return pl.BlockSpec((1, seq, bw), lambda b, h: (b, 0, base + off + h))

    nstep = N_HEADS_SB // nh
    o = pl.pallas_call(
        _sb_body,
        grid=(bsz, nstep),
        in_specs=[spec(0), spec(nstep), spec(2 * nstep)],
        out_specs=pl.BlockSpec((1, seq, bw), lambda b, h: (b, 0, h)),
        out_shape=jax.ShapeDtypeStruct((bsz, seq, WIDTH_SB), BF16),
        scratch_shapes=[pltpu.VMEM((nh, Q_BLOCK, Q_BLOCK), F32), pltpu.VMEM((nh, Q_BLOCK, Q_BLOCK), F32)],
        compiler_params=_cparams(("arbitrary", "arbitrary")),
        name="stick_breaking",
    )(pv, pv, pv)
    return o.reshape(bsz * seq, WIDTH_SB)


def _mixout_body(o1, o2, o3, l1, l2, l3, osb, gd_ref, gs_ref, x_ref, mod_ref, gpost, gpre,
                 wd, ws, wm, x1_ref, h2_ref):
    heads = []
    for h in range(HEADS_PER_GROUP):
        la, lb, lc = l1[h], l2[h], l3[h]
        m = jnp.maximum(la, jnp.maximum(lb, lc))
        ea, eb, ec = jnp.exp(la - m), jnp.exp(lb - m), jnp.exp(lc - m)
        heads.append(((ea * o1[h] + eb * o2[h] + ec * o3[h]) / (ea + eb + ec)).astype(BF16))
    yd = jnp.dot(jnp.concatenate(heads, axis=1), wd[...], preferred_element_type=F32)
    ys = jnp.dot(osb[...], ws[...], preferred_element_type=F32)
    mix = jax.nn.sigmoid(gd_ref[...].astype(F32)) * yd + jax.nn.sigmoid(gs_ref[...].astype(F32)) * ys
    y = jnp.dot(mix.astype(BF16), wm[...], preferred_element_type=F32)
    gate1 = mod_ref[0, 2:3, :]
    shift2 = mod_ref[0, 3:4, :]
    scale2 = mod_ref[0, 4:5, :]
    x1 = x_ref[...] + gate1 * (_rms(y) * gpost[...])
    x1_ref[...] = x1
    h2_ref[...] = ((_rms(x1) * gpre[...]) * (1.0 + scale2) + shift2).astype(h2_ref.dtype)


def _const_spec(shape):
    return pl.BlockSpec(shape, lambda i: (0,) * len(shape), pipeline_mode=pl.Buffered(1))


def _mixout(o_dil, lse_dil, o_sb, proj, xf, mod3, g_post, g_pre, wd_b, ws_b, wm_b, seq):
    n, d = xf.shape
    tm = 256
    per_b = seq // tm
    row = lambda w: pl.BlockSpec((tm, w), lambda i: (i, 0))
    head_major = pl.BlockSpec((HEADS_PER_GROUP, tm, HEAD_DIM), lambda i: (0, i, 0))
    in_specs = (
        [head_major] * 6 + [row(WIDTH_SB)]
        + [pl.BlockSpec((tm, d), lambda i: (i, 0)), pl.BlockSpec((tm, d), lambda i: (i, 1))]
        + [row(d), pl.BlockSpec((1, N_MOD, d), lambda i: (i // per_b, 0, 0))]
        + [_const_spec((1, d)), _const_spec((1, d))]
        + [_const_spec(wd_b.shape), _const_spec(ws_b.shape), _const_spec(wm_b.shape)]
    )
    return pl.pallas_call(
        _mixout_body,
        grid=(n // tm,),
        in_specs=in_specs,
        out_specs=[row(d), row(d)],
        out_shape=[jax.ShapeDtypeStruct((n, d), F32), jax.ShapeDtypeStruct((n, d), BF16)],
        compiler_params=_cparams(("arbitrary",)),
        name="mixout",
    )(*o_dil, *lse_dil, o_sb, proj, proj, xf, mod3, g_post.reshape(1, d), g_pre.reshape(1, d), wd_b, ws_b, wm_b)


def _topk_rows(x, k, iota0):
    big = x.shape[0]
    out = []
    for _ in range(k):
        m = jnp.max(x, axis=0, keepdims=True)
        i = jnp.min(jnp.where(x == m, iota0, big), axis=0, keepdims=True)
        out.append((m, i))
        x = jnp.where(iota0 == i, -jnp.inf, x)
    return out


def _router_body(h_ref, wr_ref, bias_ref, idx_ref, wt_ref):
    tm = h_ref.shape[0]
    logits = lax.dot_general(wr_ref[...], h_ref[...], (((1,), (1,)), ((), ())), preferred_element_type=F32)
    scores = jax.nn.sigmoid(logits)
    sel = scores + bias_ref[...]
    sub = lax.broadcasted_iota(I32, (GROUP_SIZE, tm), 0)
    grp = []
    for g in range(N_GROUPS):
        (m1, _), (m2, _) = _topk_rows(sel[g * GROUP_SIZE:(g + 1) * GROUP_SIZE], 2, sub)
        grp.append(m1 + m2)
    gscore = jnp.concatenate(grp, axis=0)
    giota = lax.broadcasted_iota(I32, (N_GROUPS, tm), 0)
    gmask = jnp.zeros((N_GROUPS, tm), F32)
    for _, gi in _topk_rows(gscore, TOPK_GROUPS, giota):
        gmask = jnp.where(giota == gi, 1.0, gmask)
    masked = jnp.concatenate(
        [jnp.where(gmask[g:g + 1] > 0.0, sel[g * GROUP_SIZE:(g + 1) * GROUP_SIZE], -jnp.inf)
         for g in range(N_GROUPS)], axis=0)
    eiota = lax.broadcasted_iota(I32, (N_EXPERTS, tm), 0)
    picks = _topk_rows(masked, TOP_K, eiota)
    idx = jnp.concatenate([i for _, i in picks], axis=0)
    top_s = jnp.concatenate(
        [jnp.sum(jnp.where(eiota == i, scores, 0.0), axis=0, keepdims=True) for _, i in picks], axis=0)
    top_w = top_s / jnp.sum(top_s, axis=0, keepdims=True) * ROUTED_SCALE
    idx_ref[...] = idx
    wpad = jnp.concatenate([top_w, jnp.zeros((LANES - TOP_K, tm), F32)], axis=0)
    wt_ref[...] = wpad.T


def _router(h2, w_router, router_bias):
    n, d = h2.shape
    tm = 512
    return pl.pallas_call(
        _router_body,
        grid=(n // tm,),
        in_specs=[
            pl.BlockSpec((tm, d), lambda i: (i, 0)),
            pl.BlockSpec((N_EXPERTS, d), lambda i: (0, 0)),
            pl.BlockSpec((N_EXPERTS, 1), lambda i: (0, 0)),
        ],
        out_specs=[pl.BlockSpec((TOP_K, tm), lambda i: (0, i)), pl.BlockSpec((tm, LANES), lambda i: (i, 0))],
        out_shape=[jax.ShapeDtypeStruct((TOP_K, n), I32), jax.ShapeDtypeStruct((n, LANES), F32)],
        compiler_params=_cparams(("arbitrary",)),
        name="router",
    )(h2, w_router.T.astype(BF16), router_bias.reshape(N_EXPERTS, 1))


def _plan_tile(idx):
    tm = idx.shape[1]
    eiota = lax.broadcasted_iota(I32, (N_EXPERTS, tm), 0)
    hit = jnp.zeros((N_EXPERTS, tm), F32)
    for k in range(TOP_K):
        hit = jnp.where(eiota == idx[k:k + 1], 1.0, hit)
    r = lax.broadcasted_iota(I32, (tm, tm), 0)
    c = lax.broadcasted_iota(I32, (tm, tm), 1)
    before = jnp.where(r < c, 1.0, 0.0).astype(BF16)
    excl = jnp.dot(hit.astype(BF16), before, preferred_element_type=F32)
    tot = excl[:, tm - 1:tm] + hit[:, tm - 1:tm]
    return eiota, hit, excl, tot


def _plan_counts_body(idx_ref, cnt_ref):
    @pl.when(pl.program_id(0) == 0)
    def _():
        cnt_ref[...] = jnp.zeros(cnt_ref.shape, F32)

    _, _, _, tot = _plan_tile(idx_ref[...])
    cnt_ref[...] += tot


def _plan_counts(top_idx):
    n = top_idx.shape[1]
    tm = 512
    return pl.pallas_call(
        _plan_counts_body,
        grid=(n // tm,),
        in_specs=[pl.BlockSpec((TOP_K, tm), lambda i: (0, i))],
        out_specs=pl.BlockSpec((N_EXPERTS, LANES), lambda i: (0, 0)),
        out_shape=jax.ShapeDtypeStruct((N_EXPERTS, LANES), F32),
        compiler_params=_cparams(("arbitrary",)),
        name="plan_counts",
    )(top_idx)


def _plan_dest_body(idx_ref, base_ref, dest_ref, run_ref):
    @pl.when(pl.program_id(0) == 0)
    def _():
        run_ref[...] = base_ref[...]

    idx = idx_ref[...]
    eiota, _, excl, tot = _plan_tile(idx)
    pos = run_ref[:, 0:1] + excl
    rows = [jnp.sum(jnp.where(eiota == idx[k:k + 1], pos, 0.0), axis=0, keepdims=True) for k in range(TOP_K)]
    dest_ref[...] = jnp.concatenate(rows, axis=0).astype(I32)
    run_ref[...] += tot


def _plan_dest(top_idx, pstart):
    n = top_idx.shape[1]
    tm = 512
    base = jnp.broadcast_to(pstart.astype(F32).reshape(N_EXPERTS, 1), (N_EXPERTS, LANES))
    return pl.pallas_call(
        _plan_dest_body,
        grid=(n // tm,),
        in_specs=[pl.BlockSpec((TOP_K, tm), lambda i: (0, i)), pl.BlockSpec((N_EXPERTS, LANES), lambda i: (0, 0))],
        out_specs=pl.BlockSpec((TOP_K, tm), lambda i: (0, i)),
        out_shape=jax.ShapeDtypeStruct((TOP_K, n), I32),
        scratch_shapes=[pltpu.VMEM((N_EXPERTS, LANES), F32)],
        compiler_params=_cparams(("arbitrary",)),
        name="plan_dest",
    )(top_idx, base)


def _capacity(n):
    return n * TOP_K + N_EXPERTS * ROW_BLOCK


def _pack_halves(y):
    w = y.shape[1] // 2
    hi = pltpu.bitcast(y[:, :w].astype(BF16).astype(F32), U32)
    lo = pltpu.bitcast(y[:, w:].astype(BF16).astype(F32), U32)
    return hi | (lo >> 16)


def _unpack_halves(p):
    a = pltpu.bitcast(p & jnp.uint32(0xFFFF0000), F32)
    b = pltpu.bitcast(p << 16, F32)
    return a, b


TOKENS_PER_ISSUE = 2


def _row_dma_loops(row_copy):
    def issue(i, _):
        for u in range(TOKENS_PER_ISSUE):
            for k in range(TOP_K):
                row_copy(i * TOKENS_PER_ISSUE + u, k).start(priority=k % 2)
        return 0

    def drain(i, _):
        for u in range(TOKENS_PER_ISSUE):
            for k in range(TOP_K):
                row_copy(i * TOKENS_PER_ISSUE + u, k).wait()
        return 0

    return issue, drain


def _pad_chunks():
    sizes, s = [], ROW_BLOCK // 2
    while s >= SUBLANES:
        sizes.append(s)
        s //= 2
    return sizes


def _dispatch_body(pstart_ref, cnt_ref, dest_ref, h_ref, xs_ref, xbuf, zbuf, sem, zsem):
    tm = h_ref.shape[0]
    step = pl.program_id(0)

    def pad_dmas(wait):
        def per_expert(e, _):
            cnt = cnt_ref[e]
            pad = (ROW_BLOCK - cnt % ROW_BLOCK) % ROW_BLOCK
            start = pstart_ref[e] + cnt
            head = (SUBLANES - start % SUBLANES) % SUBLANES
            for r in range(SUBLANES - 1):
                @pl.when(r < head)
                def _():
                    cp = pltpu.make_async_copy(zbuf.at[pl.ds(0, 1)], xs_ref.at[pl.ds(start + r, 1)], zsem)
                    cp.wait() if wait else cp.start()
            body = start + head
            rem = pad - head
            for size in _pad_chunks():
                @pl.when((rem & size) != 0)
                def _():
                    off = pl.multiple_of(body + (rem & ~(2 * size - 1)), SUBLANES)
                    cp = pltpu.make_async_copy(zbuf.at[pl.ds(0, size)], xs_ref.at[pl.ds(off, size)], zsem)
                    cp.wait() if wait else cp.start()
            return 0
        lax.fori_loop(0, N_EXPERTS, per_expert, 0)

    @pl.when(step == 0)
    def _():
        zbuf[...] = jnp.zeros(zbuf.shape, zbuf.dtype)
        pad_dmas(False)
        pad_dmas(True)

    xbuf[...] = _pack_halves(h_ref[...].astype(F32))

    def row_copy(t, k):
        return pltpu.make_async_copy(xbuf.at[pl.ds(t, 1)], xs_ref.at[pl.ds(dest_ref[k, t], 1)], sem)

    issue, drain = _row_dma_loops(row_copy)
    lax.fori_loop(0, tm // TOKENS_PER_ISSUE, issue, 0)
    lax.fori_loop(0, tm // TOKENS_PER_ISSUE, drain, 0)


def _dispatch(h2, dest, pstart, counts):
    n, d = h2.shape
    tm = 256
    w = d // 2
    grid_spec = pltpu.PrefetchScalarGridSpec(
        num_scalar_prefetch=2,
        grid=(n // tm,),
        in_specs=[
            pl.BlockSpec((TOP_K, tm), lambda i, *_: (0, i), memory_space=pltpu.SMEM),
            pl.BlockSpec((tm, d), lambda i, *_: (i, 0)),
        ],
        out_specs=pl.BlockSpec(memory_space=pl.ANY),
        scratch_shapes=[
            pltpu.VMEM((tm, w), U32),
            pltpu.VMEM((ROW_BLOCK // 2, w), U32),
            pltpu.SemaphoreType.DMA(()),
            pltpu.SemaphoreType.DMA(()),
        ],
    )
    return pl.pallas_call(
        _dispatch_body,
        grid_spec=grid_spec,
        out_shape=jax.ShapeDtypeStruct((_capacity(n), w), U32),
        compiler_params=_cparams(("arbitrary",), has_side_effects=True, disable_bounds_checks=True),
        name="dispatch",
    )(pstart, counts, dest, h2)


def _experts_body(be_ref, nu_ref, nxt_ref, ord_ref, x_ref, wg_hbm, wu_hbm, wd_hbm, y_ref,
                  wg_s, wu_s, wd_s, wg_b, wu_b, wd_b, sems):
    i = pl.program_id(0)
    used = i < nu_ref[0]
    e = be_ref[i]
    fresh = jnp.logical_or(i == 0, e != be_ref[jnp.maximum(i - 1, 0)])
    slot = ord_ref[e] % 2

    def weight_copies(expert, s):
        return (pltpu.make_async_copy(wg_hbm.at[expert], wg_s.at[s], sems.at[s, 0]),
                pltpu.make_async_copy(wu_hbm.at[expert], wu_s.at[s], sems.at[s, 1]),
                pltpu.make_async_copy(wd_hbm.at[expert], wd_s.at[s], sems.at[s, 2]))

    @pl.when(i == 0)
    def _():
        for cp in weight_copies(e, slot):
            cp.start()

    @pl.when(jnp.logical_and(used, fresh))
    def _():
        copies = weight_copies(e, slot)
        for cp, dst, src in zip(copies, (wg_b, wu_b, wd_b), (wg_s, wu_s, wd_s)):
            cp.wait()
            dst[...] = src[slot].astype(BF16)

        @pl.when(nxt_ref[e] != e)
        def _():
            for cp in weight_copies(nxt_ref[e], 1 - slot):
                cp.start()

    @pl.when(used)
    def _():
        half = x_ref.shape[1]
        xa, xb = _unpack_halves(x_ref[...])
        xa = xa.astype(BF16)
        xb = xb.astype(BF16)
        g = (jnp.dot(xa, wg_b[:half], preferred_element_type=F32)
             + jnp.dot(xb, wg_b[half:], preferred_element_type=F32))
        u = (jnp.dot(xa, wu_b[:half], preferred_element_type=F32)
             + jnp.dot(xb, wu_b[half:], preferred_element_type=F32))
        hmid = (g * jax.nn.sigmoid(g) * u).astype(BF16)
        y_ref[...] = _pack_halves(jnp.dot(hmid, wd_b[...], preferred_element_type=F32))


def _experts(xs, block_e, n_used, next_e, ord_e, w_gate_e, w_up_e, w_down_e):
    cap, w = xs.shape
    d = 2 * w
    nblk = cap // ROW_BLOCK

    def row_map(i, be, nu, nxt, od):
        return (jnp.minimum(i, nu[0] - 1), 0)

    grid_spec = pltpu.PrefetchScalarGridSpec(
        num_scalar_prefetch=4,
        grid=(nblk,),
        in_specs=[
            pl.BlockSpec((ROW_BLOCK, w), row_map),
            pl.BlockSpec(memory_space=pl.ANY),
            pl.BlockSpec(memory_space=pl.ANY),
            pl.BlockSpec(memory_space=pl.ANY),
        ],
        out_specs=pl.BlockSpec((ROW_BLOCK, w), row_map),
        scratch_shapes=[
            pltpu.VMEM((2, d, D_EXPERT), F32),
            pltpu.VMEM((2, d, D_EXPERT), F32),
            pltpu.VMEM((2, D_EXPERT, d), F32),
            pltpu.VMEM((d, D_EXPERT), BF16),
            pltpu.VMEM((d, D_EXPERT), BF16),
            pltpu.VMEM((D_EXPERT, d), BF16),
            pltpu.SemaphoreType.DMA((2, 3)),
        ],
    )
    return pl.pallas_call(
        _experts_body,
        grid_spec=grid_spec,
        out_shape=jax.ShapeDtypeStruct((cap, w), U32),
        compiler_params=_cparams(("arbitrary",)),
        name="experts",
    )(block_e, n_used, next_e, ord_e, xs, w_gate_e, w_up_e, w_down_e)


def _combine_body(dest_ref, ys_ref, wt_ref, h_ref, x1_ref, mod_ref, gpost, wg, wu, wd, o_ref, buf, sem):
    tm = h_ref.shape[0]

    def row_copy(t, k):
        return pltpu.make_async_copy(ys_ref.at[pl.ds(dest_ref[k, t], 1)], buf.at[k, pl.ds(t, 1)], sem)

    issue, drain = _row_dma_loops(row_copy)
    lax.fori_loop(0, tm // TOKENS_PER_ISSUE, issue, 0)

    h = h_ref[...]
    g = jnp.dot(h, wg[...], preferred_element_type=F32)
    u = jnp.dot(h, wu[...], preferred_element_type=F32)
    shared = jnp.dot((g * jax.nn.sigmoid(g) * u).astype(BF16), wd[...], preferred_element_type=F32)

    lax.fori_loop(0, tm // TOKENS_PER_ISSUE, drain, 0)

    half = buf.shape[2]
    wt = wt_ref[...]
    ya = shared[:, :half]
    yb = shared[:, half:]
    for k in range(TOP_K):
        a, b = _unpack_halves(buf[k])
        ya = ya + wt[:, k:k + 1] * a
        yb = yb + wt[:, k:k + 1] * b
    y = jnp.concatenate([ya, yb], axis=1)
    gate2 = mod_ref[0, 5:6, :]
    o_ref[...] = x1_ref[...] + gate2 * (_rms(y) * gpost[...])


def _combine(ys, dest, wt, h2, x1, mod3, g_post, wg_b, wu_b, wd_b, seq):
    n, d = h2.shape
    tm = 256
    w = d // 2
    per_b = seq // tm
    const = lambda shape: pl.BlockSpec(shape, lambda i, *_: (0,) * len(shape), pipeline_mode=pl.Buffered(1))
    grid_spec = pltpu.PrefetchScalarGridSpec(
        num_scalar_prefetch=0,
        grid=(n // tm,),
        in_specs=[
            pl.BlockSpec((TOP_K, tm), lambda i: (0, i), memory_space=pltpu.SMEM),
            pl.BlockSpec(memory_space=pl.ANY),
            pl.BlockSpec((tm, LANES), lambda i: (i, 0)),
            pl.BlockSpec((tm, d), lambda i: (i, 0)),
            pl.BlockSpec((tm, d), lambda i: (i, 0)),
            pl.BlockSpec((1, N_MOD, d), lambda i: (i // per_b, 0, 0)),
            const((1, d)), const(wg_b.shape), const(wu_b.shape), const(wd_b.shape),
        ],
        out_specs=pl.BlockSpec((tm, d), lambda i: (i, 0)),
        scratch_shapes=[pltpu.VMEM((TOP_K, tm, w), U32), pltpu.SemaphoreType.DMA(())],
    )
    return pl.pallas_call(
        _combine_body,
        grid_spec=grid_spec,
        out_shape=jax.ShapeDtypeStruct((n, d), F32),
        compiler_params=_cparams(("arbitrary",), disable_bounds_checks=True),
        name="combine",
    )(dest, ys, wt, h2, x1, mod3, g_post.reshape(1, d), wg_b, wu_b, wd_b)


def _layer(x, mod, positions, g_pre_mix, g_post_mix, g_pre_ffn, g_post_ffn, w_in, w_dil_out, w_sb_out, w_mix_out,
           w_router, router_bias, w_gate_e, w_up_e, w_down_e, w_gate_s, w_up_s, w_down_s):
    bsz, seq, d = x.shape
    n = bsz * seq
    xf = x.reshape(n, d)
    mod3 = mod.reshape(bsz, N_MOD, d)

    nd, nq = 3 * WIDTH_DIL, 3 * (WIDTH_DIL + WIDTH_SB)
    w_plain = jnp.concatenate([w_in[:, nq:], w_in[:, nd:nq]], axis=1).astype(BF16)
    cols = []
    for g in range(len(DIL_PATTERNS)):
        for part in range(3):
            lo = part * WIDTH_DIL + g * D_DIL_OUT
            cols.append(w_in[:, lo:lo + D_DIL_OUT])
    w_dil = jnp.concatenate(cols, axis=1).astype(BF16)

    proj = _inproj(xf, g_pre_mix, mod3, w_plain, seq)
    tables = _rope_tables(positions)
    qkv_dil = _inproj_dil(xf, g_pre_mix, mod3, w_dil, tables, bsz, seq)
    o_dil, lse_dil = [], []
    for g, (window, dilation) in enumerate(DIL_PATTERNS):
        assert window // dilation == Q_BLOCK
        o, lse = _dilated_attention(qkv_dil[g], g, bsz, seq)
        o_dil.append(o)
        lse_dil.append(lse)
    o_sb = _stick_breaking(proj, bsz, seq)

    x1, h2 = _mixout(o_dil, lse_dil, o_sb, proj, xf, mod3, g_post_mix, g_pre_ffn,
                     w_dil_out.astype(BF16), w_sb_out.astype(BF16), w_mix_out.astype(BF16), seq)

    top_idx, top_wt = _router(h2, w_router, router_bias)
    counts = _plan_counts(top_idx)[:, 0].astype(I32)
    padded = (counts + ROW_BLOCK - 1) // ROW_BLOCK * ROW_BLOCK
    pends = jnp.cumsum(padded)
    pstart = pends - padded
    nblk = _capacity(n) // ROW_BLOCK
    n_used = (pends[-1] // ROW_BLOCK).astype(I32)
    blk = jnp.minimum(jnp.arange(nblk, dtype=I32), n_used - 1)
    block_e = jnp.minimum(jnp.sum(pends[None, :] <= (blk * ROW_BLOCK)[:, None], axis=1), N_EXPERTS - 1).astype(I32)
    eids = jnp.arange(N_EXPERTS, dtype=I32)
    has = counts > 0
    ord_e = (jnp.cumsum(has) - has).astype(I32)
    later = jnp.where((eids[None, :] > eids[:, None]) & has[None, :], eids[None, :], N_EXPERTS)
    nxt = jnp.min(later, axis=1)
    next_e = jnp.where(nxt < N_EXPERTS, nxt, eids).astype(I32)
    dest = _plan_dest(top_idx, pstart)

    xs = _dispatch(h2, dest, pstart.astype(I32), counts)
    ys = _experts(xs, block_e, n_used.reshape(1), next_e, ord_e, w_gate_e, w_up_e, w_down_e)
    out = _combine(ys, dest, top_wt, h2, x1, mod3, g_post_ffn,
                   w_gate_s.astype(BF16), w_up_s.astype(BF16), w_down_s.astype(BF16), seq)
    return out.reshape(bsz, seq, d)


def kernel(x, c, positions, w_ada, b_ada, g_pre_mix, g_post_mix, g_pre_ffn, g_post_ffn, w_in, w_dil_out,
           w_sb_out, w_mix_out, w_router, router_bias, w_gate_e, w_up_e, w_down_e, w_gate_s, w_up_s, w_down_s):
    for l in range(w_ada.shape[0]):
        mod = _adaln(c, w_ada[l], b_ada[l])
        x = _layer(x, mod, positions, g_pre_mix[l], g_post_mix[l], g_pre_ffn[l], g_post_ffn[l], w_in[l],
                   w_dil_out[l], w_sb_out[l], w_mix_out[l], w_router[l], router_bias[l],
                   w_gate_e[l], w_up_e[l], w_down_e[l], w_gate_s[l], w_up_s[l], w_down_s[l])
    return x
```

```python
import functools

import jax
import jax.numpy as jnp
from jax import lax
from jax.experimental import pallas as pl
from jax.experimental.pallas import tpu as pltpu

F32 = jnp.float32
BF16 = jnp.bfloat16
I32 = jnp.int32
U32 = jnp.uint32

D_MODEL = 2048
HEAD_DIM = 128
DIL_PATTERNS = ((128, 1), (512, 4), (2048, 16))
HEADS_PER_GROUP = 4
N_HEADS_DIL = 12
N_HEADS_SB = 8
WIDTH_DIL = N_HEADS_DIL * HEAD_DIM
WIDTH_SB = N_HEADS_SB * HEAD_DIM
D_DIL_OUT = HEADS_PER_GROUP * HEAD_DIM
Q_BLOCK = 128
ROPE_THETA = 500000.0
ROPE_DIM = HEAD_DIM // 4
N_GATE = 2 * D_MODEL
N_EXPERTS = 64
TOP_K = 8
N_GROUPS = 8
GROUP_SIZE = N_EXPERTS // N_GROUPS
TOPK_GROUPS = 4
D_EXPERT = 512
D_SHARED = 512
ROUTED_SCALE = 2.5
RMS_EPS = 1e-6
N_MOD = 6
ATTN_SCALE = HEAD_DIM ** -0.5

LANES = 128
SUBLANES = 8
VMEM_LIMIT = 56 * 1024 * 1024

ROW_BLOCK = 256
SB_DEAD = -110.0


def _cparams(sem, **kw):
    return pltpu.CompilerParams(dimension_semantics=sem, vmem_limit_bytes=VMEM_LIMIT, **kw)


def _adaln_body(ct_ref, w_ref, b_ref, o_ref, *, kc):
    nb = ct_ref.shape[1]
    nk = w_ref.shape[0] // kc

    def step(i, acc):
        k0 = pl.multiple_of(i * kc, kc)
        w = w_ref[pl.ds(k0, kc), :]
        c = ct_ref[pl.ds(k0, kc), :]
        s = c * jax.nn.sigmoid(c)
        parts = [jnp.sum(w * s[:, b:b + 1], axis=0, keepdims=True) for b in range(nb)]
        return acc + jnp.concatenate(parts, axis=0)

    acc = lax.fori_loop(0, nk, step, jnp.zeros(o_ref.shape, F32))
    o_ref[...] = acc + b_ref[...]


def _adaln(c, w_ada, b_ada):
    nb, d = c.shape
    n_out = w_ada.shape[1]
    tn = 1024
    return pl.pallas_call(
        functools.partial(_adaln_body, kc=256),
        grid=(n_out // tn,),
        in_specs=[
            pl.BlockSpec((d, nb), lambda j: (0, 0)),
            pl.BlockSpec((d, tn), lambda j: (0, j)),
            pl.BlockSpec((1, tn), lambda j: (0, j)),
        ],
        out_specs=pl.BlockSpec((nb, tn), lambda j: (0, j)),
        out_shape=jax.ShapeDtypeStruct((nb, n_out), F32),
        compiler_params=_cparams(("arbitrary",)),
        name="adaln",
    )(c.T, w_ada, b_ada.reshape(1, n_out))


def _rms(x):
    return x * lax.rsqrt(jnp.mean(x * x, axis=-1, keepdims=True) + RMS_EPS)


def _prenorm(x_ref, g_ref, mod_ref):
    y = _rms(x_ref[...]) * g_ref[...]
    shift = mod_ref[0, 0:1, :]
    scale = mod_ref[0, 1:2, :]
    return (y * (1.0 + scale) + shift).astype(BF16)


def _inproj_body(x_ref, g_ref, mod_ref, w_ref, o_ref, h_ref):
    @pl.when(pl.program_id(1) == 0)
    def _():
        h_ref[...] = _prenorm(x_ref, g_ref, mod_ref)

    o_ref[...] = jnp.dot(h_ref[...], w_ref[...], preferred_element_type=F32).astype(o_ref.dtype)


def _inproj(xf, g_pre, mod3, w_b, seq):
    n, d = xf.shape
    width = w_b.shape[1]
    tm, tn = 512, width // 4
    per_b = seq // tm
    return pl.pallas_call(
        _inproj_body,
        grid=(n // tm, width // tn),
        in_specs=[
            pl.BlockSpec((tm, d), lambda i, j: (i, 0)),
            pl.BlockSpec((1, d), lambda i, j: (0, 0)),
            pl.BlockSpec((1, N_MOD, d), lambda i, j: (i // per_b, 0, 0)),
            pl.BlockSpec((d, tn), lambda i, j: (0, j)),
        ],
        out_specs=pl.BlockSpec((tm, tn), lambda i, j: (i, j)),
        out_shape=jax.ShapeDtypeStruct((n, width), BF16),
        scratch_shapes=[pltpu.VMEM((tm, d), BF16)],
        compiler_params=_cparams(("arbitrary", "arbitrary")),
        name="inproj",
    )(xf, g_pre.reshape(1, d), mod3, w_b)


def _inproj_dil_body(x_ref, g_ref, mod_ref, w_ref, t_ref, o0, o1, o2, res_ref):
    tm = x_ref.shape[0]
    h = _prenorm(x_ref, g_ref, mod_ref)
    t = t_ref[...]
    gw = 3 * D_DIL_OUT
    for gi, o_ref in enumerate((o0, o1, o2)):
        dil = DIL_PATTERNS[gi][1]
        res = jnp.dot(h, w_ref[:, gi * gw:(gi + 1) * gw], preferred_element_type=F32)
        for hs in range(3 * HEADS_PER_GROUP):
            sl = slice(hs * HEAD_DIM, (hs + 1) * HEAD_DIM)
            res_ref[hs] = _apply_rope(res[:, sl], t) if hs < 2 * HEADS_PER_GROUP else res[:, sl]
        for r in range(dil):
            for hs in range(3 * HEADS_PER_GROUP):
                rows = res_ref[hs] if dil == 1 else res_ref[hs, pl.ds(r, tm // dil, stride=dil), :]
                o_ref[0, r, :, hs * HEAD_DIM:(hs + 1) * HEAD_DIM] = rows.astype(o_ref.dtype)


def _inproj_dil(xf, g_pre, mod3, w_b, tables, bsz, seq):
    n, d = xf.shape
    gw = 3 * D_DIL_OUT
    tm = 512
    per_b = seq // tm
    dils = [p[1] for p in DIL_PATTERNS]
    return pl.pallas_call(
        _inproj_dil_body,
        grid=(n // tm,),
        in_specs=[
            pl.BlockSpec((tm, d), lambda i: (i, 0)),
            pl.BlockSpec((1, d), lambda i: (0, 0)),
            pl.BlockSpec((1, N_MOD, d), lambda i: (i // per_b, 0, 0)),
            pl.BlockSpec(w_b.shape, lambda i: (0, 0), pipeline_mode=pl.Buffered(1)),
            pl.BlockSpec((tm, 3 * LANES), lambda i: (i, 0)),
        ],
        out_specs=[pl.BlockSpec((1, dl, tm // dl, gw), lambda i: (i // per_b, 0, i % per_b, 0)) for dl in dils],
        out_shape=[jax.ShapeDtypeStruct((bsz, dl, seq // dl, gw), BF16) for dl in dils],
        scratch_shapes=[pltpu.VMEM((gw // HEAD_DIM, tm, HEAD_DIM), F32)],
        compiler_params=_cparams(("arbitrary",)),
        name="inproj_dil",
    )(xf, g_pre.reshape(1, d), mod3, w_b, tables)


def _rope_body(pos_ref, f_ref, o_ref):
    ang = pos_ref[...].astype(F32) * f_ref[...]
    c = jnp.cos(ang)
    s = jnp.sin(ang)
    lane = lax.broadcasted_iota(I32, ang.shape, 1)
    half = ROPE_DIM // 2
    o_ref[:, 0:LANES] = c
    o_ref[:, LANES:2 * LANES] = jnp.where(lane >= half, s, 0.0)
    o_ref[:, 2 * LANES:3 * LANES] = jnp.where(lane < half, -s, 0.0)


def _rope_tables(positions):
    n = positions.size
    half = ROPE_DIM // 2
    inv_freq = ROPE_THETA ** (-jnp.arange(0, ROPE_DIM, 2, dtype=F32) / ROPE_DIM)
    f = jnp.concatenate([inv_freq, inv_freq, jnp.zeros((LANES - 2 * half,), F32)]).reshape(1, LANES)
    tm = 2048
    return pl.pallas_call(
        _rope_body,
        grid=(n // tm,),
        in_specs=[pl.BlockSpec((tm, 1), lambda i: (i, 0)), pl.BlockSpec((1, LANES), lambda i: (0, 0))],
        out_specs=pl.BlockSpec((tm, 3 * LANES), lambda i: (i, 0)),
        out_shape=jax.ShapeDtypeStruct((n, 3 * LANES), F32),
        compiler_params=_cparams(("arbitrary",)),
        name="rope_tables",
    )(positions.reshape(n, 1), f)


def _apply_rope(x, t):
    half = ROPE_DIM // 2
    return (x * t[:, 0:LANES]
            + pltpu.roll(x, half, 1) * t[:, LANES:2 * LANES]
            + pltpu.roll(x, LANES - half, 1) * t[:, 2 * LANES:3 * LANES])


def _dil_body(cur_ref, kp_ref, vp_ref, o_ref, lse_ref, obuf, lbuf, *, dil, nsub):
    n = pl.program_id(1)
    tq = nsub * Q_BLOCK
    row = lax.broadcasted_iota(I32, (Q_BLOCK, 2 * Q_BLOCK), 0)
    col = lax.broadcasted_iota(I32, (Q_BLOCK, 2 * Q_BLOCK), 1)
    rel = row + Q_BLOCK - col
    band = jnp.where(rel >= 0, jnp.where(rel <= Q_BLOCK, 1.0, 0.0), 0.0)
    first = jnp.where(col >= Q_BLOCK, band, jnp.where(n > 0, band, 0.0))
    for r in range(dil):
        for h in range(HEADS_PER_GROUP):
            sl = slice(h * HEAD_DIM, (h + 1) * HEAD_DIM)
            ksl = slice(D_DIL_OUT + h * HEAD_DIM, D_DIL_OUT + (h + 1) * HEAD_DIM)
            vsl = slice(2 * D_DIL_OUT + h * HEAD_DIM, 2 * D_DIL_OUT + (h + 1) * HEAD_DIM)
            for j in range(nsub):
                rs = slice(j * Q_BLOCK, (j + 1) * Q_BLOCK)
                ps = slice((j - 1) * Q_BLOCK, j * Q_BLOCK)
                kprev = kp_ref[0, r, :, sl] if j == 0 else cur_ref[0, r, ps, ksl]
                vprev = vp_ref[0, r, :, sl] if j == 0 else cur_ref[0, r, ps, vsl]
                kcat = jnp.concatenate([kprev, cur_ref[0, r, rs, ksl]], axis=0)
                vcat = jnp.concatenate([vprev, cur_ref[0, r, rs, vsl]], axis=0)
                s = lax.dot_general(cur_ref[0, r, rs, sl], kcat, (((1,), (1,)), ((), ())),
                                    preferred_element_type=F32) * ATTN_SCALE
                s = jnp.where((first if j == 0 else band) > 0.0, s, -jnp.inf)
                m = jnp.max(s, axis=-1, keepdims=True)
                p = jnp.exp(s - m)
                l = jnp.sum(p, axis=-1, keepdims=True)
                o = jnp.dot((p / l).astype(BF16), vcat, preferred_element_type=F32)
                lse = jnp.broadcast_to(m + jnp.log(l), (Q_BLOCK, HEAD_DIM))
                if dil == 1:
                    o_ref[h, rs, :] = o
                    lse_ref[h, rs, :] = lse
                else:
                    obuf[h, rs, :] = o
                    lbuf[h, rs, :] = lse
            if dil > 1:
                o_ref[h, pl.ds(r, tq, stride=dil), :] = obuf[h]
                lse_ref[h, pl.ds(r, tq, stride=dil), :] = lbuf[h]


def _dilated_attention(qkv, g, bsz, seq):
    dil = DIL_PATTERNS[g][1]
    length = seq // dil
    tq = min(4 * Q_BLOCK, (16 * Q_BLOCK) // dil, length)
    nsub = tq // Q_BLOCK
    nq = length // tq
    gw = 3 * D_DIL_OUT
    n = bsz * seq

    def prev(colblk):
        return pl.BlockSpec((1, dil, Q_BLOCK, D_DIL_OUT),
                            lambda b, i: (b, 0, jnp.maximum(i * nsub - 1, 0), colblk))

    nh = HEADS_PER_GROUP
    out_spec = pl.BlockSpec((nh, tq * dil, HEAD_DIM), lambda b, i: (0, b * nq + i, 0))
    out_shape = jax.ShapeDtypeStruct((nh, n, HEAD_DIM), F32)
    return pl.pallas_call(
        functools.partial(_dil_body, dil=dil, nsub=nsub),
        grid=(bsz, nq),
        in_specs=[pl.BlockSpec((1, dil, tq, gw), lambda b, i: (b, 0, i, 0)), prev(1), prev(2)],
        out_specs=[out_spec, out_spec],
        out_shape=[out_shape, out_shape],
        scratch_shapes=[pltpu.VMEM((nh, tq, HEAD_DIM), F32), pltpu.VMEM((nh, tq, HEAD_DIM), F32)],
        compiler_params=_cparams(("arbitrary", "arbitrary")),
        name=f"dilated_d{dil}",
    )(qkv, qkv, qkv)


def _sb_body(q_ref, k_ref, v_ref, o_ref, acc_ref, car_ref):
    nblk = q_ref.shape[1] // Q_BLOCK
    r = lax.broadcasted_iota(I32, (Q_BLOCK, Q_BLOCK), 0)
    c = lax.broadcasted_iota(I32, (Q_BLOCK, Q_BLOCK), 1)
    causal = c < r
    rr = lax.broadcasted_iota(I32, (Q_BLOCK, 2 * Q_BLOCK), 0)
    cc = lax.broadcasted_iota(I32, (Q_BLOCK, 2 * Q_BLOCK), 1)
    uo = jnp.where(cc >= Q_BLOCK, 1.0, jnp.where(rr > cc, 1.0, 0.0)).astype(BF16)

    nh = acc_ref.shape[0]
    heads = range(nh)

    def tiles(qs, kb, carries, diag):
        k0 = pl.multiple_of(kb * Q_BLOCK, Q_BLOCK)
        hs = [slice(h * HEAD_DIM, (h + 1) * HEAD_DIM) for h in heads]
        zs = [lax.dot_general(qs[h], k_ref[0, pl.ds(k0, Q_BLOCK), hs[h]], (((1,), (1,)), ((), ())),
                              preferred_element_type=F32) * ATTN_SCALE for h in heads]
        stacked, log_s = [], []
        for z in zs:
            sp = jnp.log(1.0 + jnp.exp(-jnp.abs(z)))
            mx = jnp.maximum(z, 0.0)
            log_1m = -(mx + sp)
            if diag:
                log_1m = jnp.where(causal, log_1m, 0.0)
            hi = log_1m.astype(BF16)
            lo = (log_1m - hi.astype(F32)).astype(BF16)
            stacked.append(jnp.concatenate([hi, lo], axis=0))
            log_s.append((z - mx) - sp)
        r2s = [jnp.dot(s, uo, preferred_element_type=F32) for s in stacked]
        probs, new_carries = [], []
        for h in heads:
            sums = r2s[h][:Q_BLOCK] + r2s[h][Q_BLOCK:]
            a = jnp.exp(log_s[h] + carries[h] + sums[:, :Q_BLOCK])
            if diag:
                a = jnp.where(causal, a, 0.0)
            probs.append(a.astype(BF16))
            new_carries.append(carries[h] + sums[:, Q_BLOCK:])
        pvs = [jnp.dot(probs[h], v_ref[0, pl.ds(k0, Q_BLOCK), hs[h]], preferred_element_type=F32) for h in heads]
        return pvs, new_carries

    def all_max(xs):
        m = xs[0]
        for x in xs[1:]:
            m = jnp.maximum(m, x)
        return jnp.max(m)

    def qblock(qi, _):
        q0 = pl.multiple_of(qi * Q_BLOCK, Q_BLOCK)
        qs = [q_ref[0, pl.ds(q0, Q_BLOCK), h * HEAD_DIM:(h + 1) * HEAD_DIM] for h in heads]
        zero = jnp.zeros((Q_BLOCK, Q_BLOCK), F32)
        pvs, cars = tiles(qs, qi, [zero] * nh, True)
        for h in heads:
            acc_ref[h] = pvs[h]
            car_ref[h] = cars[h]

        def cond(st):
            return jnp.logical_and(st[0] >= 0, st[1] > SB_DEAD)

        def body(st):
            pvs, cars = tiles(qs, st[0], [car_ref[h] for h in heads], False)
            for h in heads:
                acc_ref[h] += pvs[h]
                car_ref[h] = cars[h]
            return st[0] - 1, all_max(cars)

        lax.while_loop(cond, body, (qi - 1, all_max(cars)))
        for h in range(nh):
            o_ref[0, pl.ds(q0, Q_BLOCK), h * HEAD_DIM:(h + 1) * HEAD_DIM] = acc_ref[h].astype(o_ref.dtype)
        return 0

    lax.fori_loop(0, nblk, qblock, 0)


SB_HEADS_PER_STEP = 4


def _stick_breaking(proj, bsz, seq):
    width = proj.shape[1]
    pv = proj.reshape(bsz, seq, width)
    nh = SB_HEADS_PER_STEP
    bw = nh * HEAD_DIM
    base = N_GATE // bw

    def spec(off):
        return pl.BlockSpec((1, seq, bw), lambda b, h: (b, 0, base + off + h))

    nstep = N_HEADS_SB // nh
    o = pl.pallas_call(
        _sb_body,
        grid=(bsz, nstep),
        in_specs=[spec(0), spec(nstep), spec(2 * nstep)],
        out_specs=pl.BlockSpec((1, seq, bw), lambda b, h: (b, 0, h)),
        out_shape=jax.ShapeDtypeStruct((bsz, seq, WIDTH_SB), BF16),
        scratch_shapes=[pltpu.VMEM((nh, Q_BLOCK, Q_BLOCK), F32), pltpu.VMEM((nh, Q_BLOCK, Q_BLOCK), F32)],
        compiler_params=_cparams(("arbitrary", "arbitrary")),
        name="stick_breaking",
    )(pv, pv, pv)
    return o.reshape(bsz * seq, WIDTH_SB)


def _mixout_body(o1, o2, o3, l1, l2, l3, osb, gd_ref, gs_ref, x_ref, mod_ref, gpost, gpre,
                 wd, ws, wm, x1_ref, h2_ref):
    heads = []
    for h in range(HEADS_PER_GROUP):
        la, lb, lc = l1[h], l2[h], l3[h]
        m = jnp.maximum(la, jnp.maximum(lb, lc))
        ea, eb, ec = jnp.exp(la - m), jnp.exp(lb - m), jnp.exp(lc - m)
        heads.append(((ea * o1[h] + eb * o2[h] + ec * o3[h]) / (ea + eb + ec)).astype(BF16))
    yd = jnp.dot(jnp.concatenate(heads, axis=1), wd[...], preferred_element_type=F32)
    ys = jnp.dot(osb[...], ws[...], preferred_element_type=F32)
    mix = jax.nn.sigmoid(gd_ref[...].astype(F32)) * yd + jax.nn.sigmoid(gs_ref[...].astype(F32)) * ys
    y = jnp.dot(mix.astype(BF16), wm[...], preferred_element_type=F32)
    gate1 = mod_ref[0, 2:3, :]
    shift2 = mod_ref[0, 3:4, :]
    scale2 = mod_ref[0, 4:5, :]
    x1 = x_ref[...] + gate1 * (_rms(y) * gpost[...])
    x1_ref[...] = x1
    h2_ref[...] = ((_rms(x1) * gpre[...]) * (1.0 + scale2) + shift2).astype(h2_ref.dtype)


def _const_spec(shape):
    return pl.BlockSpec(shape, lambda i: (0,) * len(shape), pipeline_mode=pl.Buffered(1))


def _mixout(o_dil, lse_dil, o_sb, proj, xf, mod3, g_post, g_pre, wd_b, ws_b, wm_b, seq):
    n, d = xf.shape
    tm = 256
    per_b = seq // tm
    row = lambda w: pl.BlockSpec((tm, w), lambda i: (i, 0))
    head_major = pl.BlockSpec((HEADS_PER_GROUP, tm, HEAD_DIM), lambda i: (0, i, 0))
    in_specs = (
        [head_major] * 6 + [row(WIDTH_SB)]
        + [pl.BlockSpec((tm, d), lambda i: (i, 0)), pl.BlockSpec((tm, d), lambda i: (i, 1))]
        + [row(d), pl.BlockSpec((1, N_MOD, d), lambda i: (i // per_b, 0, 0))]
        + [_const_spec((1, d)), _const_spec((1, d))]
        + [_const_spec(wd_b.shape), _const_spec(ws_b.shape), _const_spec(wm_b.shape)]
    )
    return pl.pallas_call(
        _mixout_body,
        grid=(n // tm,),
        in_specs=in_specs,
        out_specs=[row(d), row(d)],
        out_shape=[jax.ShapeDtypeStruct((n, d), F32), jax.ShapeDtypeStruct((n, d), BF16)],
        compiler_params=_cparams(("arbitrary",)),
        name="mixout",
    )(*o_dil, *lse_dil, o_sb, proj, proj, xf, mod3, g_post.reshape(1, d), g_pre.reshape(1, d), wd_b, ws_b, wm_b)


def _topk_rows(x, k, iota0):
    big = x.shape[0]
    out = []
    for _ in range(k):
        m = jnp.max(x, axis=0, keepdims=True)
        i = jnp.min(jnp.where(x == m, iota0, big), axis=0, keepdims=True)
        out.append((m, i))
        x = jnp.where(iota0 == i, -jnp.inf, x)
    return out


def _router_body(h_ref, wr_ref, bias_ref, idx_ref, wt_ref):
    tm = h_ref.shape[0]
    logits = lax.dot_general(wr_ref[...], h_ref[...], (((1,), (1,)), ((), ())), preferred_element_type=F32)
    scores = jax.nn.sigmoid(logits)
    sel = scores + bias_ref[...]
    sub = lax.broadcasted_iota(I32, (GROUP_SIZE, tm), 0)
    grp = []
    for g in range(N_GROUPS):
        (m1, _), (m2, _) = _topk_rows(sel[g * GROUP_SIZE:(g + 1) * GROUP_SIZE], 2, sub)
        grp.append(m1 + m2)
    gscore = jnp.concatenate(grp, axis=0)
    giota = lax.broadcasted_iota(I32, (N_GROUPS, tm), 0)
    gmask = jnp.zeros((N_GROUPS, tm), F32)
    for _, gi in _topk_rows(gscore, TOPK_GROUPS, giota):
        gmask = jnp.where(giota == gi, 1.0, gmask)
    masked = jnp.concatenate(
        [jnp.where(gmask[g:g + 1] > 0.0, sel[g * GROUP_SIZE:(g + 1) * GROUP_SIZE], -jnp.inf)
         for g in range(N_GROUPS)], axis=0)
    eiota = lax.broadcasted_iota(I32, (N_EXPERTS, tm), 0)
    picks = _topk_rows(masked, TOP_K, eiota)
    idx = jnp.concatenate([i for _, i in picks], axis=0)
    top_s = jnp.concatenate(
        [jnp.sum(jnp.where(eiota == i, scores, 0.0), axis=0, keepdims=True) for _, i in picks], axis=0)
    top_w = top_s / jnp.sum(top_s, axis=0, keepdims=True) * ROUTED_SCALE
    idx_ref[...] = idx
    wpad = jnp.concatenate([top_w, jnp.zeros((LANES - TOP_K, tm), F32)], axis=0)
    wt_ref[...] = wpad.T


def _router(h2, w_router, router_bias):
    n, d = h2.shape
    tm = 512
    return pl.pallas_call(
        _router_body,
        grid=(n // tm,),
        in_specs=[
            pl.BlockSpec((tm, d), lambda i: (i, 0)),
            pl.BlockSpec((N_EXPERTS, d), lambda i: (0, 0)),
            pl.BlockSpec((N_EXPERTS, 1), lambda i: (0, 0)),
        ],
        out_specs=[pl.BlockSpec((TOP_K, tm), lambda i: (0, i)), pl.BlockSpec((tm, LANES), lambda i: (i, 0))],
        out_shape=[jax.ShapeDtypeStruct((TOP_K, n), I32), jax.ShapeDtypeStruct((n, LANES), F32)],
        compiler_params=_cparams(("arbitrary",)),
        name="router",
    )(h2, w_router.T.astype(BF16), router_bias.reshape(N_EXPERTS, 1))


def _plan_tile(idx):
    tm = idx.shape[1]
    eiota = lax.broadcasted_iota(I32, (N_EXPERTS, tm), 0)
    hit = jnp.zeros((N_EXPERTS, tm), F32)
    for k in range(TOP_K):
        hit = jnp.where(eiota == idx[k:k + 1], 1.0, hit)
    r = lax.broadcasted_iota(I32, (tm, tm), 0)
    c = lax.broadcasted_iota(I32, (tm, tm), 1)
    before = jnp.where(r < c, 1.0, 0.0).astype(BF16)
    excl = jnp.dot(hit.astype(BF16), before, preferred_element_type=F32)
    tot = excl[:, tm - 1:tm] + hit[:, tm - 1:tm]
    return eiota, hit, excl, tot


def _plan_counts_body(idx_ref, cnt_ref):
    @pl.when(pl.program_id(0) == 0)
    def _():
        cnt_ref[...] = jnp.zeros(cnt_ref.shape, F32)

    _, _, _, tot = _plan_tile(idx_ref[...])
    cnt_ref[...] += tot


def _plan_counts(top_idx):
    n = top_idx.shape[1]
    tm = 512
    return pl.pallas_call(
        _plan_counts_body,
        grid=(n // tm,),
        in_specs=[pl.BlockSpec((TOP_K, tm), lambda i: (0, i))],
        out_specs=pl.BlockSpec((N_EXPERTS, LANES), lambda i: (0, 0)),
        out_shape=jax.ShapeDtypeStruct((N_EXPERTS, LANES), F32),
        compiler_params=_cparams(("arbitrary",)),
        name="plan_counts",
    )(top_idx)


def _plan_dest_body(idx_ref, base_ref, dest_ref, run_ref):
    @pl.when(pl.program_id(0) == 0)
    def _():
        run_ref[...] = base_ref[...]

    idx = idx_ref[...]
    eiota, _, excl, tot = _plan_tile(idx)
    pos = run_ref[:, 0:1] + excl
    rows = [jnp.sum(jnp.where(eiota == idx[k:k + 1], pos, 0.0), axis=0, keepdims=True) for k in range(TOP_K)]
    dest_ref[...] = jnp.concatenate(rows, axis=0).astype(I32)
    run_ref[...] += tot


def _plan_dest(top_idx, pstart):
    n = top_idx.shape[1]
    tm = 512
    base = jnp.broadcast_to(pstart.astype(F32).reshape(N_EXPERTS, 1), (N_EXPERTS, LANES))
    return pl.pallas_call(
        _plan_dest_body,
        grid=(n // tm,),
        in_specs=[pl.BlockSpec((TOP_K, tm), lambda i: (0, i)), pl.BlockSpec((N_EXPERTS, LANES), lambda i: (0, 0))],
        out_specs=pl.BlockSpec((TOP_K, tm), lambda i: (0, i)),
        out_shape=jax.ShapeDtypeStruct((TOP_K, n), I32),
        scratch_shapes=[pltpu.VMEM((N_EXPERTS, LANES), F32)],
        compiler_params=_cparams(("arbitrary",)),
        name="plan_dest",
    )(top_idx, base)


def _capacity(n):
    return n * TOP_K + N_EXPERTS * ROW_BLOCK


def _pack_halves(y):
    w = y.shape[1] // 2
    hi = pltpu.bitcast(y[:, :w].astype(BF16).astype(F32), U32)
    lo = pltpu.bitcast(y[:, w:].astype(BF16).astype(F32), U32)
    return hi | (lo >> 16)


def _unpack_halves(p):
    a = pltpu.bitcast(p & jnp.uint32(0xFFFF0000), F32)
    b = pltpu.bitcast(p << 16, F32)
    return a, b


TOKENS_PER_ISSUE = 2


def _row_dma_loops(row_copy):
    def issue(i, _):
        for u in range(TOKENS_PER_ISSUE):
            for k in range(TOP_K):
                row_copy(i * TOKENS_PER_ISSUE + u, k).start(priority=k % 2)
        return 0

    def drain(i, _):
        for u in range(TOKENS_PER_ISSUE):
            for k in range(TOP_K):
                row_copy(i * TOKENS_PER_ISSUE + u, k).wait()
        return 0

    return issue, drain


def _pad_chunks():
    sizes, s = [], ROW_BLOCK // 2
    while s >= 1:
        sizes.append(s)
        s //= 2
    return sizes


def _tile_rows(t):
    return pl.ds(pl.multiple_of(t * SUBLANES, SUBLANES), SUBLANES)


def _to_row_tiles(ref, packed, lead=()):
    rows = packed.shape[0]
    for s in range(SUBLANES):
        ref[(*lead, pl.ds(s, rows, stride=SUBLANES), slice(None))] = packed[:, s * LANES:(s + 1) * LANES]


def _from_row_tiles(ref, rows, lead=()):
    return [ref[(*lead, pl.ds(s, rows, stride=SUBLANES), slice(None))] for s in range(SUBLANES)]


def _dispatch_body(pstart_ref, cnt_ref, dest_ref, h_ref, xs_ref, xbuf, zbuf, sem, zsem):
    tm = h_ref.shape[0]
    step = pl.program_id(0)

    def pad_dmas(wait):
        def per_expert(e, _):
            cnt = cnt_ref[e]
            pad = (ROW_BLOCK - cnt % ROW_BLOCK) % ROW_BLOCK
            start = pstart_ref[e] + cnt
            for size in _pad_chunks():
                @pl.when((pad & size) != 0)
                def _():
                    off = pl.multiple_of((start + (pad & ~(2 * size - 1))) * SUBLANES, SUBLANES)
                    cp = pltpu.make_async_copy(zbuf.at[pl.ds(0, size * SUBLANES)],
                                               xs_ref.at[pl.ds(off, size * SUBLANES)], zsem)
                    cp.wait() if wait else cp.start()
            return 0
        lax.fori_loop(0, N_EXPERTS, per_expert, 0)

    @pl.when(step == 0)
    def _():
        zbuf[...] = jnp.zeros(zbuf.shape, zbuf.dtype)
        pad_dmas(False)
        pad_dmas(True)

    _to_row_tiles(xbuf, _pack_halves(h_ref[...].astype(F32)))

    def row_copy(t, k):
        return pltpu.make_async_copy(xbuf.at[_tile_rows(t)], xs_ref.at[_tile_rows(dest_ref[k, t])], sem)

    issue, drain = _row_dma_loops(row_copy)
    lax.fori_loop(0, tm // TOKENS_PER_ISSUE, issue, 0)
    lax.fori_loop(0, tm // TOKENS_PER_ISSUE, drain, 0)


def _dispatch(h2, dest, pstart, counts):
    n, d = h2.shape
    tm = 256
    assert d // 2 == SUBLANES * LANES
    grid_spec = pltpu.PrefetchScalarGridSpec(
        num_scalar_prefetch=2,
        grid=(n // tm,),
        in_specs=[
            pl.BlockSpec((TOP_K, tm), lambda i, *_: (0, i), memory_space=pltpu.SMEM),
            pl.BlockSpec((tm, d), lambda i, *_: (i, 0)),
        ],
        out_specs=pl.BlockSpec(memory_space=pl.ANY),
        scratch_shapes=[
            pltpu.VMEM((tm * SUBLANES, LANES), U32),
            pltpu.VMEM((ROW_BLOCK // 2 * SUBLANES, LANES), U32),
            pltpu.SemaphoreType.DMA(()),
            pltpu.SemaphoreType.DMA(()),
        ],
    )
    return pl.pallas_call(
        _dispatch_body,
        grid_spec=grid_spec,
        out_shape=jax.ShapeDtypeStruct((_capacity(n) * SUBLANES, LANES), U32),
        compiler_params=_cparams(("arbitrary",), has_side_effects=True, disable_bounds_checks=True),
        name="dispatch",
    )(pstart, counts, dest, h2)


def _experts_body(be_ref, nu_ref, nxt_ref, ord_ref, x_ref, wg_hbm, wu_hbm, wd_hbm, y_ref,
                  wg_s, wu_s, wd_s, wg_b, wu_b, wd_b, sems):
    i = pl.program_id(0)
    used = i < nu_ref[0]
    e = be_ref[i]
    fresh = jnp.logical_or(i == 0, e != be_ref[jnp.maximum(i - 1, 0)])
    slot = ord_ref[e] % 2

    def weight_copies(expert, s):
        return (pltpu.make_async_copy(wg_hbm.at[expert], wg_s.at[s], sems.at[s, 0]),
                pltpu.make_async_copy(wu_hbm.at[expert], wu_s.at[s], sems.at[s, 1]),
                pltpu.make_async_copy(wd_hbm.at[expert], wd_s.at[s], sems.at[s, 2]))

    @pl.when(i == 0)
    def _():
        for cp in weight_copies(e, slot):
            cp.start()

    @pl.when(jnp.logical_and(used, fresh))
    def _():
        copies = weight_copies(e, slot)
        for cp, dst, src in zip(copies, (wg_b, wu_b, wd_b), (wg_s, wu_s, wd_s)):
            cp.wait()
            dst[...] = src[slot].astype(BF16)

        @pl.when(nxt_ref[e] != e)
        def _():
            for cp in weight_copies(nxt_ref[e], 1 - slot):
                cp.start()

    @pl.when(used)
    def _():
        half = SUBLANES * LANES
        slabs = [_unpack_halves(p) for p in _from_row_tiles(x_ref, ROW_BLOCK)]
        xa = jnp.concatenate([a.astype(BF16) for a, _ in slabs], axis=1)
        xb = jnp.concatenate([b.astype(BF16) for _, b in slabs], axis=1)
        g = (jnp.dot(xa, wg_b[:half], preferred_element_type=F32)
             + jnp.dot(xb, wg_b[half:], preferred_element_type=F32))
        u = (jnp.dot(xa, wu_b[:half], preferred_element_type=F32)
             + jnp.dot(xb, wu_b[half:], preferred_element_type=F32))
        hmid = (g * jax.nn.sigmoid(g) * u).astype(BF16)
        _to_row_tiles(y_ref, _pack_halves(jnp.dot(hmid, wd_b[...], preferred_element_type=F32)))


def _experts(xs, block_e, n_used, next_e, ord_e, w_gate_e, w_up_e, w_down_e):
    cap = xs.shape[0] // SUBLANES
    d = 2 * SUBLANES * LANES
    nblk = cap // ROW_BLOCK
    row_block = (ROW_BLOCK * SUBLANES, LANES)

    def row_map(i, be, nu, nxt, od):
        return (jnp.minimum(i, nu[0] - 1), 0)

    grid_spec = pltpu.PrefetchScalarGridSpec(
        num_scalar_prefetch=4,
        grid=(nblk,),
        in_specs=[
            pl.BlockSpec(row_block, row_map),
            pl.BlockSpec(memory_space=pl.ANY),
            pl.BlockSpec(memory_space=pl.ANY),
            pl.BlockSpec(memory_space=pl.ANY),
        ],
        out_specs=pl.BlockSpec(row_block, row_map),
        scratch_shapes=[
            pltpu.VMEM((2, d, D_EXPERT), F32),
            pltpu.VMEM((2, d, D_EXPERT), F32),
            pltpu.VMEM((2, D_EXPERT, d), F32),
            pltpu.VMEM((d, D_EXPERT), BF16),
            pltpu.VMEM((d, D_EXPERT), BF16),
            pltpu.VMEM((D_EXPERT, d), BF16),
            pltpu.SemaphoreType.DMA((2, 3)),
        ],
    )
    return pl.pallas_call(
        _experts_body,
        grid_spec=grid_spec,
        out_shape=jax.ShapeDtypeStruct(xs.shape, U32),
        compiler_params=_cparams(("arbitrary",)),
        name="experts",
    )(block_e, n_used, next_e, ord_e, xs, w_gate_e, w_up_e, w_down_e)


def _combine_body(dest_ref, ys_ref, wt_ref, h_ref, x1_ref, mod_ref, gpost, wg, wu, wd, o_ref, buf, sem):
    tm = h_ref.shape[0]

    def row_copy(t, k):
        return pltpu.make_async_copy(ys_ref.at[_tile_rows(dest_ref[k, t])], buf.at[k, _tile_rows(t)], sem)

    issue, drain = _row_dma_loops(row_copy)
    lax.fori_loop(0, tm // TOKENS_PER_ISSUE, issue, 0)

    h = h_ref[...]
    g = jnp.dot(h, wg[...], preferred_element_type=F32)
    u = jnp.dot(h, wu[...], preferred_element_type=F32)
    shared = jnp.dot((g * jax.nn.sigmoid(g) * u).astype(BF16), wd[...], preferred_element_type=F32)

    lax.fori_loop(0, tm // TOKENS_PER_ISSUE, drain, 0)

    half = SUBLANES * LANES
    wt = wt_ref[...]
    wk = [jnp.broadcast_to(wt[:, k:k + 1], (tm, LANES)) for k in range(TOP_K)]
    ya, yb = [], []
    for s in range(SUBLANES):
        sa = shared[:, s * LANES:(s + 1) * LANES]
        sb = shared[:, half + s * LANES:half + (s + 1) * LANES]
        for k in range(TOP_K):
            a, b = _unpack_halves(buf[k, pl.ds(s, tm, stride=SUBLANES), :])
            sa = sa + wk[k] * a
            sb = sb + wk[k] * b
        ya.append(sa)
        yb.append(sb)
    y = jnp.concatenate(ya + yb, axis=1)
    gate2 = mod_ref[0, 5:6, :]
    o_ref[...] = x1_ref[...] + gate2 * (_rms(y) * gpost[...])


def _combine(ys, dest, wt, h2, x1, mod3, g_post, wg_b, wu_b, wd_b, seq):
    n, d = h2.shape
    tm = 256
    per_b = seq // tm
    const = lambda shape: pl.BlockSpec(shape, lambda i, *_: (0,) * len(shape), pipeline_mode=pl.Buffered(1))
    grid_spec = pltpu.PrefetchScalarGridSpec(
        num_scalar_prefetch=0,
        grid=(n // tm,),
        in_specs=[
            pl.BlockSpec((TOP_K, tm), lambda i: (0, i), memory_space=pltpu.SMEM),
            pl.BlockSpec(memory_space=pl.ANY),
            pl.BlockSpec((tm, LANES), lambda i: (i, 0)),
            pl.BlockSpec((tm, d), lambda i: (i, 0)),
            pl.BlockSpec((tm, d), lambda i: (i, 0)),
            pl.BlockSpec((1, N_MOD, d), lambda i: (i // per_b, 0, 0)),
            const((1, d)), const(wg_b.shape), const(wu_b.shape), const(wd_b.shape),
        ],
        out_specs=pl.BlockSpec((tm, d), lambda i: (i, 0)),
        scratch_shapes=[pltpu.VMEM((TOP_K, tm * SUBLANES, LANES), U32), pltpu.SemaphoreType.DMA(())],
    )
    return pl.pallas_call(
        _combine_body,
        grid_spec=grid_spec,
        out_shape=jax.ShapeDtypeStruct((n, d), F32),
        compiler_params=_cparams(("arbitrary",), disable_bounds_checks=True),
        name="combine",
    )(dest, ys, wt, h2, x1, mod3, g_post.reshape(1, d), wg_b, wu_b, wd_b)


def _layer(x, mod, positions, g_pre_mix, g_post_mix, g_pre_ffn, g_post_ffn, w_in, w_dil_out, w_sb_out, w_mix_out,
           w_router, router_bias, w_gate_e, w_up_e, w_down_e, w_gate_s, w_up_s, w_down_s):
    bsz, seq, d = x.shape
    n = bsz * seq
    xf = x.reshape(n, d)
    mod3 = mod.reshape(bsz, N_MOD, d)

    nd, nq = 3 * WIDTH_DIL, 3 * (WIDTH_DIL + WIDTH_SB)
    w_plain = jnp.concatenate([w_in[:, nq:], w_in[:, nd:nq]], axis=1).astype(BF16)
    cols = []
    for g in range(len(DIL_PATTERNS)):
        for part in range(3):
            lo = part * WIDTH_DIL + g * D_DIL_OUT
            cols.append(w_in[:, lo:lo + D_DIL_OUT])
    w_dil = jnp.concatenate(cols, axis=1).astype(BF16)

    proj = _inproj(xf, g_pre_mix, mod3, w_plain, seq)
    tables = _rope_tables(positions)
    qkv_dil = _inproj_dil(xf, g_pre_mix, mod3, w_dil, tables, bsz, seq)
    o_dil, lse_dil = [], []
    for g, (window, dilation) in enumerate(DIL_PATTERNS):
        assert window // dilation == Q_BLOCK
        o, lse = _dilated_attention(qkv_dil[g], g, bsz, seq)
        o_dil.append(o)
        lse_dil.append(lse)
    o_sb = _stick_breaking(proj, bsz, seq)

    x1, h2 = _mixout(o_dil, lse_dil, o_sb, proj, xf, mod3, g_post_mix, g_pre_ffn,
                     w_dil_out.astype(BF16), w_sb_out.astype(BF16), w_mix_out.astype(BF16), seq)

    top_idx, top_wt = _router(h2, w_router, router_bias)
    counts = _plan_counts(top_idx)[:, 0].astype(I32)
    padded = (counts + ROW_BLOCK - 1) // ROW_BLOCK * ROW_BLOCK
    pends = jnp.cumsum(padded)
    pstart = pends - padded
    nblk = _capacity(n) // ROW_BLOCK
    n_used = (pends[-1] // ROW_BLOCK).astype(I32)
    blk = jnp.minimum(jnp.arange(nblk, dtype=I32), n_used - 1)
    block_e = jnp.minimum(jnp.sum(pends[None, :] <= (blk * ROW_BLOCK)[:, None], axis=1), N_EXPERTS - 1).astype(I32)
    eids = jnp.arange(N_EXPERTS, dtype=I32)
    has = counts > 0
    ord_e = (jnp.cumsum(has) - has).astype(I32)
    later = jnp.where((eids[None, :] > eids[:, None]) & has[None, :], eids[None, :], N_EXPERTS)
    nxt = jnp.min(later, axis=1)
    next_e = jnp.where(nxt < N_EXPERTS, nxt, eids).astype(I32)
    dest = _plan_dest(top_idx, pstart)

    xs = _dispatch(h2, dest, pstart.astype(I32), counts)
    ys = _experts(xs, block_e, n_used.reshape(1), next_e, ord_e, w_gate_e, w_up_e, w_down_e)
    out = _combine(ys, dest, top_wt, h2, x1, mod3, g_post_ffn,
                   w_gate_s.astype(BF16), w_up_s.astype(BF16), w_down_s.astype(BF16), seq)
    return out.reshape(bsz, seq, d)


def kernel(x, c, positions, w_ada, b_ada, g_pre_mix, g_post_mix, g_pre_ffn, g_post_ffn, w_in, w_dil_out,
           w_sb_out, w_mix_out, w_router, router_bias, w_gate_e, w_up_e, w_down_e, w_gate_s, w_up_s, w_down_s):
    for l in range(w_ada.shape[0]):
        mod = _adaln(c, w_ada[l], b_ada[l])
        x = _layer(x, mod, positions, g_pre_mix[l], g_post_mix[l], g_pre_ffn[l], g_post_ffn[l], w_in[l],
                   w_dil_out[l], w_sb_out[l], w_mix_out[l], w_router[l], router_bias[l],
                   w_gate_e[l], w_up_e[l], w_down_e[l], w_gate_s[l], w_up_s[l], w_down_s[l])
    return x
```

```python
import functools

import jax
import jax.numpy as jnp
from jax import lax
from jax.experimental import pallas as pl
from jax.experimental.pallas import tpu as pltpu

F32 = jnp.float32
BF16 = jnp.bfloat16
I32 = jnp.int32
U32 = jnp.uint32

D_MODEL = 2048
HEAD_DIM = 128
DIL_PATTERNS = ((128, 1), (512, 4), (2048, 16))
HEADS_PER_GROUP = 4
N_HEADS_DIL = 12
N_HEADS_SB = 8
WIDTH_DIL = N_HEADS_DIL * HEAD_DIM
WIDTH_SB = N_HEADS_SB * HEAD_DIM
D_DIL_OUT = HEADS_PER_GROUP * HEAD_DIM
Q_BLOCK = 128
ROPE_THETA = 500000.0
ROPE_DIM = HEAD_DIM // 4
N_GATE = 2 * D_MODEL
N_EXPERTS = 64
TOP_K = 8
N_GROUPS = 8
GROUP_SIZE = N_EXPERTS // N_GROUPS
TOPK_GROUPS = 4
D_EXPERT = 512
D_SHARED = 512
ROUTED_SCALE = 2.5
RMS_EPS = 1e-6
N_MOD = 6
ATTN_SCALE = HEAD_DIM ** -0.5

LANES = 128
SUBLANES = 8
VMEM_LIMIT = 56 * 1024 * 1024

ROW_BLOCK = 512
SB_DEAD = -110.0


def _cparams(sem, **kw):
    return pltpu.CompilerParams(dimension_semantics=sem, vmem_limit_bytes=VMEM_LIMIT, **kw)


def _adaln_body(ct_ref, w_ref, b_ref, o_ref, *, kc):
    nb = ct_ref.shape[1]
    nk = w_ref.shape[0] // kc

    def step(i, acc):
        k0 = pl.multiple_of(i * kc, kc)
        w = w_ref[pl.ds(k0, kc), :]
        c = ct_ref[pl.ds(k0, kc), :]
        s = c * jax.nn.sigmoid(c)
        parts = [jnp.sum(w * s[:, b:b + 1], axis=0, keepdims=True) for b in range(nb)]
        return acc + jnp.concatenate(parts, axis=0)

    acc = lax.fori_loop(0, nk, step, jnp.zeros(o_ref.shape, F32))
    o_ref[...] = acc + b_ref[...]


def _adaln(c, w_ada, b_ada):
    nb, d = c.shape
    n_out = w_ada.shape[1]
    tn = 1024
    return pl.pallas_call(
        functools.partial(_adaln_body, kc=256),
        grid=(n_out // tn,),
        in_specs=[
            pl.BlockSpec((d, nb), lambda j: (0, 0)),
            pl.BlockSpec((d, tn), lambda j: (0, j)),
            pl.BlockSpec((1, tn), lambda j: (0, j)),
        ],
        out_specs=pl.BlockSpec((nb, tn), lambda j: (0, j)),
        out_shape=jax.ShapeDtypeStruct((nb, n_out), F32),
        compiler_params=_cparams(("arbitrary",)),
        name="adaln",
    )(c.T, w_ada, b_ada.reshape(1, n_out))


def _rms(x):
    return x * lax.rsqrt(jnp.mean(x * x, axis=-1, keepdims=True) + RMS_EPS)


def _prenorm(x_ref, g_ref, mod_ref):
    y = _rms(x_ref[...]) * g_ref[...]
    shift = mod_ref[0, 0:1, :]
    scale = mod_ref[0, 1:2, :]
    return (y * (1.0 + scale) + shift).astype(BF16)


def _inproj_body(x_ref, g_ref, mod_ref, w_ref, o_ref, h_ref):
    @pl.when(pl.program_id(1) == 0)
    def _():
        h_ref[...] = _prenorm(x_ref, g_ref, mod_ref)

    o_ref[...] = jnp.dot(h_ref[...], w_ref[...], preferred_element_type=F32).astype(o_ref.dtype)


def _inproj(xf, g_pre, mod3, w_b, seq):
    n, d = xf.shape
    width = w_b.shape[1]
    tm, tn = 512, width // 4
    per_b = seq // tm
    return pl.pallas_call(
        _inproj_body,
        grid=(n // tm, width // tn),
        in_specs=[
            pl.BlockSpec((tm, d), lambda i, j: (i, 0)),
            pl.BlockSpec((1, d), lambda i, j: (0, 0)),
            pl.BlockSpec((1, N_MOD, d), lambda i, j: (i // per_b, 0, 0)),
            pl.BlockSpec((d, tn), lambda i, j: (0, j)),
        ],
        out_specs=pl.BlockSpec((tm, tn), lambda i, j: (i, j)),
        out_shape=jax.ShapeDtypeStruct((n, width), BF16),
        scratch_shapes=[pltpu.VMEM((tm, d), BF16)],
        compiler_params=_cparams(("arbitrary", "arbitrary")),
        name="inproj",
    )(xf, g_pre.reshape(1, d), mod3, w_b)


def _inproj_dil_body(x_ref, g_ref, mod_ref, w_ref, t_ref, o0, o1, o2, res_ref):
    tm = x_ref.shape[0]
    h = _prenorm(x_ref, g_ref, mod_ref)
    t = t_ref[...]
    gw = 3 * D_DIL_OUT
    for gi, o_ref in enumerate((o0, o1, o2)):
        dil = DIL_PATTERNS[gi][1]
        res = jnp.dot(h, w_ref[:, gi * gw:(gi + 1) * gw], preferred_element_type=F32)
        for hs in range(3 * HEADS_PER_GROUP):
            sl = slice(hs * HEAD_DIM, (hs + 1) * HEAD_DIM)
            res_ref[hs] = _apply_rope(res[:, sl], t) if hs < 2 * HEADS_PER_GROUP else res[:, sl]
        for r in range(dil):
            for hs in range(3 * HEADS_PER_GROUP):
                rows = res_ref[hs] if dil == 1 else res_ref[hs, pl.ds(r, tm // dil, stride=dil), :]
                o_ref[0, r, :, hs * HEAD_DIM:(hs + 1) * HEAD_DIM] = rows.astype(o_ref.dtype)


def _inproj_dil(xf, g_pre, mod3, w_b, tables, bsz, seq):
    n, d = xf.shape
    gw = 3 * D_DIL_OUT
    tm = 512
    per_b = seq // tm
    dils = [p[1] for p in DIL_PATTERNS]
    return pl.pallas_call(
        _inproj_dil_body,
        grid=(n // tm,),
        in_specs=[
            pl.BlockSpec((tm, d), lambda i: (i, 0)),
            pl.BlockSpec((1, d), lambda i: (0, 0)),
            pl.BlockSpec((1, N_MOD, d), lambda i: (i // per_b, 0, 0)),
            pl.BlockSpec(w_b.shape, lambda i: (0, 0), pipeline_mode=pl.Buffered(1)),
            pl.BlockSpec((tm, 3 * LANES), lambda i: (i, 0)),
        ],
        out_specs=[pl.BlockSpec((1, dl, tm // dl, gw), lambda i: (i // per_b, 0, i % per_b, 0)) for dl in dils],
        out_shape=[jax.ShapeDtypeStruct((bsz, dl, seq // dl, gw), BF16) for dl in dils],
        scratch_shapes=[pltpu.VMEM((gw // HEAD_DIM, tm, HEAD_DIM), F32)],
        compiler_params=_cparams(("arbitrary",)),
        name="inproj_dil",
    )(xf, g_pre.reshape(1, d), mod3, w_b, tables)


def _rope_body(pos_ref, f_ref, o_ref):
    ang = pos_ref[...].astype(F32) * f_ref[...]
    c = jnp.cos(ang)
    s = jnp.sin(ang)
    lane = lax.broadcasted_iota(I32, ang.shape, 1)
    half = ROPE_DIM // 2
    o_ref[:, 0:LANES] = c
    o_ref[:, LANES:2 * LANES] = jnp.where(lane >= half, s, 0.0)
    o_ref[:, 2 * LANES:3 * LANES] = jnp.where(lane < half, -s, 0.0)


def _rope_tables(positions):
    n = positions.size
    half = ROPE_DIM // 2
    inv_freq = ROPE_THETA ** (-jnp.arange(0, ROPE_DIM, 2, dtype=F32) / ROPE_DIM)
    f = jnp.concatenate([inv_freq, inv_freq, jnp.zeros((LANES - 2 * half,), F32)]).reshape(1, LANES)
    tm = 2048
    return pl.pallas_call(
        _rope_body,
        grid=(n // tm,),
        in_specs=[pl.BlockSpec((tm, 1), lambda i: (i, 0)), pl.BlockSpec((1, LANES), lambda i: (0, 0))],
        out_specs=pl.BlockSpec((tm, 3 * LANES), lambda i: (i, 0)),
        out_shape=jax.ShapeDtypeStruct((n, 3 * LANES), F32),
        compiler_params=_cparams(("arbitrary",)),
        name="rope_tables",
    )(positions.reshape(n, 1), f)


def _apply_rope(x, t):
    half = ROPE_DIM // 2
    return (x * t[:, 0:LANES]
            + pltpu.roll(x, half, 1) * t[:, LANES:2 * LANES]
            + pltpu.roll(x, LANES - half, 1) * t[:, 2 * LANES:3 * LANES])


def _dil_body(cur_ref, kp_ref, vp_ref, o_ref, lse_ref, obuf, lbuf, *, dil, nsub):
    n = pl.program_id(1)
    tq = nsub * Q_BLOCK
    row = lax.broadcasted_iota(I32, (Q_BLOCK, 2 * Q_BLOCK), 0)
    col = lax.broadcasted_iota(I32, (Q_BLOCK, 2 * Q_BLOCK), 1)
    rel = row + Q_BLOCK - col
    band = jnp.where(rel >= 0, jnp.where(rel <= Q_BLOCK, 1.0, 0.0), 0.0)
    first = jnp.where(col >= Q_BLOCK, band, jnp.where(n > 0, band, 0.0))
    for r in range(dil):
        for h in range(HEADS_PER_GROUP):
            sl = slice(h * HEAD_DIM, (h + 1) * HEAD_DIM)
            ksl = slice(D_DIL_OUT + h * HEAD_DIM, D_DIL_OUT + (h + 1) * HEAD_DIM)
            vsl = slice(2 * D_DIL_OUT + h * HEAD_DIM, 2 * D_DIL_OUT + (h + 1) * HEAD_DIM)
            for j in range(nsub):
                rs = slice(j * Q_BLOCK, (j + 1) * Q_BLOCK)
                ps = slice((j - 1) * Q_BLOCK, j * Q_BLOCK)
                kprev = kp_ref[0, r, :, sl] if j == 0 else cur_ref[0, r, ps, ksl]
                vprev = vp_ref[0, r, :, sl] if j == 0 else cur_ref[0, r, ps, vsl]
                kcat = jnp.concatenate([kprev, cur_ref[0, r, rs, ksl]], axis=0)
                vcat = jnp.concatenate([vprev, cur_ref[0, r, rs, vsl]], axis=0)
                s = lax.dot_general(cur_ref[0, r, rs, sl], kcat, (((1,), (1,)), ((), ())),
                                    preferred_element_type=F32) * ATTN_SCALE
                s = jnp.where((first if j == 0 else band) > 0.0, s, -jnp.inf)
                m = jnp.max(s, axis=-1, keepdims=True)
                p = jnp.exp(s - m)
                l = jnp.sum(p, axis=-1, keepdims=True)
                o = jnp.dot((p / l).astype(BF16), vcat, preferred_element_type=F32)
                lse = jnp.broadcast_to(m + jnp.log(l), (Q_BLOCK, HEAD_DIM))
                if dil == 1:
                    o_ref[h, rs, :] = o
                    lse_ref[h, rs, :] = lse
                else:
                    obuf[h, rs, :] = o
                    lbuf[h, rs, :] = lse
            if dil > 1:
                o_ref[h, pl.ds(r, tq, stride=dil), :] = obuf[h]
                lse_ref[h, pl.ds(r, tq, stride=dil), :] = lbuf[h]


def _dilated_attention(qkv, g, bsz, seq):
    dil = DIL_PATTERNS[g][1]
    length = seq // dil
    tq = min(4 * Q_BLOCK, (16 * Q_BLOCK) // dil, length)
    nsub = tq // Q_BLOCK
    nq = length // tq
    gw = 3 * D_DIL_OUT
    n = bsz * seq

    def prev(colblk):
        return pl.BlockSpec((1, dil, Q_BLOCK, D_DIL_OUT),
                            lambda b, i: (b, 0, jnp.maximum(i * nsub - 1, 0), colblk))

    nh = HEADS_PER_GROUP
    out_spec = pl.BlockSpec((nh, tq * dil, HEAD_DIM), lambda b, i: (0, b * nq + i, 0))
    out_shape = jax.ShapeDtypeStruct((nh, n, HEAD_DIM), F32)
    return pl.pallas_call(
        functools.partial(_dil_body, dil=dil, nsub=nsub),
        grid=(bsz, nq),
        in_specs=[pl.BlockSpec((1, dil, tq, gw), lambda b, i: (b, 0, i, 0)), prev(1), prev(2)],
        out_specs=[out_spec, out_spec],
        out_shape=[out_shape, out_shape],
        scratch_shapes=[pltpu.VMEM((nh, tq, HEAD_DIM), F32), pltpu.VMEM((nh, tq, HEAD_DIM), F32)],
        compiler_params=_cparams(("arbitrary", "arbitrary")),
        name=f"dilated_d{dil}",
    )(qkv, qkv, qkv)


def _sb_body(q_ref, k_ref, v_ref, o_ref, acc_ref, car_ref):
    nblk = q_ref.shape[1] // Q_BLOCK
    r = lax.broadcasted_iota(I32, (Q_BLOCK, Q_BLOCK), 0)
    c = lax.broadcasted_iota(I32, (Q_BLOCK, Q_BLOCK), 1)
    causal = c < r
    rr = lax.broadcasted_iota(I32, (Q_BLOCK, 2 * Q_BLOCK), 0)
    cc = lax.broadcasted_iota(I32, (Q_BLOCK, 2 * Q_BLOCK), 1)
    uo = jnp.where(cc >= Q_BLOCK, 1.0, jnp.where(rr > cc, 1.0, 0.0)).astype(BF16)

    nh = acc_ref.shape[0]
    heads = range(nh)

    def tiles(qs, kb, carries, diag):
        k0 = pl.multiple_of(kb * Q_BLOCK, Q_BLOCK)
        hs = [slice(h * HEAD_DIM, (h + 1) * HEAD_DIM) for h in heads]
        zs = [lax.dot_general(qs[h], k_ref[0, pl.ds(k0, Q_BLOCK), hs[h]], (((1,), (1,)), ((), ())),
                              preferred_element_type=F32) * ATTN_SCALE for h in heads]
        stacked, log_s = [], []
        for z in zs:
            sp = jnp.log(1.0 + jnp.exp(-jnp.abs(z)))
            mx = jnp.maximum(z, 0.0)
            log_1m = -(mx + sp)
            if diag:
                log_1m = jnp.where(causal, log_1m, 0.0)
            hi = log_1m.astype(BF16)
            lo = (log_1m - hi.astype(F32)).astype(BF16)
            stacked.append(jnp.concatenate([hi, lo], axis=0))
            log_s.append((z - mx) - sp)
        r2s = [jnp.dot(s, uo, preferred_element_type=F32) for s in stacked]
        probs, new_carries = [], []
        for h in heads:
            sums = r2s[h][:Q_BLOCK] + r2s[h][Q_BLOCK:]
            a = jnp.exp(log_s[h] + carries[h] + sums[:, :Q_BLOCK])
            if diag:
                a = jnp.where(causal, a, 0.0)
            probs.append(a.astype(BF16))
            new_carries.append(carries[h] + sums[:, Q_BLOCK:])
        pvs = [jnp.dot(probs[h], v_ref[0, pl.ds(k0, Q_BLOCK), hs[h]], preferred_element_type=F32) for h in heads]
        return pvs, new_carries

    def all_max(xs):
        m = xs[0]
        for x in xs[1:]:
            m = jnp.maximum(m, x)
        return jnp.max(m)

    def qblock(qi, _):
        q0 = pl.multiple_of(qi * Q_BLOCK, Q_BLOCK)
        qs = [q_ref[0, pl.ds(q0, Q_BLOCK), h * HEAD_DIM:(h + 1) * HEAD_DIM] for h in heads]
        zero = jnp.zeros((Q_BLOCK, Q_BLOCK), F32)
        pvs, cars = tiles(qs, qi, [zero] * nh, True)
        for h in heads:
            acc_ref[h] = pvs[h]
            car_ref[h] = cars[h]

        def cond(st):
            return jnp.logical_and(st[0] >= 0, st[1] > SB_DEAD)

        def body(st):
            pvs, cars = tiles(qs, st[0], [car_ref[h] for h in heads], False)
            for h in heads:
                acc_ref[h] += pvs[h]
                car_ref[h] = cars[h]
            return st[0] - 1, all_max(cars)

        lax.while_loop(cond, body, (qi - 1, all_max(cars)))
        for h in range(nh):
            o_ref[0, pl.ds(q0, Q_BLOCK), h * HEAD_DIM:(h + 1) * HEAD_DIM] = acc_ref[h].astype(o_ref.dtype)
        return 0

    lax.fori_loop(0, nblk, qblock, 0)


SB_HEADS_PER_STEP = 8


def _stick_breaking(proj, bsz, seq):
    width = proj.shape[1]
    pv = proj.reshape(bsz, seq, width)
    nh = SB_HEADS_PER_STEP
    bw = nh * HEAD_DIM
    base = N_GATE // bw

    def spec(off):
        return pl.BlockSpec((1, seq, bw), lambda b, h: (b, 0, base + off + h), pipeline_mode=pl.Buffered(1))

    nstep = N_HEADS_SB // nh
    o = pl.pallas_call(
        _sb_body,
        grid=(bsz, nstep),
        in_specs=[spec(0), spec(nstep), spec(2 * nstep)],
        out_specs=pl.BlockSpec((1, seq, bw), lambda b, h: (b, 0, h)),
        out_shape=jax.ShapeDtypeStruct((bsz, seq, WIDTH_SB), BF16),
        scratch_shapes=[pltpu.VMEM((nh, Q_BLOCK, Q_BLOCK), F32), pltpu.VMEM((nh, Q_BLOCK, Q_BLOCK), F32)],
        compiler_params=_cparams(("arbitrary", "arbitrary")),
        name="stick_breaking",
    )(pv, pv, pv)
    return o.reshape(bsz * seq, WIDTH_SB)


def _mixout_body(o1, o2, o3, l1, l2, l3, osb, gd_ref, gs_ref, x_ref, mod_ref, gpost, gpre,
                 wd, ws, wm, x1_ref, h2_ref):
    heads = []
    for h in range(HEADS_PER_GROUP):
        la, lb, lc = l1[h], l2[h], l3[h]
        m = jnp.maximum(la, jnp.maximum(lb, lc))
        ea, eb, ec = jnp.exp(la - m), jnp.exp(lb - m), jnp.exp(lc - m)
        heads.append(((ea * o1[h] + eb * o2[h] + ec * o3[h]) / (ea + eb + ec)).astype(BF16))
    yd = jnp.dot(jnp.concatenate(heads, axis=1), wd[...], preferred_element_type=F32)
    ys = jnp.dot(osb[...], ws[...], preferred_element_type=F32)
    mix = jax.nn.sigmoid(gd_ref[...].astype(F32)) * yd + jax.nn.sigmoid(gs_ref[...].astype(F32)) * ys
    y = jnp.dot(mix.astype(BF16), wm[...], preferred_element_type=F32)
    gate1 = mod_ref[0, 2:3, :]
    shift2 = mod_ref[0, 3:4, :]
    scale2 = mod_ref[0, 4:5, :]
    x1 = x_ref[...] + gate1 * (_rms(y) * gpost[...])
    x1_ref[...] = x1
    h2_ref[...] = ((_rms(x1) * gpre[...]) * (1.0 + scale2) + shift2).astype(h2_ref.dtype)


def _const_spec(shape):
    return pl.BlockSpec(shape, lambda i: (0,) * len(shape), pipeline_mode=pl.Buffered(1))


def _mixout(o_dil, lse_dil, o_sb, proj, xf, mod3, g_post, g_pre, wd_b, ws_b, wm_b, seq):
    n, d = xf.shape
    tm = 256
    per_b = seq // tm
    row = lambda w: pl.BlockSpec((tm, w), lambda i: (i, 0))
    head_major = pl.BlockSpec((HEADS_PER_GROUP, tm, HEAD_DIM), lambda i: (0, i, 0))
    in_specs = (
        [head_major] * 6 + [row(WIDTH_SB)]
        + [pl.BlockSpec((tm, d), lambda i: (i, 0)), pl.BlockSpec((tm, d), lambda i: (i, 1))]
        + [row(d), pl.BlockSpec((1, N_MOD, d), lambda i: (i // per_b, 0, 0))]
        + [_const_spec((1, d)), _const_spec((1, d))]
        + [_const_spec(wd_b.shape), _const_spec(ws_b.shape), _const_spec(wm_b.shape)]
    )
    return pl.pallas_call(
        _mixout_body,
        grid=(n // tm,),
        in_specs=in_specs,
        out_specs=[row(d), row(d)],
        out_shape=[jax.ShapeDtypeStruct((n, d), F32), jax.ShapeDtypeStruct((n, d), BF16)],
        compiler_params=_cparams(("arbitrary",)),
        name="mixout",
    )(*o_dil, *lse_dil, o_sb, proj, proj, xf, mod3, g_post.reshape(1, d), g_pre.reshape(1, d), wd_b, ws_b, wm_b)


def _topk_rows(x, k, iota0):
    big = x.shape[0]
    out = []
    for _ in range(k):
        m = jnp.max(x, axis=0, keepdims=True)
        i = jnp.min(jnp.where(x == m, iota0, big), axis=0, keepdims=True)
        out.append((m, i))
        x = jnp.where(iota0 == i, -jnp.inf, x)
    return out


def _router_body(h_ref, wr_ref, bias_ref, idx_ref, wt_ref):
    tm = h_ref.shape[0]
    logits = lax.dot_general(wr_ref[...], h_ref[...], (((1,), (1,)), ((), ())), preferred_element_type=F32)
    scores = jax.nn.sigmoid(logits)
    sel = scores + bias_ref[...]
    sub = lax.broadcasted_iota(I32, (GROUP_SIZE, tm), 0)
    grp = []
    for g in range(N_GROUPS):
        (m1, _), (m2, _) = _topk_rows(sel[g * GROUP_SIZE:(g + 1) * GROUP_SIZE], 2, sub)
        grp.append(m1 + m2)
    gscore = jnp.concatenate(grp, axis=0)
    giota = lax.broadcasted_iota(I32, (N_GROUPS, tm), 0)
    gmask = jnp.zeros((N_GROUPS, tm), F32)
    for _, gi in _topk_rows(gscore, TOPK_GROUPS, giota):
        gmask = jnp.where(giota == gi, 1.0, gmask)
    masked = jnp.concatenate(
        [jnp.where(gmask[g:g + 1] > 0.0, sel[g * GROUP_SIZE:(g + 1) * GROUP_SIZE], -jnp.inf)
         for g in range(N_GROUPS)], axis=0)
    eiota = lax.broadcasted_iota(I32, (N_EXPERTS, tm), 0)
    picks = _topk_rows(masked, TOP_K, eiota)
    idx = jnp.concatenate([i for _, i in picks], axis=0)
    top_s = jnp.concatenate(
        [jnp.sum(jnp.where(eiota == i, scores, 0.0), axis=0, keepdims=True) for _, i in picks], axis=0)
    top_w = top_s / jnp.sum(top_s, axis=0, keepdims=True) * ROUTED_SCALE
    idx_ref[...] = idx
    wpad = jnp.concatenate([top_w, jnp.zeros((LANES - TOP_K, tm), F32)], axis=0)
    wt_ref[...] = wpad.T


def _router(h2, w_router, router_bias):
    n, d = h2.shape
    tm = 512
    return pl.pallas_call(
        _router_body,
        grid=(n // tm,),
        in_specs=[
            pl.BlockSpec((tm, d), lambda i: (i, 0)),
            pl.BlockSpec((N_EXPERTS, d), lambda i: (0, 0)),
            pl.BlockSpec((N_EXPERTS, 1), lambda i: (0, 0)),
        ],
        out_specs=[pl.BlockSpec((TOP_K, tm), lambda i: (0, i)), pl.BlockSpec((tm, LANES), lambda i: (i, 0))],
        out_shape=[jax.ShapeDtypeStruct((TOP_K, n), I32), jax.ShapeDtypeStruct((n, LANES), F32)],
        compiler_params=_cparams(("arbitrary",)),
        name="router",
    )(h2, w_router.T.astype(BF16), router_bias.reshape(N_EXPERTS, 1))


def _plan_tile(idx):
    tm = idx.shape[1]
    eiota = lax.broadcasted_iota(I32, (N_EXPERTS, tm), 0)
    hit = jnp.zeros((N_EXPERTS, tm), F32)
    for k in range(TOP_K):
        hit = jnp.where(eiota == idx[k:k + 1], 1.0, hit)
    r = lax.broadcasted_iota(I32, (tm, tm), 0)
    c = lax.broadcasted_iota(I32, (tm, tm), 1)
    before = jnp.where(r < c, 1.0, 0.0).astype(BF16)
    excl = jnp.dot(hit.astype(BF16), before, preferred_element_type=F32)
    tot = excl[:, tm - 1:tm] + hit[:, tm - 1:tm]
    return eiota, hit, excl, tot


def _plan_counts_body(idx_ref, cnt_ref):
    @pl.when(pl.program_id(0) == 0)
    def _():
        cnt_ref[...] = jnp.zeros(cnt_ref.shape, F32)

    _, _, _, tot = _plan_tile(idx_ref[...])
    cnt_ref[...] += tot


def _plan_counts(top_idx):
    n = top_idx.shape[1]
    tm = 512
    return pl.pallas_call(
        _plan_counts_body,
        grid=(n // tm,),
        in_specs=[pl.BlockSpec((TOP_K, tm), lambda i: (0, i))],
        out_specs=pl.BlockSpec((N_EXPERTS, LANES), lambda i: (0, 0)),
        out_shape=jax.ShapeDtypeStruct((N_EXPERTS, LANES), F32),
        compiler_params=_cparams(("arbitrary",)),
        name="plan_counts",
    )(top_idx)


def _plan_dest_body(idx_ref, base_ref, dest_ref, run_ref):
    @pl.when(pl.program_id(0) == 0)
    def _():
        run_ref[...] = base_ref[...]

    idx = idx_ref[...]
    eiota, _, excl, tot = _plan_tile(idx)
    pos = run_ref[:, 0:1] + excl
    rows = [jnp.sum(jnp.where(eiota == idx[k:k + 1], pos, 0.0), axis=0, keepdims=True) for k in range(TOP_K)]
    dest_ref[...] = jnp.concatenate(rows, axis=0).astype(I32)
    run_ref[...] += tot


def _plan_dest(top_idx, pstart):
    n = top_idx.shape[1]
    tm = 512
    base = jnp.broadcast_to(pstart.astype(F32).reshape(N_EXPERTS, 1), (N_EXPERTS, LANES))
    return pl.pallas_call(
        _plan_dest_body,
        grid=(n // tm,),
        in_specs=[pl.BlockSpec((TOP_K, tm), lambda i: (0, i)), pl.BlockSpec((N_EXPERTS, LANES), lambda i: (0, 0))],
        out_specs=pl.BlockSpec((TOP_K, tm), lambda i: (0, i)),
        out_shape=jax.ShapeDtypeStruct((TOP_K, n), I32),
        scratch_shapes=[pltpu.VMEM((N_EXPERTS, LANES), F32)],
        compiler_params=_cparams(("arbitrary",)),
        name="plan_dest",
    )(top_idx, base)


def _capacity(n):
    return n * TOP_K + N_EXPERTS * ROW_BLOCK


def _pack_halves(y):
    w = y.shape[1] // 2
    hi = pltpu.bitcast(y[:, :w].astype(BF16).astype(F32), U32)
    lo = pltpu.bitcast(y[:, w:].astype(BF16).astype(F32), U32)
    return hi | (lo >> 16)


def _unpack_halves(p):
    a = pltpu.bitcast(p & jnp.uint32(0xFFFF0000), F32)
    b = pltpu.bitcast(p << 16, F32)
    return a, b


TOKENS_PER_ISSUE = 2


def _row_dma_loops(row_copy):
    def issue(i, _):
        for u in range(TOKENS_PER_ISSUE):
            for k in range(TOP_K):
                row_copy(i * TOKENS_PER_ISSUE + u, k).start(priority=k % 2)
        return 0

    def drain(i, _):
        for u in range(TOKENS_PER_ISSUE):
            for k in range(TOP_K):
                row_copy(i * TOKENS_PER_ISSUE + u, k).wait()
        return 0

    return issue, drain


def _pad_chunks():
    sizes, s = [], ROW_BLOCK // 2
    while s >= 1:
        sizes.append(s)
        s //= 2
    return sizes


def _tile_rows(t):
    return pl.ds(pl.multiple_of(t * SUBLANES, SUBLANES), SUBLANES)


def _to_row_tiles(ref, packed, lead=()):
    rows = packed.shape[0]
    for s in range(SUBLANES):
        ref[(*lead, pl.ds(s, rows, stride=SUBLANES), slice(None))] = packed[:, s * LANES:(s + 1) * LANES]


def _from_row_tiles(ref, rows, lead=()):
    return [ref[(*lead, pl.ds(s, rows, stride=SUBLANES), slice(None))] for s in range(SUBLANES)]


def _dispatch_body(pstart_ref, cnt_ref, dest_ref, h_ref, xs_ref, xbuf, zbuf, sem, zsem):
    tm = h_ref.shape[0]
    step = pl.program_id(0)

    def pad_dmas(wait):
        def per_expert(e, _):
            cnt = cnt_ref[e]
            pad = (ROW_BLOCK - cnt % ROW_BLOCK) % ROW_BLOCK
            start = pstart_ref[e] + cnt
            for size in _pad_chunks():
                @pl.when((pad & size) != 0)
                def _():
                    off = pl.multiple_of((start + (pad & ~(2 * size - 1))) * SUBLANES, SUBLANES)
                    cp = pltpu.make_async_copy(zbuf.at[pl.ds(0, size * SUBLANES)],
                                               xs_ref.at[pl.ds(off, size * SUBLANES)], zsem)
                    cp.wait() if wait else cp.start()
            return 0
        lax.fori_loop(0, N_EXPERTS, per_expert, 0)

    @pl.when(step == 0)
    def _():
        zbuf[...] = jnp.zeros(zbuf.shape, zbuf.dtype)
        pad_dmas(False)
        pad_dmas(True)

    _to_row_tiles(xbuf, _pack_halves(h_ref[...].astype(F32)))

    def row_copy(t, k):
        return pltpu.make_async_copy(xbuf.at[_tile_rows(t)], xs_ref.at[_tile_rows(dest_ref[k, t])], sem)

    issue, drain = _row_dma_loops(row_copy)
    lax.fori_loop(0, tm // TOKENS_PER_ISSUE, issue, 0)
    lax.fori_loop(0, tm // TOKENS_PER_ISSUE, drain, 0)


def _dispatch(h2, dest, pstart, counts):
    n, d = h2.shape
    tm = 256
    assert d // 2 == SUBLANES * LANES
    grid_spec = pltpu.PrefetchScalarGridSpec(
        num_scalar_prefetch=2,
        grid=(n // tm,),
        in_specs=[
            pl.BlockSpec((TOP_K, tm), lambda i, *_: (0, i), memory_space=pltpu.SMEM),
            pl.BlockSpec((tm, d), lambda i, *_: (i, 0)),
        ],
        out_specs=pl.BlockSpec(memory_space=pl.ANY),
        scratch_shapes=[
            pltpu.VMEM((tm * SUBLANES, LANES), U32),
            pltpu.VMEM((ROW_BLOCK // 2 * SUBLANES, LANES), U32),
            pltpu.SemaphoreType.DMA(()),
            pltpu.SemaphoreType.DMA(()),
        ],
    )
    return pl.pallas_call(
        _dispatch_body,
        grid_spec=grid_spec,
        out_shape=jax.ShapeDtypeStruct((_capacity(n) * SUBLANES, LANES), U32),
        compiler_params=_cparams(("arbitrary",), has_side_effects=True, disable_bounds_checks=True),
        name="dispatch",
    )(pstart, counts, dest, h2)


def _experts_body(be_ref, nu_ref, nxt_ref, ord_ref, x_ref, wg_hbm, wu_hbm, wd_hbm, y_ref,
                  wg_s, wu_s, wd_s, wg_b, wu_b, wd_b, sems):
    i = pl.program_id(0)
    used = i < nu_ref[0]
    e = be_ref[i]
    fresh = jnp.logical_or(i == 0, e != be_ref[jnp.maximum(i - 1, 0)])
    slot = ord_ref[e] % 2

    def weight_copies(expert, s):
        return (pltpu.make_async_copy(wg_hbm.at[expert], wg_s.at[s], sems.at[s, 0]),
                pltpu.make_async_copy(wu_hbm.at[expert], wu_s.at[s], sems.at[s, 1]),
                pltpu.make_async_copy(wd_hbm.at[expert], wd_s.at[s], sems.at[s, 2]))

    @pl.when(i == 0)
    def _():
        for cp in weight_copies(e, slot):
            cp.start()

    @pl.when(jnp.logical_and(used, fresh))
    def _():
        copies = weight_copies(e, slot)
        for cp, dst, src in zip(copies, (wg_b, wu_b, wd_b), (wg_s, wu_s, wd_s)):
            cp.wait()
            dst[...] = src[slot].astype(BF16)

        @pl.when(nxt_ref[e] != e)
        def _():
            for cp in weight_copies(nxt_ref[e], 1 - slot):
                cp.start()

    @pl.when(used)
    def _():
        half = SUBLANES * LANES
        slabs = [_unpack_halves(p) for p in _from_row_tiles(x_ref, ROW_BLOCK)]
        xa = jnp.concatenate([a.astype(BF16) for a, _ in slabs], axis=1)
        xb = jnp.concatenate([b.astype(BF16) for _, b in slabs], axis=1)
        g = (jnp.dot(xa, wg_b[:half], preferred_element_type=F32)
             + jnp.dot(xb, wg_b[half:], preferred_element_type=F32))
        u = (jnp.dot(xa, wu_b[:half], preferred_element_type=F32)
             + jnp.dot(xb, wu_b[half:], preferred_element_type=F32))
        hmid = (g * jax.nn.sigmoid(g) * u).astype(BF16)
        _to_row_tiles(y_ref, _pack_halves(jnp.dot(hmid, wd_b[...], preferred_element_type=F32)))


def _experts(xs, block_e, n_used, next_e, ord_e, w_gate_e, w_up_e, w_down_e):
    cap = xs.shape[0] // SUBLANES
    d = 2 * SUBLANES * LANES
    nblk = cap // ROW_BLOCK
    row_block = (ROW_BLOCK * SUBLANES, LANES)

    def row_map(i, be, nu, nxt, od):
        return (jnp.minimum(i, nu[0] - 1), 0)

    grid_spec = pltpu.PrefetchScalarGridSpec(
        num_scalar_prefetch=4,
        grid=(nblk,),
        in_specs=[
            pl.BlockSpec(row_block, row_map),
            pl.BlockSpec(memory_space=pl.ANY),
            pl.BlockSpec(memory_space=pl.ANY),
            pl.BlockSpec(memory_space=pl.ANY),
        ],
        out_specs=pl.BlockSpec(row_block, row_map),
        scratch_shapes=[
            pltpu.VMEM((2, d, D_EXPERT), F32),
            pltpu.VMEM((2, d, D_EXPERT), F32),
            pltpu.VMEM((2, D_EXPERT, d), F32),
            pltpu.VMEM((d, D_EXPERT), BF16),
            pltpu.VMEM((d, D_EXPERT), BF16),
            pltpu.VMEM((D_EXPERT, d), BF16),
            pltpu.SemaphoreType.DMA((2, 3)),
        ],
    )
    return pl.pallas_call(
        _experts_body,
        grid_spec=grid_spec,
        out_shape=jax.ShapeDtypeStruct(xs.shape, U32),
        compiler_params=_cparams(("arbitrary",)),
        name="experts",
    )(block_e, n_used, next_e, ord_e, xs, w_gate_e, w_up_e, w_down_e)


def _combine_body(dest_ref, ys_ref, wt_ref, h_ref, x1_ref, mod_ref, gpost, wg, wu, wd, o_ref, buf, sem):
    tm = h_ref.shape[0]

    def row_copy(t, k):
        return pltpu.make_async_copy(ys_ref.at[_tile_rows(dest_ref[k, t])], buf.at[k, _tile_rows(t)], sem)

    issue, drain = _row_dma_loops(row_copy)
    lax.fori_loop(0, tm // TOKENS_PER_ISSUE, issue, 0)

    h = h_ref[...]
    g = jnp.dot(h, wg[...], preferred_element_type=F32)
    u = jnp.dot(h, wu[...], preferred_element_type=F32)
    shared = jnp.dot((g * jax.nn.sigmoid(g) * u).astype(BF16), wd[...], preferred_element_type=F32)

    lax.fori_loop(0, tm // TOKENS_PER_ISSUE, drain, 0)

    half = SUBLANES * LANES
    wt = wt_ref[...]
    wk = [jnp.broadcast_to(wt[:, k:k + 1], (tm, LANES)) for k in range(TOP_K)]
    ya, yb = [], []
    for s in range(SUBLANES):
        sa = shared[:, s * LANES:(s + 1) * LANES]
        sb = shared[:, half + s * LANES:half + (s + 1) * LANES]
        for k in range(TOP_K):
            a, b = _unpack_halves(buf[k, pl.ds(s, tm, stride=SUBLANES), :])
            sa = sa + wk[k] * a
            sb = sb + wk[k] * b
        ya.append(sa)
        yb.append(sb)
    y = jnp.concatenate(ya + yb, axis=1)
    gate2 = mod_ref[0, 5:6, :]
    o_ref[...] = x1_ref[...] + gate2 * (_rms(y) * gpost[...])


def _combine(ys, dest, wt, h2, x1, mod3, g_post, wg_b, wu_b, wd_b, seq):
    n, d = h2.shape
    tm = 256
    per_b = seq // tm
    const = lambda shape: pl.BlockSpec(shape, lambda i, *_: (0,) * len(shape), pipeline_mode=pl.Buffered(1))
    grid_spec = pltpu.PrefetchScalarGridSpec(
        num_scalar_prefetch=0,
        grid=(n // tm,),
        in_specs=[
            pl.BlockSpec((TOP_K, tm), lambda i: (0, i), memory_space=pltpu.SMEM),
            pl.BlockSpec(memory_space=pl.ANY),
            pl.BlockSpec((tm, LANES), lambda i: (i, 0)),
            pl.BlockSpec((tm, d), lambda i: (i, 0)),
            pl.BlockSpec((tm, d), lambda i: (i, 0)),
            pl.BlockSpec((1, N_MOD, d), lambda i: (i // per_b, 0, 0)),
            const((1, d)), const(wg_b.shape), const(wu_b.shape), const(wd_b.shape),
        ],
        out_specs=pl.BlockSpec((tm, d), lambda i: (i, 0)),
        scratch_shapes=[pltpu.VMEM((TOP_K, tm * SUBLANES, LANES), U32), pltpu.SemaphoreType.DMA(())],
    )
    return pl.pallas_call(
        _combine_body,
        grid_spec=grid_spec,
        out_shape=jax.ShapeDtypeStruct((n, d), F32),
        compiler_params=_cparams(("arbitrary",), disable_bounds_checks=True),
        name="combine",
    )(dest, ys, wt, h2, x1, mod3, g_post.reshape(1, d), wg_b, wu_b, wd_b)


def _layer(x, mod, positions, g_pre_mix, g_post_mix, g_pre_ffn, g_post_ffn, w_in, w_dil_out, w_sb_out, w_mix_out,
           w_router, router_bias, w_gate_e, w_up_e, w_down_e, w_gate_s, w_up_s, w_down_s):
    bsz, seq, d = x.shape
    n = bsz * seq
    xf = x.reshape(n, d)
    mod3 = mod.reshape(bsz, N_MOD, d)

    nd, nq = 3 * WIDTH_DIL, 3 * (WIDTH_DIL + WIDTH_SB)
    w_plain = jnp.concatenate([w_in[:, nq:], w_in[:, nd:nq]], axis=1).astype(BF16)
    cols = []
    for g in range(len(DIL_PATTERNS)):
        for part in range(3):
            lo = part * WIDTH_DIL + g * D_DIL_OUT
            cols.append(w_in[:, lo:lo + D_DIL_OUT])
    w_dil = jnp.concatenate(cols, axis=1).astype(BF16)

    proj = _inproj(xf, g_pre_mix, mod3, w_plain, seq)
    tables = _rope_tables(positions)
    qkv_dil = _inproj_dil(xf, g_pre_mix, mod3, w_dil, tables, bsz, seq)
    o_dil, lse_dil = [], []
    for g, (window, dilation) in enumerate(DIL_PATTERNS):
        assert window // dilation == Q_BLOCK
        o, lse = _dilated_attention(qkv_dil[g], g, bsz, seq)
        o_dil.append(o)
        lse_dil.append(lse)
    o_sb = _stick_breaking(proj, bsz, seq)

    x1, h2 = _mixout(o_dil, lse_dil, o_sb, proj, xf, mod3, g_post_mix, g_pre_ffn,
                     w_dil_out.astype(BF16), w_sb_out.astype(BF16), w_mix_out.astype(BF16), seq)

    top_idx, top_wt = _router(h2, w_router, router_bias)
    counts = _plan_counts(top_idx)[:, 0].astype(I32)
    padded = (counts + ROW_BLOCK - 1) // ROW_BLOCK * ROW_BLOCK
    pends = jnp.cumsum(padded)
    pstart = pends - padded
    nblk = _capacity(n) // ROW_BLOCK
    n_used = (pends[-1] // ROW_BLOCK).astype(I32)
    blk = jnp.minimum(jnp.arange(nblk, dtype=I32), n_used - 1)
    block_e = jnp.minimum(jnp.sum(pends[None, :] <= (blk * ROW_BLOCK)[:, None], axis=1), N_EXPERTS - 1).astype(I32)
    eids = jnp.arange(N_EXPERTS, dtype=I32)
    has = counts > 0
    ord_e = (jnp.cumsum(has) - has).astype(I32)
    later = jnp.where((eids[None, :] > eids[:, None]) & has[None, :], eids[None, :], N_EXPERTS)
    nxt = jnp.min(later, axis=1)
    next_e = jnp.where(nxt < N_EXPERTS, nxt, eids).astype(I32)
    dest = _plan_dest(top_idx, pstart)

    xs = _dispatch(h2, dest, pstart.astype(I32), counts)
    ys = _experts(xs, block_e, n_used.reshape(1), next_e, ord_e, w_gate_e, w_up_e, w_down_e)
    out = _combine(ys, dest, top_wt, h2, x1, mod3, g_post_ffn,
                   w_gate_s.astype(BF16), w_up_s.astype(BF16), w_down_s.astype(BF16), seq)
    return out.reshape(bsz, seq, d)


def kernel(x, c, positions, w_ada, b_ada, g_pre_mix, g_post_mix, g_pre_ffn, g_post_ffn, w_in, w_dil_out,
           w_sb_out, w_mix_out, w_router, router_bias, w_gate_e, w_up_e, w_down_e, w_gate_s, w_up_s, w_down_s):
    for l in range(w_ada.shape[0]):
        mod = _adaln(c, w_ada[l], b_ada[l])
        x = _layer(x, mod, positions, g_pre_mix[l], g_post_mix[l], g_pre_ffn[l], g_post_ffn[l], w_in[l],
                   w_dil_out[l], w_sb_out[l], w_mix_out[l], w_router[l], router_bias[l],
                   w_gate_e[l], w_up_e[l], w_down_e[l], w_gate_s[l], w_up_s[l], w_down_s[l])
    return x
```

```python
import functools

import jax
import jax.numpy as jnp
from jax import lax
from jax.experimental import pallas as pl
from jax.experimental.pallas import tpu as pltpu

F32 = jnp.float32
BF16 = jnp.bfloat16
I32 = jnp.int32
U32 = jnp.uint32

D_MODEL = 2048
HEAD_DIM = 128
DIL_PATTERNS = ((128, 1), (512, 4), (2048, 16))
HEADS_PER_GROUP = 4
N_HEADS_DIL = 12
N_HEADS_SB = 8
WIDTH_DIL = N_HEADS_DIL * HEAD_DIM
WIDTH_SB = N_HEADS_SB * HEAD_DIM
D_DIL_OUT = HEADS_PER_GROUP * HEAD_DIM
Q_BLOCK = 128
ROPE_THETA = 500000.0
ROPE_DIM = HEAD_DIM // 4
N_GATE = 2 * D_MODEL
N_EXPERTS = 64
TOP_K = 8
N_GROUPS = 8
GROUP_SIZE = N_EXPERTS // N_GROUPS
TOPK_GROUPS = 4
D_EXPERT = 512
D_SHARED = 512
ROUTED_SCALE = 2.5
RMS_EPS = 1e-6
N_MOD = 6
ATTN_SCALE = HEAD_DIM ** -0.5

LANES = 128
SUBLANES = 8
VMEM_LIMIT = 56 * 1024 * 1024

ROW_BLOCK = 512
SB_DEAD = -110.0


def _cparams(sem, **kw):
    return pltpu.CompilerParams(dimension_semantics=sem, vmem_limit_bytes=VMEM_LIMIT, **kw)


def _adaln_body(ct_ref, w_ref, b_ref, o_ref, *, kc):
    nb = ct_ref.shape[1]
    nk = w_ref.shape[0] // kc

    def step(i, acc):
        k0 = pl.multiple_of(i * kc, kc)
        w = w_ref[pl.ds(k0, kc), :]
        c = ct_ref[pl.ds(k0, kc), :]
        s = c * jax.nn.sigmoid(c)
        parts = [jnp.sum(w * s[:, b:b + 1], axis=0, keepdims=True) for b in range(nb)]
        return acc + jnp.concatenate(parts, axis=0)

    acc = lax.fori_loop(0, nk, step, jnp.zeros(o_ref.shape, F32))
    o_ref[...] = acc + b_ref[...]


def _adaln(c, w_ada, b_ada):
    nb, d = c.shape
    n_out = w_ada.shape[1]
    tn = 1024
    return pl.pallas_call(
        functools.partial(_adaln_body, kc=256),
        grid=(n_out // tn,),
        in_specs=[
            pl.BlockSpec((d, nb), lambda j: (0, 0)),
            pl.BlockSpec((d, tn), lambda j: (0, j)),
            pl.BlockSpec((1, tn), lambda j: (0, j)),
        ],
        out_specs=pl.BlockSpec((nb, tn), lambda j: (0, j)),
        out_shape=jax.ShapeDtypeStruct((nb, n_out), F32),
        compiler_params=_cparams(("arbitrary",)),
        name="adaln",
    )(c.T, w_ada, b_ada.reshape(1, n_out))


def _rms(x):
    return x * lax.rsqrt(jnp.mean(x * x, axis=-1, keepdims=True) + RMS_EPS)


def _prenorm(x_ref, g_ref, mod_ref):
    y = _rms(x_ref[...]) * g_ref[...]
    shift = mod_ref[0, 0:1, :]
    scale = mod_ref[0, 1:2, :]
    return (y * (1.0 + scale) + shift).astype(BF16)


def _inproj_body(x_ref, g_ref, mod_ref, w_ref, o_ref, h_ref):
    @pl.when(pl.program_id(1) == 0)
    def _():
        h_ref[...] = _prenorm(x_ref, g_ref, mod_ref)

    o_ref[...] = jnp.dot(h_ref[...], w_ref[...], preferred_element_type=F32).astype(o_ref.dtype)


def _inproj(xf, g_pre, mod3, w_b, seq):
    n, d = xf.shape
    width = w_b.shape[1]
    tm, tn = 1024, width // 4
    per_b = seq // tm
    return pl.pallas_call(
        _inproj_body,
        grid=(n // tm, width // tn),
        in_specs=[
            pl.BlockSpec((tm, d), lambda i, j: (i, 0)),
            pl.BlockSpec((1, d), lambda i, j: (0, 0)),
            pl.BlockSpec((1, N_MOD, d), lambda i, j: (i // per_b, 0, 0)),
            pl.BlockSpec((d, tn), lambda i, j: (0, j)),
        ],
        out_specs=pl.BlockSpec((tm, tn), lambda i, j: (i, j)),
        out_shape=jax.ShapeDtypeStruct((n, width), BF16),
        scratch_shapes=[pltpu.VMEM((tm, d), BF16)],
        compiler_params=_cparams(("arbitrary", "arbitrary")),
        name="inproj",
    )(xf, g_pre.reshape(1, d), mod3, w_b)


def _inproj_dil_body(x_ref, g_ref, mod_ref, w_ref, t_ref, o0, o1, o2, res_ref):
    tm = x_ref.shape[0]
    h = _prenorm(x_ref, g_ref, mod_ref)
    t = t_ref[...]
    gw = 3 * D_DIL_OUT
    for gi, o_ref in enumerate((o0, o1, o2)):
        dil = DIL_PATTERNS[gi][1]
        res = jnp.dot(h, w_ref[:, gi * gw:(gi + 1) * gw], preferred_element_type=F32)
        for hs in range(3 * HEADS_PER_GROUP):
            sl = slice(hs * HEAD_DIM, (hs + 1) * HEAD_DIM)
            res_ref[hs] = _apply_rope(res[:, sl], t) if hs < 2 * HEADS_PER_GROUP else res[:, sl]
        for r in range(dil):
            for hs in range(3 * HEADS_PER_GROUP):
                rows = res_ref[hs] if dil == 1 else res_ref[hs, pl.ds(r, tm // dil, stride=dil), :]
                o_ref[0, r, :, hs * HEAD_DIM:(hs + 1) * HEAD_DIM] = rows.astype(o_ref.dtype)


def _inproj_dil(xf, g_pre, mod3, w_b, tables, bsz, seq):
    n, d = xf.shape
    gw = 3 * D_DIL_OUT
    tm = 512
    per_b = seq // tm
    dils = [p[1] for p in DIL_PATTERNS]
    return pl.pallas_call(
        _inproj_dil_body,
        grid=(n // tm,),
        in_specs=[
            pl.BlockSpec((tm, d), lambda i: (i, 0)),
            pl.BlockSpec((1, d), lambda i: (0, 0)),
            pl.BlockSpec((1, N_MOD, d), lambda i: (i // per_b, 0, 0)),
            pl.BlockSpec(w_b.shape, lambda i: (0, 0), pipeline_mode=pl.Buffered(1)),
            pl.BlockSpec((tm, 3 * LANES), lambda i: (i, 0)),
        ],
        out_specs=[pl.BlockSpec((1, dl, tm // dl, gw), lambda i: (i // per_b, 0, i % per_b, 0)) for dl in dils],
        out_shape=[jax.ShapeDtypeStruct((bsz, dl, seq // dl, gw), BF16) for dl in dils],
        scratch_shapes=[pltpu.VMEM((gw // HEAD_DIM, tm, HEAD_DIM), F32)],
        compiler_params=_cparams(("arbitrary",)),
        name="inproj_dil",
    )(xf, g_pre.reshape(1, d), mod3, w_b, tables)


def _rope_body(pos_ref, f_ref, o_ref):
    ang = pos_ref[...].astype(F32) * f_ref[...]
    c = jnp.cos(ang)
    s = jnp.sin(ang)
    lane = lax.broadcasted_iota(I32, ang.shape, 1)
    half = ROPE_DIM // 2
    o_ref[:, 0:LANES] = c
    o_ref[:, LANES:2 * LANES] = jnp.where(lane >= half, s, 0.0)
    o_ref[:, 2 * LANES:3 * LANES] = jnp.where(lane < half, -s, 0.0)


def _rope_tables(positions):
    n = positions.size
    half = ROPE_DIM // 2
    inv_freq = ROPE_THETA ** (-jnp.arange(0, ROPE_DIM, 2, dtype=F32) / ROPE_DIM)
    f = jnp.concatenate([inv_freq, inv_freq, jnp.zeros((LANES - 2 * half,), F32)]).reshape(1, LANES)
    tm = 2048
    return pl.pallas_call(
        _rope_body,
        grid=(n // tm,),
        in_specs=[pl.BlockSpec((tm, 1), lambda i: (i, 0)), pl.BlockSpec((1, LANES), lambda i: (0, 0))],
        out_specs=pl.BlockSpec((tm, 3 * LANES), lambda i: (i, 0)),
        out_shape=jax.ShapeDtypeStruct((n, 3 * LANES), F32),
        compiler_params=_cparams(("arbitrary",)),
        name="rope_tables",
    )(positions.reshape(n, 1), f)


def _apply_rope(x, t):
    half = ROPE_DIM // 2
    return (x * t[:, 0:LANES]
            + pltpu.roll(x, half, 1) * t[:, LANES:2 * LANES]
            + pltpu.roll(x, LANES - half, 1) * t[:, 2 * LANES:3 * LANES])


def _dil_body(cur_ref, kp_ref, vp_ref, o_ref, lse_ref, obuf, lbuf, *, dil, nsub):
    n = pl.program_id(1)
    tq = nsub * Q_BLOCK
    row = lax.broadcasted_iota(I32, (Q_BLOCK, 2 * Q_BLOCK), 0)
    col = lax.broadcasted_iota(I32, (Q_BLOCK, 2 * Q_BLOCK), 1)
    rel = row + Q_BLOCK - col
    band = jnp.where(rel >= 0, jnp.where(rel <= Q_BLOCK, 1.0, 0.0), 0.0)
    first = jnp.where(col >= Q_BLOCK, band, jnp.where(n > 0, band, 0.0))
    for r in range(dil):
        for h in range(HEADS_PER_GROUP):
            sl = slice(h * HEAD_DIM, (h + 1) * HEAD_DIM)
            ksl = slice(D_DIL_OUT + h * HEAD_DIM, D_DIL_OUT + (h + 1) * HEAD_DIM)
            vsl = slice(2 * D_DIL_OUT + h * HEAD_DIM, 2 * D_DIL_OUT + (h + 1) * HEAD_DIM)
            for j in range(nsub):
                rs = slice(j * Q_BLOCK, (j + 1) * Q_BLOCK)
                ps = slice((j - 1) * Q_BLOCK, j * Q_BLOCK)
                kprev = kp_ref[0, r, :, sl] if j == 0 else cur_ref[0, r, ps, ksl]
                vprev = vp_ref[0, r, :, sl] if j == 0 else cur_ref[0, r, ps, vsl]
                kcat = jnp.concatenate([kprev, cur_ref[0, r, rs, ksl]], axis=0)
                vcat = jnp.concatenate([vprev, cur_ref[0, r, rs, vsl]], axis=0)
                s = lax.dot_general(cur_ref[0, r, rs, sl], kcat, (((1,), (1,)), ((), ())),
                                    preferred_element_type=F32) * ATTN_SCALE
                s = jnp.where((first if j == 0 else band) > 0.0, s, -jnp.inf)
                m = jnp.max(s, axis=-1, keepdims=True)
                p = jnp.exp(s - m)
                l = jnp.sum(p, axis=-1, keepdims=True)
                o = jnp.dot((p / l).astype(BF16), vcat, preferred_element_type=F32)
                lse = jnp.broadcast_to(m + jnp.log(l), (Q_BLOCK, HEAD_DIM))
                if dil == 1:
                    o_ref[h, rs, :] = o
                    lse_ref[h, rs, :] = lse
                else:
                    obuf[h, rs, :] = o
                    lbuf[h, rs, :] = lse
            if dil > 1:
                o_ref[h, pl.ds(r, tq, stride=dil), :] = obuf[h]
                lse_ref[h, pl.ds(r, tq, stride=dil), :] = lbuf[h]


def _dilated_attention(qkv, g, bsz, seq):
    dil = DIL_PATTERNS[g][1]
    length = seq // dil
    tq = min(4 * Q_BLOCK, (16 * Q_BLOCK) // dil, length)
    nsub = tq // Q_BLOCK
    nq = length // tq
    gw = 3 * D_DIL_OUT
    n = bsz * seq

    def prev(colblk):
        return pl.BlockSpec((1, dil, Q_BLOCK, D_DIL_OUT),
                            lambda b, i: (b, 0, jnp.maximum(i * nsub - 1, 0), colblk))

    nh = HEADS_PER_GROUP
    out_spec = pl.BlockSpec((nh, tq * dil, HEAD_DIM), lambda b, i: (0, b * nq + i, 0))
    out_shape = jax.ShapeDtypeStruct((nh, n, HEAD_DIM), F32)
    return pl.pallas_call(
        functools.partial(_dil_body, dil=dil, nsub=nsub),
        grid=(bsz, nq),
        in_specs=[pl.BlockSpec((1, dil, tq, gw), lambda b, i: (b, 0, i, 0)), prev(1), prev(2)],
        out_specs=[out_spec, out_spec],
        out_shape=[out_shape, out_shape],
        scratch_shapes=[pltpu.VMEM((nh, tq, HEAD_DIM), F32), pltpu.VMEM((nh, tq, HEAD_DIM), F32)],
        compiler_params=_cparams(("arbitrary", "arbitrary")),
        name=f"dilated_d{dil}",
    )(qkv, qkv, qkv)


def _sb_body(q_ref, k_ref, v_ref, o_ref, acc_ref, car_ref):
    nblk = q_ref.shape[1] // Q_BLOCK
    r = lax.broadcasted_iota(I32, (Q_BLOCK, Q_BLOCK), 0)
    c = lax.broadcasted_iota(I32, (Q_BLOCK, Q_BLOCK), 1)
    causal = c < r
    rr = lax.broadcasted_iota(I32, (Q_BLOCK, 2 * Q_BLOCK), 0)
    cc = lax.broadcasted_iota(I32, (Q_BLOCK, 2 * Q_BLOCK), 1)
    uo = jnp.where(cc >= Q_BLOCK, 1.0, jnp.where(rr > cc, 1.0, 0.0)).astype(BF16)

    nh = acc_ref.shape[0]
    heads = range(nh)

    def tiles(qs, kb, carries, diag):
        k0 = pl.multiple_of(kb * Q_BLOCK, Q_BLOCK)
        hs = [slice(h * HEAD_DIM, (h + 1) * HEAD_DIM) for h in heads]
        zs = [lax.dot_general(qs[h], k_ref[0, pl.ds(k0, Q_BLOCK), hs[h]], (((1,), (1,)), ((), ())),
                              preferred_element_type=F32) * ATTN_SCALE for h in heads]
        stacked, log_s = [], []
        for z in zs:
            sp = jnp.log(1.0 + jnp.exp(-jnp.abs(z)))
            mx = jnp.maximum(z, 0.0)
            log_1m = -(mx + sp)
            if diag:
                log_1m = jnp.where(causal, log_1m, 0.0)
            hi = log_1m.astype(BF16)
            lo = (log_1m - hi.astype(F32)).astype(BF16)
            stacked.append(jnp.concatenate([hi, lo], axis=0))
            log_s.append((z - mx) - sp)
        r2s = [jnp.dot(s, uo, preferred_element_type=F32) for s in stacked]
        probs, new_carries = [], []
        for h in heads:
            sums = r2s[h][:Q_BLOCK] + r2s[h][Q_BLOCK:]
            a = jnp.exp(log_s[h] + carries[h] + sums[:, :Q_BLOCK])
            if diag:
                a = jnp.where(causal, a, 0.0)
            probs.append(a.astype(BF16))
            new_carries.append(carries[h] + sums[:, Q_BLOCK:])
        pvs = [jnp.dot(probs[h], v_ref[0, pl.ds(k0, Q_BLOCK), hs[h]], preferred_element_type=F32) for h in heads]
        return pvs, new_carries

    def all_max(xs):
        m = xs[0]
        for x in xs[1:]:
            m = jnp.maximum(m, x)
        return jnp.max(m)

    def qblock(qi, _):
        q0 = pl.multiple_of(qi * Q_BLOCK, Q_BLOCK)
        qs = [q_ref[0, pl.ds(q0, Q_BLOCK), h * HEAD_DIM:(h + 1) * HEAD_DIM] for h in heads]
        zero = jnp.zeros((Q_BLOCK, Q_BLOCK), F32)
        pvs, cars = tiles(qs, qi, [zero] * nh, True)
        for h in heads:
            acc_ref[h] = pvs[h]
            car_ref[h] = cars[h]

        def cond(st):
            return jnp.logical_and(st[0] >= 0, st[1] > SB_DEAD)

        def body(st):
            pvs, cars = tiles(qs, st[0], [car_ref[h] for h in heads], False)
            for h in heads:
                acc_ref[h] += pvs[h]
                car_ref[h] = cars[h]
            return st[0] - 1, all_max(cars)

        lax.while_loop(cond, body, (qi - 1, all_max(cars)))
        for h in range(nh):
            o_ref[0, pl.ds(q0, Q_BLOCK), h * HEAD_DIM:(h + 1) * HEAD_DIM] = acc_ref[h].astype(o_ref.dtype)
        return 0

    lax.fori_loop(0, nblk, qblock, 0)


SB_HEADS_PER_STEP = 8


def _stick_breaking(proj, bsz, seq):
    width = proj.shape[1]
    pv = proj.reshape(bsz, seq, width)
    nh = SB_HEADS_PER_STEP
    bw = nh * HEAD_DIM
    base = N_GATE // bw

    def spec(off):
        return pl.BlockSpec((1, seq, bw), lambda b, h: (b, 0, base + off + h), pipeline_mode=pl.Buffered(1))

    nstep = N_HEADS_SB // nh
    o = pl.pallas_call(
        _sb_body,
        grid=(bsz, nstep),
        in_specs=[spec(0), spec(nstep), spec(2 * nstep)],
        out_specs=pl.BlockSpec((1, seq, bw), lambda b, h: (b, 0, h)),
        out_shape=jax.ShapeDtypeStruct((bsz, seq, WIDTH_SB), BF16),
        scratch_shapes=[pltpu.VMEM((nh, Q_BLOCK, Q_BLOCK), F32), pltpu.VMEM((nh, Q_BLOCK, Q_BLOCK), F32)],
        compiler_params=_cparams(("arbitrary", "arbitrary")),
        name="stick_breaking",
    )(pv, pv, pv)
    return o.reshape(bsz * seq, WIDTH_SB)


def _mixout_body(o1, o2, o3, l1, l2, l3, osb, gd_ref, gs_ref, x_ref, mod_ref, gpost, gpre,
                 wd, ws, wm, x1_ref, h2_ref):
    heads = []
    for h in range(HEADS_PER_GROUP):
        la, lb, lc = l1[h], l2[h], l3[h]
        m = jnp.maximum(la, jnp.maximum(lb, lc))
        ea, eb, ec = jnp.exp(la - m), jnp.exp(lb - m), jnp.exp(lc - m)
        heads.append(((ea * o1[h] + eb * o2[h] + ec * o3[h]) / (ea + eb + ec)).astype(BF16))
    yd = jnp.dot(jnp.concatenate(heads, axis=1), wd[...], preferred_element_type=F32)
    ys = jnp.dot(osb[...], ws[...], preferred_element_type=F32)
    mix = jax.nn.sigmoid(gd_ref[...].astype(F32)) * yd + jax.nn.sigmoid(gs_ref[...].astype(F32)) * ys
    y = jnp.dot(mix.astype(BF16), wm[...], preferred_element_type=F32)
    gate1 = mod_ref[0, 2:3, :]
    shift2 = mod_ref[0, 3:4, :]
    scale2 = mod_ref[0, 4:5, :]
    x1 = x_ref[...] + gate1 * (_rms(y) * gpost[...])
    x1_ref[...] = x1
    h2_ref[...] = ((_rms(x1) * gpre[...]) * (1.0 + scale2) + shift2).astype(h2_ref.dtype)


def _const_spec(shape):
    return pl.BlockSpec(shape, lambda i: (0,) * len(shape), pipeline_mode=pl.Buffered(1))


def _mixout(o_dil, lse_dil, o_sb, proj, xf, mod3, g_post, g_pre, wd_b, ws_b, wm_b, seq):
    n, d = xf.shape
    tm = 256
    per_b = seq // tm
    row = lambda w: pl.BlockSpec((tm, w), lambda i: (i, 0))
    head_major = pl.BlockSpec((HEADS_PER_GROUP, tm, HEAD_DIM), lambda i: (0, i, 0))
    in_specs = (
        [head_major] * 6 + [row(WIDTH_SB)]
        + [pl.BlockSpec((tm, d), lambda i: (i, 0)), pl.BlockSpec((tm, d), lambda i: (i, 1))]
        + [row(d), pl.BlockSpec((1, N_MOD, d), lambda i: (i // per_b, 0, 0))]
        + [_const_spec((1, d)), _const_spec((1, d))]
        + [_const_spec(wd_b.shape), _const_spec(ws_b.shape), _const_spec(wm_b.shape)]
    )
    return pl.pallas_call(
        _mixout_body,
        grid=(n // tm,),
        in_specs=in_specs,
        out_specs=[row(d), row(d)],
        out_shape=[jax.ShapeDtypeStruct((n, d), F32), jax.ShapeDtypeStruct((n, d), BF16)],
        compiler_params=_cparams(("arbitrary",)),
        name="mixout",
    )(*o_dil, *lse_dil, o_sb, proj, proj, xf, mod3, g_post.reshape(1, d), g_pre.reshape(1, d), wd_b, ws_b, wm_b)


def _topk_rows(x, k, iota0):
    big = x.shape[0]
    out = []
    for _ in range(k):
        m = jnp.max(x, axis=0, keepdims=True)
        i = jnp.min(jnp.where(x == m, iota0, big), axis=0, keepdims=True)
        out.append((m, i))
        x = jnp.where(iota0 == i, -jnp.inf, x)
    return out


def _router_body(h_ref, wr_ref, bias_ref, idx_ref, wt_ref):
    tm = h_ref.shape[0]
    logits = lax.dot_general(wr_ref[...], h_ref[...], (((1,), (1,)), ((), ())), preferred_element_type=F32)
    scores = jax.nn.sigmoid(logits)
    sel = scores + bias_ref[...]
    sub = lax.broadcasted_iota(I32, (GROUP_SIZE, tm), 0)
    grp = []
    for g in range(N_GROUPS):
        (m1, _), (m2, _) = _topk_rows(sel[g * GROUP_SIZE:(g + 1) * GROUP_SIZE], 2, sub)
        grp.append(m1 + m2)
    gscore = jnp.concatenate(grp, axis=0)
    giota = lax.broadcasted_iota(I32, (N_GROUPS, tm), 0)
    gmask = jnp.zeros((N_GROUPS, tm), F32)
    for _, gi in _topk_rows(gscore, TOPK_GROUPS, giota):
        gmask = jnp.where(giota == gi, 1.0, gmask)
    masked = jnp.concatenate(
        [jnp.where(gmask[g:g + 1] > 0.0, sel[g * GROUP_SIZE:(g + 1) * GROUP_SIZE], -jnp.inf)
         for g in range(N_GROUPS)], axis=0)
    eiota = lax.broadcasted_iota(I32, (N_EXPERTS, tm), 0)
    picks = _topk_rows(masked, TOP_K, eiota)
    idx = jnp.concatenate([i for _, i in picks], axis=0)
    top_s = jnp.concatenate(
        [jnp.sum(jnp.where(eiota == i, scores, 0.0), axis=0, keepdims=True) for _, i in picks], axis=0)
    top_w = top_s / jnp.sum(top_s, axis=0, keepdims=True) * ROUTED_SCALE
    idx_ref[...] = idx
    wpad = jnp.concatenate([top_w, jnp.zeros((LANES - TOP_K, tm), F32)], axis=0)
    wt_ref[...] = wpad.T


def _router(h2, w_router, router_bias):
    n, d = h2.shape
    tm = 512
    return pl.pallas_call(
        _router_body,
        grid=(n // tm,),
        in_specs=[
            pl.BlockSpec((tm, d), lambda i: (i, 0)),
            pl.BlockSpec((N_EXPERTS, d), lambda i: (0, 0)),
            pl.BlockSpec((N_EXPERTS, 1), lambda i: (0, 0)),
        ],
        out_specs=[pl.BlockSpec((TOP_K, tm), lambda i: (0, i)), pl.BlockSpec((tm, LANES), lambda i: (i, 0))],
        out_shape=[jax.ShapeDtypeStruct((TOP_K, n), I32), jax.ShapeDtypeStruct((n, LANES), F32)],
        compiler_params=_cparams(("arbitrary",)),
        name="router",
    )(h2, w_router.T.astype(BF16), router_bias.reshape(N_EXPERTS, 1))


def _plan_tile(idx):
    tm = idx.shape[1]
    eiota = lax.broadcasted_iota(I32, (N_EXPERTS, tm), 0)
    hit = jnp.zeros((N_EXPERTS, tm), F32)
    for k in range(TOP_K):
        hit = jnp.where(eiota == idx[k:k + 1], 1.0, hit)
    r = lax.broadcasted_iota(I32, (tm, tm), 0)
    c = lax.broadcasted_iota(I32, (tm, tm), 1)
    before = jnp.where(r < c, 1.0, 0.0).astype(BF16)
    excl = jnp.dot(hit.astype(BF16), before, preferred_element_type=F32)
    tot = excl[:, tm - 1:tm] + hit[:, tm - 1:tm]
    return eiota, hit, excl, tot


def _plan_counts_body(idx_ref, cnt_ref):
    @pl.when(pl.program_id(0) == 0)
    def _():
        cnt_ref[...] = jnp.zeros(cnt_ref.shape, F32)

    _, _, _, tot = _plan_tile(idx_ref[...])
    cnt_ref[...] += tot


def _plan_counts(top_idx):
    n = top_idx.shape[1]
    tm = 512
    return pl.pallas_call(
        _plan_counts_body,
        grid=(n // tm,),
        in_specs=[pl.BlockSpec((TOP_K, tm), lambda i: (0, i))],
        out_specs=pl.BlockSpec((N_EXPERTS, LANES), lambda i: (0, 0)),
        out_shape=jax.ShapeDtypeStruct((N_EXPERTS, LANES), F32),
        compiler_params=_cparams(("arbitrary",)),
        name="plan_counts",
    )(top_idx)


def _plan_dest_body(idx_ref, base_ref, dest_ref, run_ref):
    @pl.when(pl.program_id(0) == 0)
    def _():
        run_ref[...] = base_ref[...]

    idx = idx_ref[...]
    eiota, _, excl, tot = _plan_tile(idx)
    pos = run_ref[:, 0:1] + excl
    rows = [jnp.sum(jnp.where(eiota == idx[k:k + 1], pos, 0.0), axis=0, keepdims=True) for k in range(TOP_K)]
    dest_ref[...] = jnp.concatenate(rows, axis=0).astype(I32)
    run_ref[...] += tot


def _plan_dest(top_idx, pstart):
    n = top_idx.shape[1]
    tm = 512
    base = jnp.broadcast_to(pstart.astype(F32).reshape(N_EXPERTS, 1), (N_EXPERTS, LANES))
    return pl.pallas_call(
        _plan_dest_body,
        grid=(n // tm,),
        in_specs=[pl.BlockSpec((TOP_K, tm), lambda i: (0, i)), pl.BlockSpec((N_EXPERTS, LANES), lambda i: (0, 0))],
        out_specs=pl.BlockSpec((TOP_K, tm), lambda i: (0, i)),
        out_shape=jax.ShapeDtypeStruct((TOP_K, n), I32),
        scratch_shapes=[pltpu.VMEM((N_EXPERTS, LANES), F32)],
        compiler_params=_cparams(("arbitrary",)),
        name="plan_dest",
    )(top_idx, base)


def _capacity(n):
    return n * TOP_K + N_EXPERTS * ROW_BLOCK


def _pack_halves(y):
    w = y.shape[1] // 2
    hi = pltpu.bitcast(y[:, :w].astype(BF16).astype(F32), U32)
    lo = pltpu.bitcast(y[:, w:].astype(BF16).astype(F32), U32)
    return hi | (lo >> 16)


def _unpack_halves(p):
    a = pltpu.bitcast(p & jnp.uint32(0xFFFF0000), F32)
    b = pltpu.bitcast(p << 16, F32)
    return a, b


TOKENS_PER_ISSUE = 2


def _row_dma_loops(row_copy):
    def issue(i, _):
        for u in range(TOKENS_PER_ISSUE):
            for k in range(TOP_K):
                row_copy(i * TOKENS_PER_ISSUE + u, k).start(priority=k % 2)
        return 0

    def drain(i, _):
        for u in range(TOKENS_PER_ISSUE):
            for k in range(TOP_K):
                row_copy(i * TOKENS_PER_ISSUE + u, k).wait()
        return 0

    return issue, drain


def _pad_chunks():
    sizes, s = [], ROW_BLOCK // 2
    while s >= 1:
        sizes.append(s)
        s //= 2
    return sizes


def _tile_rows(t):
    return pl.ds(pl.multiple_of(t * SUBLANES, SUBLANES), SUBLANES)


def _to_row_tiles(ref, packed, lead=()):
    rows = packed.shape[0]
    for s in range(SUBLANES):
        ref[(*lead, pl.ds(s, rows, stride=SUBLANES), slice(None))] = packed[:, s * LANES:(s + 1) * LANES]


def _from_row_tiles(ref, rows, lead=()):
    return [ref[(*lead, pl.ds(s, rows, stride=SUBLANES), slice(None))] for s in range(SUBLANES)]


def _dispatch_body(pstart_ref, cnt_ref, dest_ref, h_ref, xs_ref, xbuf, zbuf, sem, zsem):
    tm = h_ref.shape[0]
    step = pl.program_id(0)

    def pad_dmas(wait):
        def per_expert(e, _):
            cnt = cnt_ref[e]
            pad = (ROW_BLOCK - cnt % ROW_BLOCK) % ROW_BLOCK
            start = pstart_ref[e] + cnt
            for size in _pad_chunks():
                @pl.when((pad & size) != 0)
                def _():
                    off = pl.multiple_of((start + (pad & ~(2 * size - 1))) * SUBLANES, SUBLANES)
                    cp = pltpu.make_async_copy(zbuf.at[pl.ds(0, size * SUBLANES)],
                                               xs_ref.at[pl.ds(off, size * SUBLANES)], zsem)
                    cp.wait() if wait else cp.start()
            return 0
        lax.fori_loop(0, N_EXPERTS, per_expert, 0)

    @pl.when(step == 0)
    def _():
        zbuf[...] = jnp.zeros(zbuf.shape, zbuf.dtype)
        pad_dmas(False)
        pad_dmas(True)

    _to_row_tiles(xbuf, _pack_halves(h_ref[...].astype(F32)))

    def row_copy(t, k):
        return pltpu.make_async_copy(xbuf.at[_tile_rows(t)], xs_ref.at[_tile_rows(dest_ref[k, t])], sem)

    issue, drain = _row_dma_loops(row_copy)
    lax.fori_loop(0, tm // TOKENS_PER_ISSUE, issue, 0)
    lax.fori_loop(0, tm // TOKENS_PER_ISSUE, drain, 0)


def _dispatch(h2, dest, pstart, counts):
    n, d = h2.shape
    tm = 256
    assert d // 2 == SUBLANES * LANES
    grid_spec = pltpu.PrefetchScalarGridSpec(
        num_scalar_prefetch=2,
        grid=(n // tm,),
        in_specs=[
            pl.BlockSpec((TOP_K, tm), lambda i, *_: (0, i), memory_space=pltpu.SMEM),
            pl.BlockSpec((tm, d), lambda i, *_: (i, 0)),
        ],
        out_specs=pl.BlockSpec(memory_space=pl.ANY),
        scratch_shapes=[
            pltpu.VMEM((tm * SUBLANES, LANES), U32),
            pltpu.VMEM((ROW_BLOCK // 2 * SUBLANES, LANES), U32),
            pltpu.SemaphoreType.DMA(()),
            pltpu.SemaphoreType.DMA(()),
        ],
    )
    return pl.pallas_call(
        _dispatch_body,
        grid_spec=grid_spec,
        out_shape=jax.ShapeDtypeStruct((_capacity(n) * SUBLANES, LANES), U32),
        compiler_params=_cparams(("arbitrary",), has_side_effects=True, disable_bounds_checks=True),
        name="dispatch",
    )(pstart, counts, dest, h2)


def _experts_body(be_ref, nu_ref, nxt_ref, ord_ref, x_ref, wg_hbm, wu_hbm, wd_hbm, y_ref,
                  wg_s, wu_s, wd_s, wg_b, wu_b, wd_b, sems):
    i = pl.program_id(0)
    used = i < nu_ref[0]
    e = be_ref[i]
    fresh = jnp.logical_or(i == 0, e != be_ref[jnp.maximum(i - 1, 0)])
    slot = ord_ref[e] % 2

    def weight_copies(expert, s):
        return (pltpu.make_async_copy(wg_hbm.at[expert], wg_s.at[s], sems.at[s, 0]),
                pltpu.make_async_copy(wu_hbm.at[expert], wu_s.at[s], sems.at[s, 1]),
                pltpu.make_async_copy(wd_hbm.at[expert], wd_s.at[s], sems.at[s, 2]))

    @pl.when(i == 0)
    def _():
        for cp in weight_copies(e, slot):
            cp.start()

    @pl.when(jnp.logical_and(used, fresh))
    def _():
        copies = weight_copies(e, slot)
        for cp, dst, src in zip(copies, (wg_b, wu_b, wd_b), (wg_s, wu_s, wd_s)):
            cp.wait()
            dst[...] = src[slot].astype(BF16)

        @pl.when(nxt_ref[e] != e)
        def _():
            for cp in weight_copies(nxt_ref[e], 1 - slot):
                cp.start()

    @pl.when(used)
    def _():
        half = SUBLANES * LANES
        slabs = [_unpack_halves(p) for p in _from_row_tiles(x_ref, ROW_BLOCK)]
        xa = jnp.concatenate([a.astype(BF16) for a, _ in slabs], axis=1)
        xb = jnp.concatenate([b.astype(BF16) for _, b in slabs], axis=1)
        g = (jnp.dot(xa, wg_b[:half], preferred_element_type=F32)
             + jnp.dot(xb, wg_b[half:], preferred_element_type=F32))
        u = (jnp.dot(xa, wu_b[:half], preferred_element_type=F32)
             + jnp.dot(xb, wu_b[half:], preferred_element_type=F32))
        hmid = (g * jax.nn.sigmoid(g) * u).astype(BF16)
        _to_row_tiles(y_ref, _pack_halves(jnp.dot(hmid, wd_b[...], preferred_element_type=F32)))


def _experts(xs, block_e, n_used, next_e, ord_e, w_gate_e, w_up_e, w_down_e):
    cap = xs.shape[0] // SUBLANES
    d = 2 * SUBLANES * LANES
    nblk = cap // ROW_BLOCK
    row_block = (ROW_BLOCK * SUBLANES, LANES)

    def row_map(i, be, nu, nxt, od):
        return (jnp.minimum(i, nu[0] - 1), 0)

    grid_spec = pltpu.PrefetchScalarGridSpec(
        num_scalar_prefetch=4,
        grid=(nblk,),
        in_specs=[
            pl.BlockSpec(row_block, row_map),
            pl.BlockSpec(memory_space=pl.ANY),
            pl.BlockSpec(memory_space=pl.ANY),
            pl.BlockSpec(memory_space=pl.ANY),
        ],
        out_specs=pl.BlockSpec(row_block, row_map),
        scratch_shapes=[
            pltpu.VMEM((2, d, D_EXPERT), F32),
            pltpu.VMEM((2, d, D_EXPERT), F32),
            pltpu.VMEM((2, D_EXPERT, d), F32),
            pltpu.VMEM((d, D_EXPERT), BF16),
            pltpu.VMEM((d, D_EXPERT), BF16),
            pltpu.VMEM((D_EXPERT, d), BF16),
            pltpu.SemaphoreType.DMA((2, 3)),
        ],
    )
    return pl.pallas_call(
        _experts_body,
        grid_spec=grid_spec,
        out_shape=jax.ShapeDtypeStruct(xs.shape, U32),
        compiler_params=_cparams(("arbitrary",)),
        name="experts",
    )(block_e, n_used, next_e, ord_e, xs, w_gate_e, w_up_e, w_down_e)


def _combine_body(dest_ref, dest_next_ref, ys_ref, wt_ref, h_ref, x1_ref, mod_ref, gpost, wg, wu, wd, o_ref,
                  bufs, sems):
    tm = h_ref.shape[0]
    i = pl.program_id(0)
    slot = i % 2
    buf = bufs.at[slot]

    def loops(dref, s):
        def row_copy(t, k):
            return pltpu.make_async_copy(ys_ref.at[_tile_rows(dref[k, t])], bufs.at[s, k, _tile_rows(t)],
                                         sems.at[s])
        return _row_dma_loops(row_copy)

    @pl.when(i == 0)
    def _():
        lax.fori_loop(0, tm // TOKENS_PER_ISSUE, loops(dest_ref, slot)[0], 0)

    @pl.when(i + 1 < pl.num_programs(0))
    def _():
        lax.fori_loop(0, tm // TOKENS_PER_ISSUE, loops(dest_next_ref, 1 - slot)[0], 0)

    lax.fori_loop(0, tm // TOKENS_PER_ISSUE, loops(dest_ref, slot)[1], 0)

    h = h_ref[...]
    g = jnp.dot(h, wg[...], preferred_element_type=F32)
    u = jnp.dot(h, wu[...], preferred_element_type=F32)
    shared = jnp.dot((g * jax.nn.sigmoid(g) * u).astype(BF16), wd[...], preferred_element_type=F32)

    half = SUBLANES * LANES
    wt = wt_ref[...]
    wk = [jnp.broadcast_to(wt[:, k:k + 1], (tm, LANES)) for k in range(TOP_K)]
    ya, yb = [], []
    for s in range(SUBLANES):
        sa = shared[:, s * LANES:(s + 1) * LANES]
        sb = shared[:, half + s * LANES:half + (s + 1) * LANES]
        for k in range(TOP_K):
            a, b = _unpack_halves(buf[k, pl.ds(s, tm, stride=SUBLANES), :])
            sa = sa + wk[k] * a
            sb = sb + wk[k] * b
        ya.append(sa)
        yb.append(sb)
    y = jnp.concatenate(ya + yb, axis=1)
    gate2 = mod_ref[0, 5:6, :]
    o_ref[...] = x1_ref[...] + gate2 * (_rms(y) * gpost[...])


def _combine(ys, dest, wt, h2, x1, mod3, g_post, wg_b, wu_b, wd_b, seq):
    n, d = h2.shape
    tm = 256
    per_b = seq // tm
    const = lambda shape: pl.BlockSpec(shape, lambda i, *_: (0,) * len(shape), pipeline_mode=pl.Buffered(1))
    grid_spec = pltpu.PrefetchScalarGridSpec(
        num_scalar_prefetch=0,
        grid=(n // tm,),
        in_specs=[
            pl.BlockSpec((TOP_K, tm), lambda i: (0, i), memory_space=pltpu.SMEM),
            pl.BlockSpec((TOP_K, tm), lambda i: (0, jnp.minimum(i + 1, n // tm - 1)), memory_space=pltpu.SMEM),
            pl.BlockSpec(memory_space=pl.ANY),
            pl.BlockSpec((tm, LANES), lambda i: (i, 0)),
            pl.BlockSpec((tm, d), lambda i: (i, 0)),
            pl.BlockSpec((tm, d), lambda i: (i, 0)),
            pl.BlockSpec((1, N_MOD, d), lambda i: (i // per_b, 0, 0)),
            const((1, d)), const(wg_b.shape), const(wu_b.shape), const(wd_b.shape),
        ],
        out_specs=pl.BlockSpec((tm, d), lambda i: (i, 0)),
        scratch_shapes=[pltpu.VMEM((2, TOP_K, tm * SUBLANES, LANES), U32), pltpu.SemaphoreType.DMA((2,))],
    )
    return pl.pallas_call(
        _combine_body,
        grid_spec=grid_spec,
        out_shape=jax.ShapeDtypeStruct((n, d), F32),
        compiler_params=_cparams(("arbitrary",), disable_bounds_checks=True),
        name="combine",
    )(dest, dest, ys, wt, h2, x1, mod3, g_post.reshape(1, d), wg_b, wu_b, wd_b)


def _layer(x, mod, positions, g_pre_mix, g_post_mix, g_pre_ffn, g_post_ffn, w_in, w_dil_out, w_sb_out, w_mix_out,
           w_router, router_bias, w_gate_e, w_up_e, w_down_e, w_gate_s, w_up_s, w_down_s):
    bsz, seq, d = x.shape
    n = bsz * seq
    xf = x.reshape(n, d)
    mod3 = mod.reshape(bsz, N_MOD, d)

    nd, nq = 3 * WIDTH_DIL, 3 * (WIDTH_DIL + WIDTH_SB)
    w_plain = jnp.concatenate([w_in[:, nq:], w_in[:, nd:nq]], axis=1).astype(BF16)
    cols = []
    for g in range(len(DIL_PATTERNS)):
        for part in range(3):
            lo = part * WIDTH_DIL + g * D_DIL_OUT
            cols.append(w_in[:, lo:lo + D_DIL_OUT])
    w_dil = jnp.concatenate(cols, axis=1).astype(BF16)

    proj = _inproj(xf, g_pre_mix, mod3, w_plain, seq)
    tables = _rope_tables(positions)
    qkv_dil = _inproj_dil(xf, g_pre_mix, mod3, w_dil, tables, bsz, seq)
    o_dil, lse_dil = [], []
    for g, (window, dilation) in enumerate(DIL_PATTERNS):
        assert window // dilation == Q_BLOCK
        o, lse = _dilated_attention(qkv_dil[g], g, bsz, seq)
        o_dil.append(o)
        lse_dil.append(lse)
    o_sb = _stick_breaking(proj, bsz, seq)

    x1, h2 = _mixout(o_dil, lse_dil, o_sb, proj, xf, mod3, g_post_mix, g_pre_ffn,
                     w_dil_out.astype(BF16), w_sb_out.astype(BF16), w_mix_out.astype(BF16), seq)

    top_idx, top_wt = _router(h2, w_router, router_bias)
    counts = _plan_counts(top_idx)[:, 0].astype(I32)
    padded = (counts + ROW_BLOCK - 1) // ROW_BLOCK * ROW_BLOCK
    pends = jnp.cumsum(padded)
    pstart = pends - padded
    nblk = _capacity(n) // ROW_BLOCK
    n_used = (pends[-1] // ROW_BLOCK).astype(I32)
    blk = jnp.minimum(jnp.arange(nblk, dtype=I32), n_used - 1)
    block_e = jnp.minimum(jnp.sum(pends[None, :] <= (blk * ROW_BLOCK)[:, None], axis=1), N_EXPERTS - 1).astype(I32)
    eids = jnp.arange(N_EXPERTS, dtype=I32)
    has = counts > 0
    ord_e = (jnp.cumsum(has) - has).astype(I32)
    later = jnp.where((eids[None, :] > eids[:, None]) & has[None, :], eids[None, :], N_EXPERTS)
    nxt = jnp.min(later, axis=1)
    next_e = jnp.where(nxt < N_EXPERTS, nxt, eids).astype(I32)
    dest = _plan_dest(top_idx, pstart)

    xs = _dispatch(h2, dest, pstart.astype(I32), counts)
    ys = _experts(xs, block_e, n_used.reshape(1), next_e, ord_e, w_gate_e, w_up_e, w_down_e)
    out = _combine(ys, dest, top_wt, h2, x1, mod3, g_post_ffn,
                   w_gate_s.astype(BF16), w_up_s.astype(BF16), w_down_s.astype(BF16), seq)
    return out.reshape(bsz, seq, d)


def kernel(x, c, positions, w_ada, b_ada, g_pre_mix, g_post_mix, g_pre_ffn, g_post_ffn, w_in, w_dil_out,
           w_sb_out, w_mix_out, w_router, router_bias, w_gate_e, w_up_e, w_down_e, w_gate_s, w_up_s, w_down_s):
    for l in range(w_ada.shape[0]):
        mod = _adaln(c, w_ada[l], b_ada[l])
        x = _layer(x, mod, positions, g_pre_mix[l], g_post_mix[l], g_pre_ffn[l], g_post_ffn[l], w_in[l],
                   w_dil_out[l], w_sb_out[l], w_mix_out[l], w_router[l], router_bias[l],
                   w_gate_e[l], w_up_e[l], w_down_e[l], w_gate_s[l], w_up_s[l], w_down_s[l])
    return x
```

```python
import functools

import jax
import jax.numpy as jnp
from jax import lax
from jax.experimental import pallas as pl
from jax.experimental.pallas import tpu as pltpu

F32 = jnp.float32
BF16 = jnp.bfloat16
I32 = jnp.int32
U32 = jnp.uint32

D_MODEL = 2048
HEAD_DIM = 128
DIL_PATTERNS = ((128, 1), (512, 4), (2048, 16))
HEADS_PER_GROUP = 4
N_HEADS_DIL = 12
N_HEADS_SB = 8
WIDTH_DIL = N_HEADS_DIL * HEAD_DIM
WIDTH_SB = N_HEADS_SB * HEAD_DIM
D_DIL_OUT = HEADS_PER_GROUP * HEAD_DIM
Q_BLOCK = 128
ROPE_THETA = 500000.0
ROPE_DIM = HEAD_DIM // 4
N_GATE = 2 * D_MODEL
N_EXPERTS = 64
TOP_K = 8
N_GROUPS = 8
GROUP_SIZE = N_EXPERTS // N_GROUPS
TOPK_GROUPS = 4
D_EXPERT = 512
D_SHARED = 512
ROUTED_SCALE = 2.5
RMS_EPS = 1e-6
N_MOD = 6
ATTN_SCALE = HEAD_DIM ** -0.5

LANES = 128
SUBLANES = 8
VMEM_LIMIT = 56 * 1024 * 1024

ROW_BLOCK = 512
SB_DEAD = -110.0


def _cparams(sem, **kw):
    return pltpu.CompilerParams(dimension_semantics=sem, vmem_limit_bytes=VMEM_LIMIT, **kw)


def _adaln_body(ct_ref, w_ref, b_ref, o_ref, *, kc):
    nb = ct_ref.shape[1]
    nk = w_ref.shape[0] // kc

    def step(i, acc):
        k0 = pl.multiple_of(i * kc, kc)
        w = w_ref[pl.ds(k0, kc), :]
        c = ct_ref[pl.ds(k0, kc), :]
        s = c * jax.nn.sigmoid(c)
        parts = [jnp.sum(w * s[:, b:b + 1], axis=0, keepdims=True) for b in range(nb)]
        return acc + jnp.concatenate(parts, axis=0)

    acc = lax.fori_loop(0, nk, step, jnp.zeros(o_ref.shape, F32))
    o_ref[...] = acc + b_ref[...]


def _adaln(c, w_ada, b_ada):
    nb, d = c.shape
    n_out = w_ada.shape[1]
    tn = 1024
    return pl.pallas_call(
        functools.partial(_adaln_body, kc=256),
        grid=(n_out // tn,),
        in_specs=[
            pl.BlockSpec((d, nb), lambda j: (0, 0)),
            pl.BlockSpec((d, tn), lambda j: (0, j)),
            pl.BlockSpec((1, tn), lambda j: (0, j)),
        ],
        out_specs=pl.BlockSpec((nb, tn), lambda j: (0, j)),
        out_shape=jax.ShapeDtypeStruct((nb, n_out), F32),
        compiler_params=_cparams(("arbitrary",)),
        name="adaln",
    )(c.T, w_ada, b_ada.reshape(1, n_out))


def _rms(x):
    return x * lax.rsqrt(jnp.mean(x * x, axis=-1, keepdims=True) + RMS_EPS)


def _prenorm(x_ref, g_ref, mod_ref):
    y = _rms(x_ref[...]) * g_ref[...]
    shift = mod_ref[0, 0:1, :]
    scale = mod_ref[0, 1:2, :]
    return (y * (1.0 + scale) + shift).astype(BF16)


def _inproj_body(x_ref, g_ref, mod_ref, w_ref, o_ref, h_ref):
    @pl.when(pl.program_id(1) == 0)
    def _():
        h_ref[...] = _prenorm(x_ref, g_ref, mod_ref)

    o_ref[...] = jnp.dot(h_ref[...], w_ref[...], preferred_element_type=F32).astype(o_ref.dtype)


def _inproj(xf, g_pre, mod3, w_b, seq):
    n, d = xf.shape
    width = w_b.shape[1]
    tm, tn = 1024, width // 4
    per_b = seq // tm
    return pl.pallas_call(
        _inproj_body,
        grid=(n // tm, width // tn),
        in_specs=[
            pl.BlockSpec((tm, d), lambda i, j: (i, 0)),
            pl.BlockSpec((1, d), lambda i, j: (0, 0)),
            pl.BlockSpec((1, N_MOD, d), lambda i, j: (i // per_b, 0, 0)),
            pl.BlockSpec((d, tn), lambda i, j: (0, j)),
        ],
        out_specs=pl.BlockSpec((tm, tn), lambda i, j: (i, j)),
        out_shape=jax.ShapeDtypeStruct((n, width), BF16),
        scratch_shapes=[pltpu.VMEM((tm, d), BF16)],
        compiler_params=_cparams(("arbitrary", "arbitrary")),
        name="inproj",
    )(xf, g_pre.reshape(1, d), mod3, w_b)


def _inproj_dil_body(x_ref, g_ref, mod_ref, w_ref, t_ref, o0, o1, o2, res_ref):
    tm = x_ref.shape[0]
    h = _prenorm(x_ref, g_ref, mod_ref)
    t = t_ref[...]
    gw = 3 * D_DIL_OUT
    for gi, o_ref in enumerate((o0, o1, o2)):
        dil = DIL_PATTERNS[gi][1]
        res = jnp.dot(h, w_ref[:, gi * gw:(gi + 1) * gw], preferred_element_type=F32)
        for hs in range(3 * HEADS_PER_GROUP):
            sl = slice(hs * HEAD_DIM, (hs + 1) * HEAD_DIM)
            res_ref[hs] = _apply_rope(res[:, sl], t) if hs < 2 * HEADS_PER_GROUP else res[:, sl]
        for r in range(dil):
            for hs in range(3 * HEADS_PER_GROUP):
                rows = res_ref[hs] if dil == 1 else res_ref[hs, pl.ds(r, tm // dil, stride=dil), :]
                o_ref[0, r, :, hs * HEAD_DIM:(hs + 1) * HEAD_DIM] = rows.astype(o_ref.dtype)


def _inproj_dil(xf, g_pre, mod3, w_b, tables, bsz, seq):
    n, d = xf.shape
    gw = 3 * D_DIL_OUT
    tm = 512
    per_b = seq // tm
    dils = [p[1] for p in DIL_PATTERNS]
    return pl.pallas_call(
        _inproj_dil_body,
        grid=(n // tm,),
        in_specs=[
            pl.BlockSpec((tm, d), lambda i: (i, 0)),
            pl.BlockSpec((1, d), lambda i: (0, 0)),
            pl.BlockSpec((1, N_MOD, d), lambda i: (i // per_b, 0, 0)),
            pl.BlockSpec(w_b.shape, lambda i: (0, 0), pipeline_mode=pl.Buffered(1)),
            pl.BlockSpec((tm, 3 * LANES), lambda i: (i, 0)),
        ],
        out_specs=[pl.BlockSpec((1, dl, tm // dl, gw), lambda i: (i // per_b, 0, i % per_b, 0)) for dl in dils],
        out_shape=[jax.ShapeDtypeStruct((bsz, dl, seq // dl, gw), BF16) for dl in dils],
        scratch_shapes=[pltpu.VMEM((gw // HEAD_DIM, tm, HEAD_DIM), F32)],
        compiler_params=_cparams(("arbitrary",)),
        name="inproj_dil",
    )(xf, g_pre.reshape(1, d), mod3, w_b, tables)


def _rope_body(pos_ref, f_ref, o_ref):
    ang = pos_ref[...].astype(F32) * f_ref[...]
    c = jnp.cos(ang)
    s = jnp.sin(ang)
    lane = lax.broadcasted_iota(I32, ang.shape, 1)
    half = ROPE_DIM // 2
    o_ref[:, 0:LANES] = c
    o_ref[:, LANES:2 * LANES] = jnp.where(lane >= half, s, 0.0)
    o_ref[:, 2 * LANES:3 * LANES] = jnp.where(lane < half, -s, 0.0)


def _rope_tables(positions):
    n = positions.size
    half = ROPE_DIM // 2
    inv_freq = ROPE_THETA ** (-jnp.arange(0, ROPE_DIM, 2, dtype=F32) / ROPE_DIM)
    f = jnp.concatenate([inv_freq, inv_freq, jnp.zeros((LANES - 2 * half,), F32)]).reshape(1, LANES)
    tm = 2048
    return pl.pallas_call(
        _rope_body,
        grid=(n // tm,),
        in_specs=[pl.BlockSpec((tm, 1), lambda i: (i, 0)), pl.BlockSpec((1, LANES), lambda i: (0, 0))],
        out_specs=pl.BlockSpec((tm, 3 * LANES), lambda i: (i, 0)),
        out_shape=jax.ShapeDtypeStruct((n, 3 * LANES), F32),
        compiler_params=_cparams(("arbitrary",)),
        name="rope_tables",
    )(positions.reshape(n, 1), f)


def _apply_rope(x, t):
    half = ROPE_DIM // 2
    return (x * t[:, 0:LANES]
            + pltpu.roll(x, half, 1) * t[:, LANES:2 * LANES]
            + pltpu.roll(x, LANES - half, 1) * t[:, 2 * LANES:3 * LANES])


def _dil_body(cur_ref, kp_ref, vp_ref, o_ref, lse_ref, obuf, lbuf, *, dil, nsub):
    n = pl.program_id(1)
    tq = nsub * Q_BLOCK
    row = lax.broadcasted_iota(I32, (Q_BLOCK, 2 * Q_BLOCK), 0)
    col = lax.broadcasted_iota(I32, (Q_BLOCK, 2 * Q_BLOCK), 1)
    rel = row + Q_BLOCK - col
    band = jnp.where(rel >= 0, jnp.where(rel <= Q_BLOCK, 1.0, 0.0), 0.0)
    first = jnp.where(col >= Q_BLOCK, band, jnp.where(n > 0, band, 0.0))
    for r in range(dil):
        for h in range(HEADS_PER_GROUP):
            sl = slice(h * HEAD_DIM, (h + 1) * HEAD_DIM)
            ksl = slice(D_DIL_OUT + h * HEAD_DIM, D_DIL_OUT + (h + 1) * HEAD_DIM)
            vsl = slice(2 * D_DIL_OUT + h * HEAD_DIM, 2 * D_DIL_OUT + (h + 1) * HEAD_DIM)
            for j in range(nsub):
                rs = slice(j * Q_BLOCK, (j + 1) * Q_BLOCK)
                ps = slice((j - 1) * Q_BLOCK, j * Q_BLOCK)
                kprev = kp_ref[0, r, :, sl] if j == 0 else cur_ref[0, r, ps, ksl]
                vprev = vp_ref[0, r, :, sl] if j == 0 else cur_ref[0, r, ps, vsl]
                kcat = jnp.concatenate([kprev, cur_ref[0, r, rs, ksl]], axis=0)
                vcat = jnp.concatenate([vprev, cur_ref[0, r, rs, vsl]], axis=0)
                s = lax.dot_general(cur_ref[0, r, rs, sl], kcat, (((1,), (1,)), ((), ())),
                                    preferred_element_type=F32) * ATTN_SCALE
                s = jnp.where((first if j == 0 else band) > 0.0, s, -jnp.inf)
                m = jnp.max(s, axis=-1, keepdims=True)
                p = jnp.exp(s - m)
                l = jnp.sum(p, axis=-1, keepdims=True)
                o = jnp.dot((p / l).astype(BF16), vcat, preferred_element_type=F32)
                lse = jnp.broadcast_to(m + jnp.log(l), (Q_BLOCK, HEAD_DIM))
                if dil == 1:
                    o_ref[h, rs, :] = o
                    lse_ref[h, rs, :] = lse
                else:
                    obuf[h, rs, :] = o
                    lbuf[h, rs, :] = lse
            if dil > 1:
                o_ref[h, pl.ds(r, tq, stride=dil), :] = obuf[h]
                lse_ref[h, pl.ds(r, tq, stride=dil), :] = lbuf[h]


def _dilated_attention(qkv, g, bsz, seq):
    dil = DIL_PATTERNS[g][1]
    length = seq // dil
    tq = min(4 * Q_BLOCK, (16 * Q_BLOCK) // dil, length)
    nsub = tq // Q_BLOCK
    nq = length // tq
    gw = 3 * D_DIL_OUT
    n = bsz * seq

    def prev(colblk):
        return pl.BlockSpec((1, dil, Q_BLOCK, D_DIL_OUT),
                            lambda b, i: (b, 0, jnp.maximum(i * nsub - 1, 0), colblk))

    nh = HEADS_PER_GROUP
    out_spec = pl.BlockSpec((nh, tq * dil, HEAD_DIM), lambda b, i: (0, b * nq + i, 0))
    out_shape = jax.ShapeDtypeStruct((nh, n, HEAD_DIM), F32)
    return pl.pallas_call(
        functools.partial(_dil_body, dil=dil, nsub=nsub),
        grid=(bsz, nq),
        in_specs=[pl.BlockSpec((1, dil, tq, gw), lambda b, i: (b, 0, i, 0)), prev(1), prev(2)],
        out_specs=[out_spec, out_spec],
        out_shape=[out_shape, out_shape],
        scratch_shapes=[pltpu.VMEM((nh, tq, HEAD_DIM), F32), pltpu.VMEM((nh, tq, HEAD_DIM), F32)],
        compiler_params=_cparams(("arbitrary", "arbitrary")),
        name=f"dilated_d{dil}",
    )(qkv, qkv, qkv)


def _sb_body(q_ref, k_ref, v_ref, o_ref, acc_ref, car_ref):
    nblk = q_ref.shape[1] // Q_BLOCK
    r = lax.broadcasted_iota(I32, (Q_BLOCK, Q_BLOCK), 0)
    c = lax.broadcasted_iota(I32, (Q_BLOCK, Q_BLOCK), 1)
    causal = c < r
    rr = lax.broadcasted_iota(I32, (Q_BLOCK, 2 * Q_BLOCK), 0)
    cc = lax.broadcasted_iota(I32, (Q_BLOCK, 2 * Q_BLOCK), 1)
    uo = jnp.where(cc >= Q_BLOCK, 1.0, jnp.where(rr > cc, 1.0, 0.0)).astype(BF16)

    nh = acc_ref.shape[0]
    heads = range(nh)

    def tiles(qs, kb, carries, diag):
        k0 = pl.multiple_of(kb * Q_BLOCK, Q_BLOCK)
        hs = [slice(h * HEAD_DIM, (h + 1) * HEAD_DIM) for h in heads]
        zs = [lax.dot_general(qs[h], k_ref[0, pl.ds(k0, Q_BLOCK), hs[h]], (((1,), (1,)), ((), ())),
                              preferred_element_type=F32) * ATTN_SCALE for h in heads]
        stacked, log_s = [], []
        for z in zs:
            sp = jnp.log(1.0 + jnp.exp(-jnp.abs(z)))
            mx = jnp.maximum(z, 0.0)
            log_1m = -(mx + sp)
            if diag:
                log_1m = jnp.where(causal, log_1m, 0.0)
            hi = log_1m.astype(BF16)
            lo = (log_1m - hi.astype(F32)).astype(BF16)
            stacked.append(jnp.concatenate([hi, lo], axis=0))
            log_s.append((z - mx) - sp)
        r2s = [jnp.dot(s, uo, preferred_element_type=F32) for s in stacked]
        probs, new_carries = [], []
        for h in heads:
            sums = r2s[h][:Q_BLOCK] + r2s[h][Q_BLOCK:]
            a = jnp.exp(log_s[h] + carries[h] + sums[:, :Q_BLOCK])
            if diag:
                a = jnp.where(causal, a, 0.0)
            probs.append(a.astype(BF16))
            new_carries.append(carries[h] + sums[:, Q_BLOCK:])
        pvs = [jnp.dot(probs[h], v_ref[0, pl.ds(k0, Q_BLOCK), hs[h]], preferred_element_type=F32) for h in heads]
        return pvs, new_carries

    def all_max(xs):
        m = xs[0]
        for x in xs[1:]:
            m = jnp.maximum(m, x)
        return jnp.max(m)

    def qblock(qi, _):
        q0 = pl.multiple_of(qi * Q_BLOCK, Q_BLOCK)
        qs = [q_ref[0, pl.ds(q0, Q_BLOCK), h * HEAD_DIM:(h + 1) * HEAD_DIM] for h in heads]
        zero = jnp.zeros((Q_BLOCK, Q_BLOCK), F32)
        pvs, cars = tiles(qs, qi, [zero] * nh, True)
        for h in heads:
            acc_ref[h] = pvs[h]
            car_ref[h] = cars[h]

        def cond(st):
            return jnp.logical_and(st[0] >= 0, st[1] > SB_DEAD)

        def body(st):
            pvs, cars = tiles(qs, st[0], [car_ref[h] for h in heads], False)
            for h in heads:
                acc_ref[h] += pvs[h]
                car_ref[h] = cars[h]
            return st[0] - 1, all_max(cars)

        lax.while_loop(cond, body, (qi - 1, all_max(cars)))
        for h in range(nh):
            o_ref[0, pl.ds(q0, Q_BLOCK), h * HEAD_DIM:(h + 1) * HEAD_DIM] = acc_ref[h].astype(o_ref.dtype)
        return 0

    lax.fori_loop(0, nblk, qblock, 0)


SB_HEADS_PER_STEP = 8


def _stick_breaking(proj, bsz, seq):
    width = proj.shape[1]
    pv = proj.reshape(bsz, seq, width)
    nh = SB_HEADS_PER_STEP
    bw = nh * HEAD_DIM
    base = N_GATE // bw

    def spec(off):
        return pl.BlockSpec((1, seq, bw), lambda b, h: (b, 0, base + off + h), pipeline_mode=pl.Buffered(1))

    nstep = N_HEADS_SB // nh
    o = pl.pallas_call(
        _sb_body,
        grid=(bsz, nstep),
        in_specs=[spec(0), spec(nstep), spec(2 * nstep)],
        out_specs=pl.BlockSpec((1, seq, bw), lambda b, h: (b, 0, h)),
        out_shape=jax.ShapeDtypeStruct((bsz, seq, WIDTH_SB), BF16),
        scratch_shapes=[pltpu.VMEM((nh, Q_BLOCK, Q_BLOCK), F32), pltpu.VMEM((nh, Q_BLOCK, Q_BLOCK), F32)],
        compiler_params=_cparams(("arbitrary", "arbitrary")),
        name="stick_breaking",
    )(pv, pv, pv)
    return o.reshape(bsz * seq, WIDTH_SB)


def _mixout_body(o1, o2, o3, l1, l2, l3, osb, gd_ref, gs_ref, x_ref, mod_ref, gpost, gpre,
                 wd, ws, wm, x1_ref, h2_ref):
    heads = []
    for h in range(HEADS_PER_GROUP):
        la, lb, lc = l1[h], l2[h], l3[h]
        m = jnp.maximum(la, jnp.maximum(lb, lc))
        ea, eb, ec = jnp.exp(la - m), jnp.exp(lb - m), jnp.exp(lc - m)
        heads.append(((ea * o1[h] + eb * o2[h] + ec * o3[h]) / (ea + eb + ec)).astype(BF16))
    yd = jnp.dot(jnp.concatenate(heads, axis=1), wd[...], preferred_element_type=F32)
    ys = jnp.dot(osb[...], ws[...], preferred_element_type=F32)
    mix = jax.nn.sigmoid(gd_ref[...].astype(F32)) * yd + jax.nn.sigmoid(gs_ref[...].astype(F32)) * ys
    y = jnp.dot(mix.astype(BF16), wm[...], preferred_element_type=F32)
    gate1 = mod_ref[0, 2:3, :]
    shift2 = mod_ref[0, 3:4, :]
    scale2 = mod_ref[0, 4:5, :]
    x1 = x_ref[...] + gate1 * (_rms(y) * gpost[...])
    x1_ref[...] = x1
    h2_ref[...] = ((_rms(x1) * gpre[...]) * (1.0 + scale2) + shift2).astype(h2_ref.dtype)


def _const_spec(shape):
    return pl.BlockSpec(shape, lambda i: (0,) * len(shape), pipeline_mode=pl.Buffered(1))


def _mixout(o_dil, lse_dil, o_sb, proj, xf, mod3, g_post, g_pre, wd_b, ws_b, wm_b, seq):
    n, d = xf.shape
    tm = 256
    per_b = seq // tm
    row = lambda w: pl.BlockSpec((tm, w), lambda i: (i, 0))
    head_major = pl.BlockSpec((HEADS_PER_GROUP, tm, HEAD_DIM), lambda i: (0, i, 0))
    in_specs = (
        [head_major] * 6 + [row(WIDTH_SB)]
        + [pl.BlockSpec((tm, d), lambda i: (i, 0)), pl.BlockSpec((tm, d), lambda i: (i, 1))]
        + [row(d), pl.BlockSpec((1, N_MOD, d), lambda i: (i // per_b, 0, 0))]
        + [_const_spec((1, d)), _const_spec((1, d))]
        + [_const_spec(wd_b.shape), _const_spec(ws_b.shape), _const_spec(wm_b.shape)]
    )
    return pl.pallas_call(
        _mixout_body,
        grid=(n // tm,),
        in_specs=in_specs,
        out_specs=[row(d), row(d)],
        out_shape=[jax.ShapeDtypeStruct((n, d), F32), jax.ShapeDtypeStruct((n, d), BF16)],
        compiler_params=_cparams(("arbitrary",)),
        name="mixout",
    )(*o_dil, *lse_dil, o_sb, proj, proj, xf, mod3, g_post.reshape(1, d), g_pre.reshape(1, d), wd_b, ws_b, wm_b)


def _topk_rows(x, k, iota0):
    big = x.shape[0]
    out = []
    for _ in range(k):
        m = jnp.max(x, axis=0, keepdims=True)
        i = jnp.min(jnp.where(x == m, iota0, big), axis=0, keepdims=True)
        out.append((m, i))
        x = jnp.where(iota0 == i, -jnp.inf, x)
    return out


def _router_body(h_ref, wr_ref, bias_ref, idx_ref, wt_ref):
    tm = h_ref.shape[0]
    logits = lax.dot_general(wr_ref[...], h_ref[...], (((1,), (1,)), ((), ())), preferred_element_type=F32)
    scores = jax.nn.sigmoid(logits)
    sel = scores + bias_ref[...]
    sub = lax.broadcasted_iota(I32, (GROUP_SIZE, tm), 0)
    grp = []
    for g in range(N_GROUPS):
        (m1, _), (m2, _) = _topk_rows(sel[g * GROUP_SIZE:(g + 1) * GROUP_SIZE], 2, sub)
        grp.append(m1 + m2)
    gscore = jnp.concatenate(grp, axis=0)
    giota = lax.broadcasted_iota(I32, (N_GROUPS, tm), 0)
    gmask = jnp.zeros((N_GROUPS, tm), F32)
    for _, gi in _topk_rows(gscore, TOPK_GROUPS, giota):
        gmask = jnp.where(giota == gi, 1.0, gmask)
    masked = jnp.concatenate(
        [jnp.where(gmask[g:g + 1] > 0.0, sel[g * GROUP_SIZE:(g + 1) * GROUP_SIZE], -jnp.inf)
         for g in range(N_GROUPS)], axis=0)
    eiota = lax.broadcasted_iota(I32, (N_EXPERTS, tm), 0)
    picks = _topk_rows(masked, TOP_K, eiota)
    idx = jnp.concatenate([i for _, i in picks], axis=0)
    top_s = jnp.concatenate(
        [jnp.sum(jnp.where(eiota == i, scores, 0.0), axis=0, keepdims=True) for _, i in picks], axis=0)
    top_w = top_s / jnp.sum(top_s, axis=0, keepdims=True) * ROUTED_SCALE
    idx_ref[...] = idx
    wpad = jnp.concatenate([top_w, jnp.zeros((LANES - TOP_K, tm), F32)], axis=0)
    wt_ref[...] = wpad.T


def _router(h2, w_router, router_bias):
    n, d = h2.shape
    tm = 512
    return pl.pallas_call(
        _router_body,
        grid=(n // tm,),
        in_specs=[
            pl.BlockSpec((tm, d), lambda i: (i, 0)),
            pl.BlockSpec((N_EXPERTS, d), lambda i: (0, 0)),
            pl.BlockSpec((N_EXPERTS, 1), lambda i: (0, 0)),
        ],
        out_specs=[pl.BlockSpec((TOP_K, tm), lambda i: (0, i)), pl.BlockSpec((tm, LANES), lambda i: (i, 0))],
        out_shape=[jax.ShapeDtypeStruct((TOP_K, n), I32), jax.ShapeDtypeStruct((n, LANES), F32)],
        compiler_params=_cparams(("arbitrary",)),
        name="router",
    )(h2, w_router.T.astype(BF16), router_bias.reshape(N_EXPERTS, 1))


def _plan_tile(idx):
    tm = idx.shape[1]
    eiota = lax.broadcasted_iota(I32, (N_EXPERTS, tm), 0)
    hit = jnp.zeros((N_EXPERTS, tm), F32)
    for k in range(TOP_K):
        hit = jnp.where(eiota == idx[k:k + 1], 1.0, hit)
    r = lax.broadcasted_iota(I32, (tm, tm), 0)
    c = lax.broadcasted_iota(I32, (tm, tm), 1)
    before = jnp.where(r < c, 1.0, 0.0).astype(BF16)
    excl = jnp.dot(hit.astype(BF16), before, preferred_element_type=F32)
    tot = excl[:, tm - 1:tm] + hit[:, tm - 1:tm]
    return eiota, hit, excl, tot


def _plan_counts_body(idx_ref, cnt_ref):
    @pl.when(pl.program_id(0) == 0)
    def _():
        cnt_ref[...] = jnp.zeros(cnt_ref.shape, F32)

    _, _, _, tot = _plan_tile(idx_ref[...])
    cnt_ref[...] += tot


def _plan_counts(top_idx):
    n = top_idx.shape[1]
    tm = 512
    return pl.pallas_call(
        _plan_counts_body,
        grid=(n // tm,),
        in_specs=[pl.BlockSpec((TOP_K, tm), lambda i: (0, i))],
        out_specs=pl.BlockSpec((N_EXPERTS, LANES), lambda i: (0, 0)),
        out_shape=jax.ShapeDtypeStruct((N_EXPERTS, LANES), F32),
        compiler_params=_cparams(("arbitrary",)),
        name="plan_counts",
    )(top_idx)


def _plan_dest_body(idx_ref, base_ref, dest_ref, run_ref):
    @pl.when(pl.program_id(0) == 0)
    def _():
        run_ref[...] = base_ref[...]

    idx = idx_ref[...]
    eiota, _, excl, tot = _plan_tile(idx)
    pos = run_ref[:, 0:1] + excl
    rows = [jnp.sum(jnp.where(eiota == idx[k:k + 1], pos, 0.0), axis=0, keepdims=True) for k in range(TOP_K)]
    dest_ref[...] = jnp.concatenate(rows, axis=0).astype(I32)
    run_ref[...] += tot


def _plan_dest(top_idx, pstart):
    n = top_idx.shape[1]
    tm = 512
    base = jnp.broadcast_to(pstart.astype(F32).reshape(N_EXPERTS, 1), (N_EXPERTS, LANES))
    return pl.pallas_call(
        _plan_dest_body,
        grid=(n // tm,),
        in_specs=[pl.BlockSpec((TOP_K, tm), lambda i: (0, i)), pl.BlockSpec((N_EXPERTS, LANES), lambda i: (0, 0))],
        out_specs=pl.BlockSpec((TOP_K, tm), lambda i: (0, i)),
        out_shape=jax.ShapeDtypeStruct((TOP_K, n), I32),
        scratch_shapes=[pltpu.VMEM((N_EXPERTS, LANES), F32)],
        compiler_params=_cparams(("arbitrary",)),
        name="plan_dest",
    )(top_idx, base)


def _capacity(n):
    return n * TOP_K + N_EXPERTS * ROW_BLOCK


def _pack_halves(y):
    w = y.shape[1] // 2
    hi = pltpu.bitcast(y[:, :w].astype(BF16).astype(F32), U32)
    lo = pltpu.bitcast(y[:, w:].astype(BF16).astype(F32), U32)
    return hi | (lo >> 16)


def _unpack_halves(p):
    a = pltpu.bitcast(p & jnp.uint32(0xFFFF0000), F32)
    b = pltpu.bitcast(p << 16, F32)
    return a, b


TOKENS_PER_ISSUE = 4


def _row_dma_loops(row_copy):
    def issue(i, _):
        for u in range(TOKENS_PER_ISSUE):
            for k in range(TOP_K):
                row_copy(i * TOKENS_PER_ISSUE + u, k).start(priority=k % 2)
        return 0

    def drain(i, _):
        for u in range(TOKENS_PER_ISSUE):
            for k in range(TOP_K):
                row_copy(i * TOKENS_PER_ISSUE + u, k).wait()
        return 0

    return issue, drain


def _pad_chunks():
    sizes, s = [], ROW_BLOCK // 2
    while s >= 1:
        sizes.append(s)
        s //= 2
    return sizes


def _tile_rows(t):
    return pl.ds(pl.multiple_of(t * SUBLANES, SUBLANES), SUBLANES)


def _to_row_tiles(ref, packed, lead=()):
    rows = packed.shape[0]
    for s in range(SUBLANES):
        ref[(*lead, pl.ds(s, rows, stride=SUBLANES), slice(None))] = packed[:, s * LANES:(s + 1) * LANES]


def _from_row_tiles(ref, rows, lead=()):
    return [ref[(*lead, pl.ds(s, rows, stride=SUBLANES), slice(None))] for s in range(SUBLANES)]


def _dispatch_body(pstart_ref, cnt_ref, dest_ref, h_ref, xs_ref, xbuf, zbuf, sem, zsem):
    tm = h_ref.shape[0]
    step = pl.program_id(0)

    def pad_dmas(wait):
        def per_expert(e, _):
            cnt = cnt_ref[e]
            pad = (ROW_BLOCK - cnt % ROW_BLOCK) % ROW_BLOCK
            start = pstart_ref[e] + cnt
            for size in _pad_chunks():
                @pl.when((pad & size) != 0)
                def _():
                    off = pl.multiple_of((start + (pad & ~(2 * size - 1))) * SUBLANES, SUBLANES)
                    cp = pltpu.make_async_copy(zbuf.at[pl.ds(0, size * SUBLANES)],
                                               xs_ref.at[pl.ds(off, size * SUBLANES)], zsem)
                    cp.wait() if wait else cp.start()
            return 0
        lax.fori_loop(0, N_EXPERTS, per_expert, 0)

    @pl.when(step == 0)
    def _():
        zbuf[...] = jnp.zeros(zbuf.shape, zbuf.dtype)
        pad_dmas(False)
        pad_dmas(True)

    _to_row_tiles(xbuf, _pack_halves(h_ref[...].astype(F32)))

    def row_copy(t, k):
        return pltpu.make_async_copy(xbuf.at[_tile_rows(t)], xs_ref.at[_tile_rows(dest_ref[k, t])], sem)

    issue, drain = _row_dma_loops(row_copy)
    lax.fori_loop(0, tm // TOKENS_PER_ISSUE, issue, 0)
    lax.fori_loop(0, tm // TOKENS_PER_ISSUE, drain, 0)


def _dispatch(h2, dest, pstart, counts):
    n, d = h2.shape
    tm = 256
    assert d // 2 == SUBLANES * LANES
    grid_spec = pltpu.PrefetchScalarGridSpec(
        num_scalar_prefetch=2,
        grid=(n // tm,),
        in_specs=[
            pl.BlockSpec((TOP_K, tm), lambda i, *_: (0, i), memory_space=pltpu.SMEM),
            pl.BlockSpec((tm, d), lambda i, *_: (i, 0)),
        ],
        out_specs=pl.BlockSpec(memory_space=pl.ANY),
        scratch_shapes=[
            pltpu.VMEM((tm * SUBLANES, LANES), U32),
            pltpu.VMEM((ROW_BLOCK // 2 * SUBLANES, LANES), U32),
            pltpu.SemaphoreType.DMA(()),
            pltpu.SemaphoreType.DMA(()),
        ],
    )
    return pl.pallas_call(
        _dispatch_body,
        grid_spec=grid_spec,
        out_shape=jax.ShapeDtypeStruct((_capacity(n) * SUBLANES, LANES), U32),
        compiler_params=_cparams(("arbitrary",), has_side_effects=True, disable_bounds_checks=True),
        name="dispatch",
    )(pstart, counts, dest, h2)


def _experts_body(be_ref, nu_ref, nxt_ref, ord_ref, x_ref, wg_hbm, wu_hbm, wd_hbm, y_ref,
                  wg_s, wu_s, wd_s, wg_b, wu_b, wd_b, sems):
    i = pl.program_id(0)
    used = i < nu_ref[0]
    e = be_ref[i]
    fresh = jnp.logical_or(i == 0, e != be_ref[jnp.maximum(i - 1, 0)])
    slot = ord_ref[e] % 2

    def weight_copies(expert, s):
        return (pltpu.make_async_copy(wg_hbm.at[expert], wg_s.at[s], sems.at[s, 0]),
                pltpu.make_async_copy(wu_hbm.at[expert], wu_s.at[s], sems.at[s, 1]),
                pltpu.make_async_copy(wd_hbm.at[expert], wd_s.at[s], sems.at[s, 2]))

    @pl.when(i == 0)
    def _():
        for cp in weight_copies(e, slot):
            cp.start()

    @pl.when(jnp.logical_and(used, fresh))
    def _():
        copies = weight_copies(e, slot)
        for cp, dst, src in zip(copies, (wg_b, wu_b, wd_b), (wg_s, wu_s, wd_s)):
            cp.wait()
            dst[...] = src[slot].astype(BF16)

        @pl.when(nxt_ref[e] != e)
        def _():
            for cp in weight_copies(nxt_ref[e], 1 - slot):
                cp.start()

    @pl.when(used)
    def _():
        half = SUBLANES * LANES
        slabs = [_unpack_halves(p) for p in _from_row_tiles(x_ref, ROW_BLOCK)]
        xa = jnp.concatenate([a.astype(BF16) for a, _ in slabs], axis=1)
        xb = jnp.concatenate([b.astype(BF16) for _, b in slabs], axis=1)
        g = (jnp.dot(xa, wg_b[:half], preferred_element_type=F32)
             + jnp.dot(xb, wg_b[half:], preferred_element_type=F32))
        u = (jnp.dot(xa, wu_b[:half], preferred_element_type=F32)
             + jnp.dot(xb, wu_b[half:], preferred_element_type=F32))
        hmid = (g * jax.nn.sigmoid(g) * u).astype(BF16)
        _to_row_tiles(y_ref, _pack_halves(jnp.dot(hmid, wd_b[...], preferred_element_type=F32)))


def _experts(xs, block_e, n_used, next_e, ord_e, w_gate_e, w_up_e, w_down_e):
    cap = xs.shape[0] // SUBLANES
    d = 2 * SUBLANES * LANES
    nblk = cap // ROW_BLOCK
    row_block = (ROW_BLOCK * SUBLANES, LANES)

    def row_map(i, be, nu, nxt, od):
        return (jnp.minimum(i, nu[0] - 1), 0)

    grid_spec = pltpu.PrefetchScalarGridSpec(
        num_scalar_prefetch=4,
        grid=(nblk,),
        in_specs=[
            pl.BlockSpec(row_block, row_map),
            pl.BlockSpec(memory_space=pl.ANY),
            pl.BlockSpec(memory_space=pl.ANY),
            pl.BlockSpec(memory_space=pl.ANY),
        ],
        out_specs=pl.BlockSpec(row_block, row_map),
        scratch_shapes=[
            pltpu.VMEM((2, d, D_EXPERT), F32),
            pltpu.VMEM((2, d, D_EXPERT), F32),
            pltpu.VMEM((2, D_EXPERT, d), F32),
            pltpu.VMEM((d, D_EXPERT), BF16),
            pltpu.VMEM((d, D_EXPERT), BF16),
            pltpu.VMEM((D_EXPERT, d), BF16),
            pltpu.SemaphoreType.DMA((2, 3)),
        ],
    )
    return pl.pallas_call(
        _experts_body,
        grid_spec=grid_spec,
        out_shape=jax.ShapeDtypeStruct(xs.shape, U32),
        compiler_params=_cparams(("arbitrary",)),
        name="experts",
    )(block_e, n_used, next_e, ord_e, xs, w_gate_e, w_up_e, w_down_e)


def _combine_body(dest_ref, ys_ref, wt_ref, h_ref, x1_ref, mod_ref, gpost, wg, wu, wd, o_ref, buf, sem):
    tm = h_ref.shape[0]

    def row_copy(t, k):
        return pltpu.make_async_copy(ys_ref.at[_tile_rows(dest_ref[k, t])], buf.at[k, _tile_rows(t)], sem)

    issue, drain = _row_dma_loops(row_copy)
    lax.fori_loop(0, tm // TOKENS_PER_ISSUE, issue, 0)

    h = h_ref[...]
    g = jnp.dot(h, wg[...], preferred_element_type=F32)
    u = jnp.dot(h, wu[...], preferred_element_type=F32)
    shared = jnp.dot((g * jax.nn.sigmoid(g) * u).astype(BF16), wd[...], preferred_element_type=F32)

    lax.fori_loop(0, tm // TOKENS_PER_ISSUE, drain, 0)

    half = SUBLANES * LANES
    wt = wt_ref[...]
    wk = [jnp.broadcast_to(wt[:, k:k + 1], (tm, LANES)) for k in range(TOP_K)]
    ya, yb = [], []
    for s in range(SUBLANES):
        sa = shared[:, s * LANES:(s + 1) * LANES]
        sb = shared[:, half + s * LANES:half + (s + 1) * LANES]
        for k in range(TOP_K):
            a, b = _unpack_halves(buf[k, pl.ds(s, tm, stride=SUBLANES), :])
            sa = sa + wk[k] * a
            sb = sb + wk[k] * b
        ya.append(sa)
        yb.append(sb)
    y = jnp.concatenate(ya + yb, axis=1)
    gate2 = mod_ref[0, 5:6, :]
    o_ref[...] = x1_ref[...] + gate2 * (_rms(y) * gpost[...])


def _combine(ys, dest, wt, h2, x1, mod3, g_post, wg_b, wu_b, wd_b, seq):
    n, d = h2.shape
    tm = 256
    per_b = seq // tm
    const = lambda shape: pl.BlockSpec(shape, lambda i, *_: (0,) * len(shape), pipeline_mode=pl.Buffered(1))
    grid_spec = pltpu.PrefetchScalarGridSpec(
        num_scalar_prefetch=0,
        grid=(n // tm,),
        in_specs=[
            pl.BlockSpec((TOP_K, tm), lambda i: (0, i), memory_space=pltpu.SMEM),
            pl.BlockSpec(memory_space=pl.ANY),
            pl.BlockSpec((tm, LANES), lambda i: (i, 0)),
            pl.BlockSpec((tm, d), lambda i: (i, 0)),
            pl.BlockSpec((tm, d), lambda i: (i, 0)),
            pl.BlockSpec((1, N_MOD, d), lambda i: (i // per_b, 0, 0)),
            const((1, d)), const(wg_b.shape), const(wu_b.shape), const(wd_b.shape),
        ],
        out_specs=pl.BlockSpec((tm, d), lambda i: (i, 0)),
        scratch_shapes=[pltpu.VMEM((TOP_K, tm * SUBLANES, LANES), U32), pltpu.SemaphoreType.DMA(())],
    )
    return pl.pallas_call(
        _combine_body,
        grid_spec=grid_spec,
        out_shape=jax.ShapeDtypeStruct((n, d), F32),
        compiler_params=_cparams(("arbitrary",), disable_bounds_checks=True),
        name="combine",
    )(dest, ys, wt, h2, x1, mod3, g_post.reshape(1, d), wg_b, wu_b, wd_b)


def _layer(x, mod, positions, g_pre_mix, g_post_mix, g_pre_ffn, g_post_ffn, w_in, w_dil_out, w_sb_out, w_mix_out,
           w_router, router_bias, w_gate_e, w_up_e, w_down_e, w_gate_s, w_up_s, w_down_s):
    bsz, seq, d = x.shape
    n = bsz * seq
    xf = x.reshape(n, d)
    mod3 = mod.reshape(bsz, N_MOD, d)

    nd, nq = 3 * WIDTH_DIL, 3 * (WIDTH_DIL + WIDTH_SB)
    w_plain = jnp.concatenate([w_in[:, nq:], w_in[:, nd:nq]], axis=1).astype(BF16)
    cols = []
    for g in range(len(DIL_PATTERNS)):
        for part in range(3):
            lo = part * WIDTH_DIL + g * D_DIL_OUT
            cols.append(w_in[:, lo:lo + D_DIL_OUT])
    w_dil = jnp.concatenate(cols, axis=1).astype(BF16)

    proj = _inproj(xf, g_pre_mix, mod3, w_plain, seq)
    tables = _rope_tables(positions)
    qkv_dil = _inproj_dil(xf, g_pre_mix, mod3, w_dil, tables, bsz, seq)
    o_dil, lse_dil = [], []
    for g, (window, dilation) in enumerate(DIL_PATTERNS):
        assert window // dilation == Q_BLOCK
        o, lse = _dilated_attention(qkv_dil[g], g, bsz, seq)
        o_dil.append(o)
        lse_dil.append(lse)
    o_sb = _stick_breaking(proj, bsz, seq)

    x1, h2 = _mixout(o_dil, lse_dil, o_sb, proj, xf, mod3, g_post_mix, g_pre_ffn,
                     w_dil_out.astype(BF16), w_sb_out.astype(BF16), w_mix_out.astype(BF16), seq)

    top_idx, top_wt = _router(h2, w_router, router_bias)
    counts = _plan_counts(top_idx)[:, 0].astype(I32)
    padded = (counts + ROW_BLOCK - 1) // ROW_BLOCK * ROW_BLOCK
    pends = jnp.cumsum(padded)
    pstart = pends - padded
    nblk = _capacity(n) // ROW_BLOCK
    n_used = (pends[-1] // ROW_BLOCK).astype(I32)
    blk = jnp.minimum(jnp.arange(nblk, dtype=I32), n_used - 1)
    block_e = jnp.minimum(jnp.sum(pends[None, :] <= (blk * ROW_BLOCK)[:, None], axis=1), N_EXPERTS - 1).astype(I32)
    eids = jnp.arange(N_EXPERTS, dtype=I32)
    has = counts > 0
    ord_e = (jnp.cumsum(has) - has).astype(I32)
    later = jnp.where((eids[None, :] > eids[:, None]) & has[None, :], eids[None, :], N_EXPERTS)
    nxt = jnp.min(later, axis=1)
    next_e = jnp.where(nxt < N_EXPERTS, nxt, eids).astype(I32)
    dest = _plan_dest(top_idx, pstart)

    xs = _dispatch(h2, dest, pstart.astype(I32), counts)
    ys = _experts(xs, block_e, n_used.reshape(1), next_e, ord_e, w_gate_e, w_up_e, w_down_e)
    out = _combine(ys, dest, top_wt, h2, x1, mod3, g_post_ffn,
                   w_gate_s.astype(BF16), w_up_s.astype(BF16), w_down_s.astype(BF16), seq)
    return out.reshape(bsz, seq, d)


def kernel(x, c, positions, w_ada, b_ada, g_pre_mix, g_post_mix, g_pre_ffn, g_post_ffn, w_in, w_dil_out,
           w_sb_out, w_mix_out, w_router, router_bias, w_gate_e, w_up_e, w_down_e, w_gate_s, w_up_s, w_down_s):
    for l in range(w_ada.shape[0]):
        mod = _adaln(c, w_ada[l], b_ada[l])
        x = _layer(x, mod, positions, g_pre_mix[l], g_post_mix[l], g_pre_ffn[l], g_post_ffn[l], w_in[l],
                   w_dil_out[l], w_sb_out[l], w_mix_out[l], w_router[l], router_bias[l],
                   w_gate_e[l], w_up_e[l], w_down_e[l], w_gate_s[l], w_up_s[l], w_down_s[l])
    return x
```

```python
import functools

import jax
import jax.numpy as jnp
from jax import lax
from jax.experimental import pallas as pl
from jax.experimental.pallas import tpu as pltpu

F32 = jnp.float32
BF16 = jnp.bfloat16
I32 = jnp.int32
U32 = jnp.uint32

D_MODEL = 2048
HEAD_DIM = 128
DIL_PATTERNS = ((128, 1), (512, 4), (2048, 16))
HEADS_PER_GROUP = 4
N_HEADS_DIL = 12
N_HEADS_SB = 8
WIDTH_DIL = N_HEADS_DIL * HEAD_DIM
WIDTH_SB = N_HEADS_SB * HEAD_DIM
D_DIL_OUT = HEADS_PER_GROUP * HEAD_DIM
Q_BLOCK = 128
ROPE_THETA = 500000.0
ROPE_DIM = HEAD_DIM // 4
N_GATE = 2 * D_MODEL
N_EXPERTS = 64
TOP_K = 8
N_GROUPS = 8
GROUP_SIZE = N_EXPERTS // N_GROUPS
TOPK_GROUPS = 4
D_EXPERT = 512
D_SHARED = 512
ROUTED_SCALE = 2.5
RMS_EPS = 1e-6
N_MOD = 6
ATTN_SCALE = HEAD_DIM ** -0.5

LANES = 128
SUBLANES = 8
VMEM_LIMIT = 56 * 1024 * 1024

ROW_BLOCK = 704
SB_DEAD = -110.0


def _cparams(sem, **kw):
    return pltpu.CompilerParams(dimension_semantics=sem, vmem_limit_bytes=VMEM_LIMIT, **kw)


def _adaln_body(ct_ref, w_ref, b_ref, o_ref, *, kc):
    nb = ct_ref.shape[1]
    nk = w_ref.shape[0] // kc

    def step(i, acc):
        k0 = pl.multiple_of(i * kc, kc)
        w = w_ref[pl.ds(k0, kc), :]
        c = ct_ref[pl.ds(k0, kc), :]
        s = c * jax.nn.sigmoid(c)
        parts = [jnp.sum(w * s[:, b:b + 1], axis=0, keepdims=True) for b in range(nb)]
        return acc + jnp.concatenate(parts, axis=0)

    acc = lax.fori_loop(0, nk, step, jnp.zeros(o_ref.shape, F32))
    o_ref[...] = acc + b_ref[...]


def _adaln(c, w_ada, b_ada):
    nb, d = c.shape
    n_out = w_ada.shape[1]
    tn = 1024
    return pl.pallas_call(
        functools.partial(_adaln_body, kc=256),
        grid=(n_out // tn,),
        in_specs=[
            pl.BlockSpec((d, nb), lambda j: (0, 0)),
            pl.BlockSpec((d, tn), lambda j: (0, j)),
            pl.BlockSpec((1, tn), lambda j: (0, j)),
        ],
        out_specs=pl.BlockSpec((nb, tn), lambda j: (0, j)),
        out_shape=jax.ShapeDtypeStruct((nb, n_out), F32),
        compiler_params=_cparams(("arbitrary",)),
        name="adaln",
    )(c.T, w_ada, b_ada.reshape(1, n_out))


def _rms(x):
    return x * lax.rsqrt(jnp.mean(x * x, axis=-1, keepdims=True) + RMS_EPS)


def _prenorm(x_ref, g_ref, mod_ref):
    y = _rms(x_ref[...]) * g_ref[...]
    shift = mod_ref[0, 0:1, :]
    scale = mod_ref[0, 1:2, :]
    return (y * (1.0 + scale) + shift).astype(BF16)


def _inproj_body(x_ref, g_ref, mod_ref, w_ref, o_ref, h_ref):
    @pl.when(pl.program_id(1) == 0)
    def _():
        h_ref[...] = _prenorm(x_ref, g_ref, mod_ref)

    o_ref[...] = jnp.dot(h_ref[...], w_ref[...], preferred_element_type=F32).astype(o_ref.dtype)


def _inproj(xf, g_pre, mod3, w_b, seq):
    n, d = xf.shape
    width = w_b.shape[1]
    tm, tn = 1024, width // 4
    per_b = seq // tm
    return pl.pallas_call(
        _inproj_body,
        grid=(n // tm, width // tn),
        in_specs=[
            pl.BlockSpec((tm, d), lambda i, j: (i, 0)),
            pl.BlockSpec((1, d), lambda i, j: (0, 0)),
            pl.BlockSpec((1, N_MOD, d), lambda i, j: (i // per_b, 0, 0)),
            pl.BlockSpec((d, tn), lambda i, j: (0, j)),
        ],
        out_specs=pl.BlockSpec((tm, tn), lambda i, j: (i, j)),
        out_shape=jax.ShapeDtypeStruct((n, width), BF16),
        scratch_shapes=[pltpu.VMEM((tm, d), BF16)],
        compiler_params=_cparams(("arbitrary", "arbitrary")),
        name="inproj",
    )(xf, g_pre.reshape(1, d), mod3, w_b)


def _inproj_dil_body(x_ref, g_ref, mod_ref, w_ref, t_ref, o0, o1, o2, res_ref):
    tm = x_ref.shape[0]
    h = _prenorm(x_ref, g_ref, mod_ref)
    t = t_ref[...]
    gw = 3 * D_DIL_OUT
    for gi, o_ref in enumerate((o0, o1, o2)):
        dil = DIL_PATTERNS[gi][1]
        res = jnp.dot(h, w_ref[:, gi * gw:(gi + 1) * gw], preferred_element_type=F32)
        for hs in range(3 * HEADS_PER_GROUP):
            sl = slice(hs * HEAD_DIM, (hs + 1) * HEAD_DIM)
            res_ref[hs] = _apply_rope(res[:, sl], t) if hs < 2 * HEADS_PER_GROUP else res[:, sl]
        for r in range(dil):
            for hs in range(3 * HEADS_PER_GROUP):
                rows = res_ref[hs] if dil == 1 else res_ref[hs, pl.ds(r, tm // dil, stride=dil), :]
                o_ref[0, r, :, hs * HEAD_DIM:(hs + 1) * HEAD_DIM] = rows.astype(o_ref.dtype)


def _inproj_dil(xf, g_pre, mod3, w_b, tables, bsz, seq):
    n, d = xf.shape
    gw = 3 * D_DIL_OUT
    tm = 512
    per_b = seq // tm
    dils = [p[1] for p in DIL_PATTERNS]
    return pl.pallas_call(
        _inproj_dil_body,
        grid=(n // tm,),
        in_specs=[
            pl.BlockSpec((tm, d), lambda i: (i, 0)),
            pl.BlockSpec((1, d), lambda i: (0, 0)),
            pl.BlockSpec((1, N_MOD, d), lambda i: (i // per_b, 0, 0)),
            pl.BlockSpec(w_b.shape, lambda i: (0, 0), pipeline_mode=pl.Buffered(1)),
            pl.BlockSpec((tm, 3 * LANES), lambda i: (i, 0)),
        ],
        out_specs=[pl.BlockSpec((1, dl, tm // dl, gw), lambda i: (i // per_b, 0, i % per_b, 0)) for dl in dils],
        out_shape=[jax.ShapeDtypeStruct((bsz, dl, seq // dl, gw), BF16) for dl in dils],
        scratch_shapes=[pltpu.VMEM((gw // HEAD_DIM, tm, HEAD_DIM), F32)],
        compiler_params=_cparams(("arbitrary",)),
        name="inproj_dil",
    )(xf, g_pre.reshape(1, d), mod3, w_b, tables)


def _rope_body(pos_ref, f_ref, o_ref):
    ang = pos_ref[...].astype(F32) * f_ref[...]
    c = jnp.cos(ang)
    s = jnp.sin(ang)
    lane = lax.broadcasted_iota(I32, ang.shape, 1)
    half = ROPE_DIM // 2
    o_ref[:, 0:LANES] = c
    o_ref[:, LANES:2 * LANES] = jnp.where(lane >= half, s, 0.0)
    o_ref[:, 2 * LANES:3 * LANES] = jnp.where(lane < half, -s, 0.0)


def _rope_tables(positions):
    n = positions.size
    half = ROPE_DIM // 2
    inv_freq = ROPE_THETA ** (-jnp.arange(0, ROPE_DIM, 2, dtype=F32) / ROPE_DIM)
    f = jnp.concatenate([inv_freq, inv_freq, jnp.zeros((LANES - 2 * half,), F32)]).reshape(1, LANES)
    tm = 2048
    return pl.pallas_call(
        _rope_body,
        grid=(n // tm,),
        in_specs=[pl.BlockSpec((tm, 1), lambda i: (i, 0)), pl.BlockSpec((1, LANES), lambda i: (0, 0))],
        out_specs=pl.BlockSpec((tm, 3 * LANES), lambda i: (i, 0)),
        out_shape=jax.ShapeDtypeStruct((n, 3 * LANES), F32),
        compiler_params=_cparams(("arbitrary",)),
        name="rope_tables",
    )(positions.reshape(n, 1), f)


def _apply_rope(x, t):
    half = ROPE_DIM // 2
    return (x * t[:, 0:LANES]
            + pltpu.roll(x, half, 1) * t[:, LANES:2 * LANES]
            + pltpu.roll(x, LANES - half, 1) * t[:, 2 * LANES:3 * LANES])


def _dil_body(cur_ref, kp_ref, vp_ref, o_ref, lse_ref, obuf, lbuf, *, dil, nsub):
    n = pl.program_id(1)
    tq = nsub * Q_BLOCK
    row = lax.broadcasted_iota(I32, (Q_BLOCK, 2 * Q_BLOCK), 0)
    col = lax.broadcasted_iota(I32, (Q_BLOCK, 2 * Q_BLOCK), 1)
    rel = row + Q_BLOCK - col
    band = jnp.where(rel >= 0, jnp.where(rel <= Q_BLOCK, 1.0, 0.0), 0.0)
    first = jnp.where(col >= Q_BLOCK, band, jnp.where(n > 0, band, 0.0))
    for r in range(dil):
        for h in range(HEADS_PER_GROUP):
            sl = slice(h * HEAD_DIM, (h + 1) * HEAD_DIM)
            ksl = slice(D_DIL_OUT + h * HEAD_DIM, D_DIL_OUT + (h + 1) * HEAD_DIM)
            vsl = slice(2 * D_DIL_OUT + h * HEAD_DIM, 2 * D_DIL_OUT + (h + 1) * HEAD_DIM)
            for j in range(nsub):
                rs = slice(j * Q_BLOCK, (j + 1) * Q_BLOCK)
                ps = slice((j - 1) * Q_BLOCK, j * Q_BLOCK)
                kprev = kp_ref[0, r, :, sl] if j == 0 else cur_ref[0, r, ps, ksl]
                vprev = vp_ref[0, r, :, sl] if j == 0 else cur_ref[0, r, ps, vsl]
                kcat = jnp.concatenate([kprev, cur_ref[0, r, rs, ksl]], axis=0)
                vcat = jnp.concatenate([vprev, cur_ref[0, r, rs, vsl]], axis=0)
                s = lax.dot_general(cur_ref[0, r, rs, sl], kcat, (((1,), (1,)), ((), ())),
                                    preferred_element_type=F32) * ATTN_SCALE
                s = jnp.where((first if j == 0 else band) > 0.0, s, -jnp.inf)
                m = jnp.max(s, axis=-1, keepdims=True)
                p = jnp.exp(s - m)
                l = jnp.sum(p, axis=-1, keepdims=True)
                o = jnp.dot((p / l).astype(BF16), vcat, preferred_element_type=F32)
                lse = jnp.broadcast_to(m + jnp.log(l), (Q_BLOCK, HEAD_DIM))
                if dil == 1:
                    o_ref[h, rs, :] = o
                    lse_ref[h, rs, :] = lse
                else:
                    obuf[h, rs, :] = o
                    lbuf[h, rs, :] = lse
            if dil > 1:
                o_ref[h, pl.ds(r, tq, stride=dil), :] = obuf[h]
                lse_ref[h, pl.ds(r, tq, stride=dil), :] = lbuf[h]


def _dilated_attention(qkv, g, bsz, seq):
    dil = DIL_PATTERNS[g][1]
    length = seq // dil
    tq = min(4 * Q_BLOCK, (16 * Q_BLOCK) // dil, length)
    nsub = tq // Q_BLOCK
    nq = length // tq
    gw = 3 * D_DIL_OUT
    n = bsz * seq

    def prev(colblk):
        return pl.BlockSpec((1, dil, Q_BLOCK, D_DIL_OUT),
                            lambda b, i: (b, 0, jnp.maximum(i * nsub - 1, 0), colblk))

    nh = HEADS_PER_GROUP
    out_spec = pl.BlockSpec((nh, tq * dil, HEAD_DIM), lambda b, i: (0, b * nq + i, 0))
    out_shape = jax.ShapeDtypeStruct((nh, n, HEAD_DIM), F32)
    return pl.pallas_call(
        functools.partial(_dil_body, dil=dil, nsub=nsub),
        grid=(bsz, nq),
        in_specs=[pl.BlockSpec((1, dil, tq, gw), lambda b, i: (b, 0, i, 0)), prev(1), prev(2)],
        out_specs=[out_spec, out_spec],
        out_shape=[out_shape, out_shape],
        scratch_shapes=[pltpu.VMEM((nh, tq, HEAD_DIM), F32), pltpu.VMEM((nh, tq, HEAD_DIM), F32)],
        compiler_params=_cparams(("arbitrary", "arbitrary")),
        name=f"dilated_d{dil}",
    )(qkv, qkv, qkv)


def _sb_body(q_ref, k_ref, v_ref, o_ref, acc_ref, car_ref):
    nblk = q_ref.shape[1] // Q_BLOCK
    r = lax.broadcasted_iota(I32, (Q_BLOCK, Q_BLOCK), 0)
    c = lax.broadcasted_iota(I32, (Q_BLOCK, Q_BLOCK), 1)
    causal = c < r
    rr = lax.broadcasted_iota(I32, (Q_BLOCK, 2 * Q_BLOCK), 0)
    cc = lax.broadcasted_iota(I32, (Q_BLOCK, 2 * Q_BLOCK), 1)
    uo = jnp.where(cc >= Q_BLOCK, 1.0, jnp.where(rr > cc, 1.0, 0.0)).astype(BF16)

    nh = acc_ref.shape[0]
    heads = range(nh)

    def tiles(qs, kb, carries, diag):
        k0 = pl.multiple_of(kb * Q_BLOCK, Q_BLOCK)
        hs = [slice(h * HEAD_DIM, (h + 1) * HEAD_DIM) for h in heads]
        zs = [lax.dot_general(qs[h], k_ref[0, pl.ds(k0, Q_BLOCK), hs[h]], (((1,), (1,)), ((), ())),
                              preferred_element_type=F32) * ATTN_SCALE for h in heads]
        stacked, log_s = [], []
        for z in zs:
            sp = jnp.log(1.0 + jnp.exp(-jnp.abs(z)))
            mx = jnp.maximum(z, 0.0)
            log_1m = -(mx + sp)
            if diag:
                log_1m = jnp.where(causal, log_1m, 0.0)
            hi = log_1m.astype(BF16)
            lo = (log_1m - hi.astype(F32)).astype(BF16)
            stacked.append(jnp.concatenate([hi, lo], axis=0))
            log_s.append((z - mx) - sp)
        r2s = [jnp.dot(s, uo, preferred_element_type=F32) for s in stacked]
        probs, new_carries = [], []
        for h in heads:
            sums = r2s[h][:Q_BLOCK] + r2s[h][Q_BLOCK:]
            a = jnp.exp(log_s[h] + carries[h] + sums[:, :Q_BLOCK])
            if diag:
                a = jnp.where(causal, a, 0.0)
            probs.append(a.astype(BF16))
            new_carries.append(carries[h] + sums[:, Q_BLOCK:])
        pvs = [jnp.dot(probs[h], v_ref[0, pl.ds(k0, Q_BLOCK), hs[h]], preferred_element_type=F32) for h in heads]
        return pvs, new_carries

    def all_max(xs):
        m = xs[0]
        for x in xs[1:]:
            m = jnp.maximum(m, x)
        return jnp.max(m)

    def qblock(qi, _):
        q0 = pl.multiple_of(qi * Q_BLOCK, Q_BLOCK)
        qs = [q_ref[0, pl.ds(q0, Q_BLOCK), h * HEAD_DIM:(h + 1) * HEAD_DIM] for h in heads]
        zero = jnp.zeros((Q_BLOCK, Q_BLOCK), F32)
        pvs, cars = tiles(qs, qi, [zero] * nh, True)
        for h in heads:
            acc_ref[h] = pvs[h]
            car_ref[h] = cars[h]

        def cond(st):
            return jnp.logical_and(st[0] >= 0, st[1] > SB_DEAD)

        def body(st):
            pvs, cars = tiles(qs, st[0], [car_ref[h] for h in heads], False)
            for h in heads:
                acc_ref[h] += pvs[h]
                car_ref[h] = cars[h]
            return st[0] - 1, all_max(cars)

        lax.while_loop(cond, body, (qi - 1, all_max(cars)))
        for h in range(nh):
            o_ref[0, pl.ds(q0, Q_BLOCK), h * HEAD_DIM:(h + 1) * HEAD_DIM] = acc_ref[h].astype(o_ref.dtype)
        return 0

    lax.fori_loop(0, nblk, qblock, 0)


SB_HEADS_PER_STEP = 8


def _stick_breaking(proj, bsz, seq):
    width = proj.shape[1]
    pv = proj.reshape(bsz, seq, width)
    nh = SB_HEADS_PER_STEP
    bw = nh * HEAD_DIM
    base = N_GATE // bw

    def spec(off):
        return pl.BlockSpec((1, seq, bw), lambda b, h: (b, 0, base + off + h), pipeline_mode=pl.Buffered(1))

    nstep = N_HEADS_SB // nh
    o = pl.pallas_call(
        _sb_body,
        grid=(bsz, nstep),
        in_specs=[spec(0), spec(nstep), spec(2 * nstep)],
        out_specs=pl.BlockSpec((1, seq, bw), lambda b, h: (b, 0, h)),
        out_shape=jax.ShapeDtypeStruct((bsz, seq, WIDTH_SB), BF16),
        scratch_shapes=[pltpu.VMEM((nh, Q_BLOCK, Q_BLOCK), F32), pltpu.VMEM((nh, Q_BLOCK, Q_BLOCK), F32)],
        compiler_params=_cparams(("arbitrary", "arbitrary")),
        name="stick_breaking",
    )(pv, pv, pv)
    return o.reshape(bsz * seq, WIDTH_SB)


def _mixout_body(o1, o2, o3, l1, l2, l3, osb, gd_ref, gs_ref, x_ref, mod_ref, gpost, gpre,
                 wd, ws, wm, x1_ref, h2_ref):
    heads = []
    for h in range(HEADS_PER_GROUP):
        la, lb, lc = l1[h], l2[h], l3[h]
        m = jnp.maximum(la, jnp.maximum(lb, lc))
        ea, eb, ec = jnp.exp(la - m), jnp.exp(lb - m), jnp.exp(lc - m)
        heads.append(((ea * o1[h] + eb * o2[h] + ec * o3[h]) / (ea + eb + ec)).astype(BF16))
    yd = jnp.dot(jnp.concatenate(heads, axis=1), wd[...], preferred_element_type=F32)
    ys = jnp.dot(osb[...], ws[...], preferred_element_type=F32)
    mix = jax.nn.sigmoid(gd_ref[...].astype(F32)) * yd + jax.nn.sigmoid(gs_ref[...].astype(F32)) * ys
    y = jnp.dot(mix.astype(BF16), wm[...], preferred_element_type=F32)
    gate1 = mod_ref[0, 2:3, :]
    shift2 = mod_ref[0, 3:4, :]
    scale2 = mod_ref[0, 4:5, :]
    x1 = x_ref[...] + gate1 * (_rms(y) * gpost[...])
    x1_ref[...] = x1
    h2_ref[...] = ((_rms(x1) * gpre[...]) * (1.0 + scale2) + shift2).astype(h2_ref.dtype)


def _const_spec(shape):
    return pl.BlockSpec(shape, lambda i: (0,) * len(shape), pipeline_mode=pl.Buffered(1))


def _mixout(o_dil, lse_dil, o_sb, proj, xf, mod3, g_post, g_pre, wd_b, ws_b, wm_b, seq):
    n, d = xf.shape
    tm = 256
    per_b = seq // tm
    row = lambda w: pl.BlockSpec((tm, w), lambda i: (i, 0))
    head_major = pl.BlockSpec((HEADS_PER_GROUP, tm, HEAD_DIM), lambda i: (0, i, 0))
    in_specs = (
        [head_major] * 6 + [row(WIDTH_SB)]
        + [pl.BlockSpec((tm, d), lambda i: (i, 0)), pl.BlockSpec((tm, d), lambda i: (i, 1))]
        + [row(d), pl.BlockSpec((1, N_MOD, d), lambda i: (i // per_b, 0, 0))]
        + [_const_spec((1, d)), _const_spec((1, d))]
        + [_const_spec(wd_b.shape), _const_spec(ws_b.shape), _const_spec(wm_b.shape)]
    )
    return pl.pallas_call(
        _mixout_body,
        grid=(n // tm,),
        in_specs=in_specs,
        out_specs=[row(d), row(d)],
        out_shape=[jax.ShapeDtypeStruct((n, d), F32), jax.ShapeDtypeStruct((n, d), BF16)],
        compiler_params=_cparams(("arbitrary",)),
        name="mixout",
    )(*o_dil, *lse_dil, o_sb, proj, proj, xf, mod3, g_post.reshape(1, d), g_pre.reshape(1, d), wd_b, ws_b, wm_b)


def _topk_rows(x, k, iota0):
    big = x.shape[0]
    out = []
    for _ in range(k):
        m = jnp.max(x, axis=0, keepdims=True)
        i = jnp.min(jnp.where(x == m, iota0, big), axis=0, keepdims=True)
        out.append((m, i))
        x = jnp.where(iota0 == i, -jnp.inf, x)
    return out


def _router_body(h_ref, wr_ref, bias_ref, idx_ref, wt_ref):
    tm = h_ref.shape[0]
    logits = lax.dot_general(wr_ref[...], h_ref[...], (((1,), (1,)), ((), ())), preferred_element_type=F32)
    scores = jax.nn.sigmoid(logits)
    sel = scores + bias_ref[...]
    sub = lax.broadcasted_iota(I32, (GROUP_SIZE, tm), 0)
    grp = []
    for g in range(N_GROUPS):
        (m1, _), (m2, _) = _topk_rows(sel[g * GROUP_SIZE:(g + 1) * GROUP_SIZE], 2, sub)
        grp.append(m1 + m2)
    gscore = jnp.concatenate(grp, axis=0)
    giota = lax.broadcasted_iota(I32, (N_GROUPS, tm), 0)
    gmask = jnp.zeros((N_GROUPS, tm), F32)
    for _, gi in _topk_rows(gscore, TOPK_GROUPS, giota):
        gmask = jnp.where(giota == gi, 1.0, gmask)
    masked = jnp.concatenate(
        [jnp.where(gmask[g:g + 1] > 0.0, sel[g * GROUP_SIZE:(g + 1) * GROUP_SIZE], -jnp.inf)
         for g in range(N_GROUPS)], axis=0)
    eiota = lax.broadcasted_iota(I32, (N_EXPERTS, tm), 0)
    picks = _topk_rows(masked, TOP_K, eiota)
    idx = jnp.concatenate([i for _, i in picks], axis=0)
    top_s = jnp.concatenate(
        [jnp.sum(jnp.where(eiota == i, scores, 0.0), axis=0, keepdims=True) for _, i in picks], axis=0)
    top_w = top_s / jnp.sum(top_s, axis=0, keepdims=True) * ROUTED_SCALE
    idx_ref[...] = idx
    wpad = jnp.concatenate([top_w, jnp.zeros((LANES - TOP_K, tm), F32)], axis=0)
    wt_ref[...] = wpad.T


def _router(h2, w_router, router_bias):
    n, d = h2.shape
    tm = 512
    return pl.pallas_call(
        _router_body,
        grid=(n // tm,),
        in_specs=[
            pl.BlockSpec((tm, d), lambda i: (i, 0)),
            pl.BlockSpec((N_EXPERTS, d), lambda i: (0, 0)),
            pl.BlockSpec((N_EXPERTS, 1), lambda i: (0, 0)),
        ],
        out_specs=[pl.BlockSpec((TOP_K, tm), lambda i: (0, i)), pl.BlockSpec((tm, LANES), lambda i: (i, 0))],
        out_shape=[jax.ShapeDtypeStruct((TOP_K, n), I32), jax.ShapeDtypeStruct((n, LANES), F32)],
        compiler_params=_cparams(("arbitrary",)),
        name="router",
    )(h2, w_router.T.astype(BF16), router_bias.reshape(N_EXPERTS, 1))


def _plan_tile(idx):
    tm = idx.shape[1]
    eiota = lax.broadcasted_iota(I32, (N_EXPERTS, tm), 0)
    hit = jnp.zeros((N_EXPERTS, tm), F32)
    for k in range(TOP_K):
        hit = jnp.where(eiota == idx[k:k + 1], 1.0, hit)
    r = lax.broadcasted_iota(I32, (tm, tm), 0)
    c = lax.broadcasted_iota(I32, (tm, tm), 1)
    before = jnp.where(r < c, 1.0, 0.0).astype(BF16)
    excl = jnp.dot(hit.astype(BF16), before, preferred_element_type=F32)
    tot = excl[:, tm - 1:tm] + hit[:, tm - 1:tm]
    return eiota, hit, excl, tot


def _plan_counts_body(idx_ref, cnt_ref):
    @pl.when(pl.program_id(0) == 0)
    def _():
        cnt_ref[...] = jnp.zeros(cnt_ref.shape, F32)

    _, _, _, tot = _plan_tile(idx_ref[...])
    cnt_ref[...] += tot


def _plan_counts(top_idx):
    n = top_idx.shape[1]
    tm = 512
    return pl.pallas_call(
        _plan_counts_body,
        grid=(n // tm,),
        in_specs=[pl.BlockSpec((TOP_K, tm), lambda i: (0, i))],
        out_specs=pl.BlockSpec((N_EXPERTS, LANES), lambda i: (0, 0)),
        out_shape=jax.ShapeDtypeStruct((N_EXPERTS, LANES), F32),
        compiler_params=_cparams(("arbitrary",)),
        name="plan_counts",
    )(top_idx)


def _plan_dest_body(idx_ref, base_ref, dest_ref, run_ref):
    @pl.when(pl.program_id(0) == 0)
    def _():
        run_ref[...] = base_ref[...]

    idx = idx_ref[...]
    eiota, _, excl, tot = _plan_tile(idx)
    pos = run_ref[:, 0:1] + excl
    rows = [jnp.sum(jnp.where(eiota == idx[k:k + 1], pos, 0.0), axis=0, keepdims=True) for k in range(TOP_K)]
    dest_ref[...] = jnp.concatenate(rows, axis=0).astype(I32)
    run_ref[...] += tot


def _plan_dest(top_idx, pstart):
    n = top_idx.shape[1]
    tm = 512
    base = jnp.broadcast_to(pstart.astype(F32).reshape(N_EXPERTS, 1), (N_EXPERTS, LANES))
    return pl.pallas_call(
        _plan_dest_body,
        grid=(n // tm,),
        in_specs=[pl.BlockSpec((TOP_K, tm), lambda i: (0, i)), pl.BlockSpec((N_EXPERTS, LANES), lambda i: (0, 0))],
        out_specs=pl.BlockSpec((TOP_K, tm), lambda i: (0, i)),
        out_shape=jax.ShapeDtypeStruct((TOP_K, n), I32),
        scratch_shapes=[pltpu.VMEM((N_EXPERTS, LANES), F32)],
        compiler_params=_cparams(("arbitrary",)),
        name="plan_dest",
    )(top_idx, base)


def _capacity(n):
    return (pl.cdiv(n * TOP_K, ROW_BLOCK) + N_EXPERTS) * ROW_BLOCK


def _pack_halves(y):
    w = y.shape[1] // 2
    hi = pltpu.bitcast(y[:, :w].astype(BF16).astype(F32), U32)
    lo = pltpu.bitcast(y[:, w:].astype(BF16).astype(F32), U32)
    return hi | (lo >> 16)


def _unpack_halves(p):
    a = pltpu.bitcast(p & jnp.uint32(0xFFFF0000), F32)
    b = pltpu.bitcast(p << 16, F32)
    return a, b


TOKENS_PER_ISSUE = 4


def _row_dma_loops(row_copy):
    def issue(i, _):
        for u in range(TOKENS_PER_ISSUE):
            for k in range(TOP_K):
                row_copy(i * TOKENS_PER_ISSUE + u, k).start(priority=k % 2)
        return 0

    def drain(i, _):
        for u in range(TOKENS_PER_ISSUE):
            for k in range(TOP_K):
                row_copy(i * TOKENS_PER_ISSUE + u, k).wait()
        return 0

    return issue, drain


def _pad_chunks():
    s = 1
    while 2 * s < ROW_BLOCK:
        s *= 2
    sizes = []
    while s >= 1:
        sizes.append(s)
        s //= 2
    return sizes


def _tile_rows(t):
    return pl.ds(pl.multiple_of(t * SUBLANES, SUBLANES), SUBLANES)


def _to_row_tiles(ref, packed, lead=()):
    rows = packed.shape[0]
    for s in range(SUBLANES):
        ref[(*lead, pl.ds(s, rows, stride=SUBLANES), slice(None))] = packed[:, s * LANES:(s + 1) * LANES]


def _from_row_tiles(ref, rows, lead=()):
    return [ref[(*lead, pl.ds(s, rows, stride=SUBLANES), slice(None))] for s in range(SUBLANES)]


def _dispatch_body(pstart_ref, cnt_ref, dest_ref, h_ref, xs_ref, xbuf, zbuf, sem, zsem):
    tm = h_ref.shape[0]
    step = pl.program_id(0)

    def pad_dmas(wait):
        def per_expert(e, _):
            cnt = cnt_ref[e]
            pad = (ROW_BLOCK - cnt % ROW_BLOCK) % ROW_BLOCK
            start = pstart_ref[e] + cnt
            for size in _pad_chunks():
                @pl.when((pad & size) != 0)
                def _():
                    off = pl.multiple_of((start + (pad & ~(2 * size - 1))) * SUBLANES, SUBLANES)
                    cp = pltpu.make_async_copy(zbuf.at[pl.ds(0, size * SUBLANES)],
                                               xs_ref.at[pl.ds(off, size * SUBLANES)], zsem)
                    cp.wait() if wait else cp.start()
            return 0
        lax.fori_loop(0, N_EXPERTS, per_expert, 0)

    @pl.when(step == 0)
    def _():
        zbuf[...] = jnp.zeros(zbuf.shape, zbuf.dtype)
        pad_dmas(False)
        pad_dmas(True)

    _to_row_tiles(xbuf, _pack_halves(h_ref[...].astype(F32)))

    def row_copy(t, k):
        return pltpu.make_async_copy(xbuf.at[_tile_rows(t)], xs_ref.at[_tile_rows(dest_ref[k, t])], sem)

    issue, drain = _row_dma_loops(row_copy)
    lax.fori_loop(0, tm // TOKENS_PER_ISSUE, issue, 0)
    lax.fori_loop(0, tm // TOKENS_PER_ISSUE, drain, 0)


def _dispatch(h2, dest, pstart, counts):
    n, d = h2.shape
    tm = 256
    assert d // 2 == SUBLANES * LANES
    grid_spec = pltpu.PrefetchScalarGridSpec(
        num_scalar_prefetch=2,
        grid=(n // tm,),
        in_specs=[
            pl.BlockSpec((TOP_K, tm), lambda i, *_: (0, i), memory_space=pltpu.SMEM),
            pl.BlockSpec((tm, d), lambda i, *_: (i, 0)),
        ],
        out_specs=pl.BlockSpec(memory_space=pl.ANY),
        scratch_shapes=[
            pltpu.VMEM((tm * SUBLANES, LANES), U32),
            pltpu.VMEM((_pad_chunks()[0] * SUBLANES, LANES), U32),
            pltpu.SemaphoreType.DMA(()),
            pltpu.SemaphoreType.DMA(()),
        ],
    )
    return pl.pallas_call(
        _dispatch_body,
        grid_spec=grid_spec,
        out_shape=jax.ShapeDtypeStruct((_capacity(n) * SUBLANES, LANES), U32),
        compiler_params=_cparams(("arbitrary",), has_side_effects=True, disable_bounds_checks=True),
        name="dispatch",
    )(pstart, counts, dest, h2)


def _experts_body(be_ref, nu_ref, nxt_ref, ord_ref, x_ref, wg_hbm, wu_hbm, wd_hbm, y_ref,
                  wg_s, wu_s, wd_s, wg_b, wu_b, wd_b, sems):
    i = pl.program_id(0)
    used = i < nu_ref[0]
    e = be_ref[i]
    fresh = jnp.logical_or(i == 0, e != be_ref[jnp.maximum(i - 1, 0)])
    slot = ord_ref[e] % 2

    def weight_copies(expert, s):
        return (pltpu.make_async_copy(wg_hbm.at[expert], wg_s.at[s], sems.at[s, 0]),
                pltpu.make_async_copy(wu_hbm.at[expert], wu_s.at[s], sems.at[s, 1]),
                pltpu.make_async_copy(wd_hbm.at[expert], wd_s.at[s], sems.at[s, 2]))

    @pl.when(i == 0)
    def _():
        for cp in weight_copies(e, slot):
            cp.start()

    @pl.when(jnp.logical_and(used, fresh))
    def _():
        copies = weight_copies(e, slot)
        for cp, dst, src in zip(copies, (wg_b, wu_b, wd_b), (wg_s, wu_s, wd_s)):
            cp.wait()
            dst[...] = src[slot].astype(BF16)

        @pl.when(nxt_ref[e] != e)
        def _():
            for cp in weight_copies(nxt_ref[e], 1 - slot):
                cp.start()

    @pl.when(used)
    def _():
        half = SUBLANES * LANES
        slabs = [_unpack_halves(p) for p in _from_row_tiles(x_ref, ROW_BLOCK)]
        xa = jnp.concatenate([a.astype(BF16) for a, _ in slabs], axis=1)
        xb = jnp.concatenate([b.astype(BF16) for _, b in slabs], axis=1)
        g = (jnp.dot(xa, wg_b[:half], preferred_element_type=F32)
             + jnp.dot(xb, wg_b[half:], preferred_element_type=F32))
        u = (jnp.dot(xa, wu_b[:half], preferred_element_type=F32)
             + jnp.dot(xb, wu_b[half:], preferred_element_type=F32))
        hmid = (g * jax.nn.sigmoid(g) * u).astype(BF16)
        _to_row_tiles(y_ref, _pack_halves(jnp.dot(hmid, wd_b[...], preferred_element_type=F32)))


def _experts(xs, block_e, n_used, next_e, ord_e, w_gate_e, w_up_e, w_down_e):
    cap = xs.shape[0] // SUBLANES
    d = 2 * SUBLANES * LANES
    nblk = cap // ROW_BLOCK
    row_block = (ROW_BLOCK * SUBLANES, LANES)

    def row_map(i, be, nu, nxt, od):
        return (jnp.minimum(i, nu[0] - 1), 0)

    grid_spec = pltpu.PrefetchScalarGridSpec(
        num_scalar_prefetch=4,
        grid=(nblk,),
        in_specs=[
            pl.BlockSpec(row_block, row_map),
            pl.BlockSpec(memory_space=pl.ANY),
            pl.BlockSpec(memory_space=pl.ANY),
            pl.BlockSpec(memory_space=pl.ANY),
        ],
        out_specs=pl.BlockSpec(row_block, row_map),
        scratch_shapes=[
            pltpu.VMEM((2, d, D_EXPERT), F32),
            pltpu.VMEM((2, d, D_EXPERT), F32),
            pltpu.VMEM((2, D_EXPERT, d), F32),
            pltpu.VMEM((d, D_EXPERT), BF16),
            pltpu.VMEM((d, D_EXPERT), BF16),
            pltpu.VMEM((D_EXPERT, d), BF16),
            pltpu.SemaphoreType.DMA((2, 3)),
        ],
    )
    return pl.pallas_call(
        _experts_body,
        grid_spec=grid_spec,
        out_shape=jax.ShapeDtypeStruct(xs.shape, U32),
        compiler_params=_cparams(("arbitrary",)),
        name="experts",
    )(block_e, n_used, next_e, ord_e, xs, w_gate_e, w_up_e, w_down_e)


def _combine_body(dest_ref, ys_ref, wt_ref, h_ref, x1_ref, mod_ref, gpost, wg, wu, wd, o_ref, buf, sem):
    tm = h_ref.shape[0]

    def row_copy(t, k):
        return pltpu.make_async_copy(ys_ref.at[_tile_rows(dest_ref[k, t])], buf.at[k, _tile_rows(t)], sem)

    issue, drain = _row_dma_loops(row_copy)
    lax.fori_loop(0, tm // TOKENS_PER_ISSUE, issue, 0)

    h = h_ref[...]
    g = jnp.dot(h, wg[...], preferred_element_type=F32)
    u = jnp.dot(h, wu[...], preferred_element_type=F32)
    shared = jnp.dot((g * jax.nn.sigmoid(g) * u).astype(BF16), wd[...], preferred_element_type=F32)

    lax.fori_loop(0, tm // TOKENS_PER_ISSUE, drain, 0)

    half = SUBLANES * LANES
    wt = wt_ref[...]
    wk = [jnp.broadcast_to(wt[:, k:k + 1], (tm, LANES)) for k in range(TOP_K)]
    ya, yb = [], []
    for s in range(SUBLANES):
        sa = shared[:, s * LANES:(s + 1) * LANES]
        sb = shared[:, half + s * LANES:half + (s + 1) * LANES]
        for k in range(TOP_K):
            a, b = _unpack_halves(buf[k, pl.ds(s, tm, stride=SUBLANES), :])
            sa = sa + wk[k] * a
            sb = sb + wk[k] * b
        ya.append(sa)
        yb.append(sb)
    y = jnp.concatenate(ya + yb, axis=1)
    gate2 = mod_ref[0, 5:6, :]
    o_ref[...] = x1_ref[...] + gate2 * (_rms(y) * gpost[...])


def _combine(ys, dest, wt, h2, x1, mod3, g_post, wg_b, wu_b, wd_b, seq):
    n, d = h2.shape
    tm = 256
    per_b = seq // tm
    const = lambda shape: pl.BlockSpec(shape, lambda i, *_: (0,) * len(shape), pipeline_mode=pl.Buffered(1))
    grid_spec = pltpu.PrefetchScalarGridSpec(
        num_scalar_prefetch=0,
        grid=(n // tm,),
        in_specs=[
            pl.BlockSpec((TOP_K, tm), lambda i: (0, i), memory_space=pltpu.SMEM),
            pl.BlockSpec(memory_space=pl.ANY),
            pl.BlockSpec((tm, LANES), lambda i: (i, 0)),
            pl.BlockSpec((tm, d), lambda i: (i, 0)),
            pl.BlockSpec((tm, d), lambda i: (i, 0)),
            pl.BlockSpec((1, N_MOD, d), lambda i: (i // per_b, 0, 0)),
            const((1, d)), const(wg_b.shape), const(wu_b.shape), const(wd_b.shape),
        ],
        out_specs=pl.BlockSpec((tm, d), lambda i: (i, 0)),
        scratch_shapes=[pltpu.VMEM((TOP_K, tm * SUBLANES, LANES), U32), pltpu.SemaphoreType.DMA(())],
    )
    return pl.pallas_call(
        _combine_body,
        grid_spec=grid_spec,
        out_shape=jax.ShapeDtypeStruct((n, d), F32),
        compiler_params=_cparams(("arbitrary",), disable_bounds_checks=True),
        name="combine",
    )(dest, ys, wt, h2, x1, mod3, g_post.reshape(1, d), wg_b, wu_b, wd_b)


def _layer(x, mod, positions, g_pre_mix, g_post_mix, g_pre_ffn, g_post_ffn, w_in, w_dil_out, w_sb_out, w_mix_out,
           w_router, router_bias, w_gate_e, w_up_e, w_down_e, w_gate_s, w_up_s, w_down_s):
    bsz, seq, d = x.shape
    n = bsz * seq
    xf = x.reshape(n, d)
    mod3 = mod.reshape(bsz, N_MOD, d)

    nd, nq = 3 * WIDTH_DIL, 3 * (WIDTH_DIL + WIDTH_SB)
    w_plain = jnp.concatenate([w_in[:, nq:], w_in[:, nd:nq]], axis=1).astype(BF16)
    cols = []
    for g in range(len(DIL_PATTERNS)):
        for part in range(3):
            lo = part * WIDTH_DIL + g * D_DIL_OUT
            cols.append(w_in[:, lo:lo + D_DIL_OUT])
    w_dil = jnp.concatenate(cols, axis=1).astype(BF16)

    proj = _inproj(xf, g_pre_mix, mod3, w_plain, seq)
    tables = _rope_tables(positions)
    qkv_dil = _inproj_dil(xf, g_pre_mix, mod3, w_dil, tables, bsz, seq)
    o_dil, lse_dil = [], []
    for g, (window, dilation) in enumerate(DIL_PATTERNS):
        assert window // dilation == Q_BLOCK
        o, lse = _dilated_attention(qkv_dil[g], g, bsz, seq)
        o_dil.append(o)
        lse_dil.append(lse)
    o_sb = _stick_breaking(proj, bsz, seq)

    x1, h2 = _mixout(o_dil, lse_dil, o_sb, proj, xf, mod3, g_post_mix, g_pre_ffn,
                     w_dil_out.astype(BF16), w_sb_out.astype(BF16), w_mix_out.astype(BF16), seq)

    top_idx, top_wt = _router(h2, w_router, router_bias)
    counts = _plan_counts(top_idx)[:, 0].astype(I32)
    padded = (counts + ROW_BLOCK - 1) // ROW_BLOCK * ROW_BLOCK
    pends = jnp.cumsum(padded)
    pstart = pends - padded
    nblk = _capacity(n) // ROW_BLOCK
    n_used = (pends[-1] // ROW_BLOCK).astype(I32)
    blk = jnp.minimum(jnp.arange(nblk, dtype=I32), n_used - 1)
    block_e = jnp.minimum(jnp.sum(pends[None, :] <= (blk * ROW_BLOCK)[:, None], axis=1), N_EXPERTS - 1).astype(I32)
    eids = jnp.arange(N_EXPERTS, dtype=I32)
    has = counts > 0
    ord_e = (jnp.cumsum(has) - has).astype(I32)
    later = jnp.where((eids[None, :] > eids[:, None]) & has[None, :], eids[None, :], N_EXPERTS)
    nxt = jnp.min(later, axis=1)
    next_e = jnp.where(nxt < N_EXPERTS, nxt, eids).astype(I32)
    dest = _plan_dest(top_idx, pstart)

    xs = _dispatch(h2, dest, pstart.astype(I32), counts)
    ys = _experts(xs, block_e, n_used.reshape(1), next_e, ord_e, w_gate_e, w_up_e, w_down_e)
    out = _combine(ys, dest, top_wt, h2, x1, mod3, g_post_ffn,
                   w_gate_s.astype(BF16), w_up_s.astype(BF16), w_down_s.astype(BF16), seq)
    return out.reshape(bsz, seq, d)


def kernel(x, c, positions, w_ada, b_ada, g_pre_mix, g_post_mix, g_pre_ffn, g_post_ffn, w_in, w_dil_out,
           w_sb_out, w_mix_out, w_router, router_bias, w_gate_e, w_up_e, w_down_e, w_gate_s, w_up_s, w_down_s):
    for l in range(w_ada.shape[0]):
        mod = _adaln(c, w_ada[l], b_ada[l])
        x = _layer(x, mod, positions, g_pre_mix[l], g_post_mix[l], g_pre_ffn[l], g_post_ffn[l], w_in[l],
                   w_dil_out[l], w_sb_out[l], w_mix_out[l], w_router[l], router_bias[l],
                   w_gate_e[l], w_up_e[l], w_down_e[l], w_gate_s[l], w_up_s[l], w_down_s[l])
    return x
```

```python
import functools

import jax
import jax.numpy as jnp
from jax import lax
from jax.experimental import pallas as pl
from jax.experimental.pallas import tpu as pltpu

F32 = jnp.float32
BF16 = jnp.bfloat16
I32 = jnp.int32
U32 = jnp.uint32

D_MODEL = 2048
HEAD_DIM = 128
DIL_PATTERNS = ((128, 1), (512, 4), (2048, 16))
HEADS_PER_GROUP = 4
N_HEADS_DIL = 12
N_HEADS_SB = 8
WIDTH_DIL = N_HEADS_DIL * HEAD_DIM
WIDTH_SB = N_HEADS_SB * HEAD_DIM
D_DIL_OUT = HEADS_PER_GROUP * HEAD_DIM
Q_BLOCK = 128
ROPE_THETA = 500000.0
ROPE_DIM = HEAD_DIM // 4
N_GATE = 2 * D_MODEL
N_EXPERTS = 64
TOP_K = 8
N_GROUPS = 8
GROUP_SIZE = N_EXPERTS // N_GROUPS
TOPK_GROUPS = 4
D_EXPERT = 512
D_SHARED = 512
ROUTED_SCALE = 2.5
RMS_EPS = 1e-6
N_MOD = 6
ATTN_SCALE = HEAD_DIM ** -0.5

LANES = 128
SUBLANES = 8
VMEM_LIMIT = 56 * 1024 * 1024

ROW_BLOCK = 704
SB_DEAD = -110.0


def _cparams(sem, **kw):
    return pltpu.CompilerParams(dimension_semantics=sem, vmem_limit_bytes=VMEM_LIMIT, **kw)


def _adaln_body(ct_ref, w_ref, b_ref, o_ref, *, kc):
    nb = ct_ref.shape[1]
    nk = w_ref.shape[0] // kc

    def step(i, acc):
        k0 = pl.multiple_of(i * kc, kc)
        w = w_ref[pl.ds(k0, kc), :]
        c = ct_ref[pl.ds(k0, kc), :]
        s = c * jax.nn.sigmoid(c)
        parts = [jnp.sum(w * s[:, b:b + 1], axis=0, keepdims=True) for b in range(nb)]
        return acc + jnp.concatenate(parts, axis=0)

    acc = lax.fori_loop(0, nk, step, jnp.zeros(o_ref.shape, F32))
    o_ref[...] = acc + b_ref[...]


def _adaln(c, w_ada, b_ada):
    nb, d = c.shape
    n_out = w_ada.shape[1]
    tn = 1024
    return pl.pallas_call(
        functools.partial(_adaln_body, kc=256),
        grid=(n_out // tn,),
        in_specs=[
            pl.BlockSpec((d, nb), lambda j: (0, 0)),
            pl.BlockSpec((d, tn), lambda j: (0, j)),
            pl.BlockSpec((1, tn), lambda j: (0, j)),
        ],
        out_specs=pl.BlockSpec((nb, tn), lambda j: (0, j)),
        out_shape=jax.ShapeDtypeStruct((nb, n_out), F32),
        compiler_params=_cparams(("arbitrary",)),
        name="adaln",
    )(c.T, w_ada, b_ada.reshape(1, n_out))


def _rms(x):
    return x * lax.rsqrt(jnp.mean(x * x, axis=-1, keepdims=True) + RMS_EPS)


def _prenorm(x_ref, g_ref, mod_ref):
    y = _rms(x_ref[...]) * g_ref[...]
    shift = mod_ref[0, 0:1, :]
    scale = mod_ref[0, 1:2, :]
    return (y * (1.0 + scale) + shift).astype(BF16)


def _inproj_body(x_ref, g_ref, mod_ref, w_ref, o_ref, h_ref):
    @pl.when(pl.program_id(1) == 0)
    def _():
        h_ref[...] = _prenorm(x_ref, g_ref, mod_ref)

    o_ref[...] = jnp.dot(h_ref[...], w_ref[...], preferred_element_type=F32).astype(o_ref.dtype)


def _inproj(xf, g_pre, mod3, w_b, seq):
    n, d = xf.shape
    width = w_b.shape[1]
    tm, tn = 1024, width // 4
    per_b = seq // tm
    return pl.pallas_call(
        _inproj_body,
        grid=(n // tm, width // tn),
        in_specs=[
            pl.BlockSpec((tm, d), lambda i, j: (i, 0)),
            pl.BlockSpec((1, d), lambda i, j: (0, 0)),
            pl.BlockSpec((1, N_MOD, d), lambda i, j: (i // per_b, 0, 0)),
            pl.BlockSpec((d, tn), lambda i, j: (0, j)),
        ],
        out_specs=pl.BlockSpec((tm, tn), lambda i, j: (i, j)),
        out_shape=jax.ShapeDtypeStruct((n, width), BF16),
        scratch_shapes=[pltpu.VMEM((tm, d), BF16)],
        compiler_params=_cparams(("arbitrary", "arbitrary")),
        name="inproj",
    )(xf, g_pre.reshape(1, d), mod3, w_b)


def _inproj_dil_body(x_ref, g_ref, mod_ref, w_ref, t_ref, o0, o1, o2, res_ref):
    tm = x_ref.shape[0]
    h = _prenorm(x_ref, g_ref, mod_ref)
    t = t_ref[...]
    gw = 3 * D_DIL_OUT
    for gi, o_ref in enumerate((o0, o1, o2)):
        dil = DIL_PATTERNS[gi][1]
        res = jnp.dot(h, w_ref[:, gi * gw:(gi + 1) * gw], preferred_element_type=F32)
        for hs in range(3 * HEADS_PER_GROUP):
            sl = slice(hs * HEAD_DIM, (hs + 1) * HEAD_DIM)
            res_ref[hs] = _apply_rope(res[:, sl], t) if hs < 2 * HEADS_PER_GROUP else res[:, sl]
        for r in range(dil):
            for hs in range(3 * HEADS_PER_GROUP):
                rows = res_ref[hs] if dil == 1 else res_ref[hs, pl.ds(r, tm // dil, stride=dil), :]
                o_ref[0, r, :, hs * HEAD_DIM:(hs + 1) * HEAD_DIM] = rows.astype(o_ref.dtype)


def _inproj_dil(xf, g_pre, mod3, w_b, tables, bsz, seq):
    n, d = xf.shape
    gw = 3 * D_DIL_OUT
    tm = 512
    per_b = seq // tm
    dils = [p[1] for p in DIL_PATTERNS]
    return pl.pallas_call(
        _inproj_dil_body,
        grid=(n // tm,),
        in_specs=[
            pl.BlockSpec((tm, d), lambda i: (i, 0)),
            pl.BlockSpec((1, d), lambda i: (0, 0)),
            pl.BlockSpec((1, N_MOD, d), lambda i: (i // per_b, 0, 0)),
            pl.BlockSpec(w_b.shape, lambda i: (0, 0), pipeline_mode=pl.Buffered(1)),
            pl.BlockSpec((tm, 3 * LANES), lambda i: (i, 0)),
        ],
        out_specs=[pl.BlockSpec((1, dl, tm // dl, gw), lambda i: (i // per_b, 0, i % per_b, 0)) for dl in dils],
        out_shape=[jax.ShapeDtypeStruct((bsz, dl, seq // dl, gw), BF16) for dl in dils],
        scratch_shapes=[pltpu.VMEM((gw // HEAD_DIM, tm, HEAD_DIM), F32)],
        compiler_params=_cparams(("arbitrary",)),
        name="inproj_dil",
    )(xf, g_pre.reshape(1, d), mod3, w_b, tables)


def _rope_body(pos_ref, f_ref, o_ref):
    ang = pos_ref[...].astype(F32) * f_ref[...]
    c = jnp.cos(ang)
    s = jnp.sin(ang)
    lane = lax.broadcasted_iota(I32, ang.shape, 1)
    half = ROPE_DIM // 2
    o_ref[:, 0:LANES] = c
    o_ref[:, LANES:2 * LANES] = jnp.where(lane >= half, s, 0.0)
    o_ref[:, 2 * LANES:3 * LANES] = jnp.where(lane < half, -s, 0.0)


def _rope_tables(positions):
    n = positions.size
    half = ROPE_DIM // 2
    inv_freq = ROPE_THETA ** (-jnp.arange(0, ROPE_DIM, 2, dtype=F32) / ROPE_DIM)
    f = jnp.concatenate([inv_freq, inv_freq, jnp.zeros((LANES - 2 * half,), F32)]).reshape(1, LANES)
    tm = 2048
    return pl.pallas_call(
        _rope_body,
        grid=(n // tm,),
        in_specs=[pl.BlockSpec((tm, 1), lambda i: (i, 0)), pl.BlockSpec((1, LANES), lambda i: (0, 0))],
        out_specs=pl.BlockSpec((tm, 3 * LANES), lambda i: (i, 0)),
        out_shape=jax.ShapeDtypeStruct((n, 3 * LANES), F32),
        compiler_params=_cparams(("arbitrary",)),
        name="rope_tables",
    )(positions.reshape(n, 1), f)


def _apply_rope(x, t):
    half = ROPE_DIM // 2
    return (x * t[:, 0:LANES]
            + pltpu.roll(x, half, 1) * t[:, LANES:2 * LANES]
            + pltpu.roll(x, LANES - half, 1) * t[:, 2 * LANES:3 * LANES])


def _dil_body(cur_ref, kp_ref, vp_ref, o_ref, lse_ref, obuf, lbuf, *, dil, nsub):
    n = pl.program_id(1)
    tq = nsub * Q_BLOCK
    row = lax.broadcasted_iota(I32, (Q_BLOCK, 2 * Q_BLOCK), 0)
    col = lax.broadcasted_iota(I32, (Q_BLOCK, 2 * Q_BLOCK), 1)
    rel = row + Q_BLOCK - col
    band = jnp.where(rel >= 0, jnp.where(rel <= Q_BLOCK, 1.0, 0.0), 0.0)
    first = jnp.where(col >= Q_BLOCK, band, jnp.where(n > 0, band, 0.0))
    for r in range(dil):
        for h in range(HEADS_PER_GROUP):
            sl = slice(h * HEAD_DIM, (h + 1) * HEAD_DIM)
            ksl = slice(D_DIL_OUT + h * HEAD_DIM, D_DIL_OUT + (h + 1) * HEAD_DIM)
            vsl = slice(2 * D_DIL_OUT + h * HEAD_DIM, 2 * D_DIL_OUT + (h + 1) * HEAD_DIM)
            for j in range(nsub):
                rs = slice(j * Q_BLOCK, (j + 1) * Q_BLOCK)
                ps = slice((j - 1) * Q_BLOCK, j * Q_BLOCK)
                kprev = kp_ref[0, r, :, sl] if j == 0 else cur_ref[0, r, ps, ksl]
                vprev = vp_ref[0, r, :, sl] if j == 0 else cur_ref[0, r, ps, vsl]
                kcat = jnp.concatenate([kprev, cur_ref[0, r, rs, ksl]], axis=0)
                vcat = jnp.concatenate([vprev, cur_ref[0, r, rs, vsl]], axis=0)
                s = lax.dot_general(cur_ref[0, r, rs, sl], kcat, (((1,), (1,)), ((), ())),
                                    preferred_element_type=F32) * ATTN_SCALE
                s = jnp.where((first if j == 0 else band) > 0.0, s, -jnp.inf)
                m = jnp.max(s, axis=-1, keepdims=True)
                p = jnp.exp(s - m)
                l = jnp.sum(p, axis=-1, keepdims=True)
                o = jnp.dot((p / l).astype(BF16), vcat, preferred_element_type=F32)
                lse = jnp.broadcast_to(m + jnp.log(l), (Q_BLOCK, HEAD_DIM))
                if dil == 1:
                    o_ref[h, rs, :] = o
                    lse_ref[h, rs, :] = lse
                else:
                    obuf[h, rs, :] = o
                    lbuf[h, rs, :] = lse
            if dil > 1:
                o_ref[h, pl.ds(r, tq, stride=dil), :] = obuf[h]
                lse_ref[h, pl.ds(r, tq, stride=dil), :] = lbuf[h]


def _dilated_attention(qkv, g, bsz, seq):
    dil = DIL_PATTERNS[g][1]
    length = seq // dil
    tq = min(4 * Q_BLOCK, (16 * Q_BLOCK) // dil, length)
    nsub = tq // Q_BLOCK
    nq = length // tq
    gw = 3 * D_DIL_OUT
    n = bsz * seq

    def prev(colblk):
        return pl.BlockSpec((1, dil, Q_BLOCK, D_DIL_OUT),
                            lambda b, i: (b, 0, jnp.maximum(i * nsub - 1, 0), colblk))

    nh = HEADS_PER_GROUP
    out_spec = pl.BlockSpec((nh, tq * dil, HEAD_DIM), lambda b, i: (0, b * nq + i, 0))
    out_shape = jax.ShapeDtypeStruct((nh, n, HEAD_DIM), F32)
    return pl.pallas_call(
        functools.partial(_dil_body, dil=dil, nsub=nsub),
        grid=(bsz, nq),
        in_specs=[pl.BlockSpec((1, dil, tq, gw), lambda b, i: (b, 0, i, 0)), prev(1), prev(2)],
        out_specs=[out_spec, out_spec],
        out_shape=[out_shape, out_shape],
        scratch_shapes=[pltpu.VMEM((nh, tq, HEAD_DIM), F32), pltpu.VMEM((nh, tq, HEAD_DIM), F32)],
        compiler_params=_cparams(("arbitrary", "arbitrary")),
        name=f"dilated_d{dil}",
    )(qkv, qkv, qkv)


def _sb_body(q_ref, k_ref, v_ref, o_ref, acc_ref, car_ref):
    nblk = q_ref.shape[1] // Q_BLOCK
    r = lax.broadcasted_iota(I32, (Q_BLOCK, Q_BLOCK), 0)
    c = lax.broadcasted_iota(I32, (Q_BLOCK, Q_BLOCK), 1)
    causal = c < r
    rr = lax.broadcasted_iota(I32, (Q_BLOCK, 2 * Q_BLOCK), 0)
    cc = lax.broadcasted_iota(I32, (Q_BLOCK, 2 * Q_BLOCK), 1)
    uo = jnp.where(cc >= Q_BLOCK, 1.0, jnp.where(rr > cc, 1.0, 0.0)).astype(BF16)

    nh = acc_ref.shape[0]
    heads = range(nh)

    def tiles(qs, kb, carries, diag):
        k0 = pl.multiple_of(kb * Q_BLOCK, Q_BLOCK)
        hs = [slice(h * HEAD_DIM, (h + 1) * HEAD_DIM) for h in heads]
        zs = [lax.dot_general(qs[h], k_ref[0, pl.ds(k0, Q_BLOCK), hs[h]], (((1,), (1,)), ((), ())),
                              preferred_element_type=F32) * ATTN_SCALE for h in heads]
        stacked, log_s = [], []
        for z in zs:
            sp = jnp.log(1.0 + jnp.exp(-jnp.abs(z)))
            mx = jnp.maximum(z, 0.0)
            log_1m = -(mx + sp)
            if diag:
                log_1m = jnp.where(causal, log_1m, 0.0)
            hi = log_1m.astype(BF16)
            lo = (log_1m - hi.astype(F32)).astype(BF16)
            stacked.append(jnp.concatenate([hi, lo], axis=0))
            log_s.append((z - mx) - sp)
        r2s = [jnp.dot(s, uo, preferred_element_type=F32) for s in stacked]
        probs, new_carries = [], []
        for h in heads:
            sums = r2s[h][:Q_BLOCK] + r2s[h][Q_BLOCK:]
            a = jnp.exp(log_s[h] + carries[h] + sums[:, :Q_BLOCK])
            if diag:
                a = jnp.where(causal, a, 0.0)
            probs.append(a.astype(BF16))
            new_carries.append(carries[h] + sums[:, Q_BLOCK:])
        pvs = [jnp.dot(probs[h], v_ref[0, pl.ds(k0, Q_BLOCK), hs[h]], preferred_element_type=F32) for h in heads]
        return pvs, new_carries

    def all_max(xs):
        m = xs[0]
        for x in xs[1:]:
            m = jnp.maximum(m, x)
        return jnp.max(m)

    def qblock(qi, _):
        q0 = pl.multiple_of(qi * Q_BLOCK, Q_BLOCK)
        qs = [q_ref[0, pl.ds(q0, Q_BLOCK), h * HEAD_DIM:(h + 1) * HEAD_DIM] for h in heads]
        zero = jnp.zeros((Q_BLOCK, Q_BLOCK), F32)
        pvs, cars = tiles(qs, qi, [zero] * nh, True)
        for h in heads:
            acc_ref[h] = pvs[h]
            car_ref[h] = cars[h]

        def cond(st):
            return jnp.logical_and(st[0] >= 0, st[1] > SB_DEAD)

        def body(st):
            pvs, cars = tiles(qs, st[0], [car_ref[h] for h in heads], False)
            for h in heads:
                acc_ref[h] += pvs[h]
                car_ref[h] = cars[h]
            return st[0] - 1, all_max(cars)

        lax.while_loop(cond, body, (qi - 1, all_max(cars)))
        for h in range(nh):
            o_ref[0, pl.ds(q0, Q_BLOCK), h * HEAD_DIM:(h + 1) * HEAD_DIM] = acc_ref[h].astype(o_ref.dtype)
        return 0

    lax.fori_loop(0, nblk, qblock, 0)


SB_HEADS_PER_STEP = 8


def _stick_breaking(proj, bsz, seq):
    width = proj.shape[1]
    pv = proj.reshape(bsz, seq, width)
    nh = SB_HEADS_PER_STEP
    bw = nh * HEAD_DIM
    base = N_GATE // bw

    def spec(off):
        return pl.BlockSpec((1, seq, bw), lambda b, h: (b, 0, base + off + h), pipeline_mode=pl.Buffered(1))

    nstep = N_HEADS_SB // nh
    o = pl.pallas_call(
        _sb_body,
        grid=(bsz, nstep),
        in_specs=[spec(0), spec(nstep), spec(2 * nstep)],
        out_specs=pl.BlockSpec((1, seq, bw), lambda b, h: (b, 0, h)),
        out_shape=jax.ShapeDtypeStruct((bsz, seq, WIDTH_SB), BF16),
        scratch_shapes=[pltpu.VMEM((nh, Q_BLOCK, Q_BLOCK), F32), pltpu.VMEM((nh, Q_BLOCK, Q_BLOCK), F32)],
        compiler_params=_cparams(("arbitrary", "arbitrary")),
        name="stick_breaking",
    )(pv, pv, pv)
    return o.reshape(bsz * seq, WIDTH_SB)


def _mixout_body(o1, o2, o3, l1, l2, l3, osb, gd_ref, gs_ref, x_ref, mod_ref, gpost, gpre,
                 wd, ws, wm, x1_ref, h2_ref):
    heads = []
    for h in range(HEADS_PER_GROUP):
        la, lb, lc = l1[h], l2[h], l3[h]
        m = jnp.maximum(la, jnp.maximum(lb, lc))
        ea, eb, ec = jnp.exp(la - m), jnp.exp(lb - m), jnp.exp(lc - m)
        heads.append(((ea * o1[h] + eb * o2[h] + ec * o3[h]) / (ea + eb + ec)).astype(BF16))
    yd = jnp.dot(jnp.concatenate(heads, axis=1), wd[...], preferred_element_type=F32)
    ys = jnp.dot(osb[...], ws[...], preferred_element_type=F32)
    mix = jax.nn.sigmoid(gd_ref[...].astype(F32)) * yd + jax.nn.sigmoid(gs_ref[...].astype(F32)) * ys
    y = jnp.dot(mix.astype(BF16), wm[...], preferred_element_type=F32)
    gate1 = mod_ref[0, 2:3, :]
    shift2 = mod_ref[0, 3:4, :]
    scale2 = mod_ref[0, 4:5, :]
    x1 = x_ref[...] + gate1 * (_rms(y) * gpost[...])
    x1_ref[...] = x1
    h2_ref[...] = ((_rms(x1) * gpre[...]) * (1.0 + scale2) + shift2).astype(h2_ref.dtype)


def _const_spec(shape):
    return pl.BlockSpec(shape, lambda i: (0,) * len(shape), pipeline_mode=pl.Buffered(1))


def _mixout(o_dil, lse_dil, o_sb, proj, xf, mod3, g_post, g_pre, wd_b, ws_b, wm_b, seq):
    n, d = xf.shape
    tm = 256
    per_b = seq // tm
    row = lambda w: pl.BlockSpec((tm, w), lambda i: (i, 0))
    head_major = pl.BlockSpec((HEADS_PER_GROUP, tm, HEAD_DIM), lambda i: (0, i, 0))
    in_specs = (
        [head_major] * 6 + [row(WIDTH_SB)]
        + [pl.BlockSpec((tm, d), lambda i: (i, 0)), pl.BlockSpec((tm, d), lambda i: (i, 1))]
        + [row(d), pl.BlockSpec((1, N_MOD, d), lambda i: (i // per_b, 0, 0))]
        + [_const_spec((1, d)), _const_spec((1, d))]
        + [_const_spec(wd_b.shape), _const_spec(ws_b.shape), _const_spec(wm_b.shape)]
    )
    return pl.pallas_call(
        _mixout_body,
        grid=(n // tm,),
        in_specs=in_specs,
        out_specs=[row(d), row(d)],
        out_shape=[jax.ShapeDtypeStruct((n, d), F32), jax.ShapeDtypeStruct((n, d), BF16)],
        compiler_params=_cparams(("arbitrary",)),
        name="mixout",
    )(*o_dil, *lse_dil, o_sb, proj, proj, xf, mod3, g_post.reshape(1, d), g_pre.reshape(1, d), wd_b, ws_b, wm_b)


def _topk_rows(x, k, iota0):
    big = x.shape[0]
    out = []
    for _ in range(k):
        m = jnp.max(x, axis=0, keepdims=True)
        i = jnp.min(jnp.where(x == m, iota0, big), axis=0, keepdims=True)
        out.append((m, i))
        x = jnp.where(iota0 == i, -jnp.inf, x)
    return out


def _router_body(h_ref, wr_ref, bias_ref, idx_ref, wt_ref):
    tm = h_ref.shape[0]
    logits = lax.dot_general(wr_ref[...], h_ref[...], (((1,), (1,)), ((), ())), preferred_element_type=F32)
    scores = jax.nn.sigmoid(logits)
    sel = scores + bias_ref[...]
    sub = lax.broadcasted_iota(I32, (GROUP_SIZE, tm), 0)
    grp = []
    for g in range(N_GROUPS):
        (m1, _), (m2, _) = _topk_rows(sel[g * GROUP_SIZE:(g + 1) * GROUP_SIZE], 2, sub)
        grp.append(m1 + m2)
    gscore = jnp.concatenate(grp, axis=0)
    giota = lax.broadcasted_iota(I32, (N_GROUPS, tm), 0)
    gmask = jnp.zeros((N_GROUPS, tm), F32)
    for _, gi in _topk_rows(gscore, TOPK_GROUPS, giota):
        gmask = jnp.where(giota == gi, 1.0, gmask)
    masked = jnp.concatenate(
        [jnp.where(gmask[g:g + 1] > 0.0, sel[g * GROUP_SIZE:(g + 1) * GROUP_SIZE], -jnp.inf)
         for g in range(N_GROUPS)], axis=0)
    eiota = lax.broadcasted_iota(I32, (N_EXPERTS, tm), 0)
    picks = _topk_rows(masked, TOP_K, eiota)
    idx = jnp.concatenate([i for _, i in picks], axis=0)
    top_s = jnp.concatenate(
        [jnp.sum(jnp.where(eiota == i, scores, 0.0), axis=0, keepdims=True) for _, i in picks], axis=0)
    top_w = top_s / jnp.sum(top_s, axis=0, keepdims=True) * ROUTED_SCALE
    idx_ref[...] = idx
    wpad = jnp.concatenate([top_w, jnp.zeros((LANES - TOP_K, tm), F32)], axis=0)
    wt_ref[...] = wpad.T


def _router(h2, w_router, router_bias):
    n, d = h2.shape
    tm = 512
    return pl.pallas_call(
        _router_body,
        grid=(n // tm,),
        in_specs=[
            pl.BlockSpec((tm, d), lambda i: (i, 0)),
            pl.BlockSpec((N_EXPERTS, d), lambda i: (0, 0)),
            pl.BlockSpec((N_EXPERTS, 1), lambda i: (0, 0)),
        ],
        out_specs=[pl.BlockSpec((TOP_K, tm), lambda i: (0, i)), pl.BlockSpec((tm, LANES), lambda i: (i, 0))],
        out_shape=[jax.ShapeDtypeStruct((TOP_K, n), I32), jax.ShapeDtypeStruct((n, LANES), F32)],
        compiler_params=_cparams(("arbitrary",)),
        name="router",
    )(h2, w_router.T.astype(BF16), router_bias.reshape(N_EXPERTS, 1))


def _plan_tile(idx):
    tm = idx.shape[1]
    eiota = lax.broadcasted_iota(I32, (N_EXPERTS, tm), 0)
    hit = jnp.zeros((N_EXPERTS, tm), F32)
    for k in range(TOP_K):
        hit = jnp.where(eiota == idx[k:k + 1], 1.0, hit)
    r = lax.broadcasted_iota(I32, (tm, tm), 0)
    c = lax.broadcasted_iota(I32, (tm, tm), 1)
    before = jnp.where(r < c, 1.0, 0.0).astype(BF16)
    excl = jnp.dot(hit.astype(BF16), before, preferred_element_type=F32)
    tot = excl[:, tm - 1:tm] + hit[:, tm - 1:tm]
    return eiota, hit, excl, tot


def _plan_counts_body(idx_ref, cnt_ref):
    @pl.when(pl.program_id(0) == 0)
    def _():
        cnt_ref[...] = jnp.zeros(cnt_ref.shape, F32)

    _, _, _, tot = _plan_tile(idx_ref[...])
    cnt_ref[...] += tot


def _plan_counts(top_idx):
    n = top_idx.shape[1]
    tm = 512
    return pl.pallas_call(
        _plan_counts_body,
        grid=(n // tm,),
        in_specs=[pl.BlockSpec((TOP_K, tm), lambda i: (0, i))],
        out_specs=pl.BlockSpec((N_EXPERTS, LANES), lambda i: (0, 0)),
        out_shape=jax.ShapeDtypeStruct((N_EXPERTS, LANES), F32),
        compiler_params=_cparams(("arbitrary",)),
        name="plan_counts",
    )(top_idx)


def _plan_dest_body(idx_ref, base_ref, dest_ref, run_ref):
    @pl.when(pl.program_id(0) == 0)
    def _():
        run_ref[...] = base_ref[...]

    idx = idx_ref[...]
    eiota, _, excl, tot = _plan_tile(idx)
    pos = run_ref[:, 0:1] + excl
    rows = [jnp.sum(jnp.where(eiota == idx[k:k + 1], pos, 0.0), axis=0, keepdims=True) for k in range(TOP_K)]
    dest_ref[...] = jnp.concatenate(rows, axis=0).astype(I32)
    run_ref[...] += tot


def _plan_dest(top_idx, pstart):
    n = top_idx.shape[1]
    tm = 512
    base = jnp.broadcast_to(pstart.astype(F32).reshape(N_EXPERTS, 1), (N_EXPERTS, LANES))
    return pl.pallas_call(
        _plan_dest_body,
        grid=(n // tm,),
        in_specs=[pl.BlockSpec((TOP_K, tm), lambda i: (0, i)), pl.BlockSpec((N_EXPERTS, LANES), lambda i: (0, 0))],
        out_specs=pl.BlockSpec((TOP_K, tm), lambda i: (0, i)),
        out_shape=jax.ShapeDtypeStruct((TOP_K, n), I32),
        scratch_shapes=[pltpu.VMEM((N_EXPERTS, LANES), F32)],
        compiler_params=_cparams(("arbitrary",)),
        name="plan_dest",
    )(top_idx, base)


def _capacity(n):
    return (pl.cdiv(n * TOP_K, ROW_BLOCK) + N_EXPERTS) * ROW_BLOCK


def _pack_halves(y):
    w = y.shape[1] // 2
    hi = pltpu.bitcast(y[:, :w].astype(BF16).astype(F32), U32)
    lo = pltpu.bitcast(y[:, w:].astype(BF16).astype(F32), U32)
    return hi | (lo >> 16)


def _unpack_halves(p):
    a = pltpu.bitcast(p & jnp.uint32(0xFFFF0000), F32)
    b = pltpu.bitcast(p << 16, F32)
    return a, b


TOKENS_PER_ISSUE = 4


def _row_dma_loops(row_copy):
    def issue(i, _):
        for u in range(TOKENS_PER_ISSUE):
            for k in range(TOP_K):
                row_copy(i * TOKENS_PER_ISSUE + u, k).start(priority=k % 2)
        return 0

    def drain(i, _):
        for u in range(TOKENS_PER_ISSUE):
            for k in range(TOP_K):
                row_copy(i * TOKENS_PER_ISSUE + u, k).wait()
        return 0

    return issue, drain


def _pad_chunks():
    s = 1
    while 2 * s < ROW_BLOCK:
        s *= 2
    sizes = []
    while s >= 1:
        sizes.append(s)
        s //= 2
    return sizes


def _tile_rows(t):
    start = t * SUBLANES
    return pl.ds(start if isinstance(start, int) else pl.multiple_of(start, SUBLANES), SUBLANES)


def _to_row_tiles(ref, packed, lead=()):
    rows = packed.shape[0]
    for s in range(SUBLANES):
        ref[(*lead, pl.ds(s, rows, stride=SUBLANES), slice(None))] = packed[:, s * LANES:(s + 1) * LANES]


def _from_row_tiles(ref, rows, lead=()):
    return [ref[(*lead, pl.ds(s, rows, stride=SUBLANES), slice(None))] for s in range(SUBLANES)]


def _dispatch_body(pstart_ref, cnt_ref, dest_ref, h_ref, xs_ref, xbuf, zbuf, sem, zsem):
    tm = h_ref.shape[0]
    step = pl.program_id(0)

    def pad_dmas(wait):
        def per_expert(e, _):
            cnt = cnt_ref[e]
            pad = (ROW_BLOCK - cnt % ROW_BLOCK) % ROW_BLOCK
            start = pstart_ref[e] + cnt
            for size in _pad_chunks():
                @pl.when((pad & size) != 0)
                def _():
                    off = pl.multiple_of((start + (pad & ~(2 * size - 1))) * SUBLANES, SUBLANES)
                    cp = pltpu.make_async_copy(zbuf.at[pl.ds(0, size * SUBLANES)],
                                               xs_ref.at[pl.ds(off, size * SUBLANES)], zsem)
                    cp.wait() if wait else cp.start()
            return 0
        lax.fori_loop(0, N_EXPERTS, per_expert, 0)

    @pl.when(step == 0)
    def _():
        zbuf[...] = jnp.zeros(zbuf.shape, zbuf.dtype)
        pad_dmas(False)
        pad_dmas(True)

    _to_row_tiles(xbuf, _pack_halves(h_ref[...].astype(F32)))

    def row_copy(t, k):
        return pltpu.make_async_copy(xbuf.at[_tile_rows(t)], xs_ref.at[_tile_rows(dest_ref[k, t])], sem)

    issue, drain = _row_dma_loops(row_copy)
    lax.fori_loop(0, tm // TOKENS_PER_ISSUE, issue, 0)
    lax.fori_loop(0, tm // TOKENS_PER_ISSUE, drain, 0)


def _dispatch(h2, dest, pstart, counts):
    n, d = h2.shape
    tm = 256
    assert d // 2 == SUBLANES * LANES
    grid_spec = pltpu.PrefetchScalarGridSpec(
        num_scalar_prefetch=2,
        grid=(n // tm,),
        in_specs=[
            pl.BlockSpec((TOP_K, tm), lambda i, *_: (0, i), memory_space=pltpu.SMEM),
            pl.BlockSpec((tm, d), lambda i, *_: (i, 0)),
        ],
        out_specs=pl.BlockSpec(memory_space=pl.ANY),
        scratch_shapes=[
            pltpu.VMEM((tm * SUBLANES, LANES), U32),
            pltpu.VMEM((_pad_chunks()[0] * SUBLANES, LANES), U32),
            pltpu.SemaphoreType.DMA(()),
            pltpu.SemaphoreType.DMA(()),
        ],
    )
    return pl.pallas_call(
        _dispatch_body,
        grid_spec=grid_spec,
        out_shape=jax.ShapeDtypeStruct((_capacity(n) * SUBLANES, LANES), U32),
        compiler_params=_cparams(("arbitrary",), has_side_effects=True, disable_bounds_checks=True),
        name="dispatch",
    )(pstart, counts, dest, h2)


def _experts_body(be_ref, nu_ref, nxt_ref, ord_ref, x_ref, wg_hbm, wu_hbm, wd_hbm, y_ref,
                  wg_s, wu_s, wd_s, wg_b, wu_b, wd_b, sems):
    i = pl.program_id(0)
    used = i < nu_ref[0]
    e = be_ref[i]
    fresh = jnp.logical_or(i == 0, e != be_ref[jnp.maximum(i - 1, 0)])
    slot = ord_ref[e] % 2

    def weight_copies(expert, s):
        return (pltpu.make_async_copy(wg_hbm.at[expert], wg_s.at[s], sems.at[s, 0]),
                pltpu.make_async_copy(wu_hbm.at[expert], wu_s.at[s], sems.at[s, 1]),
                pltpu.make_async_copy(wd_hbm.at[expert], wd_s.at[s], sems.at[s, 2]))

    @pl.when(i == 0)
    def _():
        for cp in weight_copies(e, slot):
            cp.start()

    @pl.when(jnp.logical_and(used, fresh))
    def _():
        copies = weight_copies(e, slot)
        for cp, dst, src in zip(copies, (wg_b, wu_b, wd_b), (wg_s, wu_s, wd_s)):
            cp.wait()
            dst[...] = src[slot].astype(BF16)

        @pl.when(nxt_ref[e] != e)
        def _():
            for cp in weight_copies(nxt_ref[e], 1 - slot):
                cp.start()

    @pl.when(used)
    def _():
        half = SUBLANES * LANES
        slabs = [_unpack_halves(p) for p in _from_row_tiles(x_ref, ROW_BLOCK)]
        xa = jnp.concatenate([a.astype(BF16) for a, _ in slabs], axis=1)
        xb = jnp.concatenate([b.astype(BF16) for _, b in slabs], axis=1)
        g = (jnp.dot(xa, wg_b[:half], preferred_element_type=F32)
             + jnp.dot(xb, wg_b[half:], preferred_element_type=F32))
        u = (jnp.dot(xa, wu_b[:half], preferred_element_type=F32)
             + jnp.dot(xb, wu_b[half:], preferred_element_type=F32))
        hmid = (g * jax.nn.sigmoid(g) * u).astype(BF16)
        _to_row_tiles(y_ref, _pack_halves(jnp.dot(hmid, wd_b[...], preferred_element_type=F32)))


def _experts(xs, block_e, n_used, next_e, ord_e, w_gate_e, w_up_e, w_down_e):
    cap = xs.shape[0] // SUBLANES
    d = 2 * SUBLANES * LANES
    nblk = cap // ROW_BLOCK
    row_block = (ROW_BLOCK * SUBLANES, LANES)

    def row_map(i, be, nu, nxt, od):
        return (jnp.minimum(i, nu[0] - 1), 0)

    grid_spec = pltpu.PrefetchScalarGridSpec(
        num_scalar_prefetch=4,
        grid=(nblk,),
        in_specs=[
            pl.BlockSpec(row_block, row_map),
            pl.BlockSpec(memory_space=pl.ANY),
            pl.BlockSpec(memory_space=pl.ANY),
            pl.BlockSpec(memory_space=pl.ANY),
        ],
        out_specs=pl.BlockSpec(row_block, row_map),
        scratch_shapes=[
            pltpu.VMEM((2, d, D_EXPERT), F32),
            pltpu.VMEM((2, d, D_EXPERT), F32),
            pltpu.VMEM((2, D_EXPERT, d), F32),
            pltpu.VMEM((d, D_EXPERT), BF16),
            pltpu.VMEM((d, D_EXPERT), BF16),
            pltpu.VMEM((D_EXPERT, d), BF16),
            pltpu.SemaphoreType.DMA((2, 3)),
        ],
    )
    return pl.pallas_call(
        _experts_body,
        grid_spec=grid_spec,
        out_shape=jax.ShapeDtypeStruct(xs.shape, U32),
        compiler_params=_cparams(("arbitrary",)),
        name="experts",
    )(block_e, n_used, next_e, ord_e, xs, w_gate_e, w_up_e, w_down_e)


COMBINE_TILE = 256


def _combine_body(dest_ref, dest_next_ref, ys_ref, wt_ref, h_ref, x1_ref, mod_ref, gpost, wg, wu, wd, o_ref,
                  buf0, buf1, sem0, sem1):
    tile = COMBINE_TILE
    i = pl.program_id(0)
    bufs, sems = (buf0, buf1), (sem0, sem1)
    half = SUBLANES * LANES
    chunk = tile // SUBLANES

    def copies(dref, col0, slot):
        def row_copy(t, k):
            return pltpu.make_async_copy(ys_ref.at[_tile_rows(dref[k, col0 + t])],
                                         bufs[slot].at[k, _tile_rows(t)], sems[slot])
        return row_copy

    def run(loop_body):
        lax.fori_loop(0, tile // TOKENS_PER_ISSUE, loop_body, 0)

    @pl.when(i == 0)
    def _():
        run(_row_dma_loops(copies(dest_ref, 0, 0))[0])

    def phase(row0, slot, next_copy):
        run(_row_dma_loops(copies(dest_ref, row0, slot))[1])
        rows = slice(row0, row0 + tile)
        h = h_ref[rows, :]
        g = jnp.dot(h, wg[...], preferred_element_type=F32)
        u = jnp.dot(h, wu[...], preferred_element_type=F32)
        shared = jnp.dot((g * jax.nn.sigmoid(g) * u).astype(BF16), wd[...], preferred_element_type=F32)
        wt = wt_ref[rows, :]
        wk = [jnp.broadcast_to(wt[:, k:k + 1], (tile, LANES)) for k in range(TOP_K)]
        ya, yb = [], []
        for s in range(SUBLANES):
            for t in range(s * chunk, (s + 1) * chunk):
                for k in range(TOP_K):
                    next_copy(t, k).start(priority=k % 2)
            sa = shared[:, s * LANES:(s + 1) * LANES]
            sb = shared[:, half + s * LANES:half + (s + 1) * LANES]
            for k in range(TOP_K):
                a, b = _unpack_halves(bufs[slot][k, pl.ds(s, tile, stride=SUBLANES), :])
                sa = sa + wk[k] * a
                sb = sb + wk[k] * b
            ya.append(sa)
            yb.append(sb)
        y = jnp.concatenate(ya + yb, axis=1)
        gate2 = mod_ref[0, 5:6, :]
        o_ref[rows, :] = x1_ref[rows, :] + gate2 * (_rms(y) * gpost[...])

    phase(0, 0, copies(dest_ref, tile, 1))
    phase(tile, 1, copies(dest_next_ref, 0, 0))

    @pl.when(i == pl.num_programs(0) - 1)
    def _():
        run(_row_dma_loops(copies(dest_next_ref, 0, 0))[1])


def _combine(ys, dest, wt, h2, x1, mod3, g_post, wg_b, wu_b, wd_b, seq):
    n, d = h2.shape
    tm = 2 * COMBINE_TILE
    nstep = n // tm
    per_b = seq // tm
    const = lambda shape: pl.BlockSpec(shape, lambda i, *_: (0,) * len(shape), pipeline_mode=pl.Buffered(1))
    buf = pltpu.VMEM((TOP_K, COMBINE_TILE * SUBLANES, LANES), U32)
    grid_spec = pltpu.PrefetchScalarGridSpec(
        num_scalar_prefetch=0,
        grid=(nstep,),
        in_specs=[
            pl.BlockSpec((TOP_K, tm), lambda i: (0, i), memory_space=pltpu.SMEM),
            pl.BlockSpec((TOP_K, tm), lambda i: (0, jnp.minimum(i + 1, nstep - 1)), memory_space=pltpu.SMEM),
            pl.BlockSpec(memory_space=pl.ANY),
            pl.BlockSpec((tm, LANES), lambda i: (i, 0)),
            pl.BlockSpec((tm, d), lambda i: (i, 0)),
            pl.BlockSpec((tm, d), lambda i: (i, 0)),
            pl.BlockSpec((1, N_MOD, d), lambda i: (i // per_b, 0, 0)),
            const((1, d)), const(wg_b.shape), const(wu_b.shape), const(wd_b.shape),
        ],
        out_specs=pl.BlockSpec((tm, d), lambda i: (i, 0)),
        scratch_shapes=[buf, buf, pltpu.SemaphoreType.DMA(()), pltpu.SemaphoreType.DMA(())],
    )
    return pl.pallas_call(
        _combine_body,
        grid_spec=grid_spec,
        out_shape=jax.ShapeDtypeStruct((n, d), F32),
        compiler_params=_cparams(("arbitrary",), disable_bounds_checks=True),
        name="combine",
    )(dest, dest, ys, wt, h2, x1, mod3, g_post.reshape(1, d), wg_b, wu_b, wd_b)


def _layer(x, mod, positions, g_pre_mix, g_post_mix, g_pre_ffn, g_post_ffn, w_in, w_dil_out, w_sb_out, w_mix_out,
           w_router, router_bias, w_gate_e, w_up_e, w_down_e, w_gate_s, w_up_s, w_down_s):
    bsz, seq, d = x.shape
    n = bsz * seq
    xf = x.reshape(n, d)
    mod3 = mod.reshape(bsz, N_MOD, d)

    nd, nq = 3 * WIDTH_DIL, 3 * (WIDTH_DIL + WIDTH_SB)
    w_plain = jnp.concatenate([w_in[:, nq:], w_in[:, nd:nq]], axis=1).astype(BF16)
    cols = []
    for g in range(len(DIL_PATTERNS)):
        for part in range(3):
            lo = part * WIDTH_DIL + g * D_DIL_OUT
            cols.append(w_in[:, lo:lo + D_DIL_OUT])
    w_dil = jnp.concatenate(cols, axis=1).astype(BF16)

    proj = _inproj(xf, g_pre_mix, mod3, w_plain, seq)
    tables = _rope_tables(positions)
    qkv_dil = _inproj_dil(xf, g_pre_mix, mod3, w_dil, tables, bsz, seq)
    o_dil, lse_dil = [], []
    for g, (window, dilation) in enumerate(DIL_PATTERNS):
        assert window // dilation == Q_BLOCK
        o, lse = _dilated_attention(qkv_dil[g], g, bsz, seq)
        o_dil.append(o)
        lse_dil.append(lse)
    o_sb = _stick_breaking(proj, bsz, seq)

    x1, h2 = _mixout(o_dil, lse_dil, o_sb, proj, xf, mod3, g_post_mix, g_pre_ffn,
                     w_dil_out.astype(BF16), w_sb_out.astype(BF16), w_mix_out.astype(BF16), seq)

    top_idx, top_wt = _router(h2, w_router, router_bias)
    counts = _plan_counts(top_idx)[:, 0].astype(I32)
    padded = (counts + ROW_BLOCK - 1) // ROW_BLOCK * ROW_BLOCK
    pends = jnp.cumsum(padded)
    pstart = pends - padded
    nblk = _capacity(n) // ROW_BLOCK
    n_used = (pends[-1] // ROW_BLOCK).astype(I32)
    blk = jnp.minimum(jnp.arange(nblk, dtype=I32), n_used - 1)
    block_e = jnp.minimum(jnp.sum(pends[None, :] <= (blk * ROW_BLOCK)[:, None], axis=1), N_EXPERTS - 1).astype(I32)
    eids = jnp.arange(N_EXPERTS, dtype=I32)
    has = counts > 0
    ord_e = (jnp.cumsum(has) - has).astype(I32)
    later = jnp.where((eids[None, :] > eids[:, None]) & has[None, :], eids[None, :], N_EXPERTS)
    nxt = jnp.min(later, axis=1)
    next_e = jnp.where(nxt < N_EXPERTS, nxt, eids).astype(I32)
    dest = _plan_dest(top_idx, pstart)

    xs = _dispatch(h2, dest, pstart.astype(I32), counts)
    ys = _experts(xs, block_e, n_used.reshape(1), next_e, ord_e, w_gate_e, w_up_e, w_down_e)
    out = _combine(ys, dest, top_wt, h2, x1, mod3, g_post_ffn,
                   w_gate_s.astype(BF16), w_up_s.astype(BF16), w_down_s.astype(BF16), seq)
    return out.reshape(bsz, seq, d)


def kernel(x, c, positions, w_ada, b_ada, g_pre_mix, g_post_mix, g_pre_ffn, g_post_ffn, w_in, w_dil_out,
           w_sb_out, w_mix_out, w_router, router_bias, w_gate_e, w_up_e, w_down_e, w_gate_s, w_up_s, w_down_s):
    for l in range(w_ada.shape[0]):
        mod = _adaln(c, w_ada[l], b_ada[l])
        x = _layer(x, mod, positions, g_pre_mix[l], g_post_mix[l], g_pre_ffn[l], g_post_ffn[l], w_in[l],
                   w_dil_out[l], w_sb_out[l], w_mix_out[l], w_router[l], router_bias[l],
                   w_gate_e[l], w_up_e[l], w_down_e[l], w_gate_s[l], w_up_s[l], w_down_s[l])
    return x
```

```python
import functools

import jax
import jax.numpy as jnp
from jax import lax
from jax.experimental import pallas as pl
from jax.experimental.pallas import tpu as pltpu

F32 = jnp.float32
BF16 = jnp.bfloat16
I32 = jnp.int32
U32 = jnp.uint32

D_MODEL = 2048
HEAD_DIM = 128
DIL_PATTERNS = ((128, 1), (512, 4), (2048, 16))
HEADS_PER_GROUP = 4
N_HEADS_DIL = 12
N_HEADS_SB = 8
WIDTH_DIL = N_HEADS_DIL * HEAD_DIM
WIDTH_SB = N_HEADS_SB * HEAD_DIM
D_DIL_OUT = HEADS_PER_GROUP * HEAD_DIM
Q_BLOCK = 128
ROPE_THETA = 500000.0
ROPE_DIM = HEAD_DIM // 4
N_GATE = 2 * D_MODEL
N_EXPERTS = 64
TOP_K = 8
N_GROUPS = 8
GROUP_SIZE = N_EXPERTS // N_GROUPS
TOPK_GROUPS = 4
D_EXPERT = 512
D_SHARED = 512
ROUTED_SCALE = 2.5
RMS_EPS = 1e-6
N_MOD = 6
ATTN_SCALE = HEAD_DIM ** -0.5

LANES = 128
SUBLANES = 8
VMEM_LIMIT = 56 * 1024 * 1024

ROW_BLOCK = 704
SB_DEAD = -110.0


def _cparams(sem, **kw):
    return pltpu.CompilerParams(dimension_semantics=sem, vmem_limit_bytes=VMEM_LIMIT, **kw)


def _adaln_body(ct_ref, w_ref, b_ref, o_ref, *, kc):
    nb = ct_ref.shape[1]
    nk = w_ref.shape[0] // kc

    def step(i, acc):
        k0 = pl.multiple_of(i * kc, kc)
        w = w_ref[pl.ds(k0, kc), :]
        c = ct_ref[pl.ds(k0, kc), :]
        s = c * jax.nn.sigmoid(c)
        parts = [jnp.sum(w * s[:, b:b + 1], axis=0, keepdims=True) for b in range(nb)]
        return acc + jnp.concatenate(parts, axis=0)

    acc = lax.fori_loop(0, nk, step, jnp.zeros(o_ref.shape, F32))
    o_ref[...] = acc + b_ref[...]


def _adaln(c, w_ada, b_ada):
    nb, d = c.shape
    n_out = w_ada.shape[1]
    tn = 1024
    return pl.pallas_call(
        functools.partial(_adaln_body, kc=256),
        grid=(n_out // tn,),
        in_specs=[
            pl.BlockSpec((d, nb), lambda j: (0, 0)),
            pl.BlockSpec((d, tn), lambda j: (0, j)),
            pl.BlockSpec((1, tn), lambda j: (0, j)),
        ],
        out_specs=pl.BlockSpec((nb, tn), lambda j: (0, j)),
        out_shape=jax.ShapeDtypeStruct((nb, n_out), F32),
        compiler_params=_cparams(("arbitrary",)),
        name="adaln",
    )(c.T, w_ada, b_ada.reshape(1, n_out))


def _rms(x):
    return x * lax.rsqrt(jnp.mean(x * x, axis=-1, keepdims=True) + RMS_EPS)


def _prenorm(x_ref, g_ref, mod_ref):
    y = _rms(x_ref[...]) * g_ref[...]
    shift = mod_ref[0, 0:1, :]
    scale = mod_ref[0, 1:2, :]
    return (y * (1.0 + scale) + shift).astype(BF16)


def _inproj_body(x_ref, g_ref, mod_ref, w_ref, o_ref, h_ref):
    @pl.when(pl.program_id(1) == 0)
    def _():
        h_ref[...] = _prenorm(x_ref, g_ref, mod_ref)

    o_ref[...] = jnp.dot(h_ref[...], w_ref[...], preferred_element_type=F32).astype(o_ref.dtype)


def _inproj(xf, g_pre, mod3, w_b, seq):
    n, d = xf.shape
    width = w_b.shape[1]
    tm, tn = 1024, width // 4
    per_b = seq // tm
    return pl.pallas_call(
        _inproj_body,
        grid=(n // tm, width // tn),
        in_specs=[
            pl.BlockSpec((tm, d), lambda i, j: (i, 0)),
            pl.BlockSpec((1, d), lambda i, j: (0, 0)),
            pl.BlockSpec((1, N_MOD, d), lambda i, j: (i // per_b, 0, 0)),
            pl.BlockSpec((d, tn), lambda i, j: (0, j)),
        ],
        out_specs=pl.BlockSpec((tm, tn), lambda i, j: (i, j)),
        out_shape=jax.ShapeDtypeStruct((n, width), BF16),
        scratch_shapes=[pltpu.VMEM((tm, d), BF16)],
        compiler_params=_cparams(("arbitrary", "arbitrary")),
        name="inproj",
    )(xf, g_pre.reshape(1, d), mod3, w_b)


def _inproj_dil_body(x_ref, g_ref, mod_ref, w_ref, t_ref, o0, o1, o2, res_ref):
    tm = x_ref.shape[0]
    h = _prenorm(x_ref, g_ref, mod_ref)
    t = t_ref[...]
    gw = 3 * D_DIL_OUT
    for gi, o_ref in enumerate((o0, o1, o2)):
        dil = DIL_PATTERNS[gi][1]
        res = jnp.dot(h, w_ref[:, gi * gw:(gi + 1) * gw], preferred_element_type=F32)
        for hs in range(3 * HEADS_PER_GROUP):
            sl = slice(hs * HEAD_DIM, (hs + 1) * HEAD_DIM)
            res_ref[hs] = _apply_rope(res[:, sl], t) if hs < 2 * HEADS_PER_GROUP else res[:, sl]
        for r in range(dil):
            for hs in range(3 * HEADS_PER_GROUP):
                rows = res_ref[hs] if dil == 1 else res_ref[hs, pl.ds(r, tm // dil, stride=dil), :]
                o_ref[0, r, :, hs * HEAD_DIM:(hs + 1) * HEAD_DIM] = rows.astype(o_ref.dtype)


def _inproj_dil(xf, g_pre, mod3, w_b, tables, bsz, seq):
    n, d = xf.shape
    gw = 3 * D_DIL_OUT
    tm = 512
    per_b = seq // tm
    dils = [p[1] for p in DIL_PATTERNS]
    return pl.pallas_call(
        _inproj_dil_body,
        grid=(n // tm,),
        in_specs=[
            pl.BlockSpec((tm, d), lambda i: (i, 0)),
            pl.BlockSpec((1, d), lambda i: (0, 0)),
            pl.BlockSpec((1, N_MOD, d), lambda i: (i // per_b, 0, 0)),
            pl.BlockSpec(w_b.shape, lambda i: (0, 0), pipeline_mode=pl.Buffered(1)),
            pl.BlockSpec((tm, 3 * LANES), lambda i: (i, 0)),
        ],
        out_specs=[pl.BlockSpec((1, dl, tm // dl, gw), lambda i: (i // per_b, 0, i % per_b, 0)) for dl in dils],
        out_shape=[jax.ShapeDtypeStruct((bsz, dl, seq // dl, gw), BF16) for dl in dils],
        scratch_shapes=[pltpu.VMEM((gw // HEAD_DIM, tm, HEAD_DIM), F32)],
        compiler_params=_cparams(("arbitrary",)),
        name="inproj_dil",
    )(xf, g_pre.reshape(1, d), mod3, w_b, tables)


def _rope_body(pos_ref, f_ref, o_ref):
    ang = pos_ref[...].astype(F32) * f_ref[...]
    c = jnp.cos(ang)
    s = jnp.sin(ang)
    lane = lax.broadcasted_iota(I32, ang.shape, 1)
    half = ROPE_DIM // 2
    o_ref[:, 0:LANES] = c
    o_ref[:, LANES:2 * LANES] = jnp.where(lane >= half, s, 0.0)
    o_ref[:, 2 * LANES:3 * LANES] = jnp.where(lane < half, -s, 0.0)


def _rope_tables(positions):
    n = positions.size
    half = ROPE_DIM // 2
    inv_freq = ROPE_THETA ** (-jnp.arange(0, ROPE_DIM, 2, dtype=F32) / ROPE_DIM)
    f = jnp.concatenate([inv_freq, inv_freq, jnp.zeros((LANES - 2 * half,), F32)]).reshape(1, LANES)
    tm = 2048
    return pl.pallas_call(
        _rope_body,
        grid=(n // tm,),
        in_specs=[pl.BlockSpec((tm, 1), lambda i: (i, 0)), pl.BlockSpec((1, LANES), lambda i: (0, 0))],
        out_specs=pl.BlockSpec((tm, 3 * LANES), lambda i: (i, 0)),
        out_shape=jax.ShapeDtypeStruct((n, 3 * LANES), F32),
        compiler_params=_cparams(("arbitrary",)),
        name="rope_tables",
    )(positions.reshape(n, 1), f)


def _apply_rope(x, t):
    half = ROPE_DIM // 2
    return (x * t[:, 0:LANES]
            + pltpu.roll(x, half, 1) * t[:, LANES:2 * LANES]
            + pltpu.roll(x, LANES - half, 1) * t[:, 2 * LANES:3 * LANES])


def _dil_body(cur_ref, kp_ref, vp_ref, o_ref, lse_ref, obuf, lbuf, *, dil, nsub):
    n = pl.program_id(1)
    tq = nsub * Q_BLOCK
    row = lax.broadcasted_iota(I32, (Q_BLOCK, 2 * Q_BLOCK), 0)
    col = lax.broadcasted_iota(I32, (Q_BLOCK, 2 * Q_BLOCK), 1)
    rel = row + Q_BLOCK - col
    band = jnp.where(rel >= 0, jnp.where(rel <= Q_BLOCK, 1.0, 0.0), 0.0)
    first = jnp.where(col >= Q_BLOCK, band, jnp.where(n > 0, band, 0.0))
    for r in range(dil):
        for h in range(HEADS_PER_GROUP):
            sl = slice(h * HEAD_DIM, (h + 1) * HEAD_DIM)
            ksl = slice(D_DIL_OUT + h * HEAD_DIM, D_DIL_OUT + (h + 1) * HEAD_DIM)
            vsl = slice(2 * D_DIL_OUT + h * HEAD_DIM, 2 * D_DIL_OUT + (h + 1) * HEAD_DIM)
            for j in range(nsub):
                rs = slice(j * Q_BLOCK, (j + 1) * Q_BLOCK)
                ps = slice((j - 1) * Q_BLOCK, j * Q_BLOCK)
                kprev = kp_ref[0, r, :, sl] if j == 0 else cur_ref[0, r, ps, ksl]
                vprev = vp_ref[0, r, :, sl] if j == 0 else cur_ref[0, r, ps, vsl]
                kcat = jnp.concatenate([kprev, cur_ref[0, r, rs, ksl]], axis=0)
                vcat = jnp.concatenate([vprev, cur_ref[0, r, rs, vsl]], axis=0)
                s = lax.dot_general(cur_ref[0, r, rs, sl], kcat, (((1,), (1,)), ((), ())),
                                    preferred_element_type=F32) * ATTN_SCALE
                s = jnp.where((first if j == 0 else band) > 0.0, s, -jnp.inf)
                m = jnp.max(s, axis=-1, keepdims=True)
                p = jnp.exp(s - m)
                l = jnp.sum(p, axis=-1, keepdims=True)
                o = jnp.dot((p / l).astype(BF16), vcat, preferred_element_type=F32)
                lse = jnp.broadcast_to(m + jnp.log(l), (Q_BLOCK, HEAD_DIM))
                if dil == 1:
                    o_ref[h, rs, :] = o
                    lse_ref[h, rs, :] = lse
                else:
                    obuf[h, rs, :] = o
                    lbuf[h, rs, :] = lse
            if dil > 1:
                o_ref[h, pl.ds(r, tq, stride=dil), :] = obuf[h]
                lse_ref[h, pl.ds(r, tq, stride=dil), :] = lbuf[h]


def _dilated_attention(qkv, g, bsz, seq):
    dil = DIL_PATTERNS[g][1]
    length = seq // dil
    tq = min(4 * Q_BLOCK, (16 * Q_BLOCK) // dil, length)
    nsub = tq // Q_BLOCK
    nq = length // tq
    gw = 3 * D_DIL_OUT
    n = bsz * seq

    def prev(colblk):
        return pl.BlockSpec((1, dil, Q_BLOCK, D_DIL_OUT),
                            lambda b, i: (b, 0, jnp.maximum(i * nsub - 1, 0), colblk))

    nh = HEADS_PER_GROUP
    out_spec = pl.BlockSpec((nh, tq * dil, HEAD_DIM), lambda b, i: (0, b * nq + i, 0))
    out_shape = jax.ShapeDtypeStruct((nh, n, HEAD_DIM), F32)
    return pl.pallas_call(
        functools.partial(_dil_body, dil=dil, nsub=nsub),
        grid=(bsz, nq),
        in_specs=[pl.BlockSpec((1, dil, tq, gw), lambda b, i: (b, 0, i, 0)), prev(1), prev(2)],
        out_specs=[out_spec, out_spec],
        out_shape=[out_shape, out_shape],
        scratch_shapes=[pltpu.VMEM((nh, tq, HEAD_DIM), F32), pltpu.VMEM((nh, tq, HEAD_DIM), F32)],
        compiler_params=_cparams(("arbitrary", "arbitrary")),
        name=f"dilated_d{dil}",
    )(qkv, qkv, qkv)


def _sb_body(q_ref, k_ref, v_ref, o_ref, acc_ref, car_ref):
    nblk = q_ref.shape[1] // Q_BLOCK
    r = lax.broadcasted_iota(I32, (Q_BLOCK, Q_BLOCK), 0)
    c = lax.broadcasted_iota(I32, (Q_BLOCK, Q_BLOCK), 1)
    causal = c < r
    rr = lax.broadcasted_iota(I32, (Q_BLOCK, 2 * Q_BLOCK), 0)
    cc = lax.broadcasted_iota(I32, (Q_BLOCK, 2 * Q_BLOCK), 1)
    uo = jnp.where(cc >= Q_BLOCK, 1.0, jnp.where(rr > cc, 1.0, 0.0)).astype(BF16)

    nh = acc_ref.shape[0]
    heads = range(nh)

    def tiles(qs, kb, carries, diag):
        k0 = pl.multiple_of(kb * Q_BLOCK, Q_BLOCK)
        hs = [slice(h * HEAD_DIM, (h + 1) * HEAD_DIM) for h in heads]
        zs = [lax.dot_general(qs[h], k_ref[0, pl.ds(k0, Q_BLOCK), hs[h]], (((1,), (1,)), ((), ())),
                              preferred_element_type=F32) * ATTN_SCALE for h in heads]
        stacked, log_s = [], []
        for z in zs:
            sp = jnp.log(1.0 + jnp.exp(-jnp.abs(z)))
            mx = jnp.maximum(z, 0.0)
            log_1m = -(mx + sp)
            if diag:
                log_1m = jnp.where(causal, log_1m, 0.0)
            hi = log_1m.astype(BF16)
            lo = (log_1m - hi.astype(F32)).astype(BF16)
            stacked.append(jnp.concatenate([hi, lo], axis=0))
            log_s.append((z - mx) - sp)
        r2s = [jnp.dot(s, uo, preferred_element_type=F32) for s in stacked]
        probs, new_carries = [], []
        for h in heads:
            sums = r2s[h][:Q_BLOCK] + r2s[h][Q_BLOCK:]
            a = jnp.exp(log_s[h] + carries[h] + sums[:, :Q_BLOCK])
            if diag:
                a = jnp.where(causal, a, 0.0)
            probs.append(a.astype(BF16))
            new_carries.append(carries[h] + sums[:, Q_BLOCK:])
        pvs = [jnp.dot(probs[h], v_ref[0, pl.ds(k0, Q_BLOCK), hs[h]], preferred_element_type=F32) for h in heads]
        return pvs, new_carries

    def all_max(xs):
        m = xs[0]
        for x in xs[1:]:
            m = jnp.maximum(m, x)
        return jnp.max(m)

    def qblock(qi, _):
        q0 = pl.multiple_of(qi * Q_BLOCK, Q_BLOCK)
        qs = [q_ref[0, pl.ds(q0, Q_BLOCK), h * HEAD_DIM:(h + 1) * HEAD_DIM] for h in heads]
        zero = jnp.zeros((Q_BLOCK, Q_BLOCK), F32)
        pvs, cars = tiles(qs, qi, [zero] * nh, True)
        for h in heads:
            acc_ref[h] = pvs[h]
            car_ref[h] = cars[h]

        def cond(st):
            return jnp.logical_and(st[0] >= 0, st[1] > SB_DEAD)

        def body(st):
            pvs, cars = tiles(qs, st[0], [car_ref[h] for h in heads], False)
            for h in heads:
                acc_ref[h] += pvs[h]
                car_ref[h] = cars[h]
            return st[0] - 1, all_max(cars)

        lax.while_loop(cond, body, (qi - 1, all_max(cars)))
        for h in range(nh):
            o_ref[0, pl.ds(q0, Q_BLOCK), h * HEAD_DIM:(h + 1) * HEAD_DIM] = acc_ref[h].astype(o_ref.dtype)
        return 0

    lax.fori_loop(0, nblk, qblock, 0)


SB_HEADS_PER_STEP = 8


def _stick_breaking(proj, bsz, seq):
    width = proj.shape[1]
    pv = proj.reshape(bsz, seq, width)
    nh = SB_HEADS_PER_STEP
    bw = nh * HEAD_DIM
    base = N_GATE // bw

    def spec(off):
        return pl.BlockSpec((1, seq, bw), lambda b, h: (b, 0, base + off + h), pipeline_mode=pl.Buffered(1))

    nstep = N_HEADS_SB // nh
    o = pl.pallas_call(
        _sb_body,
        grid=(bsz, nstep),
        in_specs=[spec(0), spec(nstep), spec(2 * nstep)],
        out_specs=pl.BlockSpec((1, seq, bw), lambda b, h: (b, 0, h)),
        out_shape=jax.ShapeDtypeStruct((bsz, seq, WIDTH_SB), BF16),
        scratch_shapes=[pltpu.VMEM((nh, Q_BLOCK, Q_BLOCK), F32), pltpu.VMEM((nh, Q_BLOCK, Q_BLOCK), F32)],
        compiler_params=_cparams(("arbitrary", "arbitrary")),
        name="stick_breaking",
    )(pv, pv, pv)
    return o.reshape(bsz * seq, WIDTH_SB)


def _mixout_body(o1, o2, o3, l1, l2, l3, osb, gd_ref, gs_ref, x_ref, mod_ref, gpost, gpre,
                 wd, ws, wm, x1_ref, h2_ref):
    heads = []
    for h in range(HEADS_PER_GROUP):
        la, lb, lc = l1[h], l2[h], l3[h]
        m = jnp.maximum(la, jnp.maximum(lb, lc))
        ea, eb, ec = jnp.exp(la - m), jnp.exp(lb - m), jnp.exp(lc - m)
        heads.append(((ea * o1[h] + eb * o2[h] + ec * o3[h]) / (ea + eb + ec)).astype(BF16))
    yd = jnp.dot(jnp.concatenate(heads, axis=1), wd[...], preferred_element_type=F32)
    ys = jnp.dot(osb[...], ws[...], preferred_element_type=F32)
    mix = jax.nn.sigmoid(gd_ref[...].astype(F32)) * yd + jax.nn.sigmoid(gs_ref[...].astype(F32)) * ys
    y = jnp.dot(mix.astype(BF16), wm[...], preferred_element_type=F32)
    gate1 = mod_ref[0, 2:3, :]
    shift2 = mod_ref[0, 3:4, :]
    scale2 = mod_ref[0, 4:5, :]
    x1 = x_ref[...] + gate1 * (_rms(y) * gpost[...])
    x1_ref[...] = x1
    h2_ref[...] = ((_rms(x1) * gpre[...]) * (1.0 + scale2) + shift2).astype(h2_ref.dtype)


def _const_spec(shape):
    return pl.BlockSpec(shape, lambda i: (0,) * len(shape), pipeline_mode=pl.Buffered(1))


def _mixout(o_dil, lse_dil, o_sb, proj, xf, mod3, g_post, g_pre, wd_b, ws_b, wm_b, seq):
    n, d = xf.shape
    tm = 256
    per_b = seq // tm
    row = lambda w: pl.BlockSpec((tm, w), lambda i: (i, 0))
    head_major = pl.BlockSpec((HEADS_PER_GROUP, tm, HEAD_DIM), lambda i: (0, i, 0))
    in_specs = (
        [head_major] * 6 + [row(WIDTH_SB)]
        + [pl.BlockSpec((tm, d), lambda i: (i, 0)), pl.BlockSpec((tm, d), lambda i: (i, 1))]
        + [row(d), pl.BlockSpec((1, N_MOD, d), lambda i: (i // per_b, 0, 0))]
        + [_const_spec((1, d)), _const_spec((1, d))]
        + [_const_spec(wd_b.shape), _const_spec(ws_b.shape), _const_spec(wm_b.shape)]
    )
    return pl.pallas_call(
        _mixout_body,
        grid=(n // tm,),
        in_specs=in_specs,
        out_specs=[row(d), row(d)],
        out_shape=[jax.ShapeDtypeStruct((n, d), F32), jax.ShapeDtypeStruct((n, d), BF16)],
        compiler_params=_cparams(("arbitrary",)),
        name="mixout",
    )(*o_dil, *lse_dil, o_sb, proj, proj, xf, mod3, g_post.reshape(1, d), g_pre.reshape(1, d), wd_b, ws_b, wm_b)


def _topk_rows(x, k, iota0):
    big = x.shape[0]
    out = []
    for _ in range(k):
        m = jnp.max(x, axis=0, keepdims=True)
        i = jnp.min(jnp.where(x == m, iota0, big), axis=0, keepdims=True)
        out.append((m, i))
        x = jnp.where(iota0 == i, -jnp.inf, x)
    return out


def _router_body(h_ref, wr_ref, bias_ref, idx_ref, wt_ref):
    tm = h_ref.shape[0]
    logits = lax.dot_general(wr_ref[...], h_ref[...], (((1,), (1,)), ((), ())), preferred_element_type=F32)
    scores = jax.nn.sigmoid(logits)
    sel = scores + bias_ref[...]
    sub = lax.broadcasted_iota(I32, (GROUP_SIZE, tm), 0)
    grp = []
    for g in range(N_GROUPS):
        (m1, _), (m2, _) = _topk_rows(sel[g * GROUP_SIZE:(g + 1) * GROUP_SIZE], 2, sub)
        grp.append(m1 + m2)
    gscore = jnp.concatenate(grp, axis=0)
    giota = lax.broadcasted_iota(I32, (N_GROUPS, tm), 0)
    gmask = jnp.zeros((N_GROUPS, tm), F32)
    for _, gi in _topk_rows(gscore, TOPK_GROUPS, giota):
        gmask = jnp.where(giota == gi, 1.0, gmask)
    masked = jnp.concatenate(
        [jnp.where(gmask[g:g + 1] > 0.0, sel[g * GROUP_SIZE:(g + 1) * GROUP_SIZE], -jnp.inf)
         for g in range(N_GROUPS)], axis=0)
    eiota = lax.broadcasted_iota(I32, (N_EXPERTS, tm), 0)
    picks = _topk_rows(masked, TOP_K, eiota)
    idx = jnp.concatenate([i for _, i in picks], axis=0)
    top_s = jnp.concatenate(
        [jnp.sum(jnp.where(eiota == i, scores, 0.0), axis=0, keepdims=True) for _, i in picks], axis=0)
    top_w = top_s / jnp.sum(top_s, axis=0, keepdims=True) * ROUTED_SCALE
    idx_ref[...] = idx
    wpad = jnp.concatenate([top_w, jnp.zeros((LANES - TOP_K, tm), F32)], axis=0)
    wt_ref[...] = wpad.T


def _router(h2, w_router, router_bias):
    n, d = h2.shape
    tm = 512
    return pl.pallas_call(
        _router_body,
        grid=(n // tm,),
        in_specs=[
            pl.BlockSpec((tm, d), lambda i: (i, 0)),
            pl.BlockSpec((N_EXPERTS, d), lambda i: (0, 0)),
            pl.BlockSpec((N_EXPERTS, 1), lambda i: (0, 0)),
        ],
        out_specs=[pl.BlockSpec((TOP_K, tm), lambda i: (0, i)), pl.BlockSpec((tm, LANES), lambda i: (i, 0))],
        out_shape=[jax.ShapeDtypeStruct((TOP_K, n), I32), jax.ShapeDtypeStruct((n, LANES), F32)],
        compiler_params=_cparams(("arbitrary",)),
        name="router",
    )(h2, w_router.T.astype(BF16), router_bias.reshape(N_EXPERTS, 1))


def _plan_tile(idx):
    tm = idx.shape[1]
    eiota = lax.broadcasted_iota(I32, (N_EXPERTS, tm), 0)
    hit = jnp.zeros((N_EXPERTS, tm), F32)
    for k in range(TOP_K):
        hit = jnp.where(eiota == idx[k:k + 1], 1.0, hit)
    r = lax.broadcasted_iota(I32, (tm, tm), 0)
    c = lax.broadcasted_iota(I32, (tm, tm), 1)
    before = jnp.where(r < c, 1.0, 0.0).astype(BF16)
    excl = jnp.dot(hit.astype(BF16), before, preferred_element_type=F32)
    tot = excl[:, tm - 1:tm] + hit[:, tm - 1:tm]
    return eiota, hit, excl, tot


def _plan_counts_body(idx_ref, cnt_ref):
    @pl.when(pl.program_id(0) == 0)
    def _():
        cnt_ref[...] = jnp.zeros(cnt_ref.shape, F32)

    _, _, _, tot = _plan_tile(idx_ref[...])
    cnt_ref[...] += tot


def _plan_counts(top_idx):
    n = top_idx.shape[1]
    tm = 512
    return pl.pallas_call(
        _plan_counts_body,
        grid=(n // tm,),
        in_specs=[pl.BlockSpec((TOP_K, tm), lambda i: (0, i))],
        out_specs=pl.BlockSpec((N_EXPERTS, LANES), lambda i: (0, 0)),
        out_shape=jax.ShapeDtypeStruct((N_EXPERTS, LANES), F32),
        compiler_params=_cparams(("arbitrary",)),
        name="plan_counts",
    )(top_idx)


def _plan_dest_body(idx_ref, base_ref, dest_ref, run_ref):
    @pl.when(pl.program_id(0) == 0)
    def _():
        run_ref[...] = base_ref[...]

    idx = idx_ref[...]
    eiota, _, excl, tot = _plan_tile(idx)
    pos = run_ref[:, 0:1] + excl
    rows = [jnp.sum(jnp.where(eiota == idx[k:k + 1], pos, 0.0), axis=0, keepdims=True) for k in range(TOP_K)]
    dest_ref[...] = jnp.concatenate(rows, axis=0).astype(I32)
    run_ref[...] += tot


def _plan_dest(top_idx, pstart):
    n = top_idx.shape[1]
    tm = 512
    base = jnp.broadcast_to(pstart.astype(F32).reshape(N_EXPERTS, 1), (N_EXPERTS, LANES))
    return pl.pallas_call(
        _plan_dest_body,
        grid=(n // tm,),
        in_specs=[pl.BlockSpec((TOP_K, tm), lambda i: (0, i)), pl.BlockSpec((N_EXPERTS, LANES), lambda i: (0, 0))],
        out_specs=pl.BlockSpec((TOP_K, tm), lambda i: (0, i)),
        out_shape=jax.ShapeDtypeStruct((TOP_K, n), I32),
        scratch_shapes=[pltpu.VMEM((N_EXPERTS, LANES), F32)],
        compiler_params=_cparams(("arbitrary",)),
        name="plan_dest",
    )(top_idx, base)


def _capacity(n):
    return (pl.cdiv(n * TOP_K, ROW_BLOCK) + N_EXPERTS) * ROW_BLOCK


def _pack_halves(y):
    w = y.shape[1] // 2
    hi = pltpu.bitcast(y[:, :w].astype(BF16).astype(F32), U32)
    lo = pltpu.bitcast(y[:, w:].astype(BF16).astype(F32), U32)
    return hi | (lo >> 16)


def _unpack_halves(p):
    a = pltpu.bitcast(p & jnp.uint32(0xFFFF0000), F32)
    b = pltpu.bitcast(p << 16, F32)
    return a, b


TOKENS_PER_ISSUE = 4


def _row_dma_loops(row_copy):
    def issue(i, _):
        for u in range(TOKENS_PER_ISSUE):
            for k in range(TOP_K):
                row_copy(i * TOKENS_PER_ISSUE + u, k).start(priority=k % 2)
        return 0

    def drain(i, _):
        for u in range(TOKENS_PER_ISSUE):
            for k in range(TOP_K):
                row_copy(i * TOKENS_PER_ISSUE + u, k).wait()
        return 0

    return issue, drain


def _pad_chunks():
    s = 1
    while 2 * s < ROW_BLOCK:
        s *= 2
    sizes = []
    while s >= 1:
        sizes.append(s)
        s //= 2
    return sizes


def _tile_rows(t):
    start = t * SUBLANES
    return pl.ds(start if isinstance(start, int) else pl.multiple_of(start, SUBLANES), SUBLANES)


def _to_row_tiles(ref, packed, lead=()):
    rows = packed.shape[0]
    for s in range(SUBLANES):
        ref[(*lead, pl.ds(s, rows, stride=SUBLANES), slice(None))] = packed[:, s * LANES:(s + 1) * LANES]


def _from_row_tiles(ref, rows, lead=()):
    return [ref[(*lead, pl.ds(s, rows, stride=SUBLANES), slice(None))] for s in range(SUBLANES)]


def _dispatch_body(pstart_ref, cnt_ref, dest_ref, h_ref, wg, wu, wd, xs_ref, sh_ref, xbuf, zbuf, sem, zsem):
    tm = h_ref.shape[0]
    step = pl.program_id(0)

    def pad_dmas(wait):
        def per_expert(e, _):
            cnt = cnt_ref[e]
            pad = (ROW_BLOCK - cnt % ROW_BLOCK) % ROW_BLOCK
            start = pstart_ref[e] + cnt
            for size in _pad_chunks():
                @pl.when((pad & size) != 0)
                def _():
                    off = pl.multiple_of((start + (pad & ~(2 * size - 1))) * SUBLANES, SUBLANES)
                    cp = pltpu.make_async_copy(zbuf.at[pl.ds(0, size * SUBLANES)],
                                               xs_ref.at[pl.ds(off, size * SUBLANES)], zsem)
                    cp.wait() if wait else cp.start()
            return 0
        lax.fori_loop(0, N_EXPERTS, per_expert, 0)

    @pl.when(step == 0)
    def _():
        zbuf[...] = jnp.zeros(zbuf.shape, zbuf.dtype)
        pad_dmas(False)
        pad_dmas(True)

    h = h_ref[...]
    _to_row_tiles(xbuf, _pack_halves(h.astype(F32)))

    def row_copy(t, k):
        return pltpu.make_async_copy(xbuf.at[_tile_rows(t)], xs_ref.at[_tile_rows(dest_ref[k, t])], sem)

    def start_rows(lo, hi):
        for t in range(lo, hi):
            for k in range(TOP_K):
                row_copy(t, k).start(priority=k % 2)

    q = tm // 4
    g = jnp.dot(h, wg[...], preferred_element_type=F32)
    start_rows(0, q)
    u = jnp.dot(h, wu[...], preferred_element_type=F32)
    start_rows(q, 2 * q)
    hmid = (g * jax.nn.sigmoid(g) * u).astype(BF16)
    start_rows(2 * q, 3 * q)
    sh_ref[...] = jnp.dot(hmid, wd[...], preferred_element_type=F32)
    start_rows(3 * q, tm)

    lax.fori_loop(0, tm // TOKENS_PER_ISSUE, _row_dma_loops(row_copy)[1], 0)


def _dispatch(h2, dest, pstart, counts, wg_b, wu_b, wd_b):
    n, d = h2.shape
    tm = 256
    assert d // 2 == SUBLANES * LANES
    const = lambda shape: pl.BlockSpec(shape, lambda i, *_: (0,) * len(shape), pipeline_mode=pl.Buffered(1))
    grid_spec = pltpu.PrefetchScalarGridSpec(
        num_scalar_prefetch=2,
        grid=(n // tm,),
        in_specs=[
            pl.BlockSpec((TOP_K, tm), lambda i, *_: (0, i), memory_space=pltpu.SMEM),
            pl.BlockSpec((tm, d), lambda i, *_: (i, 0)),
            const(wg_b.shape), const(wu_b.shape), const(wd_b.shape),
        ],
        out_specs=[pl.BlockSpec(memory_space=pl.ANY), pl.BlockSpec((tm, d), lambda i, *_: (i, 0))],
        scratch_shapes=[
            pltpu.VMEM((tm * SUBLANES, LANES), U32),
            pltpu.VMEM((_pad_chunks()[0] * SUBLANES, LANES), U32),
            pltpu.SemaphoreType.DMA(()),
            pltpu.SemaphoreType.DMA(()),
        ],
    )
    return pl.pallas_call(
        _dispatch_body,
        grid_spec=grid_spec,
        out_shape=[jax.ShapeDtypeStruct((_capacity(n) * SUBLANES, LANES), U32), jax.ShapeDtypeStruct((n, d), F32)],
        compiler_params=_cparams(("arbitrary",), has_side_effects=True, disable_bounds_checks=True),
        name="dispatch",
    )(pstart, counts, dest, h2, wg_b, wu_b, wd_b)


def _experts_body(be_ref, nu_ref, nxt_ref, ord_ref, x_ref, wg_hbm, wu_hbm, wd_hbm, y_ref,
                  wg_s, wu_s, wd_s, wg_b, wu_b, wd_b, sems):
    i = pl.program_id(0)
    used = i < nu_ref[0]
    e = be_ref[i]
    fresh = jnp.logical_or(i == 0, e != be_ref[jnp.maximum(i - 1, 0)])
    slot = ord_ref[e] % 2

    def weight_copies(expert, s):
        return (pltpu.make_async_copy(wg_hbm.at[expert], wg_s.at[s], sems.at[s, 0]),
                pltpu.make_async_copy(wu_hbm.at[expert], wu_s.at[s], sems.at[s, 1]),
                pltpu.make_async_copy(wd_hbm.at[expert], wd_s.at[s], sems.at[s, 2]))

    @pl.when(i == 0)
    def _():
        for cp in weight_copies(e, slot):
            cp.start()

    @pl.when(jnp.logical_and(used, fresh))
    def _():
        copies = weight_copies(e, slot)
        for cp, dst, src in zip(copies, (wg_b, wu_b, wd_b), (wg_s, wu_s, wd_s)):
            cp.wait()
            dst[...] = src[slot].astype(BF16)

        @pl.when(nxt_ref[e] != e)
        def _():
            for cp in weight_copies(nxt_ref[e], 1 - slot):
                cp.start()

    @pl.when(used)
    def _():
        half = SUBLANES * LANES
        slabs = [_unpack_halves(p) for p in _from_row_tiles(x_ref, ROW_BLOCK)]
        xa = jnp.concatenate([a.astype(BF16) for a, _ in slabs], axis=1)
        xb = jnp.concatenate([b.astype(BF16) for _, b in slabs], axis=1)
        g = (jnp.dot(xa, wg_b[:half], preferred_element_type=F32)
             + jnp.dot(xb, wg_b[half:], preferred_element_type=F32))
        u = (jnp.dot(xa, wu_b[:half], preferred_element_type=F32)
             + jnp.dot(xb, wu_b[half:], preferred_element_type=F32))
        hmid = (g * jax.nn.sigmoid(g) * u).astype(BF16)
        _to_row_tiles(y_ref, _pack_halves(jnp.dot(hmid, wd_b[...], preferred_element_type=F32)))


def _experts(xs, block_e, n_used, next_e, ord_e, w_gate_e, w_up_e, w_down_e):
    cap = xs.shape[0] // SUBLANES
    d = 2 * SUBLANES * LANES
    nblk = cap // ROW_BLOCK
    row_block = (ROW_BLOCK * SUBLANES, LANES)

    def row_map(i, be, nu, nxt, od):
        return (jnp.minimum(i, nu[0] - 1), 0)

    grid_spec = pltpu.PrefetchScalarGridSpec(
        num_scalar_prefetch=4,
        grid=(nblk,),
        in_specs=[
            pl.BlockSpec(row_block, row_map),
            pl.BlockSpec(memory_space=pl.ANY),
            pl.BlockSpec(memory_space=pl.ANY),
            pl.BlockSpec(memory_space=pl.ANY),
        ],
        out_specs=pl.BlockSpec(row_block, row_map),
        scratch_shapes=[
            pltpu.VMEM((2, d, D_EXPERT), F32),
            pltpu.VMEM((2, d, D_EXPERT), F32),
            pltpu.VMEM((2, D_EXPERT, d), F32),
            pltpu.VMEM((d, D_EXPERT), BF16),
            pltpu.VMEM((d, D_EXPERT), BF16),
            pltpu.VMEM((D_EXPERT, d), BF16),
            pltpu.SemaphoreType.DMA((2, 3)),
        ],
    )
    return pl.pallas_call(
        _experts_body,
        grid_spec=grid_spec,
        out_shape=jax.ShapeDtypeStruct(xs.shape, U32),
        compiler_params=_cparams(("arbitrary",)),
        name="experts",
    )(block_e, n_used, next_e, ord_e, xs, w_gate_e, w_up_e, w_down_e)


COMBINE_TILE = 256


def _combine_body(dest_ref, dest_next_ref, ys_ref, wt_ref, sh_ref, x1_ref, mod_ref, gpost, o_ref,
                  buf0, buf1, sem0, sem1):
    tile = COMBINE_TILE
    i = pl.program_id(0)
    bufs, sems = (buf0, buf1), (sem0, sem1)
    half = SUBLANES * LANES
    chunk = tile // SUBLANES

    def copies(dref, col0, slot):
        def row_copy(t, k):
            return pltpu.make_async_copy(ys_ref.at[_tile_rows(dref[k, col0 + t])],
                                         bufs[slot].at[k, _tile_rows(t)], sems[slot])
        return row_copy

    def run(loop_body):
        lax.fori_loop(0, tile // TOKENS_PER_ISSUE, loop_body, 0)

    @pl.when(i == 0)
    def _():
        run(_row_dma_loops(copies(dest_ref, 0, 0))[0])

    def phase(row0, slot, next_copy):
        run(_row_dma_loops(copies(dest_ref, row0, slot))[1])
        rows = slice(row0, row0 + tile)
        shared = sh_ref[rows, :]
        wt = wt_ref[rows, :]
        wk = [jnp.broadcast_to(wt[:, k:k + 1], (tile, LANES)) for k in range(TOP_K)]
        ya, yb = [], []
        for s in range(SUBLANES):
            for t in range(s * chunk, (s + 1) * chunk):
                for k in range(TOP_K):
                    next_copy(t, k).start(priority=k % 2)
            sa = shared[:, s * LANES:(s + 1) * LANES]
            sb = shared[:, half + s * LANES:half + (s + 1) * LANES]
            for k in range(TOP_K):
                a, b = _unpack_halves(bufs[slot][k, pl.ds(s, tile, stride=SUBLANES), :])
                sa = sa + wk[k] * a
                sb = sb + wk[k] * b
            ya.append(sa)
            yb.append(sb)
        y = jnp.concatenate(ya + yb, axis=1)
        gate2 = mod_ref[0, 5:6, :]
        o_ref[rows, :] = x1_ref[rows, :] + gate2 * (_rms(y) * gpost[...])

    phase(0, 0, copies(dest_ref, tile, 1))
    phase(tile, 1, copies(dest_next_ref, 0, 0))

    @pl.when(i == pl.num_programs(0) - 1)
    def _():
        run(_row_dma_loops(copies(dest_next_ref, 0, 0))[1])


def _combine(ys, dest, wt, shared, x1, mod3, g_post, seq):
    n, d = x1.shape
    tm = 2 * COMBINE_TILE
    nstep = n // tm
    per_b = seq // tm
    const = lambda shape: pl.BlockSpec(shape, lambda i, *_: (0,) * len(shape), pipeline_mode=pl.Buffered(1))
    buf = pltpu.VMEM((TOP_K, COMBINE_TILE * SUBLANES, LANES), U32)
    grid_spec = pltpu.PrefetchScalarGridSpec(
        num_scalar_prefetch=0,
        grid=(nstep,),
        in_specs=[
            pl.BlockSpec((TOP_K, tm), lambda i: (0, i), memory_space=pltpu.SMEM),
            pl.BlockSpec((TOP_K, tm), lambda i: (0, jnp.minimum(i + 1, nstep - 1)), memory_space=pltpu.SMEM),
            pl.BlockSpec(memory_space=pl.ANY),
            pl.BlockSpec((tm, LANES), lambda i: (i, 0)),
            pl.BlockSpec((tm, d), lambda i: (i, 0)),
            pl.BlockSpec((tm, d), lambda i: (i, 0)),
            pl.BlockSpec((1, N_MOD, d), lambda i: (i // per_b, 0, 0)),
            const((1, d)),
        ],
        out_specs=pl.BlockSpec((tm, d), lambda i: (i, 0)),
        scratch_shapes=[buf, buf, pltpu.SemaphoreType.DMA(()), pltpu.SemaphoreType.DMA(())],
    )
    return pl.pallas_call(
        _combine_body,
        grid_spec=grid_spec,
        out_shape=jax.ShapeDtypeStruct((n, d), F32),
        compiler_params=_cparams(("arbitrary",), disable_bounds_checks=True),
        name="combine",
    )(dest, dest, ys, wt, shared, x1, mod3, g_post.reshape(1, d))


def _layer(x, mod, positions, g_pre_mix, g_post_mix, g_pre_ffn, g_post_ffn, w_in, w_dil_out, w_sb_out, w_mix_out,
           w_router, router_bias, w_gate_e, w_up_e, w_down_e, w_gate_s, w_up_s, w_down_s):
    bsz, seq, d = x.shape
    n = bsz * seq
    xf = x.reshape(n, d)
    mod3 = mod.reshape(bsz, N_MOD, d)

    nd, nq = 3 * WIDTH_DIL, 3 * (WIDTH_DIL + WIDTH_SB)
    w_plain = jnp.concatenate([w_in[:, nq:], w_in[:, nd:nq]], axis=1).astype(BF16)
    cols = []
    for g in range(len(DIL_PATTERNS)):
        for part in range(3):
            lo = part * WIDTH_DIL + g * D_DIL_OUT
            cols.append(w_in[:, lo:lo + D_DIL_OUT])
    w_dil = jnp.concatenate(cols, axis=1).astype(BF16)

    proj = _inproj(xf, g_pre_mix, mod3, w_plain, seq)
    tables = _rope_tables(positions)
    qkv_dil = _inproj_dil(xf, g_pre_mix, mod3, w_dil, tables, bsz, seq)
    o_dil, lse_dil = [], []
    for g, (window, dilation) in enumerate(DIL_PATTERNS):
        assert window // dilation == Q_BLOCK
        o, lse = _dilated_attention(qkv_dil[g], g, bsz, seq)
        o_dil.append(o)
        lse_dil.append(lse)
    o_sb = _stick_breaking(proj, bsz, seq)

    x1, h2 = _mixout(o_dil, lse_dil, o_sb, proj, xf, mod3, g_post_mix, g_pre_ffn,
                     w_dil_out.astype(BF16), w_sb_out.astype(BF16), w_mix_out.astype(BF16), seq)

    top_idx, top_wt = _router(h2, w_router, router_bias)
    counts = _plan_counts(top_idx)[:, 0].astype(I32)
    padded = (counts + ROW_BLOCK - 1) // ROW_BLOCK * ROW_BLOCK
    pends = jnp.cumsum(padded)
    pstart = pends - padded
    nblk = _capacity(n) // ROW_BLOCK
    n_used = (pends[-1] // ROW_BLOCK).astype(I32)
    blk = jnp.minimum(jnp.arange(nblk, dtype=I32), n_used - 1)
    block_e = jnp.minimum(jnp.sum(pends[None, :] <= (blk * ROW_BLOCK)[:, None], axis=1), N_EXPERTS - 1).astype(I32)
    eids = jnp.arange(N_EXPERTS, dtype=I32)
    has = counts > 0
    ord_e = (jnp.cumsum(has) - has).astype(I32)
    later = jnp.where((eids[None, :] > eids[:, None]) & has[None, :], eids[None, :], N_EXPERTS)
    nxt = jnp.min(later, axis=1)
    next_e = jnp.where(nxt < N_EXPERTS, nxt, eids).astype(I32)
    dest = _plan_dest(top_idx, pstart)

    xs, shared = _dispatch(h2, dest, pstart.astype(I32), counts,
                           w_gate_s.astype(BF16), w_up_s.astype(BF16), w_down_s.astype(BF16))
    ys = _experts(xs, block_e, n_used.reshape(1), next_e, ord_e, w_gate_e, w_up_e, w_down_e)
    out = _combine(ys, dest, top_wt, shared, x1, mod3, g_post_ffn, seq)
    return out.reshape(bsz, seq, d)


def kernel(x, c, positions, w_ada, b_ada, g_pre_mix, g_post_mix, g_pre_ffn, g_post_ffn, w_in, w_dil_out,
           w_sb_out, w_mix_out, w_router, router_bias, w_gate_e, w_up_e, w_down_e, w_gate_s, w_up_s, w_down_s):
    for l in range(w_ada.shape[0]):
        mod = _adaln(c, w_ada[l], b_ada[l])
        x = _layer(x, mod, positions, g_pre_mix[l], g_post_mix[l], g_pre_ffn[l], g_post_ffn[l], w_in[l],
                   w_dil_out[l], w_sb_out[l], w_mix_out[l], w_router[l], router_bias[l],
                   w_gate_e[l], w_up_e[l], w_down_e[l], w_gate_s[l], w_up_s[l], w_down_s[l])
    return x
```

```python
import functools

import jax
import jax.numpy as jnp
from jax import lax
from jax.experimental import pallas as pl
from jax.experimental.pallas import tpu as pltpu

F32 = jnp.float32
BF16 = jnp.bfloat16
I32 = jnp.int32
U32 = jnp.uint32

D_MODEL = 2048
HEAD_DIM = 128
DIL_PATTERNS = ((128, 1), (512, 4), (2048, 16))
HEADS_PER_GROUP = 4
N_HEADS_DIL = 12
N_HEADS_SB = 8
WIDTH_DIL = N_HEADS_DIL * HEAD_DIM
WIDTH_SB = N_HEADS_SB * HEAD_DIM
D_DIL_OUT = HEADS_PER_GROUP * HEAD_DIM
Q_BLOCK = 128
ROPE_THETA = 500000.0
ROPE_DIM = HEAD_DIM // 4
N_GATE = 2 * D_MODEL
N_EXPERTS = 64
TOP_K = 8
N_GROUPS = 8
GROUP_SIZE = N_EXPERTS // N_GROUPS
TOPK_GROUPS = 4
D_EXPERT = 512
D_SHARED = 512
ROUTED_SCALE = 2.5
RMS_EPS = 1e-6
N_MOD = 6
ATTN_SCALE = HEAD_DIM ** -0.5

LANES = 128
SUBLANES = 8
VMEM_LIMIT = 56 * 1024 * 1024

ROW_BLOCK = 704
SB_DEAD = -110.0


def _cparams(sem, **kw):
    return pltpu.CompilerParams(dimension_semantics=sem, vmem_limit_bytes=VMEM_LIMIT, **kw)


def _adaln_body(ct_ref, w_ref, b_ref, o_ref, *, kc):
    nb = ct_ref.shape[1]
    nk = w_ref.shape[0] // kc

    def step(i, acc):
        k0 = pl.multiple_of(i * kc, kc)
        w = w_ref[pl.ds(k0, kc), :]
        c = ct_ref[pl.ds(k0, kc), :]
        s = c * jax.nn.sigmoid(c)
        parts = [jnp.sum(w * s[:, b:b + 1], axis=0, keepdims=True) for b in range(nb)]
        return acc + jnp.concatenate(parts, axis=0)

    acc = lax.fori_loop(0, nk, step, jnp.zeros(o_ref.shape, F32))
    o_ref[...] = acc + b_ref[...]


def _adaln(c, w_ada, b_ada):
    nb, d = c.shape
    n_out = w_ada.shape[1]
    tn = 1024
    return pl.pallas_call(
        functools.partial(_adaln_body, kc=256),
        grid=(n_out // tn,),
        in_specs=[
            pl.BlockSpec((d, nb), lambda j: (0, 0)),
            pl.BlockSpec((d, tn), lambda j: (0, j)),
            pl.BlockSpec((1, tn), lambda j: (0, j)),
        ],
        out_specs=pl.BlockSpec((nb, tn), lambda j: (0, j)),
        out_shape=jax.ShapeDtypeStruct((nb, n_out), F32),
        compiler_params=_cparams(("arbitrary",)),
        name="adaln",
    )(c.T, w_ada, b_ada.reshape(1, n_out))


def _rms(x):
    return x * lax.rsqrt(jnp.mean(x * x, axis=-1, keepdims=True) + RMS_EPS)


def _prenorm(x_ref, g_ref, mod_ref):
    y = _rms(x_ref[...]) * g_ref[...]
    shift = mod_ref[0, 0:1, :]
    scale = mod_ref[0, 1:2, :]
    return (y * (1.0 + scale) + shift).astype(BF16)


def _inproj_body(x_ref, g_ref, mod_ref, w_ref, o_ref, h_ref):
    @pl.when(pl.program_id(1) == 0)
    def _():
        h_ref[...] = _prenorm(x_ref, g_ref, mod_ref)

    o_ref[...] = jnp.dot(h_ref[...], w_ref[...], preferred_element_type=F32).astype(o_ref.dtype)


def _inproj(xf, g_pre, mod3, w_b, seq):
    n, d = xf.shape
    width = w_b.shape[1]
    tm, tn = 1024, width // 4
    per_b = seq // tm
    return pl.pallas_call(
        _inproj_body,
        grid=(n // tm, width // tn),
        in_specs=[
            pl.BlockSpec((tm, d), lambda i, j: (i, 0)),
            pl.BlockSpec((1, d), lambda i, j: (0, 0)),
            pl.BlockSpec((1, N_MOD, d), lambda i, j: (i // per_b, 0, 0)),
            pl.BlockSpec((d, tn), lambda i, j: (0, j)),
        ],
        out_specs=pl.BlockSpec((tm, tn), lambda i, j: (i, j)),
        out_shape=jax.ShapeDtypeStruct((n, width), BF16),
        scratch_shapes=[pltpu.VMEM((tm, d), BF16)],
        compiler_params=_cparams(("arbitrary", "arbitrary")),
        name="inproj",
    )(xf, g_pre.reshape(1, d), mod3, w_b)


def _inproj_dil_body(x_ref, g_ref, mod_ref, w_ref, t_ref, o0, o1, o2, res_ref):
    tm = x_ref.shape[0]
    h = _prenorm(x_ref, g_ref, mod_ref)
    t = t_ref[...]
    gw = 3 * D_DIL_OUT
    for gi, o_ref in enumerate((o0, o1, o2)):
        dil = DIL_PATTERNS[gi][1]
        res = jnp.dot(h, w_ref[:, gi * gw:(gi + 1) * gw], preferred_element_type=F32)
        for hs in range(3 * HEADS_PER_GROUP):
            sl = slice(hs * HEAD_DIM, (hs + 1) * HEAD_DIM)
            res_ref[hs] = _apply_rope(res[:, sl], t) if hs < 2 * HEADS_PER_GROUP else res[:, sl]
        for r in range(dil):
            for hs in range(3 * HEADS_PER_GROUP):
                rows = res_ref[hs] if dil == 1 else res_ref[hs, pl.ds(r, tm // dil, stride=dil), :]
                o_ref[0, r, :, hs * HEAD_DIM:(hs + 1) * HEAD_DIM] = rows.astype(o_ref.dtype)


def _inproj_dil(xf, g_pre, mod3, w_b, tables, bsz, seq):
    n, d = xf.shape
    gw = 3 * D_DIL_OUT
    tm = 512
    per_b = seq // tm
    dils = [p[1] for p in DIL_PATTERNS]
    return pl.pallas_call(
        _inproj_dil_body,
        grid=(n // tm,),
        in_specs=[
            pl.BlockSpec((tm, d), lambda i: (i, 0)),
            pl.BlockSpec((1, d), lambda i: (0, 0)),
            pl.BlockSpec((1, N_MOD, d), lambda i: (i // per_b, 0, 0)),
            pl.BlockSpec(w_b.shape, lambda i: (0, 0), pipeline_mode=pl.Buffered(1)),
            pl.BlockSpec((tm, 3 * LANES), lambda i: (i, 0)),
        ],
        out_specs=[pl.BlockSpec((1, dl, tm // dl, gw), lambda i: (i // per_b, 0, i % per_b, 0)) for dl in dils],
        out_shape=[jax.ShapeDtypeStruct((bsz, dl, seq // dl, gw), BF16) for dl in dils],
        scratch_shapes=[pltpu.VMEM((gw // HEAD_DIM, tm, HEAD_DIM), F32)],
        compiler_params=_cparams(("arbitrary",)),
        name="inproj_dil",
    )(xf, g_pre.reshape(1, d), mod3, w_b, tables)


def _rope_body(pos_ref, f_ref, o_ref):
    ang = pos_ref[...].astype(F32) * f_ref[...]
    c = jnp.cos(ang)
    s = jnp.sin(ang)
    lane = lax.broadcasted_iota(I32, ang.shape, 1)
    half = ROPE_DIM // 2
    o_ref[:, 0:LANES] = c
    o_ref[:, LANES:2 * LANES] = jnp.where(lane >= half, s, 0.0)
    o_ref[:, 2 * LANES:3 * LANES] = jnp.where(lane < half, -s, 0.0)


def _rope_tables(positions):
    n = positions.size
    half = ROPE_DIM // 2
    inv_freq = ROPE_THETA ** (-jnp.arange(0, ROPE_DIM, 2, dtype=F32) / ROPE_DIM)
    f = jnp.concatenate([inv_freq, inv_freq, jnp.zeros((LANES - 2 * half,), F32)]).reshape(1, LANES)
    tm = 2048
    return pl.pallas_call(
        _rope_body,
        grid=(n // tm,),
        in_specs=[pl.BlockSpec((tm, 1), lambda i: (i, 0)), pl.BlockSpec((1, LANES), lambda i: (0, 0))],
        out_specs=pl.BlockSpec((tm, 3 * LANES), lambda i: (i, 0)),
        out_shape=jax.ShapeDtypeStruct((n, 3 * LANES), F32),
        compiler_params=_cparams(("arbitrary",)),
        name="rope_tables",
    )(positions.reshape(n, 1), f)


def _apply_rope(x, t):
    half = ROPE_DIM // 2
    return (x * t[:, 0:LANES]
            + pltpu.roll(x, half, 1) * t[:, LANES:2 * LANES]
            + pltpu.roll(x, LANES - half, 1) * t[:, 2 * LANES:3 * LANES])


def _dil_body(cur_ref, kp_ref, vp_ref, o_ref, lse_ref, obuf, lbuf, *, dil, nsub):
    n = pl.program_id(1)
    tq = nsub * Q_BLOCK
    row = lax.broadcasted_iota(I32, (Q_BLOCK, 2 * Q_BLOCK), 0)
    col = lax.broadcasted_iota(I32, (Q_BLOCK, 2 * Q_BLOCK), 1)
    rel = row + Q_BLOCK - col
    band = jnp.where(rel >= 0, jnp.where(rel <= Q_BLOCK, 1.0, 0.0), 0.0)
    first = jnp.where(col >= Q_BLOCK, band, jnp.where(n > 0, band, 0.0))
    for r in range(dil):
        for h in range(HEADS_PER_GROUP):
            sl = slice(h * HEAD_DIM, (h + 1) * HEAD_DIM)
            ksl = slice(D_DIL_OUT + h * HEAD_DIM, D_DIL_OUT + (h + 1) * HEAD_DIM)
            vsl = slice(2 * D_DIL_OUT + h * HEAD_DIM, 2 * D_DIL_OUT + (h + 1) * HEAD_DIM)
            for j in range(nsub):
                rs = slice(j * Q_BLOCK, (j + 1) * Q_BLOCK)
                ps = slice((j - 1) * Q_BLOCK, j * Q_BLOCK)
                kprev = kp_ref[0, r, :, sl] if j == 0 else cur_ref[0, r, ps, ksl]
                vprev = vp_ref[0, r, :, sl] if j == 0 else cur_ref[0, r, ps, vsl]
                kcat = jnp.concatenate([kprev, cur_ref[0, r, rs, ksl]], axis=0)
                vcat = jnp.concatenate([vprev, cur_ref[0, r, rs, vsl]], axis=0)
                s = lax.dot_general(cur_ref[0, r, rs, sl], kcat, (((1,), (1,)), ((), ())),
                                    preferred_element_type=F32) * ATTN_SCALE
                s = jnp.where((first if j == 0 else band) > 0.0, s, -jnp.inf)
                m = jnp.max(s, axis=-1, keepdims=True)
                p = jnp.exp(s - m)
                l = jnp.sum(p, axis=-1, keepdims=True)
                o = jnp.dot((p / l).astype(BF16), vcat, preferred_element_type=F32)
                lse = jnp.broadcast_to(m + jnp.log(l), (Q_BLOCK, HEAD_DIM))
                if dil == 1:
                    o_ref[h, rs, :] = o
                    lse_ref[h, rs, :] = lse
                else:
                    obuf[h, rs, :] = o
                    lbuf[h, rs, :] = lse
            if dil > 1:
                o_ref[h, pl.ds(r, tq, stride=dil), :] = obuf[h]
                lse_ref[h, pl.ds(r, tq, stride=dil), :] = lbuf[h]


def _dilated_attention(qkv, g, bsz, seq):
    dil = DIL_PATTERNS[g][1]
    length = seq // dil
    tq = min(4 * Q_BLOCK, (16 * Q_BLOCK) // dil, length)
    nsub = tq // Q_BLOCK
    nq = length // tq
    gw = 3 * D_DIL_OUT
    n = bsz * seq

    def prev(colblk):
        return pl.BlockSpec((1, dil, Q_BLOCK, D_DIL_OUT),
                            lambda b, i: (b, 0, jnp.maximum(i * nsub - 1, 0), colblk))

    nh = HEADS_PER_GROUP
    out_spec = pl.BlockSpec((nh, tq * dil, HEAD_DIM), lambda b, i: (0, b * nq + i, 0))
    out_shape = jax.ShapeDtypeStruct((nh, n, HEAD_DIM), F32)
    return pl.pallas_call(
        functools.partial(_dil_body, dil=dil, nsub=nsub),
        grid=(bsz, nq),
        in_specs=[pl.BlockSpec((1, dil, tq, gw), lambda b, i: (b, 0, i, 0)), prev(1), prev(2)],
        out_specs=[out_spec, out_spec],
        out_shape=[out_shape, out_shape],
        scratch_shapes=[pltpu.VMEM((nh, tq, HEAD_DIM), F32), pltpu.VMEM((nh, tq, HEAD_DIM), F32)],
        compiler_params=_cparams(("arbitrary", "arbitrary")),
        name=f"dilated_d{dil}",
    )(qkv, qkv, qkv)


def _sb_body(q_ref, k_ref, v_ref, o_ref, acc_ref, car_ref):
    nblk = q_ref.shape[1] // Q_BLOCK
    r = lax.broadcasted_iota(I32, (Q_BLOCK, Q_BLOCK), 0)
    c = lax.broadcasted_iota(I32, (Q_BLOCK, Q_BLOCK), 1)
    causal = c < r
    rr = lax.broadcasted_iota(I32, (Q_BLOCK, 2 * Q_BLOCK), 0)
    cc = lax.broadcasted_iota(I32, (Q_BLOCK, 2 * Q_BLOCK), 1)
    uo = jnp.where(cc >= Q_BLOCK, 1.0, jnp.where(rr > cc, 1.0, 0.0)).astype(BF16)

    nh = acc_ref.shape[0]
    heads = range(nh)

    def tiles(qs, kb, carries, diag):
        k0 = pl.multiple_of(kb * Q_BLOCK, Q_BLOCK)
        hs = [slice(h * HEAD_DIM, (h + 1) * HEAD_DIM) for h in heads]
        zs = [lax.dot_general(qs[h], k_ref[0, pl.ds(k0, Q_BLOCK), hs[h]], (((1,), (1,)), ((), ())),
                              preferred_element_type=F32) * ATTN_SCALE for h in heads]
        stacked, log_s = [], []
        for z in zs:
            sp = jnp.log(1.0 + jnp.exp(-jnp.abs(z)))
            mx = jnp.maximum(z, 0.0)
            log_1m = -(mx + sp)
            if diag:
                log_1m = jnp.where(causal, log_1m, 0.0)
            hi = log_1m.astype(BF16)
            lo = (log_1m - hi.astype(F32)).astype(BF16)
            stacked.append(jnp.concatenate([hi, lo], axis=0))
            log_s.append((z - mx) - sp)
        r2s = [jnp.dot(s, uo, preferred_element_type=F32) for s in stacked]
        probs, new_carries = [], []
        for h in heads:
            sums = r2s[h][:Q_BLOCK] + r2s[h][Q_BLOCK:]
            a = jnp.exp(log_s[h] + carries[h] + sums[:, :Q_BLOCK])
            if diag:
                a = jnp.where(causal, a, 0.0)
            probs.append(a.astype(BF16))
            new_carries.append(carries[h] + sums[:, Q_BLOCK:])
        pvs = [jnp.dot(probs[h], v_ref[0, pl.ds(k0, Q_BLOCK), hs[h]], preferred_element_type=F32) for h in heads]
        return pvs, new_carries

    def all_max(xs):
        m = xs[0]
        for x in xs[1:]:
            m = jnp.maximum(m, x)
        return jnp.max(m)

    def qblock(qi, _):
        q0 = pl.multiple_of(qi * Q_BLOCK, Q_BLOCK)
        qs = [q_ref[0, pl.ds(q0, Q_BLOCK), h * HEAD_DIM:(h + 1) * HEAD_DIM] for h in heads]
        zero = jnp.zeros((Q_BLOCK, Q_BLOCK), F32)
        pvs, cars = tiles(qs, qi, [zero] * nh, True)
        for h in heads:
            acc_ref[h] = pvs[h]
            car_ref[h] = cars[h]

        def cond(st):
            return jnp.logical_and(st[0] >= 0, st[1] > SB_DEAD)

        def body(st):
            pvs, cars = tiles(qs, st[0], [car_ref[h] for h in heads], False)
            for h in heads:
                acc_ref[h] += pvs[h]
                car_ref[h] = cars[h]
            return st[0] - 1, all_max(cars)

        lax.while_loop(cond, body, (qi - 1, all_max(cars)))
        for h in range(nh):
            o_ref[0, pl.ds(q0, Q_BLOCK), h * HEAD_DIM:(h + 1) * HEAD_DIM] = acc_ref[h].astype(o_ref.dtype)
        return 0

    lax.fori_loop(0, nblk, qblock, 0)


SB_HEADS_PER_STEP = 8


def _stick_breaking(proj, bsz, seq):
    width = proj.shape[1]
    pv = proj.reshape(bsz, seq, width)
    nh = SB_HEADS_PER_STEP
    bw = nh * HEAD_DIM
    base = N_GATE // bw

    def spec(off):
        return pl.BlockSpec((1, seq, bw), lambda b, h: (b, 0, base + off + h), pipeline_mode=pl.Buffered(1))

    nstep = N_HEADS_SB // nh
    o = pl.pallas_call(
        _sb_body,
        grid=(bsz, nstep),
        in_specs=[spec(0), spec(nstep), spec(2 * nstep)],
        out_specs=pl.BlockSpec((1, seq, bw), lambda b, h: (b, 0, h)),
        out_shape=jax.ShapeDtypeStruct((bsz, seq, WIDTH_SB), BF16),
        scratch_shapes=[pltpu.VMEM((nh, Q_BLOCK, Q_BLOCK), F32), pltpu.VMEM((nh, Q_BLOCK, Q_BLOCK), F32)],
        compiler_params=_cparams(("arbitrary", "arbitrary")),
        name="stick_breaking",
    )(pv, pv, pv)
    return o.reshape(bsz * seq, WIDTH_SB)


def _mixout_body(o1, o2, o3, l1, l2, l3, osb, gd_ref, gs_ref, x_ref, mod_ref, gpost, gpre,
                 wd, ws, wm, wr, rbias, x1_ref, h2_ref, idx_ref, wt_ref):
    heads = []
    for h in range(HEADS_PER_GROUP):
        la, lb, lc = l1[h], l2[h], l3[h]
        m = jnp.maximum(la, jnp.maximum(lb, lc))
        ea, eb, ec = jnp.exp(la - m), jnp.exp(lb - m), jnp.exp(lc - m)
        heads.append(((ea * o1[h] + eb * o2[h] + ec * o3[h]) / (ea + eb + ec)).astype(BF16))
    yd = jnp.dot(jnp.concatenate(heads, axis=1), wd[...], preferred_element_type=F32)
    ys = jnp.dot(osb[...], ws[...], preferred_element_type=F32)
    mix = jax.nn.sigmoid(gd_ref[...].astype(F32)) * yd + jax.nn.sigmoid(gs_ref[...].astype(F32)) * ys
    y = jnp.dot(mix.astype(BF16), wm[...], preferred_element_type=F32)
    gate1 = mod_ref[0, 2:3, :]
    shift2 = mod_ref[0, 3:4, :]
    scale2 = mod_ref[0, 4:5, :]
    x1 = x_ref[...] + gate1 * (_rms(y) * gpost[...])
    x1_ref[...] = x1
    h2 = ((_rms(x1) * gpre[...]) * (1.0 + scale2) + shift2).astype(h2_ref.dtype)
    h2_ref[...] = h2
    idx_ref[...], wt_ref[...] = _route(h2, wr[...], rbias[...])


def _const_spec(shape):
    return pl.BlockSpec(shape, lambda i: (0,) * len(shape), pipeline_mode=pl.Buffered(1))


def _mixout(o_dil, lse_dil, o_sb, proj, xf, mod3, g_post, g_pre, wd_b, ws_b, wm_b, w_router, router_bias, seq):
    n, d = xf.shape
    tm = 256
    per_b = seq // tm
    row = lambda w: pl.BlockSpec((tm, w), lambda i: (i, 0))
    head_major = pl.BlockSpec((HEADS_PER_GROUP, tm, HEAD_DIM), lambda i: (0, i, 0))
    in_specs = (
        [head_major] * 6 + [row(WIDTH_SB)]
        + [pl.BlockSpec((tm, d), lambda i: (i, 0)), pl.BlockSpec((tm, d), lambda i: (i, 1))]
        + [row(d), pl.BlockSpec((1, N_MOD, d), lambda i: (i // per_b, 0, 0))]
        + [_const_spec((1, d)), _const_spec((1, d))]
        + [_const_spec(wd_b.shape), _const_spec(ws_b.shape), _const_spec(wm_b.shape)]
        + [_const_spec((N_EXPERTS, d)), _const_spec((N_EXPERTS, 1))]
    )
    return pl.pallas_call(
        _mixout_body,
        grid=(n // tm,),
        in_specs=in_specs,
        out_specs=[row(d), row(d), pl.BlockSpec((TOP_K, tm), lambda i: (0, i)), row(LANES)],
        out_shape=[jax.ShapeDtypeStruct((n, d), F32), jax.ShapeDtypeStruct((n, d), BF16),
                   jax.ShapeDtypeStruct((TOP_K, n), I32), jax.ShapeDtypeStruct((n, LANES), F32)],
        compiler_params=_cparams(("arbitrary",)),
        name="mixout",
    )(*o_dil, *lse_dil, o_sb, proj, proj, xf, mod3, g_post.reshape(1, d), g_pre.reshape(1, d), wd_b, ws_b, wm_b,
      w_router.T.astype(BF16), router_bias.reshape(N_EXPERTS, 1))


def _topk_rows(x, k, iota0):
    big = x.shape[0]
    out = []
    for _ in range(k):
        m = jnp.max(x, axis=0, keepdims=True)
        i = jnp.min(jnp.where(x == m, iota0, big), axis=0, keepdims=True)
        out.append((m, i))
        x = jnp.where(iota0 == i, -jnp.inf, x)
    return out


def _route(h, wr, bias):
    tm = h.shape[0]
    logits = lax.dot_general(wr, h, (((1,), (1,)), ((), ())), preferred_element_type=F32)
    scores = jax.nn.sigmoid(logits)
    sel = scores + bias
    sub = lax.broadcasted_iota(I32, (GROUP_SIZE, tm), 0)
    grp = []
    for g in range(N_GROUPS):
        (m1, _), (m2, _) = _topk_rows(sel[g * GROUP_SIZE:(g + 1) * GROUP_SIZE], 2, sub)
        grp.append(m1 + m2)
    gscore = jnp.concatenate(grp, axis=0)
    giota = lax.broadcasted_iota(I32, (N_GROUPS, tm), 0)
    gmask = jnp.zeros((N_GROUPS, tm), F32)
    for _, gi in _topk_rows(gscore, TOPK_GROUPS, giota):
        gmask = jnp.where(giota == gi, 1.0, gmask)
    masked = jnp.concatenate(
        [jnp.where(gmask[g:g + 1] > 0.0, sel[g * GROUP_SIZE:(g + 1) * GROUP_SIZE], -jnp.inf)
         for g in range(N_GROUPS)], axis=0)
    eiota = lax.broadcasted_iota(I32, (N_EXPERTS, tm), 0)
    picks = _topk_rows(masked, TOP_K, eiota)
    idx = jnp.concatenate([i for _, i in picks], axis=0)
    top_s = jnp.concatenate(
        [jnp.sum(jnp.where(eiota == i, scores, 0.0), axis=0, keepdims=True) for _, i in picks], axis=0)
    top_w = top_s / jnp.sum(top_s, axis=0, keepdims=True) * ROUTED_SCALE
    wpad = jnp.concatenate([top_w, jnp.zeros((LANES - TOP_K, tm), F32)], axis=0)
    return idx, wpad.T


def _plan_tile(idx):
    tm = idx.shape[1]
    eiota = lax.broadcasted_iota(I32, (N_EXPERTS, tm), 0)
    hit = jnp.zeros((N_EXPERTS, tm), F32)
    for k in range(TOP_K):
        hit = jnp.where(eiota == idx[k:k + 1], 1.0, hit)
    r = lax.broadcasted_iota(I32, (tm, tm), 0)
    c = lax.broadcasted_iota(I32, (tm, tm), 1)
    before = jnp.where(r < c, 1.0, 0.0).astype(BF16)
    excl = jnp.dot(hit.astype(BF16), before, preferred_element_type=F32)
    tot = excl[:, tm - 1:tm] + hit[:, tm - 1:tm]
    return eiota, hit, excl, tot


def _plan_counts_body(idx_ref, cnt_ref):
    @pl.when(pl.program_id(0) == 0)
    def _():
        cnt_ref[...] = jnp.zeros(cnt_ref.shape, F32)

    _, _, _, tot = _plan_tile(idx_ref[...])
    cnt_ref[...] += tot


def _plan_counts(top_idx):
    n = top_idx.shape[1]
    tm = 512
    return pl.pallas_call(
        _plan_counts_body,
        grid=(n // tm,),
        in_specs=[pl.BlockSpec((TOP_K, tm), lambda i: (0, i))],
        out_specs=pl.BlockSpec((N_EXPERTS, LANES), lambda i: (0, 0)),
        out_shape=jax.ShapeDtypeStruct((N_EXPERTS, LANES), F32),
        compiler_params=_cparams(("arbitrary",)),
        name="plan_counts",
    )(top_idx)


def _plan_dest_body(idx_ref, base_ref, dest_ref, run_ref):
    @pl.when(pl.program_id(0) == 0)
    def _():
        run_ref[...] = base_ref[...]

    idx = idx_ref[...]
    eiota, _, excl, tot = _plan_tile(idx)
    pos = run_ref[:, 0:1] + excl
    rows = [jnp.sum(jnp.where(eiota == idx[k:k + 1], pos, 0.0), axis=0, keepdims=True) for k in range(TOP_K)]
    dest_ref[...] = jnp.concatenate(rows, axis=0).astype(I32)
    run_ref[...] += tot


def _plan_dest(top_idx, pstart):
    n = top_idx.shape[1]
    tm = 512
    base = jnp.broadcast_to(pstart.astype(F32).reshape(N_EXPERTS, 1), (N_EXPERTS, LANES))
    return pl.pallas_call(
        _plan_dest_body,
        grid=(n // tm,),
        in_specs=[pl.BlockSpec((TOP_K, tm), lambda i: (0, i)), pl.BlockSpec((N_EXPERTS, LANES), lambda i: (0, 0))],
        out_specs=pl.BlockSpec((TOP_K, tm), lambda i: (0, i)),
        out_shape=jax.ShapeDtypeStruct((TOP_K, n), I32),
        scratch_shapes=[pltpu.VMEM((N_EXPERTS, LANES), F32)],
        compiler_params=_cparams(("arbitrary",)),
        name="plan_dest",
    )(top_idx, base)


def _capacity(n):
    return (pl.cdiv(n * TOP_K, ROW_BLOCK) + N_EXPERTS) * ROW_BLOCK


def _pack_halves(y):
    w = y.shape[1] // 2
    hi = pltpu.bitcast(y[:, :w].astype(BF16).astype(F32), U32)
    lo = pltpu.bitcast(y[:, w:].astype(BF16).astype(F32), U32)
    return hi | (lo >> 16)


def _unpack_halves(p):
    a = pltpu.bitcast(p & jnp.uint32(0xFFFF0000), F32)
    b = pltpu.bitcast(p << 16, F32)
    return a, b


TOKENS_PER_ISSUE = 4


def _row_dma_loops(row_copy):
    def issue(i, _):
        for u in range(TOKENS_PER_ISSUE):
            for k in range(TOP_K):
                row_copy(i * TOKENS_PER_ISSUE + u, k).start(priority=k % 2)
        return 0

    def drain(i, _):
        for u in range(TOKENS_PER_ISSUE):
            for k in range(TOP_K):
                row_copy(i * TOKENS_PER_ISSUE + u, k).wait()
        return 0

    return issue, drain


def _pad_chunks():
    s = 1
    while 2 * s < ROW_BLOCK:
        s *= 2
    sizes = []
    while s >= 1:
        sizes.append(s)
        s //= 2
    return sizes


def _tile_rows(t):
    start = t * SUBLANES
    return pl.ds(start if isinstance(start, int) else pl.multiple_of(start, SUBLANES), SUBLANES)


def _to_row_tiles(ref, packed, lead=()):
    rows = packed.shape[0]
    for s in range(SUBLANES):
        ref[(*lead, pl.ds(s, rows, stride=SUBLANES), slice(None))] = packed[:, s * LANES:(s + 1) * LANES]


def _from_row_tiles(ref, rows, lead=()):
    return [ref[(*lead, pl.ds(s, rows, stride=SUBLANES), slice(None))] for s in range(SUBLANES)]


def _dispatch_body(pstart_ref, cnt_ref, dest_ref, h_ref, wg, wu, wd, xs_ref, sh_ref, xbuf, zbuf, sem, zsem):
    tm = h_ref.shape[0]
    step = pl.program_id(0)

    def pad_dmas(wait):
        def per_expert(e, _):
            cnt = cnt_ref[e]
            pad = (ROW_BLOCK - cnt % ROW_BLOCK) % ROW_BLOCK
            start = pstart_ref[e] + cnt
            for size in _pad_chunks():
                @pl.when((pad & size) != 0)
                def _():
                    off = pl.multiple_of((start + (pad & ~(2 * size - 1))) * SUBLANES, SUBLANES)
                    cp = pltpu.make_async_copy(zbuf.at[pl.ds(0, size * SUBLANES)],
                                               xs_ref.at[pl.ds(off, size * SUBLANES)], zsem)
                    cp.wait() if wait else cp.start()
            return 0
        lax.fori_loop(0, N_EXPERTS, per_expert, 0)

    @pl.when(step == 0)
    def _():
        zbuf[...] = jnp.zeros(zbuf.shape, zbuf.dtype)
        pad_dmas(False)
        pad_dmas(True)

    h = h_ref[...]
    _to_row_tiles(xbuf, _pack_halves(h.astype(F32)))

    def row_copy(t, k):
        return pltpu.make_async_copy(xbuf.at[_tile_rows(t)], xs_ref.at[_tile_rows(dest_ref[k, t])], sem)

    def start_rows(lo, hi):
        for t in range(lo, hi):
            for k in range(TOP_K):
                row_copy(t, k).start(priority=k % 2)

    q = tm // 4
    g = jnp.dot(h, wg[...], preferred_element_type=F32)
    start_rows(0, q)
    u = jnp.dot(h, wu[...], preferred_element_type=F32)
    start_rows(q, 2 * q)
    hmid = (g * jax.nn.sigmoid(g) * u).astype(BF16)
    start_rows(2 * q, 3 * q)
    sh_ref[...] = jnp.dot(hmid, wd[...], preferred_element_type=F32)
    start_rows(3 * q, tm)

    lax.fori_loop(0, tm // TOKENS_PER_ISSUE, _row_dma_loops(row_copy)[1], 0)


def _dispatch(h2, dest, pstart, counts, wg_b, wu_b, wd_b):
    n, d = h2.shape
    tm = 256
    assert d // 2 == SUBLANES * LANES
    const = lambda shape: pl.BlockSpec(shape, lambda i, *_: (0,) * len(shape), pipeline_mode=pl.Buffered(1))
    grid_spec = pltpu.PrefetchScalarGridSpec(
        num_scalar_prefetch=2,
        grid=(n // tm,),
        in_specs=[
            pl.BlockSpec((TOP_K, tm), lambda i, *_: (0, i), memory_space=pltpu.SMEM),
            pl.BlockSpec((tm, d), lambda i, *_: (i, 0)),
            const(wg_b.shape), const(wu_b.shape), const(wd_b.shape),
        ],
        out_specs=[pl.BlockSpec(memory_space=pl.ANY), pl.BlockSpec((tm, d), lambda i, *_: (i, 0))],
        scratch_shapes=[
            pltpu.VMEM((tm * SUBLANES, LANES), U32),
            pltpu.VMEM((_pad_chunks()[0] * SUBLANES, LANES), U32),
            pltpu.SemaphoreType.DMA(()),
            pltpu.SemaphoreType.DMA(()),
        ],
    )
    return pl.pallas_call(
        _dispatch_body,
        grid_spec=grid_spec,
        out_shape=[jax.ShapeDtypeStruct((_capacity(n) * SUBLANES, LANES), U32), jax.ShapeDtypeStruct((n, d), F32)],
        compiler_params=_cparams(("arbitrary",), has_side_effects=True, disable_bounds_checks=True),
        name="dispatch",
    )(pstart, counts, dest, h2, wg_b, wu_b, wd_b)


def _experts_body(be_ref, nu_ref, nxt_ref, ord_ref, x_ref, wg_hbm, wu_hbm, wd_hbm, y_ref,
                  wg_s, wu_s, wd_s, wg_b, wu_b, wd_b, sems):
    i = pl.program_id(0)
    used = i < nu_ref[0]
    e = be_ref[i]
    fresh = jnp.logical_or(i == 0, e != be_ref[jnp.maximum(i - 1, 0)])
    slot = ord_ref[e] % 2

    def weight_copies(expert, s):
        return (pltpu.make_async_copy(wg_hbm.at[expert], wg_s.at[s], sems.at[s, 0]),
                pltpu.make_async_copy(wu_hbm.at[expert], wu_s.at[s], sems.at[s, 1]),
                pltpu.make_async_copy(wd_hbm.at[expert], wd_s.at[s], sems.at[s, 2]))

    @pl.when(i == 0)
    def _():
        for cp in weight_copies(e, slot):
            cp.start()

    @pl.when(jnp.logical_and(used, fresh))
    def _():
        copies = weight_copies(e, slot)
        for cp, dst, src in zip(copies, (wg_b, wu_b, wd_b), (wg_s, wu_s, wd_s)):
            cp.wait()
            dst[...] = src[slot].astype(BF16)

        @pl.when(nxt_ref[e] != e)
        def _():
            for cp in weight_copies(nxt_ref[e], 1 - slot):
                cp.start()

    @pl.when(used)
    def _():
        half = SUBLANES * LANES
        slabs = [_unpack_halves(p) for p in _from_row_tiles(x_ref, ROW_BLOCK)]
        xa = jnp.concatenate([a.astype(BF16) for a, _ in slabs], axis=1)
        xb = jnp.concatenate([b.astype(BF16) for _, b in slabs], axis=1)
        g = (jnp.dot(xa, wg_b[:half], preferred_element_type=F32)
             + jnp.dot(xb, wg_b[half:], preferred_element_type=F32))
        u = (jnp.dot(xa, wu_b[:half], preferred_element_type=F32)
             + jnp.dot(xb, wu_b[half:], preferred_element_type=F32))
        hmid = (g * jax.nn.sigmoid(g) * u).astype(BF16)
        _to_row_tiles(y_ref, _pack_halves(jnp.dot(hmid, wd_b[...], preferred_element_type=F32)))


def _experts(xs, block_e, n_used, next_e, ord_e, w_gate_e, w_up_e, w_down_e):
    cap = xs.shape[0] // SUBLANES
    d = 2 * SUBLANES * LANES
    nblk = cap // ROW_BLOCK
    row_block = (ROW_BLOCK * SUBLANES, LANES)

    def row_map(i, be, nu, nxt, od):
        return (jnp.minimum(i, nu[0] - 1), 0)

    grid_spec = pltpu.PrefetchScalarGridSpec(
        num_scalar_prefetch=4,
        grid=(nblk,),
        in_specs=[
            pl.BlockSpec(row_block, row_map),
            pl.BlockSpec(memory_space=pl.ANY),
            pl.BlockSpec(memory_space=pl.ANY),
            pl.BlockSpec(memory_space=pl.ANY),
        ],
        out_specs=pl.BlockSpec(row_block, row_map),
        scratch_shapes=[
            pltpu.VMEM((2, d, D_EXPERT), F32),
            pltpu.VMEM((2, d, D_EXPERT), F32),
            pltpu.VMEM((2, D_EXPERT, d), F32),
            pltpu.VMEM((d, D_EXPERT), BF16),
            pltpu.VMEM((d, D_EXPERT), BF16),
            pltpu.VMEM((D_EXPERT, d), BF16),
            pltpu.SemaphoreType.DMA((2, 3)),
        ],
    )
    return pl.pallas_call(
        _experts_body,
        grid_spec=grid_spec,
        out_shape=jax.ShapeDtypeStruct(xs.shape, U32),
        compiler_params=_cparams(("arbitrary",)),
        name="experts",
    )(block_e, n_used, next_e, ord_e, xs, w_gate_e, w_up_e, w_down_e)


COMBINE_TILE = 256


def _combine_body(dest_ref, dest_next_ref, ys_ref, wt_ref, sh_ref, x1_ref, mod_ref, gpost, o_ref,
                  buf0, buf1, sem0, sem1):
    tile = COMBINE_TILE
    i = pl.program_id(0)
    bufs, sems = (buf0, buf1), (sem0, sem1)
    half = SUBLANES * LANES
    chunk = tile // SUBLANES

    def copies(dref, col0, slot):
        def row_copy(t, k):
            return pltpu.make_async_copy(ys_ref.at[_tile_rows(dref[k, col0 + t])],
                                         bufs[slot].at[k, _tile_rows(t)], sems[slot])
        return row_copy

    def run(loop_body):
        lax.fori_loop(0, tile // TOKENS_PER_ISSUE, loop_body, 0)

    @pl.when(i == 0)
    def _():
        run(_row_dma_loops(copies(dest_ref, 0, 0))[0])

    def phase(row0, slot, next_copy):
        run(_row_dma_loops(copies(dest_ref, row0, slot))[1])
        rows = slice(row0, row0 + tile)
        shared = sh_ref[rows, :]
        wt = wt_ref[rows, :]
        wk = [jnp.broadcast_to(wt[:, k:k + 1], (tile, LANES)) for k in range(TOP_K)]
        ya, yb = [], []
        for s in range(SUBLANES):
            for t in range(s * chunk, (s + 1) * chunk):
                for k in range(TOP_K):
                    next_copy(t, k).start(priority=k % 2)
            sa = shared[:, s * LANES:(s + 1) * LANES]
            sb = shared[:, half + s * LANES:half + (s + 1) * LANES]
            for k in range(TOP_K):
                a, b = _unpack_halves(bufs[slot][k, pl.ds(s, tile, stride=SUBLANES), :])
                sa = sa + wk[k] * a
                sb = sb + wk[k] * b
            ya.append(sa)
            yb.append(sb)
        y = jnp.concatenate(ya + yb, axis=1)
        gate2 = mod_ref[0, 5:6, :]
        o_ref[rows, :] = x1_ref[rows, :] + gate2 * (_rms(y) * gpost[...])

    phase(0, 0, copies(dest_ref, tile, 1))
    phase(tile, 1, copies(dest_next_ref, 0, 0))

    @pl.when(i == pl.num_programs(0) - 1)
    def _():
        run(_row_dma_loops(copies(dest_next_ref, 0, 0))[1])


def _combine(ys, dest, wt, shared, x1, mod3, g_post, seq):
    n, d = x1.shape
    tm = 2 * COMBINE_TILE
    nstep = n // tm
    per_b = seq // tm
    const = lambda shape: pl.BlockSpec(shape, lambda i, *_: (0,) * len(shape), pipeline_mode=pl.Buffered(1))
    buf = pltpu.VMEM((TOP_K, COMBINE_TILE * SUBLANES, LANES), U32)
    grid_spec = pltpu.PrefetchScalarGridSpec(
        num_scalar_prefetch=0,
        grid=(nstep,),
        in_specs=[
            pl.BlockSpec((TOP_K, tm), lambda i: (0, i), memory_space=pltpu.SMEM),
            pl.BlockSpec((TOP_K, tm), lambda i: (0, jnp.minimum(i + 1, nstep - 1)), memory_space=pltpu.SMEM),
            pl.BlockSpec(memory_space=pl.ANY),
            pl.BlockSpec((tm, LANES), lambda i: (i, 0)),
            pl.BlockSpec((tm, d), lambda i: (i, 0)),
            pl.BlockSpec((tm, d), lambda i: (i, 0)),
            pl.BlockSpec((1, N_MOD, d), lambda i: (i // per_b, 0, 0)),
            const((1, d)),
        ],
        out_specs=pl.BlockSpec((tm, d), lambda i: (i, 0)),
        scratch_shapes=[buf, buf, pltpu.SemaphoreType.DMA(()), pltpu.SemaphoreType.DMA(())],
    )
    return pl.pallas_call(
        _combine_body,
        grid_spec=grid_spec,
        out_shape=jax.ShapeDtypeStruct((n, d), F32),
        compiler_params=_cparams(("arbitrary",), disable_bounds_checks=True),
        name="combine",
    )(dest, dest, ys, wt, shared, x1, mod3, g_post.reshape(1, d))


def _layer(x, mod, positions, g_pre_mix, g_post_mix, g_pre_ffn, g_post_ffn, w_in, w_dil_out, w_sb_out, w_mix_out,
           w_router, router_bias, w_gate_e, w_up_e, w_down_e, w_gate_s, w_up_s, w_down_s):
    bsz, seq, d = x.shape
    n = bsz * seq
    xf = x.reshape(n, d)
    mod3 = mod.reshape(bsz, N_MOD, d)

    nd, nq = 3 * WIDTH_DIL, 3 * (WIDTH_DIL + WIDTH_SB)
    w_plain = jnp.concatenate([w_in[:, nq:], w_in[:, nd:nq]], axis=1).astype(BF16)
    cols = []
    for g in range(len(DIL_PATTERNS)):
        for part in range(3):
            lo = part * WIDTH_DIL + g * D_DIL_OUT
            cols.append(w_in[:, lo:lo + D_DIL_OUT])
    w_dil = jnp.concatenate(cols, axis=1).astype(BF16)

    proj = _inproj(xf, g_pre_mix, mod3, w_plain, seq)
    tables = _rope_tables(positions)
    qkv_dil = _inproj_dil(xf, g_pre_mix, mod3, w_dil, tables, bsz, seq)
    o_dil, lse_dil = [], []
    for g, (window, dilation) in enumerate(DIL_PATTERNS):
        assert window // dilation == Q_BLOCK
        o, lse = _dilated_attention(qkv_dil[g], g, bsz, seq)
        o_dil.append(o)
        lse_dil.append(lse)
    o_sb = _stick_breaking(proj, bsz, seq)

    x1, h2, top_idx, top_wt = _mixout(o_dil, lse_dil, o_sb, proj, xf, mod3, g_post_mix, g_pre_ffn,
                                      w_dil_out.astype(BF16), w_sb_out.astype(BF16), w_mix_out.astype(BF16),
                                      w_router, router_bias, seq)
    counts = _plan_counts(top_idx)[:, 0].astype(I32)
    padded = (counts + ROW_BLOCK - 1) // ROW_BLOCK * ROW_BLOCK
    pends = jnp.cumsum(padded)
    pstart = pends - padded
    nblk = _capacity(n) // ROW_BLOCK
    n_used = (pends[-1] // ROW_BLOCK).astype(I32)
    blk = jnp.minimum(jnp.arange(nblk, dtype=I32), n_used - 1)
    block_e = jnp.minimum(jnp.sum(pends[None, :] <= (blk * ROW_BLOCK)[:, None], axis=1), N_EXPERTS - 1).astype(I32)
    eids = jnp.arange(N_EXPERTS, dtype=I32)
    has = counts > 0
    ord_e = (jnp.cumsum(has) - has).astype(I32)
    later = jnp.where((eids[None, :] > eids[:, None]) & has[None, :], eids[None, :], N_EXPERTS)
    nxt = jnp.min(later, axis=1)
    next_e = jnp.where(nxt < N_EXPERTS, nxt, eids).astype(I32)
    dest = _plan_dest(top_idx, pstart)

    xs, shared = _dispatch(h2, dest, pstart.astype(I32), counts,
                           w_gate_s.astype(BF16), w_up_s.astype(BF16), w_down_s.astype(BF16))
    ys = _experts(xs, block_e, n_used.reshape(1), next_e, ord_e, w_gate_e, w_up_e, w_down_e)
    out = _combine(ys, dest, top_wt, shared, x1, mod3, g_post_ffn, seq)
    return out.reshape(bsz, seq, d)


def kernel(x, c, positions, w_ada, b_ada, g_pre_mix, g_post_mix, g_pre_ffn, g_post_ffn, w_in, w_dil_out,
           w_sb_out, w_mix_out, w_router, router_bias, w_gate_e, w_up_e, w_down_e, w_gate_s, w_up_s, w_down_s):
    for l in range(w_ada.shape[0]):
        mod = _adaln(c, w_ada[l], b_ada[l])
        x = _layer(x, mod, positions, g_pre_mix[l], g_post_mix[l], g_pre_ffn[l], g_post_ffn[l], w_in[l],
                   w_dil_out[l], w_sb_out[l], w_mix_out[l], w_router[l], router_bias[l],
                   w_gate_e[l], w_up_e[l], w_down_e[l], w_gate_s[l], w_up_s[l], w_down_s[l])
    return x
```

```python
import functools

import jax
import jax.numpy as jnp
from jax import lax
from jax.experimental import pallas as pl
from jax.experimental.pallas import tpu as pltpu

F32 = jnp.float32
BF16 = jnp.bfloat16
I32 = jnp.int32
U32 = jnp.uint32

D_MODEL = 2048
HEAD_DIM = 128
DIL_PATTERNS = ((128, 1), (512, 4), (2048, 16))
HEADS_PER_GROUP = 4
N_HEADS_DIL = 12
N_HEADS_SB = 8
WIDTH_DIL = N_HEADS_DIL * HEAD_DIM
WIDTH_SB = N_HEADS_SB * HEAD_DIM
D_DIL_OUT = HEADS_PER_GROUP * HEAD_DIM
Q_BLOCK = 128
ROPE_THETA = 500000.0
ROPE_DIM = HEAD_DIM // 4
N_GATE = 2 * D_MODEL
N_EXPERTS = 64
TOP_K = 8
N_GROUPS = 8
GROUP_SIZE = N_EXPERTS // N_GROUPS
TOPK_GROUPS = 4
D_EXPERT = 512
D_SHARED = 512
ROUTED_SCALE = 2.5
RMS_EPS = 1e-6
N_MOD = 6
ATTN_SCALE = HEAD_DIM ** -0.5

LANES = 128
SUBLANES = 8
VMEM_LIMIT = 56 * 1024 * 1024

ROW_BLOCK = 704
SB_DEAD = -110.0


def _cparams(sem, **kw):
    return pltpu.CompilerParams(dimension_semantics=sem, vmem_limit_bytes=VMEM_LIMIT, **kw)


def _adaln_body(ct_ref, w_ref, b_ref, o_ref, *, kc):
    nb = ct_ref.shape[1]
    nk = w_ref.shape[0] // kc

    def step(i, acc):
        k0 = pl.multiple_of(i * kc, kc)
        w = w_ref[pl.ds(k0, kc), :]
        c = ct_ref[pl.ds(k0, kc), :]
        s = c * jax.nn.sigmoid(c)
        parts = [jnp.sum(w * s[:, b:b + 1], axis=0, keepdims=True) for b in range(nb)]
        return acc + jnp.concatenate(parts, axis=0)

    acc = lax.fori_loop(0, nk, step, jnp.zeros(o_ref.shape, F32))
    o_ref[...] = acc + b_ref[...]


def _adaln(c, w_ada, b_ada):
    nb, d = c.shape
    n_out = w_ada.shape[1]
    tn = 1024
    return pl.pallas_call(
        functools.partial(_adaln_body, kc=256),
        grid=(n_out // tn,),
        in_specs=[
            pl.BlockSpec((d, nb), lambda j: (0, 0)),
            pl.BlockSpec((d, tn), lambda j: (0, j)),
            pl.BlockSpec((1, tn), lambda j: (0, j)),
        ],
        out_specs=pl.BlockSpec((nb, tn), lambda j: (0, j)),
        out_shape=jax.ShapeDtypeStruct((nb, n_out), F32),
        compiler_params=_cparams(("arbitrary",)),
        name="adaln",
    )(c.T, w_ada, b_ada.reshape(1, n_out))


def _rms(x):
    return x * lax.rsqrt(jnp.mean(x * x, axis=-1, keepdims=True) + RMS_EPS)


def _prenorm(x_ref, g_ref, mod_ref):
    y = _rms(x_ref[...]) * g_ref[...]
    shift = mod_ref[0, 0:1, :]
    scale = mod_ref[0, 1:2, :]
    return (y * (1.0 + scale) + shift).astype(BF16)


def _inproj_body(x_ref, g_ref, mod_ref, w_ref, o_ref, h_ref):
    @pl.when(pl.program_id(1) == 0)
    def _():
        h_ref[...] = _prenorm(x_ref, g_ref, mod_ref)

    o_ref[...] = jnp.dot(h_ref[...], w_ref[...], preferred_element_type=F32).astype(o_ref.dtype)


def _inproj(xf, g_pre, mod3, w_b, seq):
    n, d = xf.shape
    width = w_b.shape[1]
    tm, tn = 1024, width // 4
    per_b = seq // tm
    return pl.pallas_call(
        _inproj_body,
        grid=(n // tm, width // tn),
        in_specs=[
            pl.BlockSpec((tm, d), lambda i, j: (i, 0)),
            pl.BlockSpec((1, d), lambda i, j: (0, 0)),
            pl.BlockSpec((1, N_MOD, d), lambda i, j: (i // per_b, 0, 0)),
            pl.BlockSpec((d, tn), lambda i, j: (0, j)),
        ],
        out_specs=pl.BlockSpec((tm, tn), lambda i, j: (i, j)),
        out_shape=jax.ShapeDtypeStruct((n, width), BF16),
        scratch_shapes=[pltpu.VMEM((tm, d), BF16)],
        compiler_params=_cparams(("arbitrary", "arbitrary")),
        name="inproj",
    )(xf, g_pre.reshape(1, d), mod3, w_b)


def _inproj_dil_body(x_ref, g_ref, mod_ref, w_ref, t_ref, o0, o1, o2, res_ref):
    tm = x_ref.shape[0]
    h = _prenorm(x_ref, g_ref, mod_ref)
    t = t_ref[...]
    gw = 3 * D_DIL_OUT
    for gi, o_ref in enumerate((o0, o1, o2)):
        dil = DIL_PATTERNS[gi][1]
        res = jnp.dot(h, w_ref[:, gi * gw:(gi + 1) * gw], preferred_element_type=F32)
        for hs in range(3 * HEADS_PER_GROUP):
            sl = slice(hs * HEAD_DIM, (hs + 1) * HEAD_DIM)
            res_ref[hs] = _apply_rope(res[:, sl], t) if hs < 2 * HEADS_PER_GROUP else res[:, sl]
        for r in range(dil):
            for hs in range(3 * HEADS_PER_GROUP):
                rows = res_ref[hs] if dil == 1 else res_ref[hs, pl.ds(r, tm // dil, stride=dil), :]
                o_ref[0, r, :, hs * HEAD_DIM:(hs + 1) * HEAD_DIM] = rows.astype(o_ref.dtype)


def _inproj_dil(xf, g_pre, mod3, w_b, tables, bsz, seq):
    n, d = xf.shape
    gw = 3 * D_DIL_OUT
    tm = 512
    per_b = seq // tm
    dils = [p[1] for p in DIL_PATTERNS]
    return pl.pallas_call(
        _inproj_dil_body,
        grid=(n // tm,),
        in_specs=[
            pl.BlockSpec((tm, d), lambda i: (i, 0)),
            pl.BlockSpec((1, d), lambda i: (0, 0)),
            pl.BlockSpec((1, N_MOD, d), lambda i: (i // per_b, 0, 0)),
            pl.BlockSpec(w_b.shape, lambda i: (0, 0), pipeline_mode=pl.Buffered(1)),
            pl.BlockSpec((tm, 3 * LANES), lambda i: (i, 0)),
        ],
        out_specs=[pl.BlockSpec((1, dl, tm // dl, gw), lambda i: (i // per_b, 0, i % per_b, 0)) for dl in dils],
        out_shape=[jax.ShapeDtypeStruct((bsz, dl, seq // dl, gw), BF16) for dl in dils],
        scratch_shapes=[pltpu.VMEM((gw // HEAD_DIM, tm, HEAD_DIM), F32)],
        compiler_params=_cparams(("arbitrary",)),
        name="inproj_dil",
    )(xf, g_pre.reshape(1, d), mod3, w_b, tables)


def _rope_body(pos_ref, f_ref, o_ref):
    ang = pos_ref[...].astype(F32) * f_ref[...]
    c = jnp.cos(ang)
    s = jnp.sin(ang)
    lane = lax.broadcasted_iota(I32, ang.shape, 1)
    half = ROPE_DIM // 2
    o_ref[:, 0:LANES] = c
    o_ref[:, LANES:2 * LANES] = jnp.where(lane >= half, s, 0.0)
    o_ref[:, 2 * LANES:3 * LANES] = jnp.where(lane < half, -s, 0.0)


def _rope_tables(positions):
    n = positions.size
    half = ROPE_DIM // 2
    inv_freq = ROPE_THETA ** (-jnp.arange(0, ROPE_DIM, 2, dtype=F32) / ROPE_DIM)
    f = jnp.concatenate([inv_freq, inv_freq, jnp.zeros((LANES - 2 * half,), F32)]).reshape(1, LANES)
    tm = 2048
    return pl.pallas_call(
        _rope_body,
        grid=(n // tm,),
        in_specs=[pl.BlockSpec((tm, 1), lambda i: (i, 0)), pl.BlockSpec((1, LANES), lambda i: (0, 0))],
        out_specs=pl.BlockSpec((tm, 3 * LANES), lambda i: (i, 0)),
        out_shape=jax.ShapeDtypeStruct((n, 3 * LANES), F32),
        compiler_params=_cparams(("arbitrary",)),
        name="rope_tables",
    )(positions.reshape(n, 1), f)


def _apply_rope(x, t):
    half = ROPE_DIM // 2
    return (x * t[:, 0:LANES]
            + pltpu.roll(x, half, 1) * t[:, LANES:2 * LANES]
            + pltpu.roll(x, LANES - half, 1) * t[:, 2 * LANES:3 * LANES])


def _dil_body(cur_ref, kp_ref, vp_ref, o_ref, lse_ref, obuf, lbuf, *, dil, nsub):
    n = pl.program_id(1)
    tq = nsub * Q_BLOCK
    row = lax.broadcasted_iota(I32, (Q_BLOCK, 2 * Q_BLOCK), 0)
    col = lax.broadcasted_iota(I32, (Q_BLOCK, 2 * Q_BLOCK), 1)
    rel = row + Q_BLOCK - col
    band = jnp.where(rel >= 0, jnp.where(rel <= Q_BLOCK, 1.0, 0.0), 0.0)
    first = jnp.where(col >= Q_BLOCK, band, jnp.where(n > 0, band, 0.0))
    for r in range(dil):
        for h in range(HEADS_PER_GROUP):
            sl = slice(h * HEAD_DIM, (h + 1) * HEAD_DIM)
            ksl = slice(D_DIL_OUT + h * HEAD_DIM, D_DIL_OUT + (h + 1) * HEAD_DIM)
            vsl = slice(2 * D_DIL_OUT + h * HEAD_DIM, 2 * D_DIL_OUT + (h + 1) * HEAD_DIM)
            for j in range(nsub):
                rs = slice(j * Q_BLOCK, (j + 1) * Q_BLOCK)
                ps = slice((j - 1) * Q_BLOCK, j * Q_BLOCK)
                kprev = kp_ref[0, r, :, sl] if j == 0 else cur_ref[0, r, ps, ksl]
                vprev = vp_ref[0, r, :, sl] if j == 0 else cur_ref[0, r, ps, vsl]
                kcat = jnp.concatenate([kprev, cur_ref[0, r, rs, ksl]], axis=0)
                vcat = jnp.concatenate([vprev, cur_ref[0, r, rs, vsl]], axis=0)
                s = lax.dot_general(cur_ref[0, r, rs, sl], kcat, (((1,), (1,)), ((), ())),
                                    preferred_element_type=F32) * ATTN_SCALE
                s = jnp.where((first if j == 0 else band) > 0.0, s, -jnp.inf)
                m = jnp.max(s, axis=-1, keepdims=True)
                p = jnp.exp(s - m)
                l = jnp.sum(p, axis=-1, keepdims=True)
                o = jnp.dot((p / l).astype(BF16), vcat, preferred_element_type=F32)
                lse = jnp.broadcast_to(m + jnp.log(l), (Q_BLOCK, HEAD_DIM))
                if dil == 1:
                    o_ref[h, rs, :] = o
                    lse_ref[h, rs, :] = lse
                else:
                    obuf[h, rs, :] = o
                    lbuf[h, rs, :] = lse
            if dil > 1:
                o_ref[h, pl.ds(r, tq, stride=dil), :] = obuf[h]
                lse_ref[h, pl.ds(r, tq, stride=dil), :] = lbuf[h]


def _dilated_attention(qkv, g, bsz, seq):
    dil = DIL_PATTERNS[g][1]
    length = seq // dil
    tq = min(4 * Q_BLOCK, (16 * Q_BLOCK) // dil, length)
    nsub = tq // Q_BLOCK
    nq = length // tq
    gw = 3 * D_DIL_OUT
    n = bsz * seq

    def prev(colblk):
        return pl.BlockSpec((1, dil, Q_BLOCK, D_DIL_OUT),
                            lambda b, i: (b, 0, jnp.maximum(i * nsub - 1, 0), colblk))

    nh = HEADS_PER_GROUP
    out_spec = pl.BlockSpec((nh, tq * dil, HEAD_DIM), lambda b, i: (0, b * nq + i, 0))
    out_shape = jax.ShapeDtypeStruct((nh, n, HEAD_DIM), F32)
    return pl.pallas_call(
        functools.partial(_dil_body, dil=dil, nsub=nsub),
        grid=(bsz, nq),
        in_specs=[pl.BlockSpec((1, dil, tq, gw), lambda b, i: (b, 0, i, 0)), prev(1), prev(2)],
        out_specs=[out_spec, out_spec],
        out_shape=[out_shape, out_shape],
        scratch_shapes=[pltpu.VMEM((nh, tq, HEAD_DIM), F32), pltpu.VMEM((nh, tq, HEAD_DIM), F32)],
        compiler_params=_cparams(("arbitrary", "arbitrary")),
        name=f"dilated_d{dil}",
    )(qkv, qkv, qkv)


def _sb_body(q_ref, k_ref, v_ref, o_ref, acc_ref, car_ref):
    nblk = q_ref.shape[1] // Q_BLOCK
    r = lax.broadcasted_iota(I32, (Q_BLOCK, Q_BLOCK), 0)
    c = lax.broadcasted_iota(I32, (Q_BLOCK, Q_BLOCK), 1)
    causal = c < r
    rr = lax.broadcasted_iota(I32, (Q_BLOCK, 2 * Q_BLOCK), 0)
    cc = lax.broadcasted_iota(I32, (Q_BLOCK, 2 * Q_BLOCK), 1)
    uo = jnp.where(cc >= Q_BLOCK, 1.0, jnp.where(rr > cc, 1.0, 0.0)).astype(BF16)

    nh = acc_ref.shape[0]
    heads = range(nh)

    hs = [slice(h * HEAD_DIM, (h + 1) * HEAD_DIM) for h in heads]

    def key_rows(kb):
        start = kb * Q_BLOCK
        return pl.ds(start if isinstance(start, int) else pl.multiple_of(start, Q_BLOCK), Q_BLOCK)

    def scores(qs, kb):
        return [lax.dot_general(qs[h], k_ref[0, key_rows(kb), hs[h]], (((1,), (1,)), ((), ())),
                                preferred_element_type=F32) * ATTN_SCALE for h in heads]

    def log_terms(zs, diag):
        stacked, log_s = [], []
        for z in zs:
            sp = jnp.log(1.0 + jnp.exp(-jnp.abs(z)))
            mx = jnp.maximum(z, 0.0)
            log_1m = -(mx + sp)
            if diag:
                log_1m = jnp.where(causal, log_1m, 0.0)
            hi = log_1m.astype(BF16)
            lo = (log_1m - hi.astype(F32)).astype(BF16)
            stacked.append(jnp.concatenate([hi, lo], axis=0))
            log_s.append((z - mx) - sp)
        return stacked, log_s

    def suffix_sums(stacked):
        r2s = [jnp.dot(s, uo, preferred_element_type=F32) for s in stacked]
        return [r2[:Q_BLOCK] + r2[Q_BLOCK:] for r2 in r2s]

    def weights(log_s, carries, sums, diag):
        probs, new_carries = [], []
        for h in heads:
            a = jnp.exp(log_s[h] + carries[h] + sums[h][:, :Q_BLOCK])
            if diag:
                a = jnp.where(causal, a, 0.0)
            probs.append(a.astype(BF16))
            new_carries.append(carries[h] + sums[h][:, Q_BLOCK:])
        return probs, new_carries

    def values(probs, kb):
        return [jnp.dot(probs[h], v_ref[0, key_rows(kb), hs[h]], preferred_element_type=F32) for h in heads]

    def all_max(xs):
        m = xs[0]
        for x in xs[1:]:
            m = jnp.maximum(m, x)
        return jnp.max(m)

    def qblock(qi, with_prev):
        qs = [q_ref[0, key_rows(qi), hs[h]] for h in heads]
        zero = jnp.zeros((Q_BLOCK, Q_BLOCK), F32)
        zs_d = scores(qs, qi)
        zs_p = scores(qs, qi - 1) if with_prev else None
        st_d, ls_d = log_terms(zs_d, True)
        st_p, ls_p = log_terms(zs_p, False) if with_prev else (None, None)
        sm_d = suffix_sums(st_d)
        sm_p = suffix_sums(st_p) if with_prev else None
        pr_d, cars = weights(ls_d, [zero] * nh, sm_d, True)
        pvs = values(pr_d, qi)
        if with_prev:
            pr_p, cars = weights(ls_p, cars, sm_p, False)
            pvs = [a + b for a, b in zip(pvs, values(pr_p, qi - 1))]
        for h in heads:
            acc_ref[h] = pvs[h]
            car_ref[h] = cars[h]

        def cond(st):
            return jnp.logical_and(st[0] >= 0, st[1] > SB_DEAD)

        def body(st):
            stacked, log_s = log_terms(scores(qs, st[0]), False)
            probs, cars = weights(log_s, [car_ref[h] for h in heads], suffix_sums(stacked), False)
            pvs = values(probs, st[0])
            for h in heads:
                acc_ref[h] += pvs[h]
                car_ref[h] = cars[h]
            return st[0] - 1, all_max(cars)

        lax.while_loop(cond, body, (qi - (2 if with_prev else 1), all_max(cars)))
        for h in heads:
            o_ref[0, key_rows(qi), hs[h]] = acc_ref[h].astype(o_ref.dtype)
        return 0

    qblock(0, False)
    lax.fori_loop(1, nblk, lambda qi, c: qblock(qi, True), 0)


SB_HEADS_PER_STEP = 8


def _stick_breaking(proj, bsz, seq):
    width = proj.shape[1]
    pv = proj.reshape(bsz, seq, width)
    nh = SB_HEADS_PER_STEP
    bw = nh * HEAD_DIM
    base = N_GATE // bw

    def spec(off):
        return pl.BlockSpec((1, seq, bw), lambda b, h: (b, 0, base + off + h), pipeline_mode=pl.Buffered(1))

    nstep = N_HEADS_SB // nh
    o = pl.pallas_call(
        _sb_body,
        grid=(bsz, nstep),
        in_specs=[spec(0), spec(nstep), spec(2 * nstep)],
        out_specs=pl.BlockSpec((1, seq, bw), lambda b, h: (b, 0, h)),
        out_shape=jax.ShapeDtypeStruct((bsz, seq, WIDTH_SB), BF16),
        scratch_shapes=[pltpu.VMEM((nh, Q_BLOCK, Q_BLOCK), F32), pltpu.VMEM((nh, Q_BLOCK, Q_BLOCK), F32)],
        compiler_params=_cparams(("arbitrary", "arbitrary")),
        name="stick_breaking",
    )(pv, pv, pv)
    return o.reshape(bsz * seq, WIDTH_SB)


def _mixout_body(o1, o2, o3, l1, l2, l3, osb, gd_ref, gs_ref, x_ref, mod_ref, gpost, gpre,
                 wd, ws, wm, x1_ref, h2_ref):
    heads = []
    for h in range(HEADS_PER_GROUP):
        la, lb, lc = l1[h], l2[h], l3[h]
        m = jnp.maximum(la, jnp.maximum(lb, lc))
        ea, eb, ec = jnp.exp(la - m), jnp.exp(lb - m), jnp.exp(lc - m)
        heads.append(((ea * o1[h] + eb * o2[h] + ec * o3[h]) / (ea + eb + ec)).astype(BF16))
    yd = jnp.dot(jnp.concatenate(heads, axis=1), wd[...], preferred_element_type=F32)
    ys = jnp.dot(osb[...], ws[...], preferred_element_type=F32)
    mix = jax.nn.sigmoid(gd_ref[...].astype(F32)) * yd + jax.nn.sigmoid(gs_ref[...].astype(F32)) * ys
    y = jnp.dot(mix.astype(BF16), wm[...], preferred_element_type=F32)
    gate1 = mod_ref[0, 2:3, :]
    shift2 = mod_ref[0, 3:4, :]
    scale2 = mod_ref[0, 4:5, :]
    x1 = x_ref[...] + gate1 * (_rms(y) * gpost[...])
    x1_ref[...] = x1
    h2_ref[...] = ((_rms(x1) * gpre[...]) * (1.0 + scale2) + shift2).astype(h2_ref.dtype)


def _const_spec(shape):
    return pl.BlockSpec(shape, lambda i: (0,) * len(shape), pipeline_mode=pl.Buffered(1))


def _mixout(o_dil, lse_dil, o_sb, proj, xf, mod3, g_post, g_pre, wd_b, ws_b, wm_b, seq):
    n, d = xf.shape
    tm = 256
    per_b = seq // tm
    row = lambda w: pl.BlockSpec((tm, w), lambda i: (i, 0))
    head_major = pl.BlockSpec((HEADS_PER_GROUP, tm, HEAD_DIM), lambda i: (0, i, 0))
    in_specs = (
        [head_major] * 6 + [row(WIDTH_SB)]
        + [pl.BlockSpec((tm, d), lambda i: (i, 0)), pl.BlockSpec((tm, d), lambda i: (i, 1))]
        + [row(d), pl.BlockSpec((1, N_MOD, d), lambda i: (i // per_b, 0, 0))]
        + [_const_spec((1, d)), _const_spec((1, d))]
        + [_const_spec(wd_b.shape), _const_spec(ws_b.shape), _const_spec(wm_b.shape)]
    )
    return pl.pallas_call(
        _mixout_body,
        grid=(n // tm,),
        in_specs=in_specs,
        out_specs=[row(d), row(d)],
        out_shape=[jax.ShapeDtypeStruct((n, d), F32), jax.ShapeDtypeStruct((n, d), BF16)],
        compiler_params=_cparams(("arbitrary",)),
        name="mixout",
    )(*o_dil, *lse_dil, o_sb, proj, proj, xf, mod3, g_post.reshape(1, d), g_pre.reshape(1, d), wd_b, ws_b, wm_b)


def _topk_rows(x, k, iota0):
    big = x.shape[0]
    out = []
    for _ in range(k):
        m = jnp.max(x, axis=0, keepdims=True)
        i = jnp.min(jnp.where(x == m, iota0, big), axis=0, keepdims=True)
        out.append((m, i))
        x = jnp.where(iota0 == i, -jnp.inf, x)
    return out


def _router_body(h_ref, wr_ref, bias_ref, idx_ref, wt_ref):
    tm = h_ref.shape[0]
    logits = lax.dot_general(wr_ref[...], h_ref[...], (((1,), (1,)), ((), ())), preferred_element_type=F32)
    scores = jax.nn.sigmoid(logits)
    sel = scores + bias_ref[...]
    sub = lax.broadcasted_iota(I32, (GROUP_SIZE, tm), 0)
    grp = []
    for g in range(N_GROUPS):
        (m1, _), (m2, _) = _topk_rows(sel[g * GROUP_SIZE:(g + 1) * GROUP_SIZE], 2, sub)
        grp.append(m1 + m2)
    gscore = jnp.concatenate(grp, axis=0)
    giota = lax.broadcasted_iota(I32, (N_GROUPS, tm), 0)
    gmask = jnp.zeros((N_GROUPS, tm), F32)
    for _, gi in _topk_rows(gscore, TOPK_GROUPS, giota):
        gmask = jnp.where(giota == gi, 1.0, gmask)
    masked = jnp.concatenate(
        [jnp.where(gmask[g:g + 1] > 0.0, sel[g * GROUP_SIZE:(g + 1) * GROUP_SIZE], -jnp.inf)
         for g in range(N_GROUPS)], axis=0)
    eiota = lax.broadcasted_iota(I32, (N_EXPERTS, tm), 0)
    picks = _topk_rows(masked, TOP_K, eiota)
    idx = jnp.concatenate([i for _, i in picks], axis=0)
    top_s = jnp.concatenate(
        [jnp.sum(jnp.where(eiota == i, scores, 0.0), axis=0, keepdims=True) for _, i in picks], axis=0)
    top_w = top_s / jnp.sum(top_s, axis=0, keepdims=True) * ROUTED_SCALE
    idx_ref[...] = idx
    wpad = jnp.concatenate([top_w, jnp.zeros((LANES - TOP_K, tm), F32)], axis=0)
    wt_ref[...] = wpad.T


def _router(h2, w_router, router_bias):
    n, d = h2.shape
    tm = 512
    return pl.pallas_call(
        _router_body,
        grid=(n // tm,),
        in_specs=[
            pl.BlockSpec((tm, d), lambda i: (i, 0)),
            pl.BlockSpec((N_EXPERTS, d), lambda i: (0, 0)),
            pl.BlockSpec((N_EXPERTS, 1), lambda i: (0, 0)),
        ],
        out_specs=[pl.BlockSpec((TOP_K, tm), lambda i: (0, i)), pl.BlockSpec((tm, LANES), lambda i: (i, 0))],
        out_shape=[jax.ShapeDtypeStruct((TOP_K, n), I32), jax.ShapeDtypeStruct((n, LANES), F32)],
        compiler_params=_cparams(("arbitrary",)),
        name="router",
    )(h2, w_router.T.astype(BF16), router_bias.reshape(N_EXPERTS, 1))


def _plan_tile(idx):
    tm = idx.shape[1]
    eiota = lax.broadcasted_iota(I32, (N_EXPERTS, tm), 0)
    hit = jnp.zeros((N_EXPERTS, tm), F32)
    for k in range(TOP_K):
        hit = jnp.where(eiota == idx[k:k + 1], 1.0, hit)
    r = lax.broadcasted_iota(I32, (tm, tm), 0)
    c = lax.broadcasted_iota(I32, (tm, tm), 1)
    before = jnp.where(r < c, 1.0, 0.0).astype(BF16)
    excl = jnp.dot(hit.astype(BF16), before, preferred_element_type=F32)
    tot = excl[:, tm - 1:tm] + hit[:, tm - 1:tm]
    return eiota, hit, excl, tot


def _plan_counts_body(idx_ref, cnt_ref):
    @pl.when(pl.program_id(0) == 0)
    def _():
        cnt_ref[...] = jnp.zeros(cnt_ref.shape, F32)

    _, _, _, tot = _plan_tile(idx_ref[...])
    cnt_ref[...] += tot


def _plan_counts(top_idx):
    n = top_idx.shape[1]
    tm = 512
    return pl.pallas_call(
        _plan_counts_body,
        grid=(n // tm,),
        in_specs=[pl.BlockSpec((TOP_K, tm), lambda i: (0, i))],
        out_specs=pl.BlockSpec((N_EXPERTS, LANES), lambda i: (0, 0)),
        out_shape=jax.ShapeDtypeStruct((N_EXPERTS, LANES), F32),
        compiler_params=_cparams(("arbitrary",)),
        name="plan_counts",
    )(top_idx)


def _plan_dest_body(idx_ref, base_ref, dest_ref, run_ref):
    @pl.when(pl.program_id(0) == 0)
    def _():
        run_ref[...] = base_ref[...]

    idx = idx_ref[...]
    eiota, _, excl, tot = _plan_tile(idx)
    pos = run_ref[:, 0:1] + excl
    rows = [jnp.sum(jnp.where(eiota == idx[k:k + 1], pos, 0.0), axis=0, keepdims=True) for k in range(TOP_K)]
    dest_ref[...] = jnp.concatenate(rows, axis=0).astype(I32)
    run_ref[...] += tot


def _plan_dest(top_idx, pstart):
    n = top_idx.shape[1]
    tm = 512
    base = jnp.broadcast_to(pstart.astype(F32).reshape(N_EXPERTS, 1), (N_EXPERTS, LANES))
    return pl.pallas_call(
        _plan_dest_body,
        grid=(n // tm,),
        in_specs=[pl.BlockSpec((TOP_K, tm), lambda i: (0, i)), pl.BlockSpec((N_EXPERTS, LANES), lambda i: (0, 0))],
        out_specs=pl.BlockSpec((TOP_K, tm), lambda i: (0, i)),
        out_shape=jax.ShapeDtypeStruct((TOP_K, n), I32),
        scratch_shapes=[pltpu.VMEM((N_EXPERTS, LANES), F32)],
        compiler_params=_cparams(("arbitrary",)),
        name="plan_dest",
    )(top_idx, base)


def _capacity(n):
    return (pl.cdiv(n * TOP_K, ROW_BLOCK) + N_EXPERTS) * ROW_BLOCK


def _pack_halves(y):
    w = y.shape[1] // 2
    hi = pltpu.bitcast(y[:, :w].astype(BF16).astype(F32), U32)
    lo = pltpu.bitcast(y[:, w:].astype(BF16).astype(F32), U32)
    return hi | (lo >> 16)


def _unpack_halves(p):
    a = pltpu.bitcast(p & jnp.uint32(0xFFFF0000), F32)
    b = pltpu.bitcast(p << 16, F32)
    return a, b


TOKENS_PER_ISSUE = 4


def _row_dma_loops(row_copy):
    def issue(i, _):
        for u in range(TOKENS_PER_ISSUE):
            for k in range(TOP_K):
                row_copy(i * TOKENS_PER_ISSUE + u, k).start(priority=k % 2)
        return 0

    def drain(i, _):
        for u in range(TOKENS_PER_ISSUE):
            for k in range(TOP_K):
                row_copy(i * TOKENS_PER_ISSUE + u, k).wait()
        return 0

    return issue, drain


def _pad_chunks():
    s = 1
    while 2 * s < ROW_BLOCK:
        s *= 2
    sizes = []
    while s >= 1:
        sizes.append(s)
        s //= 2
    return sizes


def _tile_rows(t):
    start = t * SUBLANES
    return pl.ds(start if isinstance(start, int) else pl.multiple_of(start, SUBLANES), SUBLANES)


def _to_row_tiles(ref, packed, lead=()):
    rows = packed.shape[0]
    for s in range(SUBLANES):
        ref[(*lead, pl.ds(s, rows, stride=SUBLANES), slice(None))] = packed[:, s * LANES:(s + 1) * LANES]


def _from_row_tiles(ref, rows, lead=()):
    return [ref[(*lead, pl.ds(s, rows, stride=SUBLANES), slice(None))] for s in range(SUBLANES)]


def _dispatch_body(pstart_ref, cnt_ref, dest_ref, h_ref, wg, wu, wd, xs_ref, sh_ref, xbuf, zbuf, sem, zsem):
    tm = h_ref.shape[0]
    step = pl.program_id(0)

    def pad_dmas(wait):
        def per_expert(e, _):
            cnt = cnt_ref[e]
            pad = (ROW_BLOCK - cnt % ROW_BLOCK) % ROW_BLOCK
            start = pstart_ref[e] + cnt
            for size in _pad_chunks():
                @pl.when((pad & size) != 0)
                def _():
                    off = pl.multiple_of((start + (pad & ~(2 * size - 1))) * SUBLANES, SUBLANES)
                    cp = pltpu.make_async_copy(zbuf.at[pl.ds(0, size * SUBLANES)],
                                               xs_ref.at[pl.ds(off, size * SUBLANES)], zsem)
                    cp.wait() if wait else cp.start()
            return 0
        lax.fori_loop(0, N_EXPERTS, per_expert, 0)

    @pl.when(step == 0)
    def _():
        zbuf[...] = jnp.zeros(zbuf.shape, zbuf.dtype)
        pad_dmas(False)
        pad_dmas(True)

    h = h_ref[...]
    _to_row_tiles(xbuf, _pack_halves(h.astype(F32)))

    def row_copy(t, k):
        return pltpu.make_async_copy(xbuf.at[_tile_rows(t)], xs_ref.at[_tile_rows(dest_ref[k, t])], sem)

    def start_rows(lo, hi):
        for t in range(lo, hi):
            for k in range(TOP_K):
                row_copy(t, k).start(priority=k % 2)

    q = tm // 4
    g = jnp.dot(h, wg[...], preferred_element_type=F32)
    start_rows(0, q)
    u = jnp.dot(h, wu[...], preferred_element_type=F32)
    start_rows(q, 2 * q)
    hmid = (g * jax.nn.sigmoid(g) * u).astype(BF16)
    start_rows(2 * q, 3 * q)
    sh_ref[...] = jnp.dot(hmid, wd[...], preferred_element_type=F32)
    start_rows(3 * q, tm)

    lax.fori_loop(0, tm // TOKENS_PER_ISSUE, _row_dma_loops(row_copy)[1], 0)


def _dispatch(h2, dest, pstart, counts, wg_b, wu_b, wd_b):
    n, d = h2.shape
    tm = 256
    assert d // 2 == SUBLANES * LANES
    const = lambda shape: pl.BlockSpec(shape, lambda i, *_: (0,) * len(shape), pipeline_mode=pl.Buffered(1))
    grid_spec = pltpu.PrefetchScalarGridSpec(
        num_scalar_prefetch=2,
        grid=(n // tm,),
        in_specs=[
            pl.BlockSpec((TOP_K, tm), lambda i, *_: (0, i), memory_space=pltpu.SMEM),
            pl.BlockSpec((tm, d), lambda i, *_: (i, 0)),
            const(wg_b.shape), const(wu_b.shape), const(wd_b.shape),
        ],
        out_specs=[pl.BlockSpec(memory_space=pl.ANY), pl.BlockSpec((tm, d), lambda i, *_: (i, 0))],
        scratch_shapes=[
            pltpu.VMEM((tm * SUBLANES, LANES), U32),
            pltpu.VMEM((_pad_chunks()[0] * SUBLANES, LANES), U32),
            pltpu.SemaphoreType.DMA(()),
            pltpu.SemaphoreType.DMA(()),
        ],
    )
    return pl.pallas_call(
        _dispatch_body,
        grid_spec=grid_spec,
        out_shape=[jax.ShapeDtypeStruct((_capacity(n) * SUBLANES, LANES), U32), jax.ShapeDtypeStruct((n, d), F32)],
        compiler_params=_cparams(("arbitrary",), has_side_effects=True, disable_bounds_checks=True),
        name="dispatch",
    )(pstart, counts, dest, h2, wg_b, wu_b, wd_b)


def _experts_body(be_ref, nu_ref, nxt_ref, ord_ref, x_ref, wg_hbm, wu_hbm, wd_hbm, y_ref,
                  wg_s, wu_s, wd_s, wg_b, wu_b, wd_b, sems):
    i = pl.program_id(0)
    used = i < nu_ref[0]
    e = be_ref[i]
    fresh = jnp.logical_or(i == 0, e != be_ref[jnp.maximum(i - 1, 0)])
    slot = ord_ref[e] % 2

    def weight_copies(expert, s):
        return (pltpu.make_async_copy(wg_hbm.at[expert], wg_s.at[s], sems.at[s, 0]),
                pltpu.make_async_copy(wu_hbm.at[expert], wu_s.at[s], sems.at[s, 1]),
                pltpu.make_async_copy(wd_hbm.at[expert], wd_s.at[s], sems.at[s, 2]))

    @pl.when(i == 0)
    def _():
        for cp in weight_copies(e, slot):
            cp.start()

    @pl.when(jnp.logical_and(used, fresh))
    def _():
        copies = weight_copies(e, slot)
        for cp, dst, src in zip(copies, (wg_b, wu_b, wd_b), (wg_s, wu_s, wd_s)):
            cp.wait()
            dst[...] = src[slot].astype(BF16)

        @pl.when(nxt_ref[e] != e)
        def _():
            for cp in weight_copies(nxt_ref[e], 1 - slot):
                cp.start()

    @pl.when(used)
    def _():
        half = SUBLANES * LANES
        slabs = [_unpack_halves(p) for p in _from_row_tiles(x_ref, ROW_BLOCK)]
        xa = jnp.concatenate([a.astype(BF16) for a, _ in slabs], axis=1)
        xb = jnp.concatenate([b.astype(BF16) for _, b in slabs], axis=1)
        g = (jnp.dot(xa, wg_b[:half], preferred_element_type=F32)
             + jnp.dot(xb, wg_b[half:], preferred_element_type=F32))
        u = (jnp.dot(xa, wu_b[:half], preferred_element_type=F32)
             + jnp.dot(xb, wu_b[half:], preferred_element_type=F32))
        hmid = (g * jax.nn.sigmoid(g) * u).astype(BF16)
        _to_row_tiles(y_ref, _pack_halves(jnp.dot(hmid, wd_b[...], preferred_element_type=F32)))


def _experts(xs, block_e, n_used, next_e, ord_e, w_gate_e, w_up_e, w_down_e):
    cap = xs.shape[0] // SUBLANES
    d = 2 * SUBLANES * LANES
    nblk = cap // ROW_BLOCK
    row_block = (ROW_BLOCK * SUBLANES, LANES)

    def row_map(i, be, nu, nxt, od):
        return (jnp.minimum(i, nu[0] - 1), 0)

    grid_spec = pltpu.PrefetchScalarGridSpec(
        num_scalar_prefetch=4,
        grid=(nblk,),
        in_specs=[
            pl.BlockSpec(row_block, row_map),
            pl.BlockSpec(memory_space=pl.ANY),
            pl.BlockSpec(memory_space=pl.ANY),
            pl.BlockSpec(memory_space=pl.ANY),
        ],
        out_specs=pl.BlockSpec(row_block, row_map),
        scratch_shapes=[
            pltpu.VMEM((2, d, D_EXPERT), F32),
            pltpu.VMEM((2, d, D_EXPERT), F32),
            pltpu.VMEM((2, D_EXPERT, d), F32),
            pltpu.VMEM((d, D_EXPERT), BF16),
            pltpu.VMEM((d, D_EXPERT), BF16),
            pltpu.VMEM((D_EXPERT, d), BF16),
            pltpu.SemaphoreType.DMA((2, 3)),
        ],
    )
    return pl.pallas_call(
        _experts_body,
        grid_spec=grid_spec,
        out_shape=jax.ShapeDtypeStruct(xs.shape, U32),
        compiler_params=_cparams(("arbitrary",)),
        name="experts",
    )(block_e, n_used, next_e, ord_e, xs, w_gate_e, w_up_e, w_down_e)


COMBINE_TILE = 256


def _combine_body(dest_ref, dest_next_ref, ys_ref, wt_ref, sh_ref, x1_ref, mod_ref, gpost, o_ref,
                  buf0, buf1, sem0, sem1):
    tile = COMBINE_TILE
    i = pl.program_id(0)
    bufs, sems = (buf0, buf1), (sem0, sem1)
    half = SUBLANES * LANES
    chunk = tile // SUBLANES

    def copies(dref, col0, slot):
        def row_copy(t, k):
            return pltpu.make_async_copy(ys_ref.at[_tile_rows(dref[k, col0 + t])],
                                         bufs[slot].at[k, _tile_rows(t)], sems[slot])
        return row_copy

    def run(loop_body):
        lax.fori_loop(0, tile // TOKENS_PER_ISSUE, loop_body, 0)

    @pl.when(i == 0)
    def _():
        run(_row_dma_loops(copies(dest_ref, 0, 0))[0])

    def phase(row0, slot, next_copy):
        run(_row_dma_loops(copies(dest_ref, row0, slot))[1])
        rows = slice(row0, row0 + tile)
        shared = sh_ref[rows, :]
        wt = wt_ref[rows, :]
        wk = [jnp.broadcast_to(wt[:, k:k + 1], (tile, LANES)) for k in range(TOP_K)]
        ya, yb = [], []
        for s in range(SUBLANES):
            for t in range(s * chunk, (s + 1) * chunk):
                for k in range(TOP_K):
                    next_copy(t, k).start(priority=k % 2)
            sa = shared[:, s * LANES:(s + 1) * LANES]
            sb = shared[:, half + s * LANES:half + (s + 1) * LANES]
            for k in range(TOP_K):
                a, b = _unpack_halves(bufs[slot][k, pl.ds(s, tile, stride=SUBLANES), :])
                sa = sa + wk[k] * a
                sb = sb + wk[k] * b
            ya.append(sa)
            yb.append(sb)
        y = jnp.concatenate(ya + yb, axis=1)
        gate2 = mod_ref[0, 5:6, :]
        o_ref[rows, :] = x1_ref[rows, :] + gate2 * (_rms(y) * gpost[...])

    phase(0, 0, copies(dest_ref, tile, 1))
    phase(tile, 1, copies(dest_next_ref, 0, 0))

    @pl.when(i == pl.num_programs(0) - 1)
    def _():
        run(_row_dma_loops(copies(dest_next_ref, 0, 0))[1])


def _combine(ys, dest, wt, shared, x1, mod3, g_post, seq):
    n, d = x1.shape
    tm = 2 * COMBINE_TILE
    nstep = n // tm
    per_b = seq // tm
    const = lambda shape: pl.BlockSpec(shape, lambda i, *_: (0,) * len(shape), pipeline_mode=pl.Buffered(1))
    buf = pltpu.VMEM((TOP_K, COMBINE_TILE * SUBLANES, LANES), U32)
    grid_spec = pltpu.PrefetchScalarGridSpec(
        num_scalar_prefetch=0,
        grid=(nstep,),
        in_specs=[
            pl.BlockSpec((TOP_K, tm), lambda i: (0, i), memory_space=pltpu.SMEM),
            pl.BlockSpec((TOP_K, tm), lambda i: (0, jnp.minimum(i + 1, nstep - 1)), memory_space=pltpu.SMEM),
            pl.BlockSpec(memory_space=pl.ANY),
            pl.BlockSpec((tm, LANES), lambda i: (i, 0)),
            pl.BlockSpec((tm, d), lambda i: (i, 0)),
            pl.BlockSpec((tm, d), lambda i: (i, 0)),
            pl.BlockSpec((1, N_MOD, d), lambda i: (i // per_b, 0, 0)),
            const((1, d)),
        ],
        out_specs=pl.BlockSpec((tm, d), lambda i: (i, 0)),
        scratch_shapes=[buf, buf, pltpu.SemaphoreType.DMA(()), pltpu.SemaphoreType.DMA(())],
    )
    return pl.pallas_call(
        _combine_body,
        grid_spec=grid_spec,
        out_shape=jax.ShapeDtypeStruct((n, d), F32),
        compiler_params=_cparams(("arbitrary",), disable_bounds_checks=True),
        name="combine",
    )(dest, dest, ys, wt, shared, x1, mod3, g_post.reshape(1, d))


def _layer(x, mod, positions, g_pre_mix, g_post_mix, g_pre_ffn, g_post_ffn, w_in, w_dil_out, w_sb_out, w_mix_out,
           w_router, router_bias, w_gate_e, w_up_e, w_down_e, w_gate_s, w_up_s, w_down_s):
    bsz, seq, d = x.shape
    n = bsz * seq
    xf = x.reshape(n, d)
    mod3 = mod.reshape(bsz, N_MOD, d)

    nd, nq = 3 * WIDTH_DIL, 3 * (WIDTH_DIL + WIDTH_SB)
    w_plain = jnp.concatenate([w_in[:, nq:], w_in[:, nd:nq]], axis=1).astype(BF16)
    cols = []
    for g in range(len(DIL_PATTERNS)):
        for part in range(3):
            lo = part * WIDTH_DIL + g * D_DIL_OUT
            cols.append(w_in[:, lo:lo + D_DIL_OUT])
    w_dil = jnp.concatenate(cols, axis=1).astype(BF16)

    proj = _inproj(xf, g_pre_mix, mod3, w_plain, seq)
    tables = _rope_tables(positions)
    qkv_dil = _inproj_dil(xf, g_pre_mix, mod3, w_dil, tables, bsz, seq)
    o_dil, lse_dil = [], []
    for g, (window, dilation) in enumerate(DIL_PATTERNS):
        assert window // dilation == Q_BLOCK
        o, lse = _dilated_attention(qkv_dil[g], g, bsz, seq)
        o_dil.append(o)
        lse_dil.append(lse)
    o_sb = _stick_breaking(proj, bsz, seq)

    x1, h2 = _mixout(o_dil, lse_dil, o_sb, proj, xf, mod3, g_post_mix, g_pre_ffn,
                     w_dil_out.astype(BF16), w_sb_out.astype(BF16), w_mix_out.astype(BF16), seq)

    top_idx, top_wt = _router(h2, w_router, router_bias)
    counts = _plan_counts(top_idx)[:, 0].astype(I32)
    padded = (counts + ROW_BLOCK - 1) // ROW_BLOCK * ROW_BLOCK
    pends = jnp.cumsum(padded)
    pstart = pends - padded
    nblk = _capacity(n) // ROW_BLOCK
    n_used = (pends[-1] // ROW_BLOCK).astype(I32)
    blk = jnp.minimum(jnp.arange(nblk, dtype=I32), n_used - 1)
    block_e = jnp.minimum(jnp.sum(pends[None, :] <= (blk * ROW_BLOCK)[:, None], axis=1), N_EXPERTS - 1).astype(I32)
    eids = jnp.arange(N_EXPERTS, dtype=I32)
    has = counts > 0
    ord_e = (jnp.cumsum(has) - has).astype(I32)
    later = jnp.where((eids[None, :] > eids[:, None]) & has[None, :], eids[None, :], N_EXPERTS)
    nxt = jnp.min(later, axis=1)
    next_e = jnp.where(nxt < N_EXPERTS, nxt, eids).astype(I32)
    dest = _plan_dest(top_idx, pstart)

    xs, shared = _dispatch(h2, dest, pstart.astype(I32), counts,
                           w_gate_s.astype(BF16), w_up_s.astype(BF16), w_down_s.astype(BF16))
    ys = _experts(xs, block_e, n_used.reshape(1), next_e, ord_e, w_gate_e, w_up_e, w_down_e)
    out = _combine(ys, dest, top_wt, shared, x1, mod3, g_post_ffn, seq)
    return out.reshape(bsz, seq, d)


def kernel(x, c, positions, w_ada, b_ada, g_pre_mix, g_post_mix, g_pre_ffn, g_post_ffn, w_in, w_dil_out,
           w_sb_out, w_mix_out, w_router, router_bias, w_gate_e, w_up_e, w_down_e, w_gate_s, w_up_s, w_down_s):
    for l in range(w_ada.shape[0]):
        mod = _adaln(c, w_ada[l], b_ada[l])
        x = _layer(x, mod, positions, g_pre_mix[l], g_post_mix[l], g_pre_ffn[l], g_post_ffn[l], w_in[l],
                   w_dil_out[l], w_sb_out[l], w_mix_out[l], w_router[l], router_bias[l],
                   w_gate_e[l], w_up_e[l], w_down_e[l], w_gate_s[l], w_up_s[l], w_down_s[l])
    return x
```

```python
import functools

import jax
import jax.numpy as jnp
from jax import lax
from jax.experimental import pallas as pl
from jax.experimental.pallas import tpu as pltpu

F32 = jnp.float32
BF16 = jnp.bfloat16
I32 = jnp.int32
U32 = jnp.uint32

D_MODEL = 2048
HEAD_DIM = 128
DIL_PATTERNS = ((128, 1), (512, 4), (2048, 16))
HEADS_PER_GROUP = 4
N_HEADS_DIL = 12
N_HEADS_SB = 8
WIDTH_DIL = N_HEADS_DIL * HEAD_DIM
WIDTH_SB = N_HEADS_SB * HEAD_DIM
D_DIL_OUT = HEADS_PER_GROUP * HEAD_DIM
Q_BLOCK = 128
ROPE_THETA = 500000.0
ROPE_DIM = HEAD_DIM // 4
N_GATE = 2 * D_MODEL
N_EXPERTS = 64
TOP_K = 8
N_GROUPS = 8
GROUP_SIZE = N_EXPERTS // N_GROUPS
TOPK_GROUPS = 4
D_EXPERT = 512
D_SHARED = 512
ROUTED_SCALE = 2.5
RMS_EPS = 1e-6
N_MOD = 6
ATTN_SCALE = HEAD_DIM ** -0.5

LANES = 128
SUBLANES = 8
VMEM_LIMIT = 56 * 1024 * 1024

ROW_BLOCK = 704
SB_DEAD = -110.0


def _cparams(sem, **kw):
    return pltpu.CompilerParams(dimension_semantics=sem, vmem_limit_bytes=VMEM_LIMIT, **kw)


def _adaln_body(ct_ref, w_ref, b_ref, o_ref, *, kc):
    nb = ct_ref.shape[1]
    nk = w_ref.shape[0] // kc

    def step(i, acc):
        k0 = pl.multiple_of(i * kc, kc)
        w = w_ref[pl.ds(k0, kc), :]
        c = ct_ref[pl.ds(k0, kc), :]
        s = c * jax.nn.sigmoid(c)
        parts = [jnp.sum(w * s[:, b:b + 1], axis=0, keepdims=True) for b in range(nb)]
        return acc + jnp.concatenate(parts, axis=0)

    acc = lax.fori_loop(0, nk, step, jnp.zeros(o_ref.shape, F32))
    o_ref[...] = acc + b_ref[...]


def _adaln(c, w_ada, b_ada):
    nb, d = c.shape
    n_out = w_ada.shape[1]
    tn = 1024
    return pl.pallas_call(
        functools.partial(_adaln_body, kc=256),
        grid=(n_out // tn,),
        in_specs=[
            pl.BlockSpec((d, nb), lambda j: (0, 0)),
            pl.BlockSpec((d, tn), lambda j: (0, j)),
            pl.BlockSpec((1, tn), lambda j: (0, j)),
        ],
        out_specs=pl.BlockSpec((nb, tn), lambda j: (0, j)),
        out_shape=jax.ShapeDtypeStruct((nb, n_out), F32),
        compiler_params=_cparams(("arbitrary",)),
        name="adaln",
    )(c.T, w_ada, b_ada.reshape(1, n_out))


def _rms(x):
    return x * lax.rsqrt(jnp.mean(x * x, axis=-1, keepdims=True) + RMS_EPS)


def _prenorm(x_ref, g_ref, mod_ref):
    y = _rms(x_ref[...]) * g_ref[...]
    shift = mod_ref[0, 0:1, :]
    scale = mod_ref[0, 1:2, :]
    return (y * (1.0 + scale) + shift).astype(BF16)


def _inproj_body(x_ref, g_ref, mod_ref, w_ref, o_ref, h_ref):
    @pl.when(pl.program_id(1) == 0)
    def _():
        h_ref[...] = _prenorm(x_ref, g_ref, mod_ref)

    o_ref[...] = jnp.dot(h_ref[...], w_ref[...], preferred_element_type=F32).astype(o_ref.dtype)


def _inproj(xf, g_pre, mod3, w_b, seq):
    n, d = xf.shape
    width = w_b.shape[1]
    tm, tn = 1024, width // 4
    per_b = seq // tm
    return pl.pallas_call(
        _inproj_body,
        grid=(n // tm, width // tn),
        in_specs=[
            pl.BlockSpec((tm, d), lambda i, j: (i, 0)),
            pl.BlockSpec((1, d), lambda i, j: (0, 0)),
            pl.BlockSpec((1, N_MOD, d), lambda i, j: (i // per_b, 0, 0)),
            pl.BlockSpec((d, tn), lambda i, j: (0, j)),
        ],
        out_specs=pl.BlockSpec((tm, tn), lambda i, j: (i, j)),
        out_shape=jax.ShapeDtypeStruct((n, width), BF16),
        scratch_shapes=[pltpu.VMEM((tm, d), BF16)],
        compiler_params=_cparams(("arbitrary", "arbitrary")),
        name="inproj",
    )(xf, g_pre.reshape(1, d), mod3, w_b)


def _inproj_dil_body(x_ref, g_ref, mod_ref, w_ref, t_ref, o0, o1, o2, res_ref):
    tm = x_ref.shape[0]
    h = _prenorm(x_ref, g_ref, mod_ref)
    t = t_ref[...]
    gw = 3 * D_DIL_OUT
    for gi, o_ref in enumerate((o0, o1, o2)):
        dil = DIL_PATTERNS[gi][1]
        res = jnp.dot(h, w_ref[:, gi * gw:(gi + 1) * gw], preferred_element_type=F32)
        for hs in range(3 * HEADS_PER_GROUP):
            sl = slice(hs * HEAD_DIM, (hs + 1) * HEAD_DIM)
            res_ref[hs] = _apply_rope(res[:, sl], t) if hs < 2 * HEADS_PER_GROUP else res[:, sl]
        for r in range(dil):
            for hs in range(3 * HEADS_PER_GROUP):
                rows = res_ref[hs] if dil == 1 else res_ref[hs, pl.ds(r, tm // dil, stride=dil), :]
                o_ref[0, r, :, hs * HEAD_DIM:(hs + 1) * HEAD_DIM] = rows.astype(o_ref.dtype)


def _inproj_dil(xf, g_pre, mod3, w_b, tables, bsz, seq):
    n, d = xf.shape
    gw = 3 * D_DIL_OUT
    tm = 512
    per_b = seq // tm
    dils = [p[1] for p in DIL_PATTERNS]
    return pl.pallas_call(
        _inproj_dil_body,
        grid=(n // tm,),
        in_specs=[
            pl.BlockSpec((tm, d), lambda i: (i, 0)),
            pl.BlockSpec((1, d), lambda i: (0, 0)),
            pl.BlockSpec((1, N_MOD, d), lambda i: (i // per_b, 0, 0)),
            pl.BlockSpec(w_b.shape, lambda i: (0, 0), pipeline_mode=pl.Buffered(1)),
            pl.BlockSpec((tm, 3 * LANES), lambda i: (i, 0)),
        ],
        out_specs=[pl.BlockSpec((1, dl, tm // dl, gw), lambda i: (i // per_b, 0, i % per_b, 0)) for dl in dils],
        out_shape=[jax.ShapeDtypeStruct((bsz, dl, seq // dl, gw), BF16) for dl in dils],
        scratch_shapes=[pltpu.VMEM((gw // HEAD_DIM, tm, HEAD_DIM), F32)],
        compiler_params=_cparams(("arbitrary",)),
        name="inproj_dil",
    )(xf, g_pre.reshape(1, d), mod3, w_b, tables)


def _rope_body(pos_ref, f_ref, o_ref):
    ang = pos_ref[...].astype(F32) * f_ref[...]
    c = jnp.cos(ang)
    s = jnp.sin(ang)
    lane = lax.broadcasted_iota(I32, ang.shape, 1)
    half = ROPE_DIM // 2
    o_ref[:, 0:LANES] = c
    o_ref[:, LANES:2 * LANES] = jnp.where(lane >= half, s, 0.0)
    o_ref[:, 2 * LANES:3 * LANES] = jnp.where(lane < half, -s, 0.0)


def _rope_tables(positions):
    n = positions.size
    half = ROPE_DIM // 2
    inv_freq = ROPE_THETA ** (-jnp.arange(0, ROPE_DIM, 2, dtype=F32) / ROPE_DIM)
    f = jnp.concatenate([inv_freq, inv_freq, jnp.zeros((LANES - 2 * half,), F32)]).reshape(1, LANES)
    tm = 2048
    return pl.pallas_call(
        _rope_body,
        grid=(n // tm,),
        in_specs=[pl.BlockSpec((tm, 1), lambda i: (i, 0)), pl.BlockSpec((1, LANES), lambda i: (0, 0))],
        out_specs=pl.BlockSpec((tm, 3 * LANES), lambda i: (i, 0)),
        out_shape=jax.ShapeDtypeStruct((n, 3 * LANES), F32),
        compiler_params=_cparams(("arbitrary",)),
        name="rope_tables",
    )(positions.reshape(n, 1), f)


def _apply_rope(x, t):
    half = ROPE_DIM // 2
    return (x * t[:, 0:LANES]
            + pltpu.roll(x, half, 1) * t[:, LANES:2 * LANES]
            + pltpu.roll(x, LANES - half, 1) * t[:, 2 * LANES:3 * LANES])


def _dil_body(cur_ref, kp_ref, vp_ref, o_ref, lse_ref, obuf, lbuf, *, dil, nsub):
    n = pl.program_id(1)
    tq = nsub * Q_BLOCK
    row = lax.broadcasted_iota(I32, (Q_BLOCK, 2 * Q_BLOCK), 0)
    col = lax.broadcasted_iota(I32, (Q_BLOCK, 2 * Q_BLOCK), 1)
    rel = row + Q_BLOCK - col
    band = jnp.where(rel >= 0, jnp.where(rel <= Q_BLOCK, 1.0, 0.0), 0.0)
    first = jnp.where(col >= Q_BLOCK, band, jnp.where(n > 0, band, 0.0))
    for r in range(dil):
        for h in range(HEADS_PER_GROUP):
            sl = slice(h * HEAD_DIM, (h + 1) * HEAD_DIM)
            ksl = slice(D_DIL_OUT + h * HEAD_DIM, D_DIL_OUT + (h + 1) * HEAD_DIM)
            vsl = slice(2 * D_DIL_OUT + h * HEAD_DIM, 2 * D_DIL_OUT + (h + 1) * HEAD_DIM)
            for j in range(nsub):
                rs = slice(j * Q_BLOCK, (j + 1) * Q_BLOCK)
                ps = slice((j - 1) * Q_BLOCK, j * Q_BLOCK)
                kprev = kp_ref[0, r, :, sl] if j == 0 else cur_ref[0, r, ps, ksl]
                vprev = vp_ref[0, r, :, sl] if j == 0 else cur_ref[0, r, ps, vsl]
                kcat = jnp.concatenate([kprev, cur_ref[0, r, rs, ksl]], axis=0)
                vcat = jnp.concatenate([vprev, cur_ref[0, r, rs, vsl]], axis=0)
                s = lax.dot_general(cur_ref[0, r, rs, sl], kcat, (((1,), (1,)), ((), ())),
                                    preferred_element_type=F32) * ATTN_SCALE
                s = jnp.where((first if j == 0 else band) > 0.0, s, -jnp.inf)
                m = jnp.max(s, axis=-1, keepdims=True)
                p = jnp.exp(s - m)
                l = jnp.sum(p, axis=-1, keepdims=True)
                o = jnp.dot((p / l).astype(BF16), vcat, preferred_element_type=F32)
                lse = jnp.broadcast_to(m + jnp.log(l), (Q_BLOCK, HEAD_DIM))
                if dil == 1:
                    o_ref[h, rs, :] = o
                    lse_ref[h, rs, :] = lse
                else:
                    obuf[h, rs, :] = o
                    lbuf[h, rs, :] = lse
            if dil > 1:
                o_ref[h, pl.ds(r, tq, stride=dil), :] = obuf[h]
                lse_ref[h, pl.ds(r, tq, stride=dil), :] = lbuf[h]


def _dilated_attention(qkv, g, bsz, seq):
    dil = DIL_PATTERNS[g][1]
    length = seq // dil
    tq = min(4 * Q_BLOCK, (16 * Q_BLOCK) // dil, length)
    nsub = tq // Q_BLOCK
    nq = length // tq
    gw = 3 * D_DIL_OUT
    n = bsz * seq

    def prev(colblk):
        return pl.BlockSpec((1, dil, Q_BLOCK, D_DIL_OUT),
                            lambda b, i: (b, 0, jnp.maximum(i * nsub - 1, 0), colblk))

    nh = HEADS_PER_GROUP
    out_spec = pl.BlockSpec((nh, tq * dil, HEAD_DIM), lambda b, i: (0, b * nq + i, 0))
    out_shape = jax.ShapeDtypeStruct((nh, n, HEAD_DIM), F32)
    return pl.pallas_call(
        functools.partial(_dil_body, dil=dil, nsub=nsub),
        grid=(bsz, nq),
        in_specs=[pl.BlockSpec((1, dil, tq, gw), lambda b, i: (b, 0, i, 0)), prev(1), prev(2)],
        out_specs=[out_spec, out_spec],
        out_shape=[out_shape, out_shape],
        scratch_shapes=[pltpu.VMEM((nh, tq, HEAD_DIM), F32), pltpu.VMEM((nh, tq, HEAD_DIM), F32)],
        compiler_params=_cparams(("arbitrary", "arbitrary")),
        name=f"dilated_d{dil}",
    )(qkv, qkv, qkv)


def _sb_body(q_ref, k_ref, v_ref, o_ref, acc_ref, car_ref):
    nblk = q_ref.shape[1] // Q_BLOCK
    r = lax.broadcasted_iota(I32, (Q_BLOCK, Q_BLOCK), 0)
    c = lax.broadcasted_iota(I32, (Q_BLOCK, Q_BLOCK), 1)
    causal = c < r
    rr = lax.broadcasted_iota(I32, (Q_BLOCK, 2 * Q_BLOCK), 0)
    cc = lax.broadcasted_iota(I32, (Q_BLOCK, 2 * Q_BLOCK), 1)
    uo = jnp.where(cc >= Q_BLOCK, 1.0, jnp.where(rr > cc, 1.0, 0.0)).astype(BF16)

    nh = acc_ref.shape[0]
    heads = range(nh)

    hs = [slice(h * HEAD_DIM, (h + 1) * HEAD_DIM) for h in heads]

    def key_rows(kb):
        start = kb * Q_BLOCK
        return pl.ds(start if isinstance(start, int) else pl.multiple_of(start, Q_BLOCK), Q_BLOCK)

    def scores(qs, kb):
        return [lax.dot_general(qs[h], k_ref[0, key_rows(kb), hs[h]], (((1,), (1,)), ((), ())),
                                preferred_element_type=F32) * ATTN_SCALE for h in heads]

    def log_terms(zs, diag):
        stacked, log_s = [], []
        for z in zs:
            sp = jnp.log(1.0 + jnp.exp(-jnp.abs(z)))
            mx = jnp.maximum(z, 0.0)
            log_1m = -(mx + sp)
            if diag:
                log_1m = jnp.where(causal, log_1m, 0.0)
            hi = log_1m.astype(BF16)
            lo = (log_1m - hi.astype(F32)).astype(BF16)
            stacked.append(jnp.concatenate([hi, lo], axis=0))
            log_s.append((z - mx) - sp)
        return stacked, log_s

    def suffix_sums(stacked):
        r2s = [jnp.dot(s, uo, preferred_element_type=F32) for s in stacked]
        return [r2[:Q_BLOCK] + r2[Q_BLOCK:] for r2 in r2s]

    def weights(log_s, carries, sums, diag):
        probs, new_carries = [], []
        for h in heads:
            a = jnp.exp(log_s[h] + carries[h] + sums[h][:, :Q_BLOCK])
            if diag:
                a = jnp.where(causal, a, 0.0)
            probs.append(a.astype(BF16))
            new_carries.append(carries[h] + sums[h][:, Q_BLOCK:])
        return probs, new_carries

    def values(probs, kb):
        return [jnp.dot(probs[h], v_ref[0, key_rows(kb), hs[h]], preferred_element_type=F32) for h in heads]

    def all_max(xs):
        m = xs[0]
        for x in xs[1:]:
            m = jnp.maximum(m, x)
        return jnp.max(m)

    def qblock(qi, with_prev):
        qs = [q_ref[0, key_rows(qi), hs[h]] for h in heads]
        zero = jnp.zeros((Q_BLOCK, Q_BLOCK), F32)
        zs_d = scores(qs, qi)
        zs_p = scores(qs, qi - 1) if with_prev else None
        st_d, ls_d = log_terms(zs_d, True)
        st_p, ls_p = log_terms(zs_p, False) if with_prev else (None, None)
        sm_d = suffix_sums(st_d)
        sm_p = suffix_sums(st_p) if with_prev else None
        pr_d, cars = weights(ls_d, [zero] * nh, sm_d, True)
        pvs = values(pr_d, qi)
        if with_prev:
            pr_p, cars = weights(ls_p, cars, sm_p, False)
            pvs = [a + b for a, b in zip(pvs, values(pr_p, qi - 1))]
        for h in heads:
            acc_ref[h] = pvs[h]
            car_ref[h] = cars[h]

        def cond(st):
            return jnp.logical_and(st[0] >= 0, st[1] > SB_DEAD)

        def body(st):
            stacked, log_s = log_terms(scores(qs, st[0]), False)
            probs, cars = weights(log_s, [car_ref[h] for h in heads], suffix_sums(stacked), False)
            pvs = values(probs, st[0])
            for h in heads:
                acc_ref[h] += pvs[h]
                car_ref[h] = cars[h]
            return st[0] - 1, all_max(cars)

        lax.while_loop(cond, body, (qi - (2 if with_prev else 1), all_max(cars)))
        for h in heads:
            o_ref[0, key_rows(qi), hs[h]] = acc_ref[h].astype(o_ref.dtype)
        return 0

    qblock(0, False)
    lax.fori_loop(1, nblk, lambda qi, c: qblock(qi, True), 0)


SB_HEADS_PER_STEP = 8


def _stick_breaking(proj, bsz, seq):
    width = proj.shape[1]
    pv = proj.reshape(bsz, seq, width)
    nh = SB_HEADS_PER_STEP
    bw = nh * HEAD_DIM
    base = N_GATE // bw

    def spec(off):
        return pl.BlockSpec((1, seq, bw), lambda b, h: (b, 0, base + off + h), pipeline_mode=pl.Buffered(1))

    nstep = N_HEADS_SB // nh
    o = pl.pallas_call(
        _sb_body,
        grid=(bsz, nstep),
        in_specs=[spec(0), spec(nstep), spec(2 * nstep)],
        out_specs=pl.BlockSpec((1, seq, bw), lambda b, h: (b, 0, h)),
        out_shape=jax.ShapeDtypeStruct((bsz, seq, WIDTH_SB), BF16),
        scratch_shapes=[pltpu.VMEM((nh, Q_BLOCK, Q_BLOCK), F32), pltpu.VMEM((nh, Q_BLOCK, Q_BLOCK), F32)],
        compiler_params=_cparams(("arbitrary", "arbitrary")),
        name="stick_breaking",
    )(pv, pv, pv)
    return o.reshape(bsz * seq, WIDTH_SB)


def _mixout_body(o1, o2, o3, l1, l2, l3, osb, gd_ref, gs_ref, x_ref, mod_ref, gpost, gpre,
                 wd, ws, wm, x1_ref, h2_ref):
    heads = []
    for h in range(HEADS_PER_GROUP):
        la, lb, lc = l1[h], l2[h], l3[h]
        m = jnp.maximum(la, jnp.maximum(lb, lc))
        ea, eb, ec = jnp.exp(la - m), jnp.exp(lb - m), jnp.exp(lc - m)
        heads.append(((ea * o1[h] + eb * o2[h] + ec * o3[h]) / (ea + eb + ec)).astype(BF16))
    yd = jnp.dot(jnp.concatenate(heads, axis=1), wd[...], preferred_element_type=F32)
    ys = jnp.dot(osb[...], ws[...], preferred_element_type=F32)
    mix = jax.nn.sigmoid(gd_ref[...].astype(F32)) * yd + jax.nn.sigmoid(gs_ref[...].astype(F32)) * ys
    y = jnp.dot(mix.astype(BF16), wm[...], preferred_element_type=F32)
    gate1 = mod_ref[0, 2:3, :]
    shift2 = mod_ref[0, 3:4, :]
    scale2 = mod_ref[0, 4:5, :]
    x1 = x_ref[...] + gate1 * (_rms(y) * gpost[...])
    x1_ref[...] = x1
    h2_ref[...] = ((_rms(x1) * gpre[...]) * (1.0 + scale2) + shift2).astype(h2_ref.dtype)


def _const_spec(shape):
    return pl.BlockSpec(shape, lambda i: (0,) * len(shape), pipeline_mode=pl.Buffered(1))


def _mixout(o_dil, lse_dil, o_sb, proj, xf, mod3, g_post, g_pre, wd_b, ws_b, wm_b, seq):
    n, d = xf.shape
    tm = 256
    per_b = seq // tm
    row = lambda w: pl.BlockSpec((tm, w), lambda i: (i, 0))
    head_major = pl.BlockSpec((HEADS_PER_GROUP, tm, HEAD_DIM), lambda i: (0, i, 0))
    in_specs = (
        [head_major] * 6 + [row(WIDTH_SB)]
        + [pl.BlockSpec((tm, d), lambda i: (i, 0)), pl.BlockSpec((tm, d), lambda i: (i, 1))]
        + [row(d), pl.BlockSpec((1, N_MOD, d), lambda i: (i // per_b, 0, 0))]
        + [_const_spec((1, d)), _const_spec((1, d))]
        + [_const_spec(wd_b.shape), _const_spec(ws_b.shape), _const_spec(wm_b.shape)]
    )
    return pl.pallas_call(
        _mixout_body,
        grid=(n // tm,),
        in_specs=in_specs,
        out_specs=[row(d), row(d)],
        out_shape=[jax.ShapeDtypeStruct((n, d), F32), jax.ShapeDtypeStruct((n, d), BF16)],
        compiler_params=_cparams(("arbitrary",)),
        name="mixout",
    )(*o_dil, *lse_dil, o_sb, proj, proj, xf, mod3, g_post.reshape(1, d), g_pre.reshape(1, d), wd_b, ws_b, wm_b)


def _topk_rows(x, k, iota0):
    big = x.shape[0]
    out = []
    for _ in range(k):
        m = jnp.max(x, axis=0, keepdims=True)
        i = jnp.min(jnp.where(x == m, iota0, big), axis=0, keepdims=True)
        out.append((m, i))
        x = jnp.where(iota0 == i, -jnp.inf, x)
    return out


def _router_body(h_ref, wr_ref, bias_ref, idx_ref, wt_ref, cnt_ref):
    tm = h_ref.shape[0]

    @pl.when(pl.program_id(0) == 0)
    def _():
        cnt_ref[...] = jnp.zeros(cnt_ref.shape, F32)

    logits = lax.dot_general(wr_ref[...], h_ref[...], (((1,), (1,)), ((), ())), preferred_element_type=F32)
    scores = jax.nn.sigmoid(logits)
    sel = scores + bias_ref[...]
    sub = lax.broadcasted_iota(I32, (GROUP_SIZE, tm), 0)
    grp = []
    for g in range(N_GROUPS):
        (m1, _), (m2, _) = _topk_rows(sel[g * GROUP_SIZE:(g + 1) * GROUP_SIZE], 2, sub)
        grp.append(m1 + m2)
    gscore = jnp.concatenate(grp, axis=0)
    giota = lax.broadcasted_iota(I32, (N_GROUPS, tm), 0)
    gmask = jnp.zeros((N_GROUPS, tm), F32)
    for _, gi in _topk_rows(gscore, TOPK_GROUPS, giota):
        gmask = jnp.where(giota == gi, 1.0, gmask)
    masked = jnp.concatenate(
        [jnp.where(gmask[g:g + 1] > 0.0, sel[g * GROUP_SIZE:(g + 1) * GROUP_SIZE], -jnp.inf)
         for g in range(N_GROUPS)], axis=0)
    eiota = lax.broadcasted_iota(I32, (N_EXPERTS, tm), 0)
    picks = _topk_rows(masked, TOP_K, eiota)
    idx = jnp.concatenate([i for _, i in picks], axis=0)
    top_s = jnp.concatenate(
        [jnp.sum(jnp.where(eiota == i, scores, 0.0), axis=0, keepdims=True) for _, i in picks], axis=0)
    top_w = top_s / jnp.sum(top_s, axis=0, keepdims=True) * ROUTED_SCALE
    idx_ref[...] = idx
    wpad = jnp.concatenate([top_w, jnp.zeros((LANES - TOP_K, tm), F32)], axis=0)
    wt_ref[...] = wpad.T
    hit = jnp.zeros((N_EXPERTS, tm), F32)
    for _, i in picks:
        hit = jnp.where(eiota == i, 1.0, hit)
    cnt_ref[...] += jnp.sum(hit, axis=1, keepdims=True)


def _router(h2, w_router, router_bias):
    n, d = h2.shape
    tm = 512
    return pl.pallas_call(
        _router_body,
        grid=(n // tm,),
        in_specs=[
            pl.BlockSpec((tm, d), lambda i: (i, 0)),
            pl.BlockSpec((N_EXPERTS, d), lambda i: (0, 0)),
            pl.BlockSpec((N_EXPERTS, 1), lambda i: (0, 0)),
        ],
        out_specs=[pl.BlockSpec((TOP_K, tm), lambda i: (0, i)), pl.BlockSpec((tm, LANES), lambda i: (i, 0)),
                   pl.BlockSpec((N_EXPERTS, LANES), lambda i: (0, 0))],
        out_shape=[jax.ShapeDtypeStruct((TOP_K, n), I32), jax.ShapeDtypeStruct((n, LANES), F32),
                   jax.ShapeDtypeStruct((N_EXPERTS, LANES), F32)],
        compiler_params=_cparams(("arbitrary",)),
        name="router",
    )(h2, w_router.T.astype(BF16), router_bias.reshape(N_EXPERTS, 1))


def _plan_tile(idx):
    tm = idx.shape[1]
    eiota = lax.broadcasted_iota(I32, (N_EXPERTS, tm), 0)
    hit = jnp.zeros((N_EXPERTS, tm), F32)
    for k in range(TOP_K):
        hit = jnp.where(eiota == idx[k:k + 1], 1.0, hit)
    r = lax.broadcasted_iota(I32, (tm, tm), 0)
    c = lax.broadcasted_iota(I32, (tm, tm), 1)
    before = jnp.where(r < c, 1.0, 0.0).astype(BF16)
    excl = jnp.dot(hit.astype(BF16), before, preferred_element_type=F32)
    tot = excl[:, tm - 1:tm] + hit[:, tm - 1:tm]
    return eiota, hit, excl, tot


def _plan_dest_body(idx_ref, base_ref, dest_ref, run_ref):
    @pl.when(pl.program_id(0) == 0)
    def _():
        run_ref[...] = base_ref[...]

    idx = idx_ref[...]
    eiota, _, excl, tot = _plan_tile(idx)
    pos = run_ref[:, 0:1] + excl
    rows = [jnp.sum(jnp.where(eiota == idx[k:k + 1], pos, 0.0), axis=0, keepdims=True) for k in range(TOP_K)]
    dest_ref[...] = jnp.concatenate(rows, axis=0).astype(I32)
    run_ref[...] += tot


def _plan_dest(top_idx, pstart):
    n = top_idx.shape[1]
    tm = 512
    base = jnp.broadcast_to(pstart.astype(F32).reshape(N_EXPERTS, 1), (N_EXPERTS, LANES))
    return pl.pallas_call(
        _plan_dest_body,
        grid=(n // tm,),
        in_specs=[pl.BlockSpec((TOP_K, tm), lambda i: (0, i)), pl.BlockSpec((N_EXPERTS, LANES), lambda i: (0, 0))],
        out_specs=pl.BlockSpec((TOP_K, tm), lambda i: (0, i)),
        out_shape=jax.ShapeDtypeStruct((TOP_K, n), I32),
        scratch_shapes=[pltpu.VMEM((N_EXPERTS, LANES), F32)],
        compiler_params=_cparams(("arbitrary",)),
        name="plan_dest",
    )(top_idx, base)


def _capacity(n):
    return (pl.cdiv(n * TOP_K, ROW_BLOCK) + N_EXPERTS) * ROW_BLOCK


def _pack_halves(y):
    w = y.shape[1] // 2
    hi = pltpu.bitcast(y[:, :w].astype(BF16).astype(F32), U32)
    lo = pltpu.bitcast(y[:, w:].astype(BF16).astype(F32), U32)
    return hi | (lo >> 16)


def _unpack_halves(p):
    a = pltpu.bitcast(p & jnp.uint32(0xFFFF0000), F32)
    b = pltpu.bitcast(p << 16, F32)
    return a, b


TOKENS_PER_ISSUE = 4


def _row_dma_loops(row_copy):
    def issue(i, _):
        for u in range(TOKENS_PER_ISSUE):
            for k in range(TOP_K):
                row_copy(i * TOKENS_PER_ISSUE + u, k).start(priority=k % 2)
        return 0

    def drain(i, _):
        for u in range(TOKENS_PER_ISSUE):
            for k in range(TOP_K):
                row_copy(i * TOKENS_PER_ISSUE + u, k).wait()
        return 0

    return issue, drain


def _pad_chunks():
    s = 1
    while 2 * s < ROW_BLOCK:
        s *= 2
    sizes = []
    while s >= 1:
        sizes.append(s)
        s //= 2
    return sizes


def _tile_rows(t):
    start = t * SUBLANES
    return pl.ds(start if isinstance(start, int) else pl.multiple_of(start, SUBLANES), SUBLANES)


def _to_row_tiles(ref, packed, lead=()):
    rows = packed.shape[0]
    for s in range(SUBLANES):
        ref[(*lead, pl.ds(s, rows, stride=SUBLANES), slice(None))] = packed[:, s * LANES:(s + 1) * LANES]


def _from_row_tiles(ref, rows, lead=()):
    return [ref[(*lead, pl.ds(s, rows, stride=SUBLANES), slice(None))] for s in range(SUBLANES)]


def _dispatch_body(pstart_ref, cnt_ref, dest_ref, h_ref, wg, wu, wd, xs_ref, sh_ref, xbuf, zbuf, sem, zsem):
    tm = h_ref.shape[0]
    step = pl.program_id(0)

    def pad_dmas(wait):
        def per_expert(e, _):
            cnt = cnt_ref[e]
            pad = (ROW_BLOCK - cnt % ROW_BLOCK) % ROW_BLOCK
            start = pstart_ref[e] + cnt
            for size in _pad_chunks():
                @pl.when((pad & size) != 0)
                def _():
                    off = pl.multiple_of((start + (pad & ~(2 * size - 1))) * SUBLANES, SUBLANES)
                    cp = pltpu.make_async_copy(zbuf.at[pl.ds(0, size * SUBLANES)],
                                               xs_ref.at[pl.ds(off, size * SUBLANES)], zsem)
                    cp.wait() if wait else cp.start()
            return 0
        lax.fori_loop(0, N_EXPERTS, per_expert, 0)

    @pl.when(step == 0)
    def _():
        zbuf[...] = jnp.zeros(zbuf.shape, zbuf.dtype)
        pad_dmas(False)
        pad_dmas(True)

    h = h_ref[...]
    _to_row_tiles(xbuf, _pack_halves(h.astype(F32)))

    def row_copy(t, k):
        return pltpu.make_async_copy(xbuf.at[_tile_rows(t)], xs_ref.at[_tile_rows(dest_ref[k, t])], sem)

    def start_rows(lo, hi):
        for t in range(lo, hi):
            for k in range(TOP_K):
                row_copy(t, k).start(priority=k % 2)

    q = tm // 4
    g = jnp.dot(h, wg[...], preferred_element_type=F32)
    start_rows(0, q)
    u = jnp.dot(h, wu[...], preferred_element_type=F32)
    start_rows(q, 2 * q)
    hmid = (g * jax.nn.sigmoid(g) * u).astype(BF16)
    start_rows(2 * q, 3 * q)
    sh_ref[...] = jnp.dot(hmid, wd[...], preferred_element_type=F32)
    start_rows(3 * q, tm)

    lax.fori_loop(0, tm // TOKENS_PER_ISSUE, _row_dma_loops(row_copy)[1], 0)


def _dispatch(h2, dest, pstart, counts, wg_b, wu_b, wd_b):
    n, d = h2.shape
    tm = 256
    assert d // 2 == SUBLANES * LANES
    const = lambda shape: pl.BlockSpec(shape, lambda i, *_: (0,) * len(shape), pipeline_mode=pl.Buffered(1))
    grid_spec = pltpu.PrefetchScalarGridSpec(
        num_scalar_prefetch=2,
        grid=(n // tm,),
        in_specs=[
            pl.BlockSpec((TOP_K, tm), lambda i, *_: (0, i), memory_space=pltpu.SMEM),
            pl.BlockSpec((tm, d), lambda i, *_: (i, 0)),
            const(wg_b.shape), const(wu_b.shape), const(wd_b.shape),
        ],
        out_specs=[pl.BlockSpec(memory_space=pl.ANY), pl.BlockSpec((tm, d), lambda i, *_: (i, 0))],
        scratch_shapes=[
            pltpu.VMEM((tm * SUBLANES, LANES), U32),
            pltpu.VMEM((_pad_chunks()[0] * SUBLANES, LANES), U32),
            pltpu.SemaphoreType.DMA(()),
            pltpu.SemaphoreType.DMA(()),
        ],
    )
    return pl.pallas_call(
        _dispatch_body,
        grid_spec=grid_spec,
        out_shape=[jax.ShapeDtypeStruct((_capacity(n) * SUBLANES, LANES), U32), jax.ShapeDtypeStruct((n, d), F32)],
        compiler_params=_cparams(("arbitrary",), has_side_effects=True, disable_bounds_checks=True),
        name="dispatch",
    )(pstart, counts, dest, h2, wg_b, wu_b, wd_b)


def _experts_body(be_ref, nu_ref, nxt_ref, ord_ref, x_ref, wg_hbm, wu_hbm, wd_hbm, y_ref,
                  wg_s, wu_s, wd_s, wg_b, wu_b, wd_b, sems):
    i = pl.program_id(0)
    used = i < nu_ref[0]
    e = be_ref[i]
    fresh = jnp.logical_or(i == 0, e != be_ref[jnp.maximum(i - 1, 0)])
    slot = ord_ref[e] % 2

    def weight_copies(expert, s):
        return (pltpu.make_async_copy(wg_hbm.at[expert], wg_s.at[s], sems.at[s, 0]),
                pltpu.make_async_copy(wu_hbm.at[expert], wu_s.at[s], sems.at[s, 1]),
                pltpu.make_async_copy(wd_hbm.at[expert], wd_s.at[s], sems.at[s, 2]))

    @pl.when(i == 0)
    def _():
        for cp in weight_copies(e, slot):
            cp.start()

    @pl.when(jnp.logical_and(used, fresh))
    def _():
        copies = weight_copies(e, slot)
        for cp, dst, src in zip(copies, (wg_b, wu_b, wd_b), (wg_s, wu_s, wd_s)):
            cp.wait()
            dst[...] = src[slot].astype(BF16)

        @pl.when(nxt_ref[e] != e)
        def _():
            for cp in weight_copies(nxt_ref[e], 1 - slot):
                cp.start()

    @pl.when(used)
    def _():
        half = SUBLANES * LANES
        slabs = [_unpack_halves(p) for p in _from_row_tiles(x_ref, ROW_BLOCK)]
        xa = jnp.concatenate([a.astype(BF16) for a, _ in slabs], axis=1)
        xb = jnp.concatenate([b.astype(BF16) for _, b in slabs], axis=1)
        g = (jnp.dot(xa, wg_b[:half], preferred_element_type=F32)
             + jnp.dot(xb, wg_b[half:], preferred_element_type=F32))
        u = (jnp.dot(xa, wu_b[:half], preferred_element_type=F32)
             + jnp.dot(xb, wu_b[half:], preferred_element_type=F32))
        hmid = (g * jax.nn.sigmoid(g) * u).astype(BF16)
        _to_row_tiles(y_ref, _pack_halves(jnp.dot(hmid, wd_b[...], preferred_element_type=F32)))


def _experts(xs, block_e, n_used, next_e, ord_e, w_gate_e, w_up_e, w_down_e):
    cap = xs.shape[0] // SUBLANES
    d = 2 * SUBLANES * LANES
    nblk = cap // ROW_BLOCK
    row_block = (ROW_BLOCK * SUBLANES, LANES)

    def row_map(i, be, nu, nxt, od):
        return (jnp.minimum(i, nu[0] - 1), 0)

    grid_spec = pltpu.PrefetchScalarGridSpec(
        num_scalar_prefetch=4,
        grid=(nblk,),
        in_specs=[
            pl.BlockSpec(row_block, row_map),
            pl.BlockSpec(memory_space=pl.ANY),
            pl.BlockSpec(memory_space=pl.ANY),
            pl.BlockSpec(memory_space=pl.ANY),
        ],
        out_specs=pl.BlockSpec(row_block, row_map),
        scratch_shapes=[
            pltpu.VMEM((2, d, D_EXPERT), F32),
            pltpu.VMEM((2, d, D_EXPERT), F32),
            pltpu.VMEM((2, D_EXPERT, d), F32),
            pltpu.VMEM((d, D_EXPERT), BF16),
            pltpu.VMEM((d, D_EXPERT), BF16),
            pltpu.VMEM((D_EXPERT, d), BF16),
            pltpu.SemaphoreType.DMA((2, 3)),
        ],
    )
    return pl.pallas_call(
        _experts_body,
        grid_spec=grid_spec,
        out_shape=jax.ShapeDtypeStruct(xs.shape, U32),
        compiler_params=_cparams(("arbitrary",)),
        name="experts",
    )(block_e, n_used, next_e, ord_e, xs, w_gate_e, w_up_e, w_down_e)


COMBINE_TILE = 256


def _combine_body(dest_ref, dest_next_ref, ys_ref, wt_ref, sh_ref, x1_ref, mod_ref, gpost, o_ref,
                  buf0, buf1, sem0, sem1):
    tile = COMBINE_TILE
    i = pl.program_id(0)
    bufs, sems = (buf0, buf1), (sem0, sem1)
    half = SUBLANES * LANES
    chunk = tile // SUBLANES

    def copies(dref, col0, slot):
        def row_copy(t, k):
            return pltpu.make_async_copy(ys_ref.at[_tile_rows(dref[k, col0 + t])],
                                         bufs[slot].at[k, _tile_rows(t)], sems[slot])
        return row_copy

    def run(loop_body):
        lax.fori_loop(0, tile // TOKENS_PER_ISSUE, loop_body, 0)

    @pl.when(i == 0)
    def _():
        run(_row_dma_loops(copies(dest_ref, 0, 0))[0])

    def phase(row0, slot, next_copy):
        run(_row_dma_loops(copies(dest_ref, row0, slot))[1])
        rows = slice(row0, row0 + tile)
        shared = sh_ref[rows, :]
        wt = wt_ref[rows, :]
        wk = [jnp.broadcast_to(wt[:, k:k + 1], (tile, LANES)) for k in range(TOP_K)]
        ya, yb = [], []
        for s in range(SUBLANES):
            for t in range(s * chunk, (s + 1) * chunk):
                for k in range(TOP_K):
                    next_copy(t, k).start(priority=k % 2)
            sa = shared[:, s * LANES:(s + 1) * LANES]
            sb = shared[:, half + s * LANES:half + (s + 1) * LANES]
            for k in range(TOP_K):
                a, b = _unpack_halves(bufs[slot][k, pl.ds(s, tile, stride=SUBLANES), :])
                sa = sa + wk[k] * a
                sb = sb + wk[k] * b
            ya.append(sa)
            yb.append(sb)
        y = jnp.concatenate(ya + yb, axis=1)
        gate2 = mod_ref[0, 5:6, :]
        o_ref[rows, :] = x1_ref[rows, :] + gate2 * (_rms(y) * gpost[...])

    phase(0, 0, copies(dest_ref, tile, 1))
    phase(tile, 1, copies(dest_next_ref, 0, 0))

    @pl.when(i == pl.num_programs(0) - 1)
    def _():
        run(_row_dma_loops(copies(dest_next_ref, 0, 0))[1])


def _combine(ys, dest, wt, shared, x1, mod3, g_post, seq):
    n, d = x1.shape
    tm = 2 * COMBINE_TILE
    nstep = n // tm
    per_b = seq // tm
    const = lambda shape: pl.BlockSpec(shape, lambda i, *_: (0,) * len(shape), pipeline_mode=pl.Buffered(1))
    buf = pltpu.VMEM((TOP_K, COMBINE_TILE * SUBLANES, LANES), U32)
    grid_spec = pltpu.PrefetchScalarGridSpec(
        num_scalar_prefetch=0,
        grid=(nstep,),
        in_specs=[
            pl.BlockSpec((TOP_K, tm), lambda i: (0, i), memory_space=pltpu.SMEM),
            pl.BlockSpec((TOP_K, tm), lambda i: (0, jnp.minimum(i + 1, nstep - 1)), memory_space=pltpu.SMEM),
            pl.BlockSpec(memory_space=pl.ANY),
            pl.BlockSpec((tm, LANES), lambda i: (i, 0)),
            pl.BlockSpec((tm, d), lambda i: (i, 0)),
            pl.BlockSpec((tm, d), lambda i: (i, 0)),
            pl.BlockSpec((1, N_MOD, d), lambda i: (i // per_b, 0, 0)),
            const((1, d)),
        ],
        out_specs=pl.BlockSpec((tm, d), lambda i: (i, 0)),
        scratch_shapes=[buf, buf, pltpu.SemaphoreType.DMA(()), pltpu.SemaphoreType.DMA(())],
    )
    return pl.pallas_call(
        _combine_body,
        grid_spec=grid_spec,
        out_shape=jax.ShapeDtypeStruct((n, d), F32),
        compiler_params=_cparams(("arbitrary",), disable_bounds_checks=True),
        name="combine",
    )(dest, dest, ys, wt, shared, x1, mod3, g_post.reshape(1, d))


def _layer(x, mod, positions, g_pre_mix, g_post_mix, g_pre_ffn, g_post_ffn, w_in, w_dil_out, w_sb_out, w_mix_out,
           w_router, router_bias, w_gate_e, w_up_e, w_down_e, w_gate_s, w_up_s, w_down_s):
    bsz, seq, d = x.shape
    n = bsz * seq
    xf = x.reshape(n, d)
    mod3 = mod.reshape(bsz, N_MOD, d)

    nd, nq = 3 * WIDTH_DIL, 3 * (WIDTH_DIL + WIDTH_SB)
    w_plain = jnp.concatenate([w_in[:, nq:], w_in[:, nd:nq]], axis=1).astype(BF16)
    cols = []
    for g in range(len(DIL_PATTERNS)):
        for part in range(3):
            lo = part * WIDTH_DIL + g * D_DIL_OUT
            cols.append(w_in[:, lo:lo + D_DIL_OUT])
    w_dil = jnp.concatenate(cols, axis=1).astype(BF16)

    proj = _inproj(xf, g_pre_mix, mod3, w_plain, seq)
    tables = _rope_tables(positions)
    qkv_dil = _inproj_dil(xf, g_pre_mix, mod3, w_dil, tables, bsz, seq)
    o_dil, lse_dil = [], []
    for g, (window, dilation) in enumerate(DIL_PATTERNS):
        assert window // dilation == Q_BLOCK
        o, lse = _dilated_attention(qkv_dil[g], g, bsz, seq)
        o_dil.append(o)
        lse_dil.append(lse)
    o_sb = _stick_breaking(proj, bsz, seq)

    x1, h2 = _mixout(o_dil, lse_dil, o_sb, proj, xf, mod3, g_post_mix, g_pre_ffn,
                     w_dil_out.astype(BF16), w_sb_out.astype(BF16), w_mix_out.astype(BF16), seq)

    top_idx, top_wt, load = _router(h2, w_router, router_bias)
    counts = load[:, 0].astype(I32)
    padded = (counts + ROW_BLOCK - 1) // ROW_BLOCK * ROW_BLOCK
    pends = jnp.cumsum(padded)
    pstart = pends - padded
    nblk = _capacity(n) // ROW_BLOCK
    n_used = (pends[-1] // ROW_BLOCK).astype(I32)
    blk = jnp.minimum(jnp.arange(nblk, dtype=I32), n_used - 1)
    block_e = jnp.minimum(jnp.sum(pends[None, :] <= (blk * ROW_BLOCK)[:, None], axis=1), N_EXPERTS - 1).astype(I32)
    eids = jnp.arange(N_EXPERTS, dtype=I32)
    has = counts > 0
    ord_e = (jnp.cumsum(has) - has).astype(I32)
    later = jnp.where((eids[None, :] > eids[:, None]) & has[None, :], eids[None, :], N_EXPERTS)
    nxt = jnp.min(later, axis=1)
    next_e = jnp.where(nxt < N_EXPERTS, nxt, eids).astype(I32)
    dest = _plan_dest(top_idx, pstart)

    xs, shared = _dispatch(h2, dest, pstart.astype(I32), counts,
                           w_gate_s.astype(BF16), w_up_s.astype(BF16), w_down_s.astype(BF16))
    ys = _experts(xs, block_e, n_used.reshape(1), next_e, ord_e, w_gate_e, w_up_e, w_down_e)
    out = _combine(ys, dest, top_wt, shared, x1, mod3, g_post_ffn, seq)
    return out.reshape(bsz, seq, d)


def kernel(x, c, positions, w_ada, b_ada, g_pre_mix, g_post_mix, g_pre_ffn, g_post_ffn, w_in, w_dil_out,
           w_sb_out, w_mix_out, w_router, router_bias, w_gate_e, w_up_e, w_down_e, w_gate_s, w_up_s, w_down_s):
    for l in range(w_ada.shape[0]):
        mod = _adaln(c, w_ada[l], b_ada[l])
        x = _layer(x, mod, positions, g_pre_mix[l], g_post_mix[l], g_pre_ffn[l], g_post_ffn[l], w_in[l],
                   w_dil_out[l], w_sb_out[l], w_mix_out[l], w_router[l], router_bias[l],
                   w_gate_e[l], w_up_e[l], w_down_e[l], w_gate_s[l], w_up_s[l], w_down_s[l])
    return x
```

```python
import functools

import jax
import jax.numpy as jnp
from jax import lax
from jax.experimental import pallas as pl
from jax.experimental.pallas import tpu as pltpu

F32 = jnp.float32
BF16 = jnp.bfloat16
I32 = jnp.int32
U32 = jnp.uint32

D_MODEL = 2048
HEAD_DIM = 128
DIL_PATTERNS = ((128, 1), (512, 4), (2048, 16))
HEADS_PER_GROUP = 4
N_HEADS_DIL = 12
N_HEADS_SB = 8
WIDTH_DIL = N_HEADS_DIL * HEAD_DIM
WIDTH_SB = N_HEADS_SB * HEAD_DIM
D_DIL_OUT = HEADS_PER_GROUP * HEAD_DIM
Q_BLOCK = 128
ROPE_THETA = 500000.0
ROPE_DIM = HEAD_DIM // 4
N_GATE = 2 * D_MODEL
N_EXPERTS = 64
TOP_K = 8
N_GROUPS = 8
GROUP_SIZE = N_EXPERTS // N_GROUPS
TOPK_GROUPS = 4
D_EXPERT = 512
D_SHARED = 512
ROUTED_SCALE = 2.5
RMS_EPS = 1e-6
N_MOD = 6
ATTN_SCALE = HEAD_DIM ** -0.5

LANES = 128
SUBLANES = 8
VMEM_LIMIT = 56 * 1024 * 1024

ROW_BLOCK = 704
SB_DEAD = -110.0


def _cparams(sem, **kw):
    return pltpu.CompilerParams(dimension_semantics=sem, vmem_limit_bytes=VMEM_LIMIT, **kw)


def _adaln_body(ct_ref, w_ref, b_ref, o_ref, *, kc):
    nb = ct_ref.shape[1]
    nk = w_ref.shape[0] // kc

    def step(i, acc):
        k0 = pl.multiple_of(i * kc, kc)
        w = w_ref[pl.ds(k0, kc), :]
        c = ct_ref[pl.ds(k0, kc), :]
        s = c * jax.nn.sigmoid(c)
        parts = [jnp.sum(w * s[:, b:b + 1], axis=0, keepdims=True) for b in range(nb)]
        return acc + jnp.concatenate(parts, axis=0)

    acc = lax.fori_loop(0, nk, step, jnp.zeros(o_ref.shape, F32))
    o_ref[...] = acc + b_ref[...]


def _adaln(c, w_ada, b_ada):
    nb, d = c.shape
    n_out = w_ada.shape[1]
    tn = 1024
    return pl.pallas_call(
        functools.partial(_adaln_body, kc=256),
        grid=(n_out // tn,),
        in_specs=[
            pl.BlockSpec((d, nb), lambda j: (0, 0)),
            pl.BlockSpec((d, tn), lambda j: (0, j)),
            pl.BlockSpec((1, tn), lambda j: (0, j)),
        ],
        out_specs=pl.BlockSpec((nb, tn), lambda j: (0, j)),
        out_shape=jax.ShapeDtypeStruct((nb, n_out), F32),
        compiler_params=_cparams(("arbitrary",)),
        name="adaln",
    )(c.T, w_ada, b_ada.reshape(1, n_out))


def _rms(x):
    return x * lax.rsqrt(jnp.mean(x * x, axis=-1, keepdims=True) + RMS_EPS)


def _prenorm(x_ref, g_ref, mod_ref):
    y = _rms(x_ref[...]) * g_ref[...]
    shift = mod_ref[0, 0:1, :]
    scale = mod_ref[0, 1:2, :]
    return (y * (1.0 + scale) + shift).astype(BF16)


def _inproj_body(x_ref, g_ref, mod_ref, w_ref, o_ref, h_ref):
    @pl.when(pl.program_id(1) == 0)
    def _():
        h_ref[...] = _prenorm(x_ref, g_ref, mod_ref)

    o_ref[...] = jnp.dot(h_ref[...], w_ref[...], preferred_element_type=F32).astype(o_ref.dtype)


def _inproj(xf, g_pre, mod3, w_b, seq):
    n, d = xf.shape
    width = w_b.shape[1]
    tm, tn = 1024, width // 4
    per_b = seq // tm
    return pl.pallas_call(
        _inproj_body,
        grid=(n // tm, width // tn),
        in_specs=[
            pl.BlockSpec((tm, d), lambda i, j: (i, 0)),
            pl.BlockSpec((1, d), lambda i, j: (0, 0)),
            pl.BlockSpec((1, N_MOD, d), lambda i, j: (i // per_b, 0, 0)),
            pl.BlockSpec((d, tn), lambda i, j: (0, j)),
        ],
        out_specs=pl.BlockSpec((tm, tn), lambda i, j: (i, j)),
        out_shape=jax.ShapeDtypeStruct((n, width), BF16),
        scratch_shapes=[pltpu.VMEM((tm, d), BF16)],
        compiler_params=_cparams(("arbitrary", "arbitrary")),
        name="inproj",
    )(xf, g_pre.reshape(1, d), mod3, w_b)


def _inproj_dil_body(x_ref, g_ref, mod_ref, w_ref, t_ref, o0, o1, o2, res_ref):
    tm = x_ref.shape[0]
    h = _prenorm(x_ref, g_ref, mod_ref)
    t = t_ref[...]
    gw = 3 * D_DIL_OUT
    for gi, o_ref in enumerate((o0, o1, o2)):
        dil = DIL_PATTERNS[gi][1]
        res = jnp.dot(h, w_ref[:, gi * gw:(gi + 1) * gw], preferred_element_type=F32)
        for hs in range(3 * HEADS_PER_GROUP):
            sl = slice(hs * HEAD_DIM, (hs + 1) * HEAD_DIM)
            res_ref[hs] = _apply_rope(res[:, sl], t) if hs < 2 * HEADS_PER_GROUP else res[:, sl]
        for r in range(dil):
            for hs in range(3 * HEADS_PER_GROUP):
                rows = res_ref[hs] if dil == 1 else res_ref[hs, pl.ds(r, tm // dil, stride=dil), :]
                o_ref[0, r, :, hs * HEAD_DIM:(hs + 1) * HEAD_DIM] = rows.astype(o_ref.dtype)


def _inproj_dil(xf, g_pre, mod3, w_b, tables, bsz, seq):
    n, d = xf.shape
    gw = 3 * D_DIL_OUT
    tm = 512
    per_b = seq // tm
    dils = [p[1] for p in DIL_PATTERNS]
    return pl.pallas_call(
        _inproj_dil_body,
        grid=(n // tm,),
        in_specs=[
            pl.BlockSpec((tm, d), lambda i: (i, 0)),
            pl.BlockSpec((1, d), lambda i: (0, 0)),
            pl.BlockSpec((1, N_MOD, d), lambda i: (i // per_b, 0, 0)),
            pl.BlockSpec(w_b.shape, lambda i: (0, 0), pipeline_mode=pl.Buffered(1)),
            pl.BlockSpec((tm, 3 * LANES), lambda i: (i, 0)),
        ],
        out_specs=[pl.BlockSpec((1, dl, tm // dl, gw), lambda i: (i // per_b, 0, i % per_b, 0)) for dl in dils],
        out_shape=[jax.ShapeDtypeStruct((bsz, dl, seq // dl, gw), BF16) for dl in dils],
        scratch_shapes=[pltpu.VMEM((gw // HEAD_DIM, tm, HEAD_DIM), F32)],
        compiler_params=_cparams(("arbitrary",)),
        name="inproj_dil",
    )(xf, g_pre.reshape(1, d), mod3, w_b, tables)


def _rope_body(pos_ref, f_ref, o_ref):
    ang = pos_ref[...].astype(F32) * f_ref[...]
    c = jnp.cos(ang)
    s = jnp.sin(ang)
    lane = lax.broadcasted_iota(I32, ang.shape, 1)
    half = ROPE_DIM // 2
    o_ref[:, 0:LANES] = c
    o_ref[:, LANES:2 * LANES] = jnp.where(lane >= half, s, 0.0)
    o_ref[:, 2 * LANES:3 * LANES] = jnp.where(lane < half, -s, 0.0)


def _rope_tables(positions):
    n = positions.size
    half = ROPE_DIM // 2
    inv_freq = ROPE_THETA ** (-jnp.arange(0, ROPE_DIM, 2, dtype=F32) / ROPE_DIM)
    f = jnp.concatenate([inv_freq, inv_freq, jnp.zeros((LANES - 2 * half,), F32)]).reshape(1, LANES)
    tm = 2048
    return pl.pallas_call(
        _rope_body,
        grid=(n // tm,),
        in_specs=[pl.BlockSpec((tm, 1), lambda i: (i, 0)), pl.BlockSpec((1, LANES), lambda i: (0, 0))],
        out_specs=pl.BlockSpec((tm, 3 * LANES), lambda i: (i, 0)),
        out_shape=jax.ShapeDtypeStruct((n, 3 * LANES), F32),
        compiler_params=_cparams(("arbitrary",)),
        name="rope_tables",
    )(positions.reshape(n, 1), f)


def _apply_rope(x, t):
    half = ROPE_DIM // 2
    return (x * t[:, 0:LANES]
            + pltpu.roll(x, half, 1) * t[:, LANES:2 * LANES]
            + pltpu.roll(x, LANES - half, 1) * t[:, 2 * LANES:3 * LANES])


def _dil_body(cur_ref, kp_ref, vp_ref, o_ref, lse_ref, obuf, lbuf, *, dil, nsub):
    n = pl.program_id(1)
    tq = nsub * Q_BLOCK
    row = lax.broadcasted_iota(I32, (Q_BLOCK, 2 * Q_BLOCK), 0)
    col = lax.broadcasted_iota(I32, (Q_BLOCK, 2 * Q_BLOCK), 1)
    rel = row + Q_BLOCK - col
    band = jnp.where(rel >= 0, jnp.where(rel <= Q_BLOCK, 1.0, 0.0), 0.0)
    first = jnp.where(col >= Q_BLOCK, band, jnp.where(n > 0, band, 0.0))
    for r in range(dil):
        for h in range(HEADS_PER_GROUP):
            sl = slice(h * HEAD_DIM, (h + 1) * HEAD_DIM)
            ksl = slice(D_DIL_OUT + h * HEAD_DIM, D_DIL_OUT + (h + 1) * HEAD_DIM)
            vsl = slice(2 * D_DIL_OUT + h * HEAD_DIM, 2 * D_DIL_OUT + (h + 1) * HEAD_DIM)
            for j in range(nsub):
                rs = slice(j * Q_BLOCK, (j + 1) * Q_BLOCK)
                ps = slice((j - 1) * Q_BLOCK, j * Q_BLOCK)
                kprev = kp_ref[0, r, :, sl] if j == 0 else cur_ref[0, r, ps, ksl]
                vprev = vp_ref[0, r, :, sl] if j == 0 else cur_ref[0, r, ps, vsl]
                kcat = jnp.concatenate([kprev, cur_ref[0, r, rs, ksl]], axis=0)
                vcat = jnp.concatenate([vprev, cur_ref[0, r, rs, vsl]], axis=0)
                s = lax.dot_general(cur_ref[0, r, rs, sl], kcat, (((1,), (1,)), ((), ())),
                                    preferred_element_type=F32) * ATTN_SCALE
                s = jnp.where((first if j == 0 else band) > 0.0, s, -jnp.inf)
                m = jnp.max(s, axis=-1, keepdims=True)
                p = jnp.exp(s - m)
                l = jnp.sum(p, axis=-1, keepdims=True)
                o = jnp.dot((p / l).astype(BF16), vcat, preferred_element_type=F32)
                lse = jnp.broadcast_to(m + jnp.log(l), (Q_BLOCK, HEAD_DIM))
                if dil == 1:
                    o_ref[h, rs, :] = o
                    lse_ref[h, rs, :] = lse
                else:
                    obuf[h, rs, :] = o
                    lbuf[h, rs, :] = lse
            if dil > 1:
                o_ref[h, pl.ds(r, tq, stride=dil), :] = obuf[h]
                lse_ref[h, pl.ds(r, tq, stride=dil), :] = lbuf[h]


def _dilated_attention(qkv, g, bsz, seq):
    dil = DIL_PATTERNS[g][1]
    length = seq // dil
    tq = min(4 * Q_BLOCK, (16 * Q_BLOCK) // dil, length)
    nsub = tq // Q_BLOCK
    nq = length // tq
    gw = 3 * D_DIL_OUT
    n = bsz * seq

    def prev(colblk):
        return pl.BlockSpec((1, dil, Q_BLOCK, D_DIL_OUT),
                            lambda b, i: (b, 0, jnp.maximum(i * nsub - 1, 0), colblk))

    nh = HEADS_PER_GROUP
    out_spec = pl.BlockSpec((nh, tq * dil, HEAD_DIM), lambda b, i: (0, b * nq + i, 0))
    out_shape = jax.ShapeDtypeStruct((nh, n, HEAD_DIM), F32)
    return pl.pallas_call(
        functools.partial(_dil_body, dil=dil, nsub=nsub),
        grid=(bsz, nq),
        in_specs=[pl.BlockSpec((1, dil, tq, gw), lambda b, i: (b, 0, i, 0)), prev(1), prev(2)],
        out_specs=[out_spec, out_spec],
        out_shape=[out_shape, out_shape],
        scratch_shapes=[pltpu.VMEM((nh, tq, HEAD_DIM), F32), pltpu.VMEM((nh, tq, HEAD_DIM), F32)],
        compiler_params=_cparams(("arbitrary", "arbitrary")),
        name=f"dilated_d{dil}",
    )(qkv, qkv, qkv)


def _sb_body(q_ref, k_ref, v_ref, o_ref, acc_ref, car_ref):
    nblk = q_ref.shape[1] // Q_BLOCK
    r = lax.broadcasted_iota(I32, (Q_BLOCK, Q_BLOCK), 0)
    c = lax.broadcasted_iota(I32, (Q_BLOCK, Q_BLOCK), 1)
    causal = c < r
    rr = lax.broadcasted_iota(I32, (Q_BLOCK, 2 * Q_BLOCK), 0)
    cc = lax.broadcasted_iota(I32, (Q_BLOCK, 2 * Q_BLOCK), 1)
    uo = jnp.where(cc >= Q_BLOCK, 1.0, jnp.where(rr > cc, 1.0, 0.0)).astype(BF16)

    nh = acc_ref.shape[0]
    heads = range(nh)

    hs = [slice(h * HEAD_DIM, (h + 1) * HEAD_DIM) for h in heads]

    def key_rows(kb):
        start = kb * Q_BLOCK
        return pl.ds(start if isinstance(start, int) else pl.multiple_of(start, Q_BLOCK), Q_BLOCK)

    def scores(qs, kb):
        return [lax.dot_general(qs[h], k_ref[0, key_rows(kb), hs[h]], (((1,), (1,)), ((), ())),
                                preferred_element_type=F32) * ATTN_SCALE for h in heads]

    def log_terms(zs, diag):
        stacked, log_s = [], []
        for z in zs:
            sp = jnp.log(1.0 + jnp.exp(-jnp.abs(z)))
            mx = jnp.maximum(z, 0.0)
            log_1m = -(mx + sp)
            if diag:
                log_1m = jnp.where(causal, log_1m, 0.0)
            hi = log_1m.astype(BF16)
            lo = (log_1m - hi.astype(F32)).astype(BF16)
            stacked.append(jnp.concatenate([hi, lo], axis=0))
            log_s.append((z - mx) - sp)
        return stacked, log_s

    def suffix_sums(stacked):
        r2s = [jnp.dot(s, uo, preferred_element_type=F32) for s in stacked]
        return [r2[:Q_BLOCK] + r2[Q_BLOCK:] for r2 in r2s]

    def weights(log_s, carries, sums, diag):
        probs, new_carries = [], []
        for h in heads:
            a = jnp.exp(log_s[h] + carries[h] + sums[h][:, :Q_BLOCK])
            if diag:
                a = jnp.where(causal, a, 0.0)
            probs.append(a.astype(BF16))
            new_carries.append(carries[h] + sums[h][:, Q_BLOCK:])
        return probs, new_carries

    def values(probs, kb):
        return [jnp.dot(probs[h], v_ref[0, key_rows(kb), hs[h]], preferred_element_type=F32) for h in heads]

    def all_max(xs):
        m = xs[0]
        for x in xs[1:]:
            m = jnp.maximum(m, x)
        return jnp.max(m)

    def qblock(qi, with_prev):
        qs = [q_ref[0, key_rows(qi), hs[h]] for h in heads]
        zero = jnp.zeros((Q_BLOCK, Q_BLOCK), F32)
        zs_d = scores(qs, qi)
        zs_p = scores(qs, qi - 1) if with_prev else None
        st_d, ls_d = log_terms(zs_d, True)
        st_p, ls_p = log_terms(zs_p, False) if with_prev else (None, None)
        sm_d = suffix_sums(st_d)
        sm_p = suffix_sums(st_p) if with_prev else None
        pr_d, cars = weights(ls_d, [zero] * nh, sm_d, True)
        pvs = values(pr_d, qi)
        if with_prev:
            pr_p, cars = weights(ls_p, cars, sm_p, False)
            pvs = [a + b for a, b in zip(pvs, values(pr_p, qi - 1))]
        for h in heads:
            acc_ref[h] = pvs[h]
            car_ref[h] = cars[h]

        def cond(st):
            return jnp.logical_and(st[0] >= 0, st[1] > SB_DEAD)

        def body(st):
            stacked, log_s = log_terms(scores(qs, st[0]), False)
            probs, cars = weights(log_s, [car_ref[h] for h in heads], suffix_sums(stacked), False)
            pvs = values(probs, st[0])
            for h in heads:
                acc_ref[h] += pvs[h]
                car_ref[h] = cars[h]
            return st[0] - 1, all_max(cars)

        lax.while_loop(cond, body, (qi - (2 if with_prev else 1), all_max(cars)))
        for h in heads:
            o_ref[0, key_rows(qi), hs[h]] = acc_ref[h].astype(o_ref.dtype)
        return 0

    qblock(0, False)
    lax.fori_loop(1, nblk, lambda qi, c: qblock(qi, True), 0)


SB_HEADS_PER_STEP = 8


def _stick_breaking(proj, bsz, seq):
    width = proj.shape[1]
    pv = proj.reshape(bsz, seq, width)
    nh = SB_HEADS_PER_STEP
    bw = nh * HEAD_DIM
    base = N_GATE // bw

    def spec(off):
        return pl.BlockSpec((1, seq, bw), lambda b, h: (b, 0, base + off + h), pipeline_mode=pl.Buffered(1))

    nstep = N_HEADS_SB // nh
    o = pl.pallas_call(
        _sb_body,
        grid=(bsz, nstep),
        in_specs=[spec(0), spec(nstep), spec(2 * nstep)],
        out_specs=pl.BlockSpec((1, seq, bw), lambda b, h: (b, 0, h)),
        out_shape=jax.ShapeDtypeStruct((bsz, seq, WIDTH_SB), BF16),
        scratch_shapes=[pltpu.VMEM((nh, Q_BLOCK, Q_BLOCK), F32), pltpu.VMEM((nh, Q_BLOCK, Q_BLOCK), F32)],
        compiler_params=_cparams(("arbitrary", "arbitrary")),
        name="stick_breaking",
    )(pv, pv, pv)
    return o.reshape(bsz * seq, WIDTH_SB)


def _mixout_body(o1, o2, o3, l1, l2, l3, osb, gd_ref, gs_ref, x_ref, mod_ref, gpost, gpre,
                 wd, ws, wm, x1_ref, h2_ref):
    heads = []
    for h in range(HEADS_PER_GROUP):
        la, lb, lc = l1[h], l2[h], l3[h]
        m = jnp.maximum(la, jnp.maximum(lb, lc))
        ea, eb, ec = jnp.exp(la - m), jnp.exp(lb - m), jnp.exp(lc - m)
        heads.append(((ea * o1[h] + eb * o2[h] + ec * o3[h]) / (ea + eb + ec)).astype(BF16))
    yd = jnp.dot(jnp.concatenate(heads, axis=1), wd[...], preferred_element_type=F32)
    ys = jnp.dot(osb[...], ws[...], preferred_element_type=F32)
    mix = jax.nn.sigmoid(gd_ref[...].astype(F32)) * yd + jax.nn.sigmoid(gs_ref[...].astype(F32)) * ys
    y = jnp.dot(mix.astype(BF16), wm[...], preferred_element_type=F32)
    gate1 = mod_ref[0, 2:3, :]
    shift2 = mod_ref[0, 3:4, :]
    scale2 = mod_ref[0, 4:5, :]
    x1 = x_ref[...] + gate1 * (_rms(y) * gpost[...])
    x1_ref[...] = x1
    h2_ref[...] = ((_rms(x1) * gpre[...]) * (1.0 + scale2) + shift2).astype(h2_ref.dtype)


def _const_spec(shape):
    return pl.BlockSpec(shape, lambda i: (0,) * len(shape), pipeline_mode=pl.Buffered(1))


def _mixout(o_dil, lse_dil, o_sb, proj, xf, mod3, g_post, g_pre, wd_b, ws_b, wm_b, seq):
    n, d = xf.shape
    tm = 256
    per_b = seq // tm
    row = lambda w: pl.BlockSpec((tm, w), lambda i: (i, 0))
    head_major = pl.BlockSpec((HEADS_PER_GROUP, tm, HEAD_DIM), lambda i: (0, i, 0))
    in_specs = (
        [head_major] * 6 + [row(WIDTH_SB)]
        + [pl.BlockSpec((tm, d), lambda i: (i, 0)), pl.BlockSpec((tm, d), lambda i: (i, 1))]
        + [row(d), pl.BlockSpec((1, N_MOD, d), lambda i: (i // per_b, 0, 0))]
        + [_const_spec((1, d)), _const_spec((1, d))]
        + [_const_spec(wd_b.shape), _const_spec(ws_b.shape), _const_spec(wm_b.shape)]
    )
    return pl.pallas_call(
        _mixout_body,
        grid=(n // tm,),
        in_specs=in_specs,
        out_specs=[row(d), row(d)],
        out_shape=[jax.ShapeDtypeStruct((n, d), F32), jax.ShapeDtypeStruct((n, d), BF16)],
        compiler_params=_cparams(("arbitrary",)),
        name="mixout",
    )(*o_dil, *lse_dil, o_sb, proj, proj, xf, mod3, g_post.reshape(1, d), g_pre.reshape(1, d), wd_b, ws_b, wm_b)


def _topk_rows(x, k, iota0):
    big = x.shape[0]
    out = []
    for _ in range(k):
        m = jnp.max(x, axis=0, keepdims=True)
        i = jnp.min(jnp.where(x == m, iota0, big), axis=0, keepdims=True)
        out.append((m, i))
        x = jnp.where(iota0 == i, -jnp.inf, x)
    return out


def _router_body(h_ref, wr_ref, bias_ref, idx_ref, wt_ref, cnt_ref):
    tm = h_ref.shape[0]

    @pl.when(pl.program_id(0) == 0)
    def _():
        cnt_ref[...] = jnp.zeros(cnt_ref.shape, F32)

    logits = lax.dot_general(wr_ref[...], h_ref[...], (((1,), (1,)), ((), ())), preferred_element_type=F32)
    scores = jax.nn.sigmoid(logits)
    sel = scores + bias_ref[...]
    sub = lax.broadcasted_iota(I32, (GROUP_SIZE, tm), 0)
    grp = []
    for g in range(N_GROUPS):
        (m1, _), (m2, _) = _topk_rows(sel[g * GROUP_SIZE:(g + 1) * GROUP_SIZE], 2, sub)
        grp.append(m1 + m2)
    gscore = jnp.concatenate(grp, axis=0)
    giota = lax.broadcasted_iota(I32, (N_GROUPS, tm), 0)
    gmask = jnp.zeros((N_GROUPS, tm), F32)
    for _, gi in _topk_rows(gscore, TOPK_GROUPS, giota):
        gmask = jnp.where(giota == gi, 1.0, gmask)
    masked = jnp.concatenate(
        [jnp.where(gmask[g:g + 1] > 0.0, sel[g * GROUP_SIZE:(g + 1) * GROUP_SIZE], -jnp.inf)
         for g in range(N_GROUPS)], axis=0)
    eiota = lax.broadcasted_iota(I32, (N_EXPERTS, tm), 0)
    picks = _topk_rows(masked, TOP_K, eiota)
    idx = jnp.concatenate([i for _, i in picks], axis=0)
    top_s = jnp.concatenate(
        [jnp.sum(jnp.where(eiota == i, scores, 0.0), axis=0, keepdims=True) for _, i in picks], axis=0)
    top_w = top_s / jnp.sum(top_s, axis=0, keepdims=True) * ROUTED_SCALE
    idx_ref[...] = idx
    wpad = jnp.concatenate([top_w, jnp.zeros((LANES - TOP_K, tm), F32)], axis=0)
    wt_ref[...] = wpad.T
    hit = jnp.zeros((N_EXPERTS, tm), F32)
    for _, i in picks:
        hit = jnp.where(eiota == i, 1.0, hit)
    cnt_ref[...] += jnp.sum(hit, axis=1, keepdims=True)


def _router(h2, w_router, router_bias):
    n, d = h2.shape
    tm = 512
    return pl.pallas_call(
        _router_body,
        grid=(n // tm,),
        in_specs=[
            pl.BlockSpec((tm, d), lambda i: (i, 0)),
            pl.BlockSpec((N_EXPERTS, d), lambda i: (0, 0)),
            pl.BlockSpec((N_EXPERTS, 1), lambda i: (0, 0)),
        ],
        out_specs=[pl.BlockSpec((TOP_K, tm), lambda i: (0, i)), pl.BlockSpec((tm, LANES), lambda i: (i, 0)),
                   pl.BlockSpec((N_EXPERTS, LANES), lambda i: (0, 0))],
        out_shape=[jax.ShapeDtypeStruct((TOP_K, n), I32), jax.ShapeDtypeStruct((n, LANES), F32),
                   jax.ShapeDtypeStruct((N_EXPERTS, LANES), F32)],
        compiler_params=_cparams(("arbitrary",)),
        name="router",
    )(h2, w_router.T.astype(BF16), router_bias.reshape(N_EXPERTS, 1))


def _plan_tile(idx):
    tm = idx.shape[1]
    eiota = lax.broadcasted_iota(I32, (N_EXPERTS, tm), 0)
    hit = jnp.zeros((N_EXPERTS, tm), F32)
    for k in range(TOP_K):
        hit = jnp.where(eiota == idx[k:k + 1], 1.0, hit)
    r = lax.broadcasted_iota(I32, (tm, tm), 0)
    c = lax.broadcasted_iota(I32, (tm, tm), 1)
    before = jnp.where(r < c, 1.0, 0.0).astype(BF16)
    excl = jnp.dot(hit.astype(BF16), before, preferred_element_type=F32)
    tot = excl[:, tm - 1:tm] + hit[:, tm - 1:tm]
    return eiota, hit, excl, tot


def _plan_dest_body(idx_ref, base_ref, dest_ref, run_ref):
    @pl.when(pl.program_id(0) == 0)
    def _():
        run_ref[...] = base_ref[...]

    idx = idx_ref[...]
    eiota, _, excl, tot = _plan_tile(idx)
    pos = run_ref[:, 0:1] + excl
    rows = [jnp.sum(jnp.where(eiota == idx[k:k + 1], pos, 0.0), axis=0, keepdims=True) for k in range(TOP_K)]
    dest_ref[...] = jnp.concatenate(rows, axis=0).astype(I32)
    run_ref[...] += tot


def _plan_dest(top_idx, pstart):
    n = top_idx.shape[1]
    tm = 512
    base = jnp.broadcast_to(pstart.astype(F32).reshape(N_EXPERTS, 1), (N_EXPERTS, LANES))
    return pl.pallas_call(
        _plan_dest_body,
        grid=(n // tm,),
        in_specs=[pl.BlockSpec((TOP_K, tm), lambda i: (0, i)), pl.BlockSpec((N_EXPERTS, LANES), lambda i: (0, 0))],
        out_specs=pl.BlockSpec((TOP_K, tm), lambda i: (0, i)),
        out_shape=jax.ShapeDtypeStruct((TOP_K, n), I32),
        scratch_shapes=[pltpu.VMEM((N_EXPERTS, LANES), F32)],
        compiler_params=_cparams(("arbitrary",)),
        name="plan_dest",
    )(top_idx, base)


def _capacity(n):
    return (pl.cdiv(n * TOP_K, ROW_BLOCK) + N_EXPERTS) * ROW_BLOCK


def _pack_halves(y):
    w = y.shape[1] // 2
    hi = pltpu.bitcast(y[:, :w].astype(BF16).astype(F32), U32)
    lo = pltpu.bitcast(y[:, w:].astype(BF16).astype(F32), U32)
    return hi | (lo >> 16)


def _unpack_halves(p):
    a = pltpu.bitcast(p & jnp.uint32(0xFFFF0000), F32)
    b = pltpu.bitcast(p << 16, F32)
    return a, b


TOKENS_PER_ISSUE = 4


def _row_dma_loops(row_copy):
    def issue(i, _):
        for u in range(TOKENS_PER_ISSUE):
            for k in range(TOP_K):
                row_copy(i * TOKENS_PER_ISSUE + u, k).start(priority=k % 2)
        return 0

    def drain(i, _):
        for u in range(TOKENS_PER_ISSUE):
            for k in range(TOP_K):
                row_copy(i * TOKENS_PER_ISSUE + u, k).wait()
        return 0

    return issue, drain


def _pad_chunks():
    s = 1
    while 2 * s < ROW_BLOCK:
        s *= 2
    sizes = []
    while s >= 1:
        sizes.append(s)
        s //= 2
    return sizes


def _tile_rows(t):
    start = t * SUBLANES
    return pl.ds(start if isinstance(start, int) else pl.multiple_of(start, SUBLANES), SUBLANES)


def _to_row_tiles(ref, packed, lead=()):
    rows = packed.shape[0]
    for s in range(SUBLANES):
        ref[(*lead, pl.ds(s, rows, stride=SUBLANES), slice(None))] = packed[:, s * LANES:(s + 1) * LANES]


def _from_row_tiles(ref, rows, lead=()):
    return [ref[(*lead, pl.ds(s, rows, stride=SUBLANES), slice(None))] for s in range(SUBLANES)]


def _dispatch_body(pstart_ref, cnt_ref, dest_ref, h_ref, wg, wu, wd, xs_ref, sh_ref, xbuf, zbuf, sem, zsem):
    tm = h_ref.shape[0]
    step = pl.program_id(0)

    def pad_dmas(wait):
        def per_expert(e, _):
            cnt = cnt_ref[e]
            pad = (ROW_BLOCK - cnt % ROW_BLOCK) % ROW_BLOCK
            start = pstart_ref[e] + cnt
            for size in _pad_chunks():
                @pl.when((pad & size) != 0)
                def _():
                    off = pl.multiple_of((start + (pad & ~(2 * size - 1))) * SUBLANES, SUBLANES)
                    cp = pltpu.make_async_copy(zbuf.at[pl.ds(0, size * SUBLANES)],
                                               xs_ref.at[pl.ds(off, size * SUBLANES)], zsem)
                    cp.wait() if wait else cp.start()
            return 0
        lax.fori_loop(0, N_EXPERTS, per_expert, 0)

    @pl.when(step == 0)
    def _():
        zbuf[...] = jnp.zeros(zbuf.shape, zbuf.dtype)
        pad_dmas(False)
        pad_dmas(True)

    h = h_ref[...]
    _to_row_tiles(xbuf, _pack_halves(h.astype(F32)))

    def row_copy(t, k):
        return pltpu.make_async_copy(xbuf.at[_tile_rows(t)], xs_ref.at[_tile_rows(dest_ref[k, t])], sem)

    def start_rows(lo, hi):
        for t in range(lo, hi):
            for k in range(TOP_K):
                row_copy(t, k).start(priority=k % 2)

    q = tm // 4
    g = jnp.dot(h, wg[...], preferred_element_type=F32)
    start_rows(0, q)
    u = jnp.dot(h, wu[...], preferred_element_type=F32)
    start_rows(q, 2 * q)
    hmid = (g * jax.nn.sigmoid(g) * u).astype(BF16)
    start_rows(2 * q, 3 * q)
    sh_ref[...] = jnp.dot(hmid, wd[...], preferred_element_type=F32)
    start_rows(3 * q, tm)

    lax.fori_loop(0, tm // TOKENS_PER_ISSUE, _row_dma_loops(row_copy)[1], 0)


def _dispatch(h2, dest, pstart, counts, wg_b, wu_b, wd_b):
    n, d = h2.shape
    tm = 512
    assert d // 2 == SUBLANES * LANES
    const = lambda shape: pl.BlockSpec(shape, lambda i, *_: (0,) * len(shape), pipeline_mode=pl.Buffered(1))
    grid_spec = pltpu.PrefetchScalarGridSpec(
        num_scalar_prefetch=2,
        grid=(n // tm,),
        in_specs=[
            pl.BlockSpec((TOP_K, tm), lambda i, *_: (0, i), memory_space=pltpu.SMEM),
            pl.BlockSpec((tm, d), lambda i, *_: (i, 0)),
            const(wg_b.shape), const(wu_b.shape), const(wd_b.shape),
        ],
        out_specs=[pl.BlockSpec(memory_space=pl.ANY), pl.BlockSpec((tm, d), lambda i, *_: (i, 0))],
        scratch_shapes=[
            pltpu.VMEM((tm * SUBLANES, LANES), U32),
            pltpu.VMEM((_pad_chunks()[0] * SUBLANES, LANES), U32),
            pltpu.SemaphoreType.DMA(()),
            pltpu.SemaphoreType.DMA(()),
        ],
    )
    return pl.pallas_call(
        _dispatch_body,
        grid_spec=grid_spec,
        out_shape=[jax.ShapeDtypeStruct((_capacity(n) * SUBLANES, LANES), U32), jax.ShapeDtypeStruct((n, d), F32)],
        compiler_params=_cparams(("arbitrary",), has_side_effects=True, disable_bounds_checks=True),
        name="dispatch",
    )(pstart, counts, dest, h2, wg_b, wu_b, wd_b)


def _experts_body(be_ref, nu_ref, nxt_ref, ord_ref, x_ref, wg_hbm, wu_hbm, wd_hbm, y_ref,
                  wg_s, wu_s, wd_s, wg_b, wu_b, wd_b, sems):
    i = pl.program_id(0)
    used = i < nu_ref[0]
    e = be_ref[i]
    fresh = jnp.logical_or(i == 0, e != be_ref[jnp.maximum(i - 1, 0)])
    slot = ord_ref[e] % 2

    def weight_copies(expert, s):
        return (pltpu.make_async_copy(wg_hbm.at[expert], wg_s.at[s], sems.at[s, 0]),
                pltpu.make_async_copy(wu_hbm.at[expert], wu_s.at[s], sems.at[s, 1]),
                pltpu.make_async_copy(wd_hbm.at[expert], wd_s.at[s], sems.at[s, 2]))

    @pl.when(i == 0)
    def _():
        for cp in weight_copies(e, slot):
            cp.start()

    @pl.when(jnp.logical_and(used, fresh))
    def _():
        copies = weight_copies(e, slot)
        for cp, dst, src in zip(copies, (wg_b, wu_b, wd_b), (wg_s, wu_s, wd_s)):
            cp.wait()
            dst[...] = src[slot].astype(BF16)

        @pl.when(nxt_ref[e] != e)
        def _():
            for cp in weight_copies(nxt_ref[e], 1 - slot):
                cp.start()

    @pl.when(used)
    def _():
        half = SUBLANES * LANES
        slabs = [_unpack_halves(p) for p in _from_row_tiles(x_ref, ROW_BLOCK)]
        xa = jnp.concatenate([a.astype(BF16) for a, _ in slabs], axis=1)
        xb = jnp.concatenate([b.astype(BF16) for _, b in slabs], axis=1)
        g = (jnp.dot(xa, wg_b[:half], preferred_element_type=F32)
             + jnp.dot(xb, wg_b[half:], preferred_element_type=F32))
        u = (jnp.dot(xa, wu_b[:half], preferred_element_type=F32)
             + jnp.dot(xb, wu_b[half:], preferred_element_type=F32))
        hmid = (g * jax.nn.sigmoid(g) * u).astype(BF16)
        _to_row_tiles(y_ref, _pack_halves(jnp.dot(hmid, wd_b[...], preferred_element_type=F32)))


def _experts(xs, block_e, n_used, next_e, ord_e, w_gate_e, w_up_e, w_down_e):
    cap = xs.shape[0] // SUBLANES
    d = 2 * SUBLANES * LANES
    nblk = cap // ROW_BLOCK
    row_block = (ROW_BLOCK * SUBLANES, LANES)

    def row_map(i, be, nu, nxt, od):
        return (jnp.minimum(i, nu[0] - 1), 0)

    grid_spec = pltpu.PrefetchScalarGridSpec(
        num_scalar_prefetch=4,
        grid=(nblk,),
        in_specs=[
            pl.BlockSpec(row_block, row_map),
            pl.BlockSpec(memory_space=pl.ANY),
            pl.BlockSpec(memory_space=pl.ANY),
            pl.BlockSpec(memory_space=pl.ANY),
        ],
        out_specs=pl.BlockSpec(row_block, row_map),
        scratch_shapes=[
            pltpu.VMEM((2, d, D_EXPERT), F32),
            pltpu.VMEM((2, d, D_EXPERT), F32),
            pltpu.VMEM((2, D_EXPERT, d), F32),
            pltpu.VMEM((d, D_EXPERT), BF16),
            pltpu.VMEM((d, D_EXPERT), BF16),
            pltpu.VMEM((D_EXPERT, d), BF16),
            pltpu.SemaphoreType.DMA((2, 3)),
        ],
    )
    return pl.pallas_call(
        _experts_body,
        grid_spec=grid_spec,
        out_shape=jax.ShapeDtypeStruct(xs.shape, U32),
        compiler_params=_cparams(("arbitrary",)),
        name="experts",
    )(block_e, n_used, next_e, ord_e, xs, w_gate_e, w_up_e, w_down_e)


COMBINE_TILE = 256


def _combine_body(dest_ref, dest_next_ref, ys_ref, wt_ref, sh_ref, x1_ref, mod_ref, gpost, o_ref,
                  buf0, buf1, sem0, sem1):
    tile = COMBINE_TILE
    i = pl.program_id(0)
    bufs, sems = (buf0, buf1), (sem0, sem1)
    half = SUBLANES * LANES
    chunk = tile // SUBLANES

    def copies(dref, col0, slot):
        def row_copy(t, k):
            return pltpu.make_async_copy(ys_ref.at[_tile_rows(dref[k, col0 + t])],
                                         bufs[slot].at[k, _tile_rows(t)], sems[slot])
        return row_copy

    def run(loop_body):
        lax.fori_loop(0, tile // TOKENS_PER_ISSUE, loop_body, 0)

    @pl.when(i == 0)
    def _():
        run(_row_dma_loops(copies(dest_ref, 0, 0))[0])

    def phase(row0, slot, next_copy):
        run(_row_dma_loops(copies(dest_ref, row0, slot))[1])
        rows = slice(row0, row0 + tile)
        shared = sh_ref[rows, :]
        wt = wt_ref[rows, :]
        wk = [jnp.broadcast_to(wt[:, k:k + 1], (tile, LANES)) for k in range(TOP_K)]
        ya, yb = [], []
        for s in range(SUBLANES):
            for t in range(s * chunk, (s + 1) * chunk):
                for k in range(TOP_K):
                    next_copy(t, k).start(priority=k % 2)
            sa = shared[:, s * LANES:(s + 1) * LANES]
            sb = shared[:, half + s * LANES:half + (s + 1) * LANES]
            for k in range(TOP_K):
                a, b = _unpack_halves(bufs[slot][k, pl.ds(s, tile, stride=SUBLANES), :])
                sa = sa + wk[k] * a
                sb = sb + wk[k] * b
            ya.append(sa)
            yb.append(sb)
        y = jnp.concatenate(ya + yb, axis=1)
        gate2 = mod_ref[0, 5:6, :]
        o_ref[rows, :] = x1_ref[rows, :] + gate2 * (_rms(y) * gpost[...])

    phase(0, 0, copies(dest_ref, tile, 1))
    phase(tile, 1, copies(dest_next_ref, 0, 0))

    @pl.when(i == pl.num_programs(0) - 1)
    def _():
        run(_row_dma_loops(copies(dest_next_ref, 0, 0))[1])


def _combine(ys, dest, wt, shared, x1, mod3, g_post, seq):
    n, d = x1.shape
    tm = 2 * COMBINE_TILE
    nstep = n // tm
    per_b = seq // tm
    const = lambda shape: pl.BlockSpec(shape, lambda i, *_: (0,) * len(shape), pipeline_mode=pl.Buffered(1))
    buf = pltpu.VMEM((TOP_K, COMBINE_TILE * SUBLANES, LANES), U32)
    grid_spec = pltpu.PrefetchScalarGridSpec(
        num_scalar_prefetch=0,
        grid=(nstep,),
        in_specs=[
            pl.BlockSpec((TOP_K, tm), lambda i: (0, i), memory_space=pltpu.SMEM),
            pl.BlockSpec((TOP_K, tm), lambda i: (0, jnp.minimum(i + 1, nstep - 1)), memory_space=pltpu.SMEM),
            pl.BlockSpec(memory_space=pl.ANY),
            pl.BlockSpec((tm, LANES), lambda i: (i, 0)),
            pl.BlockSpec((tm, d), lambda i: (i, 0)),
            pl.BlockSpec((tm, d), lambda i: (i, 0)),
            pl.BlockSpec((1, N_MOD, d), lambda i: (i // per_b, 0, 0)),
            const((1, d)),
        ],
        out_specs=pl.BlockSpec((tm, d), lambda i: (i, 0)),
        scratch_shapes=[buf, buf, pltpu.SemaphoreType.DMA(()), pltpu.SemaphoreType.DMA(())],
    )
    return pl.pallas_call(
        _combine_body,
        grid_spec=grid_spec,
        out_shape=jax.ShapeDtypeStruct((n, d), F32),
        compiler_params=_cparams(("arbitrary",), disable_bounds_checks=True),
        name="combine",
    )(dest, dest, ys, wt, shared, x1, mod3, g_post.reshape(1, d))


def _layer(x, mod, positions, g_pre_mix, g_post_mix, g_pre_ffn, g_post_ffn, w_in, w_dil_out, w_sb_out, w_mix_out,
           w_router, router_bias, w_gate_e, w_up_e, w_down_e, w_gate_s, w_up_s, w_down_s):
    bsz, seq, d = x.shape
    n = bsz * seq
    xf = x.reshape(n, d)
    mod3 = mod.reshape(bsz, N_MOD, d)

    nd, nq = 3 * WIDTH_DIL, 3 * (WIDTH_DIL + WIDTH_SB)
    w_plain = jnp.concatenate([w_in[:, nq:], w_in[:, nd:nq]], axis=1).astype(BF16)
    cols = []
    for g in range(len(DIL_PATTERNS)):
        for part in range(3):
            lo = part * WIDTH_DIL + g * D_DIL_OUT
            cols.append(w_in[:, lo:lo + D_DIL_OUT])
    w_dil = jnp.concatenate(cols, axis=1).astype(BF16)

    proj = _inproj(xf, g_pre_mix, mod3, w_plain, seq)
    tables = _rope_tables(positions)
    qkv_dil = _inproj_dil(xf, g_pre_mix, mod3, w_dil, tables, bsz, seq)
    o_dil, lse_dil = [], []
    for g, (window, dilation) in enumerate(DIL_PATTERNS):
        assert window // dilation == Q_BLOCK
        o, lse = _dilated_attention(qkv_dil[g], g, bsz, seq)
        o_dil.append(o)
        lse_dil.append(lse)
    o_sb = _stick_breaking(proj, bsz, seq)

    x1, h2 = _mixout(o_dil, lse_dil, o_sb, proj, xf, mod3, g_post_mix, g_pre_ffn,
                     w_dil_out.astype(BF16), w_sb_out.astype(BF16), w_mix_out.astype(BF16), seq)

    top_idx, top_wt, load = _router(h2, w_router, router_bias)
    counts = load[:, 0].astype(I32)
    padded = (counts + ROW_BLOCK - 1) // ROW_BLOCK * ROW_BLOCK
    pends = jnp.cumsum(padded)
    pstart = pends - padded
    nblk = _capacity(n) // ROW_BLOCK
    n_used = (pends[-1] // ROW_BLOCK).astype(I32)
    blk = jnp.minimum(jnp.arange(nblk, dtype=I32), n_used - 1)
    block_e = jnp.minimum(jnp.sum(pends[None, :] <= (blk * ROW_BLOCK)[:, None], axis=1), N_EXPERTS - 1).astype(I32)
    eids = jnp.arange(N_EXPERTS, dtype=I32)
    has = counts > 0
    ord_e = (jnp.cumsum(has) - has).astype(I32)
    later = jnp.where((eids[None, :] > eids[:, None]) & has[None, :], eids[None, :], N_EXPERTS)
    nxt = jnp.min(later, axis=1)
    next_e = jnp.where(nxt < N_EXPERTS, nxt, eids).astype(I32)
    dest = _plan_dest(top_idx, pstart)

    xs, shared = _dispatch(h2, dest, pstart.astype(I32), counts,
                           w_gate_s.astype(BF16), w_up_s.astype(BF16), w_down_s.astype(BF16))
    ys = _experts(xs, block_e, n_used.reshape(1), next_e, ord_e, w_gate_e, w_up_e, w_down_e)
    out = _combine(ys, dest, top_wt, shared, x1, mod3, g_post_ffn, seq)
    return out.reshape(bsz, seq, d)


def kernel(x, c, positions, w_ada, b_ada, g_pre_mix, g_post_mix, g_pre_ffn, g_post_ffn, w_in, w_dil_out,
           w_sb_out, w_mix_out, w_router, router_bias, w_gate_e, w_up_e, w_down_e, w_gate_s, w_up_s, w_down_s):
    for l in range(w_ada.shape[0]):
        mod = _adaln(c, w_ada[l], b_ada[l])
        x = _layer(x, mod, positions, g_pre_mix[l], g_post_mix[l], g_pre_ffn[l], g_post_ffn[l], w_in[l],
                   w_dil_out[l], w_sb_out[l], w_mix_out[l], w_router[l], router_bias[l],
                   w_gate_e[l], w_up_e[l], w_down_e[l], w_gate_s[l], w_up_s[l], w_down_s[l])
    return x
```
